```python
import math
import jax, jax.numpy as jnp
from jax import lax
import numpy as np

D_MODEL = 1024
BATCH = 2
SEQ = 8192
DEPTH = 1

GRID_W = 64
HEAD_DIM = 64
NA_HEADS = 8
NA_KH_MAX = 8
NA_KW = 16
NA_WIDTH = NA_HEADS * HEAD_DIM
SWA_Q_HEADS = 8
SWA_KV_HEADS = 2
SWA_WINDOW = 128
SWA_BLOCK = 128
SWA_Q_WIDTH = SWA_Q_HEADS * HEAD_DIM
SWA_KV_WIDTH = SWA_KV_HEADS * HEAD_DIM
ROPE_THETA = 500000.0
ROPE_DIM = HEAD_DIM // 4
D_FF = 4 * D_MODEL
IN_WIDTH = 3 * NA_WIDTH + SWA_Q_WIDTH + 2 * SWA_KV_WIDTH + 2 * D_MODEL
IN_SPLITS = [
    NA_WIDTH,
    2 * NA_WIDTH,
    3 * NA_WIDTH,
    3 * NA_WIDTH + SWA_Q_WIDTH,
    3 * NA_WIDTH + SWA_Q_WIDTH + SWA_KV_WIDTH,
    3 * NA_WIDTH + SWA_Q_WIDTH + 2 * SWA_KV_WIDTH,
    3 * NA_WIDTH + SWA_Q_WIDTH + 2 * SWA_KV_WIDTH + D_MODEL,
]
DEEPNORM_ALPHA = (2.0 * DEPTH) ** 0.25
DEEPNORM_BETA = (8.0 * DEPTH) ** -0.25
LN_EPS = 1e-5
MASK_VALUE = -1e30

kernel_name = "hybrid_natten_swa_gated_deepnorm_encoder"


def layer_norm(x, g, b):
    xf = x.astype(jnp.float32)
    mu = jnp.mean(xf, axis=-1, keepdims=True)
    xc = xf - mu
    var = jnp.mean(xc * xc, axis=-1, keepdims=True)
    y = xc * lax.rsqrt(var + LN_EPS) * g.astype(jnp.float32) + b.astype(jnp.float32)
    return y.astype(x.dtype)


def partial_rotary(x, pos):
    half = ROPE_DIM // 2
    inv_freq = jnp.power(ROPE_THETA, -jnp.arange(0, ROPE_DIM, 2, dtype=jnp.float32) / ROPE_DIM)
    ang = pos.astype(jnp.float32)[:, None] * inv_freq[None, :]
    cos = jnp.cos(ang)[:, None, :]
    sin = jnp.sin(ang)[:, None, :]
    xr = x[..., :ROPE_DIM].astype(jnp.float32)
    x1, x2 = xr[..., :half], xr[..., half:]
    rot = jnp.concatenate([x1 * cos - x2 * sin, x2 * cos + x1 * sin], axis=-1)
    return jnp.concatenate([rot.astype(x.dtype), x[..., ROPE_DIM:]], axis=-1)


def neighbourhood_attention(q, k, v, rpb):
    B, S, H, Dh = q.shape
    rows = S // GRID_W
    kh = min(NA_KH_MAX, rows)
    kw = NA_KW
    qg = q.reshape(B, rows, GRID_W, H, Dh)
    kg = k.reshape(B, rows, GRID_W, H, Dh)
    vg = v.reshape(B, rows, GRID_W, H, Dh)
    cols = jnp.arange(GRID_W)
    col_start = jnp.clip(cols - kw // 2, 0, GRID_W - kw)
    col_idx = col_start[:, None] + jnp.arange(kw)[None, :]
    dc = col_idx - cols[:, None]
    scale = Dh ** -0.5

    def one_row(r):
        rs = jnp.clip(r - kh // 2, 0, rows - kh)
        kr = lax.dynamic_slice_in_dim(kg, rs, kh, axis=1)
        vr = lax.dynamic_slice_in_dim(vg, rs, kh, axis=1)
        kn = kr[:, :, col_idx]
        vn = vr[:, :, col_idx]
        qr = lax.dynamic_index_in_dim(qg, r, axis=1, keepdims=False)
        s = jnp.einsum('bwhd,bkwjhd->bhwkj', qr, kn).astype(jnp.float32) * scale
        dr = rs + jnp.arange(kh) - r
        bias = rpb[:, (dr + NA_KH_MAX - 1)[None, :, None], (dc + kw - 1)[:, None, :]]
        s = s + bias.astype(jnp.float32)[None]
        p = jax.nn.softmax(s.reshape(B, H, GRID_W, kh * kw), axis=-1)
        p = p.reshape(B, H, GRID_W, kh, kw).astype(v.dtype)
        return jnp.einsum('bhwkj,bkwjhd->bwhd', p, vn)

    out = lax.map(one_row, jnp.arange(rows))
    return jnp.moveaxis(out, 0, 1).reshape(B, S, H * Dh)


def windowed_gqa_with_sink(q, k, v, sink):
    B, S, Hq, Dh = q.shape
    G = k.shape[2]
    rep = Hq // G
    blk = SWA_BLOCK
    nb = S // blk
    qb = q.reshape(B, nb, blk, G, rep, Dh)
    pad = ((0, 0), (blk, blk), (0, 0), (0, 0))
    kp = jnp.pad(k, pad).reshape(B, nb + 2, blk, G, Dh)
    vp = jnp.pad(v, pad).reshape(B, nb + 2, blk, G, Dh)
    kb = jnp.concatenate([kp[:, :-2], kp[:, 1:-1], kp[:, 2:]], axis=2)
    vb = jnp.concatenate([vp[:, :-2], vp[:, 1:-1], vp[:, 2:]], axis=2)
    s = jnp.einsum('bnqgrd,bnkgd->bngrqk', qb, kb).astype(jnp.float32) * (Dh ** -0.5)
    qpos = jnp.arange(nb)[:, None] * blk + jnp.arange(blk)[None, :]
    kpos = jnp.arange(nb)[:, None] * blk - blk + jnp.arange(3 * blk)[None, :]
    rel = kpos[:, None, :] - qpos[:, :, None]
    valid = (jnp.abs(rel) <= SWA_WINDOW) & (kpos[:, None, :] >= 0) & (kpos[:, None, :] < S)
    s = jnp.where(valid[None, :, None, None], s, MASK_VALUE)
    sink_l = sink.astype(jnp.float32).reshape(1, 1, G, rep, 1, 1)
    m = jnp.maximum(jnp.max(s, axis=-1, keepdims=True), sink_l)
    e = jnp.exp(s - m)
    p = e / (jnp.sum(e, axis=-1, keepdims=True) + jnp.exp(sink_l - m))
    o = jnp.einsum('bngrqk,bnkgd->bnqgrd', p.astype(v.dtype), vb)
    return o.reshape(B, S, Hq * Dh)


def setup_inputs(seed: int = 0) -> dict:
    key = jax.random.key(seed)
    ks = jax.random.split(key, 24)
    f32 = jnp.float32
    beta = DEEPNORM_BETA

    def nrm(k, shape, scale):
        return jax.random.normal(k, shape, f32) * scale

    col_scale = jnp.concatenate([
        jnp.ones((2 * NA_WIDTH,), f32), jnp.full((NA_WIDTH,), beta, f32),
        jnp.ones((SWA_Q_WIDTH + SWA_KV_WIDTH,), f32), jnp.full((SWA_KV_WIDTH,), beta, f32),
        jnp.ones((2 * D_MODEL,), f32)])
    w_in = nrm(ks[1], (DEPTH, D_MODEL, IN_WIDTH), D_MODEL ** -0.5) * col_scale
    return {
        "x": jax.random.normal(ks[0], (BATCH, SEQ, D_MODEL), f32),
        "ln0_g": 1.0 + nrm(ks[2], (D_MODEL,), 0.05),
        "ln0_b": nrm(ks[3], (D_MODEL,), 0.02),
        "w_in": w_in,
        "b_in": nrm(ks[4], (DEPTH, IN_WIDTH), 0.02),
        "na_rpb": nrm(ks[5], (DEPTH, NA_HEADS, 2 * NA_KH_MAX - 1, 2 * NA_KW - 1), 0.1),
        "swa_sink": nrm(ks[6], (DEPTH, SWA_Q_HEADS), 0.5),
        "w_branch_na": nrm(ks[7], (DEPTH, NA_WIDTH, D_MODEL), NA_WIDTH ** -0.5 * beta),
        "w_branch_swa": nrm(ks[8], (DEPTH, SWA_Q_WIDTH, D_MODEL), SWA_Q_WIDTH ** -0.5 * beta),
        "w_out": nrm(ks[9], (DEPTH, D_MODEL, D_MODEL), D_MODEL ** -0.5 * beta),
        "b_out": nrm(ks[10], (DEPTH, D_MODEL), 0.02),
        "ln1_g": 1.0 + nrm(ks[11], (DEPTH, D_MODEL), 0.05),
        "ln1_b": nrm(ks[12], (DEPTH, D_MODEL), 0.02),
        "w_ff1": nrm(ks[13], (DEPTH, D_MODEL, D_FF), D_MODEL ** -0.5 * beta),
        "b_ff1": nrm(ks[14], (DEPTH, D_FF), 0.02),
        "w_ff2": nrm(ks[15], (DEPTH, D_FF, D_MODEL), D_FF ** -0.5 * beta),
        "b_ff2": nrm(ks[16], (DEPTH, D_MODEL), 0.02),
        "ln2_g": 1.0 + nrm(ks[17], (DEPTH, D_MODEL), 0.05),
        "ln2_b": nrm(ks[18], (DEPTH, D_MODEL), 0.02),
    }


def reference(x, ln0_g, ln0_b, w_in, b_in, na_rpb, swa_sink, w_branch_na, w_branch_swa,
              w_out, b_out, ln1_g, ln1_b, w_ff1, b_ff1, w_ff2, b_ff2, ln2_g, ln2_b):
    B, S, _ = x.shape
    pos = jnp.arange(S)
    h = layer_norm(x, ln0_g, ln0_b)
    for l in range(DEPTH):
        proj = h @ w_in[l] + b_in[l]
        qa, ka, va, qb, kb, vb, ga, gb = jnp.split(proj, IN_SPLITS, axis=-1)
        qa = qa.reshape(B, S, NA_HEADS, HEAD_DIM)
        ka = ka.reshape(B, S, NA_HEADS, HEAD_DIM)
        va = va.reshape(B, S, NA_HEADS, HEAD_DIM)
        qb = partial_rotary(qb.reshape(B, S, SWA_Q_HEADS, HEAD_DIM), pos)
        kb = partial_rotary(kb.reshape(B, S, SWA_KV_HEADS, HEAD_DIM), pos)
        vb = vb.reshape(B, S, SWA_KV_HEADS, HEAD_DIM)
        y_na = neighbourhood_attention(qa, ka, va, na_rpb[l]) @ w_branch_na[l]
        y_swa = windowed_gqa_with_sink(qb, kb, vb, swa_sink[l]) @ w_branch_swa[l]
        mixed = jax.nn.sigmoid(ga) * y_na + jax.nn.sigmoid(gb) * y_swa
        attn_out = mixed @ w_out[l] + b_out[l]
        h = layer_norm(DEEPNORM_ALPHA * h + attn_out, ln1_g[l], ln1_b[l])
        u = jnp.square(jax.nn.relu(h @ w_ff1[l] + b_ff1[l]))
        ffn_out = u @ w_ff2[l] + b_ff2[l]
        h = layer_norm(DEEPNORM_ALPHA * h + ffn_out, ln2_g[l], ln2_b[l])
    return h
```

```python
import functools

import numpy as np
import jax
import jax.numpy as jnp
from jax import lax
from jax.experimental import pallas as pl
from jax.experimental.pallas import tpu as pltpu

D_MODEL = 1024
GRID_W = 64
HEAD_DIM = 64
NA_HEADS = 8
NA_KH = 8
NA_KW = 16
NA_WIDTH = NA_HEADS * HEAD_DIM
SWA_Q_HEADS = 8
SWA_KV_HEADS = 2
SWA_WINDOW = 128
SWA_BLOCK = 128
SWA_Q_WIDTH = SWA_Q_HEADS * HEAD_DIM
SWA_KV_WIDTH = SWA_KV_HEADS * HEAD_DIM
ROPE_THETA = 500000.0
ROPE_DIM = HEAD_DIM // 4
D_FF = 4 * D_MODEL
DEPTH = 1
DEEPNORM_ALPHA = (2.0 * DEPTH) ** 0.25
LN_EPS = 1e-5
MASK_VALUE = -1e30
ATTN_SCALE = HEAD_DIM ** -0.5

LANES = 128
HEADS_PER_VREG = LANES // HEAD_DIM
VMEM_LIMIT = 56 * 1024 * 1024

_C_QA = 0
_C_KA = _C_QA + NA_WIDTH
_C_VA = _C_KA + NA_WIDTH
_C_QB = _C_VA + NA_WIDTH
_C_KB = _C_QB + SWA_Q_WIDTH
_KV_DUP = SWA_KV_HEADS * LANES
_C_VB = _C_KB + _KV_DUP
_C_G = _C_VB + _KV_DUP
_C_END = _C_G + 2 * D_MODEL


def _const_spec(shape):
    nd = len(shape)
    return pl.BlockSpec(shape, lambda *_: (0,) * nd, pipeline_mode=pl.Buffered(1))


def _layer_norm(x, g, b):
    mu = jnp.mean(x, axis=-1, keepdims=True)
    xc = x - mu
    var = jnp.mean(xc * xc, axis=-1, keepdims=True)
    return xc * lax.rsqrt(var + LN_EPS) * g + b


def _inproj_kernel(x_ref, g_ref, b_ref, w_ref, bias_ref, cos_ref, sa_ref, sb_ref,
                   qa_ref, ka_ref, va_ref, qb_ref, kb_ref, vb_ref, gate_ref):
    h = _layer_norm(x_ref[...], g_ref[...], b_ref[...])
    hb = h.astype(jnp.bfloat16)

    def proj(c0, c1):
        return (jnp.dot(hb, w_ref[:, c0:c1], preferred_element_type=jnp.float32)
                + bias_ref[:, c0:c1])

    def rotary(y):
        cos, sa, sb = cos_ref[...], sa_ref[...], sb_ref[...]
        half = ROPE_DIM // 2
        outs = []
        for c in range(y.shape[1] // LANES):
            yc = y[:, c * LANES:(c + 1) * LANES]
            outs.append(yc * cos
                        + pltpu.roll(yc, half, 1) * sa
                        + pltpu.roll(yc, LANES - half, 1) * sb)
        return jnp.concatenate(outs, axis=1)

    qa_ref[...] = proj(_C_QA, _C_KA).astype(qa_ref.dtype)
    ka_ref[...] = proj(_C_KA, _C_VA).astype(ka_ref.dtype)
    va_ref[...] = proj(_C_VA, _C_QB).astype(va_ref.dtype)
    qb_ref[...] = rotary(proj(_C_QB, _C_KB)).astype(qb_ref.dtype)
    kb_ref[...] = rotary(proj(_C_KB, _C_VB)).astype(kb_ref.dtype)
    vb_ref[...] = proj(_C_VB, _C_G).astype(vb_ref.dtype)
    gate_ref[...] = proj(_C_G, _C_END)


def _inproj(x2, g, b, w, bias, cos_t, sa_t, sb_t, batch, seq, tm):
    t = batch * seq
    nblk = seq // tm
    tok = lambda bi, i: (bi * nblk + i, 0)
    pos = lambda bi, i: (i, 0)
    bf = jnp.bfloat16
    out_shape = (
        jax.ShapeDtypeStruct((t, NA_WIDTH), bf), jax.ShapeDtypeStruct((t, NA_WIDTH), bf),
        jax.ShapeDtypeStruct((t, NA_WIDTH), bf), jax.ShapeDtypeStruct((t, SWA_Q_WIDTH), bf),
        jax.ShapeDtypeStruct((t, _KV_DUP), bf), jax.ShapeDtypeStruct((t, _KV_DUP), bf),
        jax.ShapeDtypeStruct((t, 2 * D_MODEL), jnp.float32),
    )
    return pl.pallas_call(
        _inproj_kernel,
        grid=(batch, nblk),
        in_specs=[
            pl.BlockSpec((tm, D_MODEL), tok),
            _const_spec((1, D_MODEL)), _const_spec((1, D_MODEL)),
            _const_spec((D_MODEL, _C_END)), _const_spec((1, _C_END)),
            pl.BlockSpec((tm, LANES), pos), pl.BlockSpec((tm, LANES), pos),
            pl.BlockSpec((tm, LANES), pos),
        ],
        out_specs=(
            pl.BlockSpec((tm, NA_WIDTH), tok), pl.BlockSpec((tm, NA_WIDTH), tok),
            pl.BlockSpec((tm, NA_WIDTH), tok), pl.BlockSpec((tm, SWA_Q_WIDTH), tok),
            pl.BlockSpec((tm, _KV_DUP), tok), pl.BlockSpec((tm, _KV_DUP), tok),
            pl.BlockSpec((tm, 2 * D_MODEL), tok),
        ),
        out_shape=out_shape,
        compiler_params=pltpu.CompilerParams(
            dimension_semantics=("parallel", "parallel"), vmem_limit_bytes=VMEM_LIMIT),
        name="ln_inproj_rotary",
    )(x2, g, b, w, bias, cos_t, sa_t, sb_t)


def _stack_heads(q):
    lane = lax.broadcasted_iota(jnp.int32, q.shape, 1)
    zero = jnp.zeros_like(q)
    return jnp.concatenate([jnp.where(lane < HEAD_DIM, q, zero),
                            jnp.where(lane >= HEAD_DIM, q, zero)], axis=0)


def _unstack_heads(o2):
    m = o2.shape[0] // 2
    lane = lax.broadcasted_iota(jnp.int32, (m, LANES), 1)
    return jnp.where(lane < HEAD_DIM, o2[:m], o2[m:])


_NT = (((1,), (1,)), ((), ()))


def _na_kernel(q_ref, k_ref, v_ref, tbl_ref, o_ref, *, rows_per_step, n_rows):
    i = pl.program_id(2)
    win = NA_KH * GRID_W
    for t in range(rows_per_step):
        r = i * rows_per_step + t
        rs = jnp.clip(r - NA_KH // 2, 0, n_rows - NA_KH)
        start = pl.multiple_of(rs * GRID_W, GRID_W)
        lhs = _stack_heads(q_ref[0, t * GRID_W:(t + 1) * GRID_W, :])
        kw = k_ref[0, pl.ds(start, win), :]
        s = lax.dot_general(lhs, kw, _NT, preferred_element_type=jnp.float32)
        s = s + tbl_ref[r - rs, 0]
        m = jnp.max(s, axis=-1, keepdims=True)
        p = jnp.exp(s - m)
        l = jnp.sum(p, axis=-1, keepdims=True)
        vw = v_ref[0, pl.ds(start, win), :]
        o2 = jnp.dot(p.astype(jnp.bfloat16), vw, preferred_element_type=jnp.float32)
        o_ref[0, t * GRID_W:(t + 1) * GRID_W, :] = _unstack_heads(o2 / l).astype(o_ref.dtype)


def _na_attention(qa, ka, va, tbl, rows_per_step):
    batch, seq, _ = qa.shape
    n_rows = seq // GRID_W
    n_pairs = NA_HEADS // HEADS_PER_VREG
    tq = rows_per_step * GRID_W
    kern = functools.partial(_na_kernel, rows_per_step=rows_per_step, n_rows=n_rows)
    return pl.pallas_call(
        kern,
        grid=(batch, n_pairs, n_rows // rows_per_step),
        in_specs=[
            pl.BlockSpec((1, tq, LANES), lambda b, j, i: (b, i, j)),
            pl.BlockSpec((1, seq, LANES), lambda b, j, i: (b, 0, j)),
            pl.BlockSpec((1, seq, LANES), lambda b, j, i: (b, 0, j)),
            pl.BlockSpec((NA_KH, 1, 2 * GRID_W, NA_KH * GRID_W), lambda b, j, i: (0, j, 0, 0)),
        ],
        out_specs=pl.BlockSpec((1, tq, LANES), lambda b, j, i: (b, i, j)),
        out_shape=jax.ShapeDtypeStruct(qa.shape, jnp.bfloat16),
        compiler_params=pltpu.CompilerParams(
            dimension_semantics=("parallel", "parallel", "parallel"),
            vmem_limit_bytes=VMEM_LIMIT),
        name="na_attention",
    )(qa, ka, va, tbl)


def _na_bias_table(rpb):
    cols = np.arange(GRID_W)
    col_start = np.clip(cols - NA_KW // 2, 0, GRID_W - NA_KW)
    dc = cols[None, :] - cols[:, None]
    inside = (cols[None, :] >= col_start[:, None]) & (cols[None, :] < col_start[:, None] + NA_KW)
    dc_idx = np.where(inside, dc + NA_KW - 1, 0)
    a = rpb[:, :, dc_idx]
    a = jnp.where(inside[None, None], a, MASK_VALUE)
    variants = []
    for v in range(NA_KH):
        sl = a[:, NA_KH - 1 - v: 2 * NA_KH - 1 - v]
        variants.append(jnp.transpose(sl, (0, 2, 1, 3)).reshape(NA_HEADS, GRID_W, NA_KH * GRID_W))
    tbl = jnp.stack(variants)
    return tbl.reshape(NA_KH, NA_HEADS // HEADS_PER_VREG, HEADS_PER_VREG * GRID_W, NA_KH * GRID_W)


def _swa_kernel(sink_ref, q_ref, k_ref, v_ref, o_ref, *, blocks_per_step, seq):
    j = pl.program_id(1)
    n0 = pl.program_id(2) * blocks_per_step
    blk = SWA_BLOCK
    win = 3 * blk
    row = lax.broadcasted_iota(jnp.int32, (2 * blk, 1), 0)
    sink_col = jnp.where(row < blk, sink_ref[2 * j], sink_ref[2 * j + 1])
    qi = lax.broadcasted_iota(jnp.int32, (2 * blk, win), 0) % blk
    kc = lax.broadcasted_iota(jnp.int32, (2 * blk, win), 1)
    rel0 = kc - qi
    for u in range(blocks_per_step):
        n = n0 + u
        start = jnp.clip((n - 1) * blk, 0, seq - win)
        start = pl.multiple_of(start, blk)
        lhs = _stack_heads(q_ref[0, u * blk:(u + 1) * blk, :])
        kw = k_ref[0, pl.ds(start, win), :]
        s = lax.dot_general(lhs, kw, _NT, preferred_element_type=jnp.float32)
        rel = rel0 + (start - n * blk)
        s = jnp.where(jnp.abs(rel) <= SWA_WINDOW, s, MASK_VALUE)
        m = jnp.maximum(jnp.max(s, axis=-1, keepdims=True), sink_col)
        e = jnp.exp(s - m)
        denom = jnp.sum(e, axis=-1, keepdims=True) + jnp.exp(sink_col - m)
        vw = v_ref[0, pl.ds(start, win), :]
        o2 = jnp.dot(e.astype(jnp.bfloat16), vw, preferred_element_type=jnp.float32)
        o_ref[0, u * blk:(u + 1) * blk, :] = _unstack_heads(o2 / denom).astype(o_ref.dtype)


def _swa_attention(qb, kb, vb, sink, blocks_per_step):
    batch, seq, _ = qb.shape
    n_pairs = SWA_Q_HEADS // HEADS_PER_VREG
    pairs_per_group = n_pairs // SWA_KV_HEADS
    tq = blocks_per_step * SWA_BLOCK
    kern = functools.partial(_swa_kernel, blocks_per_step=blocks_per_step, seq=seq)
    return pl.pallas_call(
        kern,
        grid=(batch, n_pairs, seq // tq),
        in_specs=[
            pl.BlockSpec(memory_space=pltpu.SMEM),
            pl.BlockSpec((1, tq, LANES), lambda b, j, n: (b, n, j)),
            pl.BlockSpec((1, seq, LANES), lambda b, j, n: (b, 0, j // pairs_per_group)),
            pl.BlockSpec((1, seq, LANES), lambda b, j, n: (b, 0, j // pairs_per_group)),
        ],
        out_specs=pl.BlockSpec((1, tq, LANES), lambda b, j, n: (b, n, j)),
        out_shape=jax.ShapeDtypeStruct(qb.shape, jnp.bfloat16),
        compiler_params=pltpu.CompilerParams(
            dimension_semantics=("parallel", "parallel", "parallel"),
            vmem_limit_bytes=VMEM_LIMIT),
        name="swa_attention",
    )(sink, qb, kb, vb)


def _mix_kernel(x_ref, g0_ref, b0_ref, na_ref, swa_ref, gate_ref, wna_ref, wswa_ref,
                wout_ref, bout_ref, g1_ref, b1_ref, o_ref):
    h = _layer_norm(x_ref[...], g0_ref[...], b0_ref[...])
    y_na = jnp.dot(na_ref[...], wna_ref[...], preferred_element_type=jnp.float32)
    y_swa = jnp.dot(swa_ref[...], wswa_ref[...], preferred_element_type=jnp.float32)
    mixed = (jax.nn.sigmoid(gate_ref[:, :D_MODEL]) * y_na
             + jax.nn.sigmoid(gate_ref[:, D_MODEL:]) * y_swa)
    attn = jnp.dot(mixed.astype(jnp.bfloat16), wout_ref[...],
                   preferred_element_type=jnp.float32) + bout_ref[...]
    o_ref[...] = _layer_norm(DEEPNORM_ALPHA * h + attn, g1_ref[...], b1_ref[...])


def _mix(x2, g0, b0, na, swa, gates, wna, wswa, wout, bout, g1, b1, tm):
    t = x2.shape[0]
    tok = lambda i: (i, 0)
    vec = _const_spec((1, D_MODEL))
    return pl.pallas_call(
        _mix_kernel,
        grid=(t // tm,),
        in_specs=[
            pl.BlockSpec((tm, D_MODEL), tok), vec, vec,
            pl.BlockSpec((tm, NA_WIDTH), tok), pl.BlockSpec((tm, SWA_Q_WIDTH), tok),
            pl.BlockSpec((tm, 2 * D_MODEL), tok),
            _const_spec((NA_WIDTH, D_MODEL)), _const_spec((SWA_Q_WIDTH, D_MODEL)),
            _const_spec((D_MODEL, D_MODEL)), vec, vec, vec,
        ],
        out_specs=pl.BlockSpec((tm, D_MODEL), tok),
        out_shape=jax.ShapeDtypeStruct((t, D_MODEL), jnp.float32),
        compiler_params=pltpu.CompilerParams(
            dimension_semantics=("parallel",), vmem_limit_bytes=VMEM_LIMIT),
        name="mix_outproj_ln",
    )(x2, g0, b0, na, swa, gates, wna, wswa, wout, bout, g1, b1)


def _ffn_kernel(h_ref, w1_ref, b1_ref, w2_ref, b2_ref, g_ref, b_ref, o_ref, *, ff_chunk):
    h = h_ref[...]
    hb = h.astype(jnp.bfloat16)
    acc = jnp.zeros(h.shape, jnp.float32)
    for c in range(D_FF // ff_chunk):
        sl = slice(c * ff_chunk, (c + 1) * ff_chunk)
        u = jnp.dot(hb, w1_ref[:, sl], preferred_element_type=jnp.float32) + b1_ref[:, sl]
        u = jnp.square(jnp.maximum(u, 0.0))
        acc = acc + jnp.dot(u.astype(jnp.bfloat16), w2_ref[sl, :],
                            preferred_element_type=jnp.float32)
    o_ref[...] = _layer_norm(DEEPNORM_ALPHA * h + acc + b2_ref[...], g_ref[...], b_ref[...])


def _ffn(h1, w1, b1, w2, b2, g, b, tm, ff_chunk):
    t = h1.shape[0]
    tok = lambda i: (i, 0)
    vec = _const_spec((1, D_MODEL))
    return pl.pallas_call(
        functools.partial(_ffn_kernel, ff_chunk=ff_chunk),
        grid=(t // tm,),
        in_specs=[
            pl.BlockSpec((tm, D_MODEL), tok),
            _const_spec((D_MODEL, D_FF)), _const_spec((1, D_FF)),
            _const_spec((D_FF, D_MODEL)), vec, vec, vec,
        ],
        out_specs=pl.BlockSpec((tm, D_MODEL), tok),
        out_shape=jax.ShapeDtypeStruct((t, D_MODEL), jnp.float32),
        compiler_params=pltpu.CompilerParams(
            dimension_semantics=("parallel",), vmem_limit_bytes=VMEM_LIMIT),
        name="ffn_ln",
    )(h1, w1, b1, w2, b2, g, b)


def _dup_kv(cols):
    lead = cols.shape[:-1]
    c = cols.reshape(lead + (SWA_KV_HEADS, 1, HEAD_DIM))
    c = jnp.broadcast_to(c, lead + (SWA_KV_HEADS, HEADS_PER_VREG, HEAD_DIM))
    return c.reshape(lead + (_KV_DUP,))


def _repack_in(w):
    o = 0
    parts = {}
    for name, width in (("qa", NA_WIDTH), ("ka", NA_WIDTH), ("va", NA_WIDTH),
                        ("qb", SWA_Q_WIDTH), ("kb", SWA_KV_WIDTH), ("vb", SWA_KV_WIDTH),
                        ("g", 2 * D_MODEL)):
        parts[name] = w[..., o:o + width]
        o += width
    return jnp.concatenate([
        parts["qa"] * ATTN_SCALE, parts["ka"], parts["va"], parts["qb"] * ATTN_SCALE,
        _dup_kv(parts["kb"]), _dup_kv(parts["vb"]), parts["g"]], axis=-1)


def _rotary_tables(seq):
    half = ROPE_DIM // 2
    inv_freq = jnp.power(ROPE_THETA, -jnp.arange(0, ROPE_DIM, 2, dtype=jnp.float32) / ROPE_DIM)
    ang = jnp.arange(seq).astype(jnp.float32)[:, None] * inv_freq[None, :]
    cos, sin = jnp.cos(ang), jnp.sin(ang)
    ones = jnp.ones((seq, HEAD_DIM - ROPE_DIM), jnp.float32)
    zeros = jnp.zeros((seq, HEAD_DIM - ROPE_DIM), jnp.float32)
    zh = jnp.zeros((seq, half), jnp.float32)
    cos_h = jnp.concatenate([cos, cos, ones], axis=1)
    sa_h = jnp.concatenate([zh, sin, zeros], axis=1)
    sb_h = jnp.concatenate([-sin, zh, zeros], axis=1)
    tile = lambda a: jnp.concatenate([a] * HEADS_PER_VREG, axis=1)
    return tile(cos_h), tile(sa_h), tile(sb_h)


def kernel(x, ln0_g, ln0_b, w_in, b_in, na_rpb, swa_sink, w_branch_na, w_branch_swa,
           w_out, b_out, ln1_g, ln1_b, w_ff1, b_ff1, w_ff2, b_ff2, ln2_g, ln2_b):
    batch, seq, d = x.shape
    assert d == D_MODEL and seq % (NA_KH * GRID_W) == 0 and seq >= 3 * SWA_BLOCK
    assert w_in.shape[0] == DEPTH == 1
    bf = jnp.bfloat16
    row = lambda v: v.reshape(1, -1)
    x2 = x.reshape(batch * seq, d)

    w_in_p = _repack_in(w_in[0]).astype(bf)
    b_in_p = row(_repack_in(b_in[0]))
    cos_t, sa_t, sb_t = _rotary_tables(seq)
    qa, ka, va, qb, kb, vb, gates = _inproj(
        x2, row(ln0_g), row(ln0_b), w_in_p, b_in_p, cos_t, sa_t, sb_t, batch, seq, tm=512)

    three = lambda a: a.reshape(batch, seq, a.shape[-1])
    o_na = _na_attention(three(qa), three(ka), three(va), _na_bias_table(na_rpb[0]),
                         rows_per_step=8)
    o_swa = _swa_attention(three(qb), three(kb), three(vb), swa_sink[0], blocks_per_step=4)

    h1 = _mix(x2, row(ln0_g), row(ln0_b), o_na.reshape(batch * seq, -1),
              o_swa.reshape(batch * seq, -1), gates,
              w_branch_na[0].astype(bf), w_branch_swa[0].astype(bf), w_out[0].astype(bf),
              row(b_out[0]), row(ln1_g[0]), row(ln1_b[0]), tm=512)
    out = _ffn(h1, w_ff1[0].astype(bf), row(b_ff1[0]), w_ff2[0].astype(bf), row(b_ff2[0]),
               row(ln2_g[0]), row(ln2_b[0]), tm=512, ff_chunk=1024)
    return out.reshape(batch, seq, d)
```

```python
import functools

import numpy as np
import jax
import jax.numpy as jnp
from jax import lax
from jax.experimental import pallas as pl
from jax.experimental.pallas import tpu as pltpu

D_MODEL = 1024
GRID_W = 64
HEAD_DIM = 64
NA_HEADS = 8
NA_KH = 8
NA_KW = 16
NA_WIDTH = NA_HEADS * HEAD_DIM
SWA_Q_HEADS = 8
SWA_KV_HEADS = 2
SWA_WINDOW = 128
SWA_BLOCK = 128
SWA_Q_WIDTH = SWA_Q_HEADS * HEAD_DIM
SWA_KV_WIDTH = SWA_KV_HEADS * HEAD_DIM
ROPE_THETA = 500000.0
ROPE_DIM = HEAD_DIM // 4
D_FF = 4 * D_MODEL
DEPTH = 1
DEEPNORM_ALPHA = (2.0 * DEPTH) ** 0.25
LN_EPS = 1e-5
MASK_VALUE = -1e30
ATTN_SCALE = HEAD_DIM ** -0.5

LANES = 128
HEADS_PER_VREG = LANES // HEAD_DIM
VMEM_LIMIT = 56 * 1024 * 1024

_C_QA = 0
_C_KA = _C_QA + NA_WIDTH
_C_VA = _C_KA + NA_WIDTH
_C_QB = _C_VA + NA_WIDTH
_C_KB = _C_QB + SWA_Q_WIDTH
_KV_DUP = SWA_KV_HEADS * LANES
_C_VB = _C_KB + _KV_DUP
_C_G = _C_VB + _KV_DUP
_C_END = _C_G + 2 * D_MODEL


def _const_spec(shape):
    nd = len(shape)
    return pl.BlockSpec(shape, lambda *_: (0,) * nd, pipeline_mode=pl.Buffered(1))


def _layer_norm(x, g, b):
    mu = jnp.mean(x, axis=-1, keepdims=True)
    xc = x - mu
    var = jnp.mean(xc * xc, axis=-1, keepdims=True)
    return xc * lax.rsqrt(var + LN_EPS) * g + b


def _inproj_kernel(x_ref, g_ref, b_ref, w_ref, bias_ref, cos_ref, sa_ref, sb_ref,
                   qa_ref, ka_ref, va_ref, qb_ref, kb_ref, vb_ref, gate_ref):
    h = _layer_norm(x_ref[...], g_ref[...], b_ref[...])
    hb = h.astype(jnp.bfloat16)

    def proj(c0, c1):
        return (jnp.dot(hb, w_ref[:, c0:c1], preferred_element_type=jnp.float32)
                + bias_ref[:, c0:c1])

    def rotary(y):
        cos, sa, sb = cos_ref[...], sa_ref[...], sb_ref[...]
        half = ROPE_DIM // 2
        outs = []
        for c in range(y.shape[1] // LANES):
            yc = y[:, c * LANES:(c + 1) * LANES]
            outs.append(yc * cos
                        + pltpu.roll(yc, half, 1) * sa
                        + pltpu.roll(yc, LANES - half, 1) * sb)
        return jnp.concatenate(outs, axis=1)

    qa_ref[...] = proj(_C_QA, _C_KA).astype(qa_ref.dtype)
    ka_ref[...] = proj(_C_KA, _C_VA).astype(ka_ref.dtype)
    va_ref[...] = proj(_C_VA, _C_QB).astype(va_ref.dtype)
    qb_ref[...] = rotary(proj(_C_QB, _C_KB)).astype(qb_ref.dtype)
    kb_ref[...] = rotary(proj(_C_KB, _C_VB)).astype(kb_ref.dtype)
    vb_ref[...] = proj(_C_VB, _C_G).astype(vb_ref.dtype)
    gate_ref[...] = proj(_C_G, _C_END)


def _inproj(x2, g, b, w, bias, cos_t, sa_t, sb_t, batch, seq, tm):
    t = batch * seq
    nblk = seq // tm
    tok = lambda bi, i: (bi * nblk + i, 0)
    pos = lambda bi, i: (i, 0)
    bf = jnp.bfloat16
    out_shape = (
        jax.ShapeDtypeStruct((t, NA_WIDTH), bf), jax.ShapeDtypeStruct((t, NA_WIDTH), bf),
        jax.ShapeDtypeStruct((t, NA_WIDTH), bf), jax.ShapeDtypeStruct((t, SWA_Q_WIDTH), bf),
        jax.ShapeDtypeStruct((t, _KV_DUP), bf), jax.ShapeDtypeStruct((t, _KV_DUP), bf),
        jax.ShapeDtypeStruct((t, 2 * D_MODEL), jnp.float32),
    )
    return pl.pallas_call(
        _inproj_kernel,
        grid=(batch, nblk),
        in_specs=[
            pl.BlockSpec((tm, D_MODEL), tok),
            _const_spec((1, D_MODEL)), _const_spec((1, D_MODEL)),
            _const_spec((D_MODEL, _C_END)), _const_spec((1, _C_END)),
            pl.BlockSpec((tm, LANES), pos), pl.BlockSpec((tm, LANES), pos),
            pl.BlockSpec((tm, LANES), pos),
        ],
        out_specs=(
            pl.BlockSpec((tm, NA_WIDTH), tok), pl.BlockSpec((tm, NA_WIDTH), tok),
            pl.BlockSpec((tm, NA_WIDTH), tok), pl.BlockSpec((tm, SWA_Q_WIDTH), tok),
            pl.BlockSpec((tm, _KV_DUP), tok), pl.BlockSpec((tm, _KV_DUP), tok),
            pl.BlockSpec((tm, 2 * D_MODEL), tok),
        ),
        out_shape=out_shape,
        compiler_params=pltpu.CompilerParams(
            dimension_semantics=("parallel", "parallel"), vmem_limit_bytes=VMEM_LIMIT),
        name="ln_inproj_rotary",
    )(x2, g, b, w, bias, cos_t, sa_t, sb_t)


def _stack_heads(q):
    lane = lax.broadcasted_iota(jnp.int32, q.shape, 1)
    zero = jnp.zeros_like(q)
    return jnp.concatenate([jnp.where(lane < HEAD_DIM, q, zero),
                            jnp.where(lane >= HEAD_DIM, q, zero)], axis=0)


def _unstack_heads(o2):
    m = o2.shape[0] // 2
    lane = lax.broadcasted_iota(jnp.int32, (m, LANES), 1)
    return jnp.where(lane < HEAD_DIM, o2[:m], o2[m:])


_NT = (((1,), (1,)), ((), ()))


def _na_kernel(q_ref, k_ref, v_ref, tbl_ref, o_ref, s_scr, p_scr, l_scr, *,
               rows_per_step, n_rows):
    i = pl.program_id(2)
    win = NA_KH * GRID_W

    def window(t):
        r = i * rows_per_step + t
        rs = jnp.clip(r - NA_KH // 2, 0, n_rows - NA_KH)
        return r - rs, pl.multiple_of(rs * GRID_W, GRID_W)

    for t in range(rows_per_step):
        variant, start = window(t)
        lhs = _stack_heads(q_ref[0, t * GRID_W:(t + 1) * GRID_W, :])
        kw = k_ref[0, pl.ds(start, win), :]
        s = lax.dot_general(lhs, kw, _NT, preferred_element_type=jnp.float32)
        s_scr[t] = s + tbl_ref[variant, 0]
    for t in range(rows_per_step):
        s = s_scr[t]
        m = jnp.max(s, axis=-1, keepdims=True)
        p = jnp.exp(s - m)
        l_scr[t] = jnp.broadcast_to(jnp.sum(p, axis=-1, keepdims=True), l_scr.shape[1:])
        p_scr[t] = p.astype(p_scr.dtype)
    for t in range(rows_per_step):
        _, start = window(t)
        vw = v_ref[0, pl.ds(start, win), :]
        o2 = jnp.dot(p_scr[t], vw, preferred_element_type=jnp.float32)
        o_ref[0, t * GRID_W:(t + 1) * GRID_W, :] = (
            _unstack_heads(o2 / l_scr[t]).astype(o_ref.dtype))


def _na_attention(qa, ka, va, tbl, rows_per_step):
    batch, seq, _ = qa.shape
    n_rows = seq // GRID_W
    n_pairs = NA_HEADS // HEADS_PER_VREG
    tq = rows_per_step * GRID_W
    m2, win = HEADS_PER_VREG * GRID_W, NA_KH * GRID_W
    kern = functools.partial(_na_kernel, rows_per_step=rows_per_step, n_rows=n_rows)
    return pl.pallas_call(
        kern,
        grid=(batch, n_pairs, n_rows // rows_per_step),
        in_specs=[
            pl.BlockSpec((1, tq, LANES), lambda b, j, i: (b, i, j)),
            pl.BlockSpec((1, seq, LANES), lambda b, j, i: (b, 0, j)),
            pl.BlockSpec((1, seq, LANES), lambda b, j, i: (b, 0, j)),
            pl.BlockSpec((NA_KH, 1, m2, win), lambda b, j, i: (0, j, 0, 0)),
        ],
        out_specs=pl.BlockSpec((1, tq, LANES), lambda b, j, i: (b, i, j)),
        out_shape=jax.ShapeDtypeStruct(qa.shape, jnp.bfloat16),
        scratch_shapes=[
            pltpu.VMEM((rows_per_step, m2, win), jnp.float32),
            pltpu.VMEM((rows_per_step, m2, win), jnp.bfloat16),
            pltpu.VMEM((rows_per_step, m2, LANES), jnp.float32),
        ],
        compiler_params=pltpu.CompilerParams(
            dimension_semantics=("parallel", "parallel", "parallel"),
            vmem_limit_bytes=VMEM_LIMIT),
        name="na_attention",
    )(qa, ka, va, tbl)


def _na_bias_table(rpb):
    cols = np.arange(GRID_W)
    col_start = np.clip(cols - NA_KW // 2, 0, GRID_W - NA_KW)
    dc = cols[None, :] - cols[:, None]
    inside = (cols[None, :] >= col_start[:, None]) & (cols[None, :] < col_start[:, None] + NA_KW)
    dc_idx = np.where(inside, dc + NA_KW - 1, 0)
    a = rpb[:, :, dc_idx]
    a = jnp.where(inside[None, None], a, MASK_VALUE)
    variants = []
    for v in range(NA_KH):
        sl = a[:, NA_KH - 1 - v: 2 * NA_KH - 1 - v]
        variants.append(jnp.transpose(sl, (0, 2, 1, 3)).reshape(NA_HEADS, GRID_W, NA_KH * GRID_W))
    tbl = jnp.stack(variants)
    return tbl.reshape(NA_KH, NA_HEADS // HEADS_PER_VREG, HEADS_PER_VREG * GRID_W, NA_KH * GRID_W)


def _swa_kernel(sink_ref, q_ref, k_ref, v_ref, o_ref, *, blocks_per_step, seq, lookahead):
    j = pl.program_id(1)
    n0 = pl.program_id(2) * blocks_per_step
    blk = SWA_BLOCK
    win = 3 * blk
    row = lax.broadcasted_iota(jnp.int32, (2 * blk, 1), 0)
    sink_col = jnp.where(row < blk, sink_ref[2 * j], sink_ref[2 * j + 1])
    qi = lax.broadcasted_iota(jnp.int32, (2 * blk, win), 0) % blk
    kc = lax.broadcasted_iota(jnp.int32, (2 * blk, win), 1)
    rel0 = kc - qi
    def window(u):
        n = n0 + u
        start = jnp.clip((n - 1) * blk, 0, seq - win)
        return n, pl.multiple_of(start, blk)

    def scores(u):
        _, start = window(u)
        lhs = _stack_heads(q_ref[0, u * blk:(u + 1) * blk, :])
        kw = k_ref[0, pl.ds(start, win), :]
        return lax.dot_general(lhs, kw, _NT, preferred_element_type=jnp.float32)

    pending = [scores(u) for u in range(min(lookahead, blocks_per_step))]
    for u in range(blocks_per_step):
        s = pending.pop(0)
        if u + lookahead < blocks_per_step:
            pending.append(scores(u + lookahead))
        n, start = window(u)
        rel = rel0 + (start - n * blk)
        s = jnp.where(jnp.abs(rel) <= SWA_WINDOW, s, MASK_VALUE)
        m = jnp.maximum(jnp.max(s, axis=-1, keepdims=True), sink_col)
        e = jnp.exp(s - m)
        denom = jnp.sum(e, axis=-1, keepdims=True) + jnp.exp(sink_col - m)
        vw = v_ref[0, pl.ds(start, win), :]
        o2 = jnp.dot(e.astype(jnp.bfloat16), vw, preferred_element_type=jnp.float32)
        o_ref[0, u * blk:(u + 1) * blk, :] = _unstack_heads(o2 / denom).astype(o_ref.dtype)


def _swa_attention(qb, kb, vb, sink, blocks_per_step, lookahead):
    batch, seq, _ = qb.shape
    n_pairs = SWA_Q_HEADS // HEADS_PER_VREG
    pairs_per_group = n_pairs // SWA_KV_HEADS
    tq = blocks_per_step * SWA_BLOCK
    kern = functools.partial(_swa_kernel, blocks_per_step=blocks_per_step, seq=seq,
                             lookahead=lookahead)
    return pl.pallas_call(
        kern,
        grid=(batch, n_pairs, seq // tq),
        in_specs=[
            pl.BlockSpec(memory_space=pltpu.SMEM),
            pl.BlockSpec((1, tq, LANES), lambda b, j, n: (b, n, j)),
            pl.BlockSpec((1, seq, LANES), lambda b, j, n: (b, 0, j // pairs_per_group)),
            pl.BlockSpec((1, seq, LANES), lambda b, j, n: (b, 0, j // pairs_per_group)),
        ],
        out_specs=pl.BlockSpec((1, tq, LANES), lambda b, j, n: (b, n, j)),
        out_shape=jax.ShapeDtypeStruct(qb.shape, jnp.bfloat16),
        compiler_params=pltpu.CompilerParams(
            dimension_semantics=("parallel", "parallel", "parallel"),
            vmem_limit_bytes=VMEM_LIMIT),
        name="swa_attention",
    )(sink, qb, kb, vb)


def _mix_kernel(x_ref, g0_ref, b0_ref, na_ref, swa_ref, gate_ref, wna_ref, wswa_ref,
                wout_ref, bout_ref, g1_ref, b1_ref, o_ref):
    h = _layer_norm(x_ref[...], g0_ref[...], b0_ref[...])
    y_na = jnp.dot(na_ref[...], wna_ref[...], preferred_element_type=jnp.float32)
    y_swa = jnp.dot(swa_ref[...], wswa_ref[...], preferred_element_type=jnp.float32)
    mixed = (jax.nn.sigmoid(gate_ref[:, :D_MODEL]) * y_na
             + jax.nn.sigmoid(gate_ref[:, D_MODEL:]) * y_swa)
    attn = jnp.dot(mixed.astype(jnp.bfloat16), wout_ref[...],
                   preferred_element_type=jnp.float32) + bout_ref[...]
    o_ref[...] = _layer_norm(DEEPNORM_ALPHA * h + attn, g1_ref[...], b1_ref[...])


def _mix(x2, g0, b0, na, swa, gates, wna, wswa, wout, bout, g1, b1, tm):
    t = x2.shape[0]
    tok = lambda i: (i, 0)
    vec = _const_spec((1, D_MODEL))
    return pl.pallas_call(
        _mix_kernel,
        grid=(t // tm,),
        in_specs=[
            pl.BlockSpec((tm, D_MODEL), tok), vec, vec,
            pl.BlockSpec((tm, NA_WIDTH), tok), pl.BlockSpec((tm, SWA_Q_WIDTH), tok),
            pl.BlockSpec((tm, 2 * D_MODEL), tok),
            _const_spec((NA_WIDTH, D_MODEL)), _const_spec((SWA_Q_WIDTH, D_MODEL)),
            _const_spec((D_MODEL, D_MODEL)), vec, vec, vec,
        ],
        out_specs=pl.BlockSpec((tm, D_MODEL), tok),
        out_shape=jax.ShapeDtypeStruct((t, D_MODEL), jnp.float32),
        compiler_params=pltpu.CompilerParams(
            dimension_semantics=("parallel",), vmem_limit_bytes=VMEM_LIMIT),
        name="mix_outproj_ln",
    )(x2, g0, b0, na, swa, gates, wna, wswa, wout, bout, g1, b1)


def _ffn_kernel(h_ref, w1_ref, b1_ref, w2_ref, b2_ref, g_ref, b_ref, o_ref, *, ff_chunk):
    h = h_ref[...]
    hb = h.astype(jnp.bfloat16)
    acc = jnp.zeros(h.shape, jnp.float32)
    for c in range(D_FF // ff_chunk):
        sl = slice(c * ff_chunk, (c + 1) * ff_chunk)
        u = jnp.dot(hb, w1_ref[:, sl], preferred_element_type=jnp.float32) + b1_ref[:, sl]
        u = jnp.square(jnp.maximum(u, 0.0))
        acc = acc + jnp.dot(u.astype(jnp.bfloat16), w2_ref[sl, :],
                            preferred_element_type=jnp.float32)
    o_ref[...] = _layer_norm(DEEPNORM_ALPHA * h + acc + b2_ref[...], g_ref[...], b_ref[...])


def _ffn(h1, w1, b1, w2, b2, g, b, tm, ff_chunk):
    t = h1.shape[0]
    tok = lambda i: (i, 0)
    vec = _const_spec((1, D_MODEL))
    return pl.pallas_call(
        functools.partial(_ffn_kernel, ff_chunk=ff_chunk),
        grid=(t // tm,),
        in_specs=[
            pl.BlockSpec((tm, D_MODEL), tok),
            _const_spec((D_MODEL, D_FF)), _const_spec((1, D_FF)),
            _const_spec((D_FF, D_MODEL)), vec, vec, vec,
        ],
        out_specs=pl.BlockSpec((tm, D_MODEL), tok),
        out_shape=jax.ShapeDtypeStruct((t, D_MODEL), jnp.float32),
        compiler_params=pltpu.CompilerParams(
            dimension_semantics=("parallel",), vmem_limit_bytes=VMEM_LIMIT),
        name="ffn_ln",
    )(h1, w1, b1, w2, b2, g, b)


def _dup_kv(cols):
    lead = cols.shape[:-1]
    c = cols.reshape(lead + (SWA_KV_HEADS, 1, HEAD_DIM))
    c = jnp.broadcast_to(c, lead + (SWA_KV_HEADS, HEADS_PER_VREG, HEAD_DIM))
    return c.reshape(lead + (_KV_DUP,))


def _repack_in(w):
    o = 0
    parts = {}
    for name, width in (("qa", NA_WIDTH), ("ka", NA_WIDTH), ("va", NA_WIDTH),
                        ("qb", SWA_Q_WIDTH), ("kb", SWA_KV_WIDTH), ("vb", SWA_KV_WIDTH),
                        ("g", 2 * D_MODEL)):
        parts[name] = w[..., o:o + width]
        o += width
    return jnp.concatenate([
        parts["qa"] * ATTN_SCALE, parts["ka"], parts["va"], parts["qb"] * ATTN_SCALE,
        _dup_kv(parts["kb"]), _dup_kv(parts["vb"]), parts["g"]], axis=-1)


def _rotary_tables(seq):
    half = ROPE_DIM // 2
    inv_freq = jnp.power(ROPE_THETA, -jnp.arange(0, ROPE_DIM, 2, dtype=jnp.float32) / ROPE_DIM)
    ang = jnp.arange(seq).astype(jnp.float32)[:, None] * inv_freq[None, :]
    cos, sin = jnp.cos(ang), jnp.sin(ang)
    ones = jnp.ones((seq, HEAD_DIM - ROPE_DIM), jnp.float32)
    zeros = jnp.zeros((seq, HEAD_DIM - ROPE_DIM), jnp.float32)
    zh = jnp.zeros((seq, half), jnp.float32)
    cos_h = jnp.concatenate([cos, cos, ones], axis=1)
    sa_h = jnp.concatenate([zh, sin, zeros], axis=1)
    sb_h = jnp.concatenate([-sin, zh, zeros], axis=1)
    tile = lambda a: jnp.concatenate([a] * HEADS_PER_VREG, axis=1)
    return tile(cos_h), tile(sa_h), tile(sb_h)


def kernel(x, ln0_g, ln0_b, w_in, b_in, na_rpb, swa_sink, w_branch_na, w_branch_swa,
           w_out, b_out, ln1_g, ln1_b, w_ff1, b_ff1, w_ff2, b_ff2, ln2_g, ln2_b):
    batch, seq, d = x.shape
    assert d == D_MODEL and seq % (NA_KH * GRID_W) == 0 and seq >= 3 * SWA_BLOCK
    assert w_in.shape[0] == DEPTH == 1
    bf = jnp.bfloat16
    row = lambda v: v.reshape(1, -1)
    x2 = x.reshape(batch * seq, d)

    w_in_p = _repack_in(w_in[0]).astype(bf)
    b_in_p = row(_repack_in(b_in[0]))
    cos_t, sa_t, sb_t = _rotary_tables(seq)
    qa, ka, va, qb, kb, vb, gates = _inproj(
        x2, row(ln0_g), row(ln0_b), w_in_p, b_in_p, cos_t, sa_t, sb_t, batch, seq, tm=512)

    three = lambda a: a.reshape(batch, seq, a.shape[-1])
    o_na = _na_attention(three(qa), three(ka), three(va), _na_bias_table(na_rpb[0]),
                         rows_per_step=8)
    o_swa = _swa_attention(three(qb), three(kb), three(vb), swa_sink[0], blocks_per_step=4,
                           lookahead=1)

    h1 = _mix(x2, row(ln0_g), row(ln0_b), o_na.reshape(batch * seq, -1),
              o_swa.reshape(batch * seq, -1), gates,
              w_branch_na[0].astype(bf), w_branch_swa[0].astype(bf), w_out[0].astype(bf),
              row(b_out[0]), row(ln1_g[0]), row(ln1_b[0]), tm=512)
    out = _ffn(h1, w_ff1[0].astype(bf), row(b_ff1[0]), w_ff2[0].astype(bf), row(b_ff2[0]),
               row(ln2_g[0]), row(ln2_b[0]), tm=512, ff_chunk=1024)
    return out.reshape(batch, seq, d)
```

```python
import functools

import numpy as np
import jax
import jax.numpy as jnp
from jax import lax
from jax.experimental import pallas as pl
from jax.experimental.pallas import tpu as pltpu

D_MODEL = 1024
GRID_W = 64
HEAD_DIM = 64
NA_HEADS = 8
NA_KH = 8
NA_KW = 16
NA_WIDTH = NA_HEADS * HEAD_DIM
SWA_Q_HEADS = 8
SWA_KV_HEADS = 2
SWA_WINDOW = 128
SWA_BLOCK = 128
SWA_Q_WIDTH = SWA_Q_HEADS * HEAD_DIM
SWA_KV_WIDTH = SWA_KV_HEADS * HEAD_DIM
ROPE_THETA = 500000.0
ROPE_DIM = HEAD_DIM // 4
D_FF = 4 * D_MODEL
DEPTH = 1
DEEPNORM_ALPHA = (2.0 * DEPTH) ** 0.25
LN_EPS = 1e-5
MASK_VALUE = -1e30
ATTN_SCALE = HEAD_DIM ** -0.5

LANES = 128
HEADS_PER_VREG = LANES // HEAD_DIM
VMEM_LIMIT = 56 * 1024 * 1024

_C_QA = 0
_C_KA = _C_QA + NA_WIDTH
_C_VA = _C_KA + NA_WIDTH
_C_QB = _C_VA + NA_WIDTH
_C_KB = _C_QB + SWA_Q_WIDTH
_C_VB = _C_KB + SWA_KV_WIDTH
_C_G = _C_VB + SWA_KV_WIDTH
_C_END = _C_G + 2 * D_MODEL
_KV_DUP = SWA_KV_HEADS * LANES


def _const_spec(shape):
    nd = len(shape)
    return pl.BlockSpec(shape, lambda *_: (0,) * nd, pipeline_mode=pl.Buffered(1))


def _layer_norm(x, g, b):
    mu = jnp.mean(x, axis=-1, keepdims=True)
    xc = x - mu
    var = jnp.mean(xc * xc, axis=-1, keepdims=True)
    return xc * lax.rsqrt(var + LN_EPS) * g + b


def _dup_heads(y):
    lane = lax.broadcasted_iota(jnp.int32, y.shape, 1)
    swapped = pltpu.roll(y, HEAD_DIM, 1)
    return jnp.concatenate([jnp.where(lane < HEAD_DIM, y, swapped),
                            jnp.where(lane >= HEAD_DIM, y, swapped)], axis=1)


def _inproj_kernel(x_ref, g_ref, b_ref, w_ref, bias_ref, cos_ref, sa_ref, sb_ref,
                   qa_ref, ka_ref, va_ref, qb_ref, kb_ref, vb_ref):
    h = _layer_norm(x_ref[...], g_ref[...], b_ref[...])
    hb = h.astype(jnp.bfloat16)

    def proj(c0, c1):
        return (jnp.dot(hb, w_ref[:, c0:c1], preferred_element_type=jnp.float32)
                + bias_ref[:, c0:c1])

    def rotary(y):
        cos, sa, sb = cos_ref[...], sa_ref[...], sb_ref[...]
        half = ROPE_DIM // 2
        outs = []
        for c in range(y.shape[1] // LANES):
            yc = y[:, c * LANES:(c + 1) * LANES]
            outs.append(yc * cos
                        + pltpu.roll(yc, half, 1) * sa
                        + pltpu.roll(yc, LANES - half, 1) * sb)
        return outs[0] if len(outs) == 1 else jnp.concatenate(outs, axis=1)

    qa_ref[...] = (proj(_C_QA, _C_KA) * ATTN_SCALE).astype(qa_ref.dtype)
    ka_ref[...] = proj(_C_KA, _C_VA).astype(ka_ref.dtype)
    va_ref[...] = proj(_C_VA, _C_QB).astype(va_ref.dtype)
    qb_ref[...] = (rotary(proj(_C_QB, _C_KB)) * ATTN_SCALE).astype(qb_ref.dtype)
    kv = proj(_C_KB, _C_G)
    kb_ref[...] = _dup_heads(rotary(kv[:, :SWA_KV_WIDTH])).astype(kb_ref.dtype)
    vb_ref[...] = _dup_heads(kv[:, SWA_KV_WIDTH:]).astype(vb_ref.dtype)


def _inproj(x2, g, b, w, bias, cos_t, sa_t, sb_t, batch, seq, tm):
    t = batch * seq
    nblk = seq // tm
    tok = lambda bi, i: (bi * nblk + i, 0)
    pos = lambda bi, i: (i, 0)
    bf = jnp.bfloat16
    out_shape = (
        jax.ShapeDtypeStruct((t, NA_WIDTH), bf), jax.ShapeDtypeStruct((t, NA_WIDTH), bf),
        jax.ShapeDtypeStruct((t, NA_WIDTH), bf), jax.ShapeDtypeStruct((t, SWA_Q_WIDTH), bf),
        jax.ShapeDtypeStruct((t, _KV_DUP), bf), jax.ShapeDtypeStruct((t, _KV_DUP), bf),
    )
    return pl.pallas_call(
        _inproj_kernel,
        grid=(batch, nblk),
        in_specs=[
            pl.BlockSpec((tm, D_MODEL), tok),
            _const_spec((1, D_MODEL)), _const_spec((1, D_MODEL)),
            _const_spec((D_MODEL, _C_G)), _const_spec((1, _C_END)),
            pl.BlockSpec((tm, LANES), pos), pl.BlockSpec((tm, LANES), pos),
            pl.BlockSpec((tm, LANES), pos),
        ],
        out_specs=(
            pl.BlockSpec((tm, NA_WIDTH), tok), pl.BlockSpec((tm, NA_WIDTH), tok),
            pl.BlockSpec((tm, NA_WIDTH), tok), pl.BlockSpec((tm, SWA_Q_WIDTH), tok),
            pl.BlockSpec((tm, _KV_DUP), tok), pl.BlockSpec((tm, _KV_DUP), tok),
        ),
        out_shape=out_shape,
        compiler_params=pltpu.CompilerParams(
            dimension_semantics=("parallel", "parallel"), vmem_limit_bytes=VMEM_LIMIT),
        name="ln_inproj_rotary",
    )(x2, g, b, w, bias, cos_t, sa_t, sb_t)


def _rotary_tables(seq):
    half = ROPE_DIM // 2
    inv_freq = np.power(ROPE_THETA, -np.arange(0, ROPE_DIM, 2, dtype=np.float64) / ROPE_DIM)
    ang = np.arange(seq, dtype=np.float64)[:, None] * inv_freq[None, :]
    cos, sin = np.cos(ang), np.sin(ang)
    ones = np.ones((seq, HEAD_DIM - ROPE_DIM))
    zeros = np.zeros((seq, HEAD_DIM - ROPE_DIM))
    zh = np.zeros((seq, half))
    cos_h = np.concatenate([cos, cos, ones], axis=1)
    sa_h = np.concatenate([zh, sin, zeros], axis=1)
    sb_h = np.concatenate([-sin, zh, zeros], axis=1)
    tile = lambda a: jnp.asarray(np.tile(a, (1, HEADS_PER_VREG)), dtype=jnp.float32)
    return tile(cos_h), tile(sa_h), tile(sb_h)


def _stack_heads(q):
    lane = lax.broadcasted_iota(jnp.int32, q.shape, 1)
    zero = jnp.zeros_like(q)
    return jnp.concatenate([jnp.where(lane < HEAD_DIM, q, zero),
                            jnp.where(lane >= HEAD_DIM, q, zero)], axis=0)


def _unstack_heads(o2):
    m = o2.shape[0] // 2
    lane = lax.broadcasted_iota(jnp.int32, (m, LANES), 1)
    return jnp.where(lane < HEAD_DIM, o2[:m], o2[m:])


_NT = (((1,), (1,)), ((), ()))


def _na_bias_kernel(rpb_ref, o_ref):
    w = lax.broadcasted_iota(jnp.int32, (GRID_W, LANES), 0)
    lane = lax.broadcasted_iota(jnp.int32, (GRID_W, LANES), 1)
    c = lane % GRID_W
    col_start = jnp.clip(w - NA_KW // 2, 0, GRID_W - NA_KW)
    inside = (c >= col_start) & (c < col_start + NA_KW)
    lower = lane < GRID_W
    n_dr = 2 * NA_KH - 1
    for e in range(HEADS_PER_VREG):
        lo, hi = [], []
        for dr in range(n_dr):
            x = jnp.broadcast_to(rpb_ref[0, e, dr:dr + 1, :], (GRID_W, LANES))
            lo.append(pltpu.roll(x, LANES - (NA_KW - 1), 1, stride=1, stride_axis=0))
            hi.append(pltpu.roll(x, GRID_W - (NA_KW - 1), 1, stride=1, stride_axis=0))
        for v in range(NA_KH):
            for kk in range(NA_KH // HEADS_PER_VREG):
                dr_a = 2 * kk - v + NA_KH - 1
                tile = jnp.where(inside, jnp.where(lower, lo[dr_a], hi[dr_a + 1]), MASK_VALUE)
                o_ref[v, 0, e * GRID_W:(e + 1) * GRID_W, kk * LANES:(kk + 1) * LANES] = tile


def _na_bias_table(rpb):
    n_pairs = NA_HEADS // HEADS_PER_VREG
    n_dr, n_dc = rpb.shape[1], rpb.shape[2]
    dr_pad = -(-n_dr // 8) * 8
    rp = jnp.pad(rpb, ((0, 0), (0, dr_pad - n_dr), (0, LANES - n_dc)))
    rp = rp.reshape(n_pairs, HEADS_PER_VREG, dr_pad, LANES)
    m2, win = HEADS_PER_VREG * GRID_W, NA_KH * GRID_W
    return pl.pallas_call(
        _na_bias_kernel,
        grid=(n_pairs,),
        in_specs=[pl.BlockSpec((1, HEADS_PER_VREG, dr_pad, LANES), lambda j: (j, 0, 0, 0))],
        out_specs=pl.BlockSpec((NA_KH, 1, m2, win), lambda j: (0, j, 0, 0)),
        out_shape=jax.ShapeDtypeStruct((NA_KH, n_pairs, m2, win), jnp.float32),
        compiler_params=pltpu.CompilerParams(dimension_semantics=("parallel",)),
        name="na_bias_table",
    )(rp)


def _na_kernel(q_ref, k_ref, v_ref, tbl_ref, o_ref, s_scr, p_scr, l_scr, *,
               rows_per_step, n_rows):
    i = pl.program_id(2)
    win = NA_KH * GRID_W

    def window(t):
        r = i * rows_per_step + t
        rs = jnp.clip(r - NA_KH // 2, 0, n_rows - NA_KH)
        return r - rs, pl.multiple_of(rs * GRID_W, GRID_W)

    for t in range(rows_per_step):
        variant, start = window(t)
        lhs = _stack_heads(q_ref[0, t * GRID_W:(t + 1) * GRID_W, :])
        kw = k_ref[0, pl.ds(start, win), :]
        s = lax.dot_general(lhs, kw, _NT, preferred_element_type=jnp.float32)
        s_scr[t] = s + tbl_ref[variant, 0]
    for t in range(rows_per_step):
        s = s_scr[t]
        m = jnp.max(s, axis=-1, keepdims=True)
        p = jnp.exp(s - m)
        l_scr[t] = jnp.broadcast_to(jnp.sum(p, axis=-1, keepdims=True), l_scr.shape[1:])
        p_scr[t] = p.astype(p_scr.dtype)
    for t in range(rows_per_step):
        _, start = window(t)
        vw = v_ref[0, pl.ds(start, win), :]
        o2 = jnp.dot(p_scr[t], vw, preferred_element_type=jnp.float32)
        o_ref[0, t * GRID_W:(t + 1) * GRID_W, :] = (
            _unstack_heads(o2 / l_scr[t]).astype(o_ref.dtype))


def _na_attention(qa, ka, va, tbl, rows_per_step):
    batch, seq, _ = qa.shape
    n_rows = seq // GRID_W
    n_pairs = NA_HEADS // HEADS_PER_VREG
    tq = rows_per_step * GRID_W
    m2, win = HEADS_PER_VREG * GRID_W, NA_KH * GRID_W
    kern = functools.partial(_na_kernel, rows_per_step=rows_per_step, n_rows=n_rows)
    return pl.pallas_call(
        kern,
        grid=(batch, n_pairs, n_rows // rows_per_step),
        in_specs=[
            pl.BlockSpec((1, tq, LANES), lambda b, j, i: (b, i, j)),
            pl.BlockSpec((1, seq, LANES), lambda b, j, i: (b, 0, j)),
            pl.BlockSpec((1, seq, LANES), lambda b, j, i: (b, 0, j)),
            pl.BlockSpec((NA_KH, 1, m2, win), lambda b, j, i: (0, j, 0, 0)),
        ],
        out_specs=pl.BlockSpec((1, tq, LANES), lambda b, j, i: (b, i, j)),
        out_shape=jax.ShapeDtypeStruct(qa.shape, jnp.bfloat16),
        scratch_shapes=[
            pltpu.VMEM((rows_per_step, m2, win), jnp.float32),
            pltpu.VMEM((rows_per_step, m2, win), jnp.bfloat16),
            pltpu.VMEM((rows_per_step, m2, LANES), jnp.float32),
        ],
        compiler_params=pltpu.CompilerParams(
            dimension_semantics=("parallel", "parallel", "parallel"),
            vmem_limit_bytes=VMEM_LIMIT),
        name="na_attention",
    )(qa, ka, va, tbl)


def _swa_kernel(sink_ref, q_ref, k_ref, v_ref, o_ref, s_scr, p_scr, l_scr, *,
                blocks_per_step, seq):
    j = pl.program_id(1)
    n0 = pl.program_id(2) * blocks_per_step
    blk = SWA_BLOCK
    win = 3 * blk
    row = lax.broadcasted_iota(jnp.int32, (2 * blk, 1), 0)
    sink_col = jnp.where(row < blk, sink_ref[2 * j], sink_ref[2 * j + 1])
    qi = lax.broadcasted_iota(jnp.int32, (2 * blk, win), 0) % blk
    kc = lax.broadcasted_iota(jnp.int32, (2 * blk, win), 1)
    rel0 = kc - qi

    def window(u):
        n = n0 + u
        start = jnp.clip((n - 1) * blk, 0, seq - win)
        return n, pl.multiple_of(start, blk)

    for u in range(blocks_per_step):
        n, start = window(u)
        lhs = _stack_heads(q_ref[0, u * blk:(u + 1) * blk, :])
        kw = k_ref[0, pl.ds(start, win), :]
        s = lax.dot_general(lhs, kw, _NT, preferred_element_type=jnp.float32)
        rel = rel0 + (start - n * blk)
        s_scr[u] = jnp.where(jnp.abs(rel) <= SWA_WINDOW, s, MASK_VALUE)
    for u in range(blocks_per_step):
        s = s_scr[u]
        m = jnp.maximum(jnp.max(s, axis=-1, keepdims=True), sink_col)
        e = jnp.exp(s - m)
        denom = jnp.sum(e, axis=-1, keepdims=True) + jnp.exp(sink_col - m)
        l_scr[u] = jnp.broadcast_to(denom, l_scr.shape[1:])
        p_scr[u] = e.astype(p_scr.dtype)
    for u in range(blocks_per_step):
        _, start = window(u)
        vw = v_ref[0, pl.ds(start, win), :]
        o2 = jnp.dot(p_scr[u], vw, preferred_element_type=jnp.float32)
        o_ref[0, u * blk:(u + 1) * blk, :] = _unstack_heads(o2 / l_scr[u]).astype(o_ref.dtype)


def _swa_attention(qb, kb, vb, sink, blocks_per_step):
    batch, seq, _ = qb.shape
    n_pairs = SWA_Q_HEADS // HEADS_PER_VREG
    pairs_per_group = n_pairs // SWA_KV_HEADS
    tq = blocks_per_step * SWA_BLOCK
    m2, win = HEADS_PER_VREG * SWA_BLOCK, 3 * SWA_BLOCK
    kern = functools.partial(_swa_kernel, blocks_per_step=blocks_per_step, seq=seq)
    return pl.pallas_call(
        kern,
        grid=(batch, n_pairs, seq // tq),
        in_specs=[
            pl.BlockSpec(memory_space=pltpu.SMEM),
            pl.BlockSpec((1, tq, LANES), lambda b, j, n: (b, n, j)),
            pl.BlockSpec((1, seq, LANES), lambda b, j, n: (b, 0, j // pairs_per_group)),
            pl.BlockSpec((1, seq, LANES), lambda b, j, n: (b, 0, j // pairs_per_group)),
        ],
        out_specs=pl.BlockSpec((1, tq, LANES), lambda b, j, n: (b, n, j)),
        out_shape=jax.ShapeDtypeStruct(qb.shape, jnp.bfloat16),
        scratch_shapes=[
            pltpu.VMEM((blocks_per_step, m2, win), jnp.float32),
            pltpu.VMEM((blocks_per_step, m2, win), jnp.bfloat16),
            pltpu.VMEM((blocks_per_step, m2, LANES), jnp.float32),
        ],
        compiler_params=pltpu.CompilerParams(
            dimension_semantics=("parallel", "parallel", "parallel"),
            vmem_limit_bytes=VMEM_LIMIT),
        name="swa_attention",
    )(sink, qb, kb, vb)


def _mix_kernel(x_ref, g0_ref, b0_ref, na_ref, swa_ref, wg_ref, bin_ref, wna_ref, wswa_ref,
                wout_ref, bout_ref, g1_ref, b1_ref, o_ref):
    h = _layer_norm(x_ref[...], g0_ref[...], b0_ref[...])
    gates = (jnp.dot(h.astype(jnp.bfloat16), wg_ref[...], preferred_element_type=jnp.float32)
             + bin_ref[:, _C_G:_C_END])
    y_na = jnp.dot(na_ref[...], wna_ref[...], preferred_element_type=jnp.float32)
    y_swa = jnp.dot(swa_ref[...], wswa_ref[...], preferred_element_type=jnp.float32)
    mixed = (jax.nn.sigmoid(gates[:, :D_MODEL]) * y_na
             + jax.nn.sigmoid(gates[:, D_MODEL:]) * y_swa)
    attn = jnp.dot(mixed.astype(jnp.bfloat16), wout_ref[...],
                   preferred_element_type=jnp.float32) + bout_ref[...]
    o_ref[...] = _layer_norm(DEEPNORM_ALPHA * h + attn, g1_ref[...], b1_ref[...])


def _mix(x2, g0, b0, na, swa, wg, b_in, wna, wswa, wout, bout, g1, b1, tm):
    t = x2.shape[0]
    tok = lambda i: (i, 0)
    vec = _const_spec((1, D_MODEL))
    return pl.pallas_call(
        _mix_kernel,
        grid=(t // tm,),
        in_specs=[
            pl.BlockSpec((tm, D_MODEL), tok), vec, vec,
            pl.BlockSpec((tm, NA_WIDTH), tok), pl.BlockSpec((tm, SWA_Q_WIDTH), tok),
            _const_spec((D_MODEL, 2 * D_MODEL)), _const_spec((1, _C_END)),
            _const_spec((NA_WIDTH, D_MODEL)), _const_spec((SWA_Q_WIDTH, D_MODEL)),
            _const_spec((D_MODEL, D_MODEL)), vec, vec, vec,
        ],
        out_specs=pl.BlockSpec((tm, D_MODEL), tok),
        out_shape=jax.ShapeDtypeStruct((t, D_MODEL), jnp.float32),
        compiler_params=pltpu.CompilerParams(
            dimension_semantics=("parallel",), vmem_limit_bytes=VMEM_LIMIT),
        name="mix_outproj_ln",
    )(x2, g0, b0, na, swa, wg, b_in, wna, wswa, wout, bout, g1, b1)


def _ffn_kernel(h_ref, w1_ref, b1_ref, w2_ref, b2_ref, g_ref, b_ref, o_ref, *, ff_chunk):
    h = h_ref[...]
    hb = h.astype(jnp.bfloat16)
    acc = jnp.zeros(h.shape, jnp.float32)
    for c in range(D_FF // ff_chunk):
        sl = slice(c * ff_chunk, (c + 1) * ff_chunk)
        u = jnp.dot(hb, w1_ref[:, sl], preferred_element_type=jnp.float32) + b1_ref[:, sl]
        u = jnp.square(jnp.maximum(u, 0.0))
        acc = acc + jnp.dot(u.astype(jnp.bfloat16), w2_ref[sl, :],
                            preferred_element_type=jnp.float32)
    o_ref[...] = _layer_norm(DEEPNORM_ALPHA * h + acc + b2_ref[...], g_ref[...], b_ref[...])


def _ffn(h1, w1, b1, w2, b2, g, b, tm, ff_chunk):
    t = h1.shape[0]
    tok = lambda i: (i, 0)
    vec = _const_spec((1, D_MODEL))
    return pl.pallas_call(
        functools.partial(_ffn_kernel, ff_chunk=ff_chunk),
        grid=(t // tm,),
        in_specs=[
            pl.BlockSpec((tm, D_MODEL), tok),
            _const_spec((D_MODEL, D_FF)), _const_spec((1, D_FF)),
            _const_spec((D_FF, D_MODEL)), vec, vec, vec,
        ],
        out_specs=pl.BlockSpec((tm, D_MODEL), tok),
        out_shape=jax.ShapeDtypeStruct((t, D_MODEL), jnp.float32),
        compiler_params=pltpu.CompilerParams(
            dimension_semantics=("parallel",), vmem_limit_bytes=VMEM_LIMIT),
        name="ffn_ln",
    )(h1, w1, b1, w2, b2, g, b)


def kernel(x, ln0_g, ln0_b, w_in, b_in, na_rpb, swa_sink, w_branch_na, w_branch_swa,
           w_out, b_out, ln1_g, ln1_b, w_ff1, b_ff1, w_ff2, b_ff2, ln2_g, ln2_b):
    batch, seq, d = x.shape
    assert d == D_MODEL and seq % (NA_KH * GRID_W) == 0 and seq >= 3 * SWA_BLOCK
    assert w_in.shape == (DEPTH, D_MODEL, _C_END) and DEPTH == 1
    bf = jnp.bfloat16
    row = lambda v: v.reshape(1, -1)
    x2 = x.reshape(batch * seq, d)
    b_in_row = row(b_in[0])

    cos_t, sa_t, sb_t = _rotary_tables(seq)
    qa, ka, va, qb, kb, vb = _inproj(
        x2, row(ln0_g), row(ln0_b), w_in[0, :, :_C_G].astype(bf), b_in_row,
        cos_t, sa_t, sb_t, batch, seq, tm=512)

    three = lambda a: a.reshape(batch, seq, a.shape[-1])
    o_na = _na_attention(three(qa), three(ka), three(va), _na_bias_table(na_rpb[0]),
                         rows_per_step=8)
    o_swa = _swa_attention(three(qb), three(kb), three(vb), swa_sink[0], blocks_per_step=4)

    h1 = _mix(x2, row(ln0_g), row(ln0_b), o_na.reshape(batch * seq, -1),
              o_swa.reshape(batch * seq, -1), w_in[0, :, _C_G:].astype(bf), b_in_row,
              w_branch_na[0].astype(bf), w_branch_swa[0].astype(bf), w_out[0].astype(bf),
              row(b_out[0]), row(ln1_g[0]), row(ln1_b[0]), tm=512)
    out = _ffn(h1, w_ff1[0].astype(bf), row(b_ff1[0]), w_ff2[0].astype(bf), row(b_ff2[0]),
               row(ln2_g[0]), row(ln2_b[0]), tm=512, ff_chunk=1024)
    return out.reshape(batch, seq, d)
```

```python
import functools

import numpy as np
import jax
import jax.numpy as jnp
from jax import lax
from jax.experimental import pallas as pl
from jax.experimental.pallas import tpu as pltpu

D_MODEL = 1024
GRID_W = 64
HEAD_DIM = 64
NA_HEADS = 8
NA_KH = 8
NA_KW = 16
NA_WIDTH = NA_HEADS * HEAD_DIM
SWA_Q_HEADS = 8
SWA_KV_HEADS = 2
SWA_WINDOW = 128
SWA_BLOCK = 128
SWA_Q_WIDTH = SWA_Q_HEADS * HEAD_DIM
SWA_KV_WIDTH = SWA_KV_HEADS * HEAD_DIM
ROPE_THETA = 500000.0
ROPE_DIM = HEAD_DIM // 4
D_FF = 4 * D_MODEL
DEPTH = 1
DEEPNORM_ALPHA = (2.0 * DEPTH) ** 0.25
LN_EPS = 1e-5
MASK_VALUE = -1e30
ATTN_SCALE = HEAD_DIM ** -0.5

LANES = 128
HEADS_PER_VREG = LANES // HEAD_DIM
VMEM_LIMIT = 56 * 1024 * 1024

_C_QA = 0
_C_KA = _C_QA + NA_WIDTH
_C_VA = _C_KA + NA_WIDTH
_C_QB = _C_VA + NA_WIDTH
_C_KB = _C_QB + SWA_Q_WIDTH
_C_VB = _C_KB + SWA_KV_WIDTH
_C_G = _C_VB + SWA_KV_WIDTH
_C_END = _C_G + 2 * D_MODEL
_KV_DUP = SWA_KV_HEADS * LANES


def _const_spec(shape):
    nd = len(shape)
    return pl.BlockSpec(shape, lambda *_: (0,) * nd, pipeline_mode=pl.Buffered(1))


def _layer_norm(x, g, b):
    mu = jnp.mean(x, axis=-1, keepdims=True)
    xc = x - mu
    var = jnp.mean(xc * xc, axis=-1, keepdims=True)
    return xc * lax.rsqrt(var + LN_EPS) * g + b


def _dup_heads(y):
    lane = lax.broadcasted_iota(jnp.int32, y.shape, 1)
    swapped = pltpu.roll(y, HEAD_DIM, 1)
    return jnp.concatenate([jnp.where(lane < HEAD_DIM, y, swapped),
                            jnp.where(lane >= HEAD_DIM, y, swapped)], axis=1)


def _inproj_kernel(x_ref, g_ref, b_ref, w_ref, bias_ref, cos_ref, sa_ref, sb_ref,
                   qa_ref, ka_ref, va_ref, qb_ref, kb_ref, vb_ref):
    h = _layer_norm(x_ref[...], g_ref[...], b_ref[...])
    hb = h.astype(jnp.bfloat16)

    def proj(c0, c1):
        return (jnp.dot(hb, w_ref[:, c0:c1], preferred_element_type=jnp.float32)
                + bias_ref[:, c0:c1])

    def rotary(y):
        cos, sa, sb = cos_ref[...], sa_ref[...], sb_ref[...]
        half = ROPE_DIM // 2
        outs = []
        for c in range(y.shape[1] // LANES):
            yc = y[:, c * LANES:(c + 1) * LANES]
            outs.append(yc * cos
                        + pltpu.roll(yc, half, 1) * sa
                        + pltpu.roll(yc, LANES - half, 1) * sb)
        return outs[0] if len(outs) == 1 else jnp.concatenate(outs, axis=1)

    qa_ref[...] = (proj(_C_QA, _C_KA) * ATTN_SCALE).astype(qa_ref.dtype)
    ka_ref[...] = proj(_C_KA, _C_VA).astype(ka_ref.dtype)
    va_ref[...] = proj(_C_VA, _C_QB).astype(va_ref.dtype)
    qb_ref[...] = (rotary(proj(_C_QB, _C_KB)) * ATTN_SCALE).astype(qb_ref.dtype)
    kv = proj(_C_KB, _C_G)
    kb_ref[...] = _dup_heads(rotary(kv[:, :SWA_KV_WIDTH])).astype(kb_ref.dtype)
    vb_ref[...] = _dup_heads(kv[:, SWA_KV_WIDTH:]).astype(vb_ref.dtype)


def _inproj(x2, g, b, w, bias, cos_t, sa_t, sb_t, batch, seq, tm):
    t = batch * seq
    nblk = seq // tm
    tok = lambda bi, i: (bi * nblk + i, 0)
    pos = lambda bi, i: (i, 0)
    bf = jnp.bfloat16
    out_shape = (
        jax.ShapeDtypeStruct((t, NA_WIDTH), bf), jax.ShapeDtypeStruct((t, NA_WIDTH), bf),
        jax.ShapeDtypeStruct((t, NA_WIDTH), bf), jax.ShapeDtypeStruct((t, SWA_Q_WIDTH), bf),
        jax.ShapeDtypeStruct((t, _KV_DUP), bf), jax.ShapeDtypeStruct((t, _KV_DUP), bf),
    )
    return pl.pallas_call(
        _inproj_kernel,
        grid=(batch, nblk),
        in_specs=[
            pl.BlockSpec((tm, D_MODEL), tok),
            _const_spec((1, D_MODEL)), _const_spec((1, D_MODEL)),
            _const_spec((D_MODEL, _C_G)), _const_spec((1, _C_END)),
            pl.BlockSpec((tm, LANES), pos), pl.BlockSpec((tm, LANES), pos),
            pl.BlockSpec((tm, LANES), pos),
        ],
        out_specs=(
            pl.BlockSpec((tm, NA_WIDTH), tok), pl.BlockSpec((tm, NA_WIDTH), tok),
            pl.BlockSpec((tm, NA_WIDTH), tok), pl.BlockSpec((tm, SWA_Q_WIDTH), tok),
            pl.BlockSpec((tm, _KV_DUP), tok), pl.BlockSpec((tm, _KV_DUP), tok),
        ),
        out_shape=out_shape,
        compiler_params=pltpu.CompilerParams(
            dimension_semantics=("parallel", "parallel"), vmem_limit_bytes=VMEM_LIMIT),
        name="ln_inproj_rotary",
    )(x2, g, b, w, bias, cos_t, sa_t, sb_t)


def _rotary_tables(seq):
    half = ROPE_DIM // 2
    inv_freq = np.power(ROPE_THETA, -np.arange(0, ROPE_DIM, 2, dtype=np.float64) / ROPE_DIM)
    ang = np.arange(seq, dtype=np.float64)[:, None] * inv_freq[None, :]
    cos, sin = np.cos(ang), np.sin(ang)
    ones = np.ones((seq, HEAD_DIM - ROPE_DIM))
    zeros = np.zeros((seq, HEAD_DIM - ROPE_DIM))
    zh = np.zeros((seq, half))
    cos_h = np.concatenate([cos, cos, ones], axis=1)
    sa_h = np.concatenate([zh, sin, zeros], axis=1)
    sb_h = np.concatenate([-sin, zh, zeros], axis=1)
    tile = lambda a: jnp.asarray(np.tile(a, (1, HEADS_PER_VREG)), dtype=jnp.float32)
    return tile(cos_h), tile(sa_h), tile(sb_h)


def _stack_heads(q):
    lane = lax.broadcasted_iota(jnp.int32, q.shape, 1)
    zero = jnp.zeros_like(q)
    return jnp.concatenate([jnp.where(lane < HEAD_DIM, q, zero),
                            jnp.where(lane >= HEAD_DIM, q, zero)], axis=0)


def _unstack_heads(o2):
    m = o2.shape[0] // 2
    lane = lax.broadcasted_iota(jnp.int32, (m, LANES), 1)
    return jnp.where(lane < HEAD_DIM, o2[:m], o2[m:])


_NT = (((1,), (1,)), ((), ()))


def _na_bias_kernel(rpb_ref, o_ref):
    w = lax.broadcasted_iota(jnp.int32, (GRID_W, LANES), 0)
    lane = lax.broadcasted_iota(jnp.int32, (GRID_W, LANES), 1)
    c = lane % GRID_W
    col_start = jnp.clip(w - NA_KW // 2, 0, GRID_W - NA_KW)
    inside = (c >= col_start) & (c < col_start + NA_KW)
    lower = lane < GRID_W
    n_dr = 2 * NA_KH - 1
    for e in range(HEADS_PER_VREG):
        lo, hi = [], []
        for dr in range(n_dr):
            x = jnp.broadcast_to(rpb_ref[0, e, dr:dr + 1, :], (GRID_W, LANES))
            lo.append(pltpu.roll(x, LANES - (NA_KW - 1), 1, stride=1, stride_axis=0))
            hi.append(pltpu.roll(x, GRID_W - (NA_KW - 1), 1, stride=1, stride_axis=0))
        for v in range(NA_KH):
            for kk in range(NA_KH // HEADS_PER_VREG):
                dr_a = 2 * kk - v + NA_KH - 1
                tile = jnp.where(inside, jnp.where(lower, lo[dr_a], hi[dr_a + 1]), MASK_VALUE)
                o_ref[v, 0, e * GRID_W:(e + 1) * GRID_W, kk * LANES:(kk + 1) * LANES] = tile


def _na_bias_table(rpb):
    n_pairs = NA_HEADS // HEADS_PER_VREG
    n_dr, n_dc = rpb.shape[1], rpb.shape[2]
    dr_pad = -(-n_dr // 8) * 8
    rp = jnp.pad(rpb, ((0, 0), (0, dr_pad - n_dr), (0, LANES - n_dc)))
    rp = rp.reshape(n_pairs, HEADS_PER_VREG, dr_pad, LANES)
    m2, win = HEADS_PER_VREG * GRID_W, NA_KH * GRID_W
    return pl.pallas_call(
        _na_bias_kernel,
        grid=(n_pairs,),
        in_specs=[pl.BlockSpec((1, HEADS_PER_VREG, dr_pad, LANES), lambda j: (j, 0, 0, 0))],
        out_specs=pl.BlockSpec((NA_KH, 1, m2, win), lambda j: (0, j, 0, 0)),
        out_shape=jax.ShapeDtypeStruct((NA_KH, n_pairs, m2, win), jnp.float32),
        compiler_params=pltpu.CompilerParams(dimension_semantics=("parallel",)),
        name="na_bias_table",
    )(rp)


def _na_kernel(q_ref, k_ref, v_ref, tbl_ref, o_ref, s_scr, p_scr, l_scr, *,
               rows_per_step, n_rows):
    i = pl.program_id(2)
    win = NA_KH * GRID_W

    def window(t):
        r = i * rows_per_step + t
        rs = jnp.clip(r - NA_KH // 2, 0, n_rows - NA_KH)
        return r - rs, pl.multiple_of(rs * GRID_W, GRID_W)

    for t in range(rows_per_step):
        variant, start = window(t)
        lhs = _stack_heads(q_ref[0, t * GRID_W:(t + 1) * GRID_W, :])
        kw = k_ref[0, pl.ds(start, win), :]
        s = lax.dot_general(lhs, kw, _NT, preferred_element_type=jnp.float32)
        s_scr[t] = s + tbl_ref[variant, 0]
    for t in range(rows_per_step):
        s = s_scr[t]
        m = jnp.max(s, axis=-1, keepdims=True)
        p = jnp.exp(s - m)
        l_scr[t] = jnp.broadcast_to(jnp.sum(p, axis=-1, keepdims=True), l_scr.shape[1:])
        p_scr[t] = p.astype(p_scr.dtype)
    for t in range(rows_per_step):
        _, start = window(t)
        vw = v_ref[0, pl.ds(start, win), :]
        o2 = jnp.dot(p_scr[t], vw, preferred_element_type=jnp.float32)
        o_ref[0, t * GRID_W:(t + 1) * GRID_W, :] = (
            _unstack_heads(o2 / l_scr[t]).astype(o_ref.dtype))


def _na_attention(qa, ka, va, tbl, rows_per_step):
    batch, seq, _ = qa.shape
    n_rows = seq // GRID_W
    n_pairs = NA_HEADS // HEADS_PER_VREG
    tq = rows_per_step * GRID_W
    m2, win = HEADS_PER_VREG * GRID_W, NA_KH * GRID_W
    kern = functools.partial(_na_kernel, rows_per_step=rows_per_step, n_rows=n_rows)
    return pl.pallas_call(
        kern,
        grid=(batch, n_pairs, n_rows // rows_per_step),
        in_specs=[
            pl.BlockSpec((1, tq, LANES), lambda b, j, i: (b, i, j)),
            pl.BlockSpec((1, seq, LANES), lambda b, j, i: (b, 0, j)),
            pl.BlockSpec((1, seq, LANES), lambda b, j, i: (b, 0, j)),
            pl.BlockSpec((NA_KH, 1, m2, win), lambda b, j, i: (0, j, 0, 0)),
        ],
        out_specs=pl.BlockSpec((1, tq, LANES), lambda b, j, i: (b, i, j)),
        out_shape=jax.ShapeDtypeStruct(qa.shape, jnp.bfloat16),
        scratch_shapes=[
            pltpu.VMEM((rows_per_step, m2, win), jnp.float32),
            pltpu.VMEM((rows_per_step, m2, win), jnp.bfloat16),
            pltpu.VMEM((rows_per_step, m2, LANES), jnp.float32),
        ],
        compiler_params=pltpu.CompilerParams(
            dimension_semantics=("parallel", "parallel", "parallel"),
            vmem_limit_bytes=VMEM_LIMIT),
        name="na_attention",
    )(qa, ka, va, tbl)


def _swa_mask_table():
    qi = np.arange(SWA_BLOCK)[:, None]
    kc = np.arange(3 * SWA_BLOCK)[None, :]
    tbl = [np.where(np.abs(kc - d * SWA_BLOCK - qi) <= SWA_WINDOW, 0.0, MASK_VALUE)
           for d in range(3)]
    return jnp.asarray(np.stack(tbl), dtype=jnp.float32)


def _swa_kernel(sink_ref, q_ref, k_ref, v_ref, mask_ref, o_ref, s_scr, p_scr, l_scr, *,
                blocks_per_step, seq):
    j = pl.program_id(1)
    n0 = pl.program_id(2) * blocks_per_step
    blk = SWA_BLOCK
    win = 3 * blk
    row = lax.broadcasted_iota(jnp.int32, (2 * blk, 1), 0)
    sink_col = jnp.where(row < blk, sink_ref[2 * j], sink_ref[2 * j + 1])

    def window(u):
        n = n0 + u
        start = jnp.clip((n - 1) * blk, 0, seq - win)
        return n - start // blk, pl.multiple_of(start, blk)

    for u in range(blocks_per_step):
        variant, start = window(u)
        lhs = _stack_heads(q_ref[0, u * blk:(u + 1) * blk, :])
        kw = k_ref[0, pl.ds(start, win), :]
        s = lax.dot_general(lhs, kw, _NT, preferred_element_type=jnp.float32)
        mask = mask_ref[variant]
        s_scr[u] = s + jnp.concatenate([mask, mask], axis=0)
    for u in range(blocks_per_step):
        s = s_scr[u]
        m = jnp.maximum(jnp.max(s, axis=-1, keepdims=True), sink_col)
        e = jnp.exp(s - m)
        denom = jnp.sum(e, axis=-1, keepdims=True) + jnp.exp(sink_col - m)
        l_scr[u] = jnp.broadcast_to(denom, l_scr.shape[1:])
        p_scr[u] = e.astype(p_scr.dtype)
    for u in range(blocks_per_step):
        _, start = window(u)
        vw = v_ref[0, pl.ds(start, win), :]
        o2 = jnp.dot(p_scr[u], vw, preferred_element_type=jnp.float32)
        o_ref[0, u * blk:(u + 1) * blk, :] = _unstack_heads(o2 / l_scr[u]).astype(o_ref.dtype)


def _swa_attention(qb, kb, vb, sink, blocks_per_step):
    batch, seq, _ = qb.shape
    n_pairs = SWA_Q_HEADS // HEADS_PER_VREG
    pairs_per_group = n_pairs // SWA_KV_HEADS
    tq = blocks_per_step * SWA_BLOCK
    m2, win = HEADS_PER_VREG * SWA_BLOCK, 3 * SWA_BLOCK
    kern = functools.partial(_swa_kernel, blocks_per_step=blocks_per_step, seq=seq)
    return pl.pallas_call(
        kern,
        grid=(batch, n_pairs, seq // tq),
        in_specs=[
            pl.BlockSpec(memory_space=pltpu.SMEM),
            pl.BlockSpec((1, tq, LANES), lambda b, j, n: (b, n, j)),
            pl.BlockSpec((1, seq, LANES), lambda b, j, n: (b, 0, j // pairs_per_group)),
            pl.BlockSpec((1, seq, LANES), lambda b, j, n: (b, 0, j // pairs_per_group)),
            _const_spec((3, SWA_BLOCK, 3 * SWA_BLOCK)),
        ],
        out_specs=pl.BlockSpec((1, tq, LANES), lambda b, j, n: (b, n, j)),
        out_shape=jax.ShapeDtypeStruct(qb.shape, jnp.bfloat16),
        scratch_shapes=[
            pltpu.VMEM((blocks_per_step, m2, win), jnp.float32),
            pltpu.VMEM((blocks_per_step, m2, win), jnp.bfloat16),
            pltpu.VMEM((blocks_per_step, m2, LANES), jnp.float32),
        ],
        compiler_params=pltpu.CompilerParams(
            dimension_semantics=("parallel", "parallel", "parallel"),
            vmem_limit_bytes=VMEM_LIMIT),
        name="swa_attention",
    )(sink, qb, kb, vb, _swa_mask_table())


def _mix_kernel(x_ref, g0_ref, b0_ref, na_ref, swa_ref, wg_ref, bin_ref, wna_ref, wswa_ref,
                wout_ref, bout_ref, g1_ref, b1_ref, o_ref):
    h = _layer_norm(x_ref[...], g0_ref[...], b0_ref[...])
    gates = (jnp.dot(h.astype(jnp.bfloat16), wg_ref[...], preferred_element_type=jnp.float32)
             + bin_ref[:, _C_G:_C_END])
    y_na = jnp.dot(na_ref[...], wna_ref[...], preferred_element_type=jnp.float32)
    y_swa = jnp.dot(swa_ref[...], wswa_ref[...], preferred_element_type=jnp.float32)
    mixed = (jax.nn.sigmoid(gates[:, :D_MODEL]) * y_na
             + jax.nn.sigmoid(gates[:, D_MODEL:]) * y_swa)
    attn = jnp.dot(mixed.astype(jnp.bfloat16), wout_ref[...],
                   preferred_element_type=jnp.float32) + bout_ref[...]
    o_ref[...] = _layer_norm(DEEPNORM_ALPHA * h + attn, g1_ref[...], b1_ref[...])


def _mix(x2, g0, b0, na, swa, wg, b_in, wna, wswa, wout, bout, g1, b1, tm):
    t = x2.shape[0]
    tok = lambda i: (i, 0)
    vec = _const_spec((1, D_MODEL))
    return pl.pallas_call(
        _mix_kernel,
        grid=(t // tm,),
        in_specs=[
            pl.BlockSpec((tm, D_MODEL), tok), vec, vec,
            pl.BlockSpec((tm, NA_WIDTH), tok), pl.BlockSpec((tm, SWA_Q_WIDTH), tok),
            _const_spec((D_MODEL, 2 * D_MODEL)), _const_spec((1, _C_END)),
            _const_spec((NA_WIDTH, D_MODEL)), _const_spec((SWA_Q_WIDTH, D_MODEL)),
            _const_spec((D_MODEL, D_MODEL)), vec, vec, vec,
        ],
        out_specs=pl.BlockSpec((tm, D_MODEL), tok),
        out_shape=jax.ShapeDtypeStruct((t, D_MODEL), jnp.float32),
        compiler_params=pltpu.CompilerParams(
            dimension_semantics=("parallel",), vmem_limit_bytes=VMEM_LIMIT),
        name="mix_outproj_ln",
    )(x2, g0, b0, na, swa, wg, b_in, wna, wswa, wout, bout, g1, b1)


def _ffn_kernel(h_ref, w1_ref, b1_ref, w2_ref, b2_ref, g_ref, b_ref, o_ref, *, ff_chunk):
    h = h_ref[...]
    hb = h.astype(jnp.bfloat16)
    acc = jnp.zeros(h.shape, jnp.float32)
    for c in range(D_FF // ff_chunk):
        sl = slice(c * ff_chunk, (c + 1) * ff_chunk)
        u = jnp.dot(hb, w1_ref[:, sl], preferred_element_type=jnp.float32) + b1_ref[:, sl]
        u = jnp.square(jnp.maximum(u, 0.0))
        acc = acc + jnp.dot(u.astype(jnp.bfloat16), w2_ref[sl, :],
                            preferred_element_type=jnp.float32)
    o_ref[...] = _layer_norm(DEEPNORM_ALPHA * h + acc + b2_ref[...], g_ref[...], b_ref[...])


def _ffn(h1, w1, b1, w2, b2, g, b, tm, ff_chunk):
    t = h1.shape[0]
    tok = lambda i: (i, 0)
    vec = _const_spec((1, D_MODEL))
    return pl.pallas_call(
        functools.partial(_ffn_kernel, ff_chunk=ff_chunk),
        grid=(t // tm,),
        in_specs=[
            pl.BlockSpec((tm, D_MODEL), tok),
            _const_spec((D_MODEL, D_FF)), _const_spec((1, D_FF)),
            _const_spec((D_FF, D_MODEL)), vec, vec, vec,
        ],
        out_specs=pl.BlockSpec((tm, D_MODEL), tok),
        out_shape=jax.ShapeDtypeStruct((t, D_MODEL), jnp.float32),
        compiler_params=pltpu.CompilerParams(
            dimension_semantics=("parallel",), vmem_limit_bytes=VMEM_LIMIT),
        name="ffn_ln",
    )(h1, w1, b1, w2, b2, g, b)


def kernel(x, ln0_g, ln0_b, w_in, b_in, na_rpb, swa_sink, w_branch_na, w_branch_swa,
           w_out, b_out, ln1_g, ln1_b, w_ff1, b_ff1, w_ff2, b_ff2, ln2_g, ln2_b):
    batch, seq, d = x.shape
    assert d == D_MODEL and seq % (NA_KH * GRID_W) == 0 and seq >= 3 * SWA_BLOCK
    assert w_in.shape == (DEPTH, D_MODEL, _C_END) and DEPTH == 1
    bf = jnp.bfloat16
    row = lambda v: v.reshape(1, -1)
    x2 = x.reshape(batch * seq, d)
    b_in_row = row(b_in[0])

    cos_t, sa_t, sb_t = _rotary_tables(seq)
    qa, ka, va, qb, kb, vb = _inproj(
        x2, row(ln0_g), row(ln0_b), w_in[0, :, :_C_G].astype(bf), b_in_row,
        cos_t, sa_t, sb_t, batch, seq, tm=512)

    three = lambda a: a.reshape(batch, seq, a.shape[-1])
    o_na = _na_attention(three(qa), three(ka), three(va), _na_bias_table(na_rpb[0]),
                         rows_per_step=16)
    o_swa = _swa_attention(three(qb), three(kb), three(vb), swa_sink[0], blocks_per_step=8)

    h1 = _mix(x2, row(ln0_g), row(ln0_b), o_na.reshape(batch * seq, -1),
              o_swa.reshape(batch * seq, -1), w_in[0, :, _C_G:].astype(bf), b_in_row,
              w_branch_na[0].astype(bf), w_branch_swa[0].astype(bf), w_out[0].astype(bf),
              row(b_out[0]), row(ln1_g[0]), row(ln1_b[0]), tm=512)
    out = _ffn(h1, w_ff1[0].astype(bf), row(b_ff1[0]), w_ff2[0].astype(bf), row(b_ff2[0]),
               row(ln2_g[0]), row(ln2_b[0]), tm=512, ff_chunk=1024)
    return out.reshape(batch, seq, d)
```

```python
import functools

import numpy as np
import jax
import jax.numpy as jnp
from jax import lax
from jax.experimental import pallas as pl
from jax.experimental.pallas import tpu as pltpu

D_MODEL = 1024
GRID_W = 64
HEAD_DIM = 64
NA_HEADS = 8
NA_KH = 8
NA_KW = 16
NA_WIDTH = NA_HEADS * HEAD_DIM
SWA_Q_HEADS = 8
SWA_KV_HEADS = 2
SWA_WINDOW = 128
SWA_BLOCK = 128
SWA_Q_WIDTH = SWA_Q_HEADS * HEAD_DIM
SWA_KV_WIDTH = SWA_KV_HEADS * HEAD_DIM
ROPE_THETA = 500000.0
ROPE_DIM = HEAD_DIM // 4
D_FF = 4 * D_MODEL
DEPTH = 1
DEEPNORM_ALPHA = (2.0 * DEPTH) ** 0.25
LN_EPS = 1e-5
MASK_VALUE = -1e30
ATTN_SCALE = HEAD_DIM ** -0.5

LANES = 128
HEADS_PER_VREG = LANES // HEAD_DIM
VMEM_LIMIT = 56 * 1024 * 1024

_C_QA = 0
_C_KA = _C_QA + NA_WIDTH
_C_VA = _C_KA + NA_WIDTH
_C_QB = _C_VA + NA_WIDTH
_C_KB = _C_QB + SWA_Q_WIDTH
_C_VB = _C_KB + SWA_KV_WIDTH
_C_G = _C_VB + SWA_KV_WIDTH
_C_END = _C_G + 2 * D_MODEL
_KV_DUP = SWA_KV_HEADS * LANES


def _const_spec(shape):
    nd = len(shape)
    return pl.BlockSpec(shape, lambda *_: (0,) * nd, pipeline_mode=pl.Buffered(1))


def _layer_norm(x, g, b):
    mu = jnp.mean(x, axis=-1, keepdims=True)
    xc = x - mu
    var = jnp.mean(xc * xc, axis=-1, keepdims=True)
    return xc * lax.rsqrt(var + LN_EPS) * g + b


def _dup_heads(y):
    lane = lax.broadcasted_iota(jnp.int32, y.shape, 1)
    swapped = pltpu.roll(y, HEAD_DIM, 1)
    return jnp.concatenate([jnp.where(lane < HEAD_DIM, y, swapped),
                            jnp.where(lane >= HEAD_DIM, y, swapped)], axis=1)


def _inproj_kernel(x_ref, g_ref, b_ref, w_ref, bias_ref, cos_ref, sa_ref, sb_ref,
                   qa_ref, ka_ref, va_ref, qb_ref, kb_ref, vb_ref, *, sub_rows):
    half = ROPE_DIM // 2

    for r0 in range(0, x_ref.shape[0], sub_rows):
        rows = slice(r0, r0 + sub_rows)
        h = _layer_norm(x_ref[rows, :], g_ref[...], b_ref[...])
        hb = h.astype(jnp.bfloat16)

        def proj(c0, c1):
            return (jnp.dot(hb, w_ref[:, c0:c1], preferred_element_type=jnp.float32)
                    + bias_ref[:, c0:c1])

        def rotary(y):
            cos, sa, sb = cos_ref[rows, :], sa_ref[rows, :], sb_ref[rows, :]
            outs = []
            for c in range(y.shape[1] // LANES):
                yc = y[:, c * LANES:(c + 1) * LANES]
                outs.append(yc * cos
                            + pltpu.roll(yc, half, 1) * sa
                            + pltpu.roll(yc, LANES - half, 1) * sb)
            return outs[0] if len(outs) == 1 else jnp.concatenate(outs, axis=1)

        qa_ref[rows, :] = (proj(_C_QA, _C_KA) * ATTN_SCALE).astype(qa_ref.dtype)
        ka_ref[rows, :] = proj(_C_KA, _C_VA).astype(ka_ref.dtype)
        va_ref[rows, :] = proj(_C_VA, _C_QB).astype(va_ref.dtype)
        qb_ref[rows, :] = (rotary(proj(_C_QB, _C_KB)) * ATTN_SCALE).astype(qb_ref.dtype)
        kv = proj(_C_KB, _C_G)
        kb_ref[rows, :] = _dup_heads(rotary(kv[:, :SWA_KV_WIDTH])).astype(kb_ref.dtype)
        vb_ref[rows, :] = _dup_heads(kv[:, SWA_KV_WIDTH:]).astype(vb_ref.dtype)


def _inproj(x2, g, b, w, bias, cos_t, sa_t, sb_t, batch, seq, tm, sub_rows):
    t = batch * seq
    nblk = seq // tm
    tok = lambda bi, i: (bi * nblk + i, 0)
    pos = lambda bi, i: (i, 0)
    bf = jnp.bfloat16
    out_shape = (
        jax.ShapeDtypeStruct((t, NA_WIDTH), bf), jax.ShapeDtypeStruct((t, NA_WIDTH), bf),
        jax.ShapeDtypeStruct((t, NA_WIDTH), bf), jax.ShapeDtypeStruct((t, SWA_Q_WIDTH), bf),
        jax.ShapeDtypeStruct((t, _KV_DUP), bf), jax.ShapeDtypeStruct((t, _KV_DUP), bf),
    )
    return pl.pallas_call(
        functools.partial(_inproj_kernel, sub_rows=sub_rows),
        grid=(batch, nblk),
        in_specs=[
            pl.BlockSpec((tm, D_MODEL), tok),
            _const_spec((1, D_MODEL)), _const_spec((1, D_MODEL)),
            _const_spec((D_MODEL, _C_G)), _const_spec((1, _C_END)),
            pl.BlockSpec((tm, LANES), pos), pl.BlockSpec((tm, LANES), pos),
            pl.BlockSpec((tm, LANES), pos),
        ],
        out_specs=(
            pl.BlockSpec((tm, NA_WIDTH), tok), pl.BlockSpec((tm, NA_WIDTH), tok),
            pl.BlockSpec((tm, NA_WIDTH), tok), pl.BlockSpec((tm, SWA_Q_WIDTH), tok),
            pl.BlockSpec((tm, _KV_DUP), tok), pl.BlockSpec((tm, _KV_DUP), tok),
        ),
        out_shape=out_shape,
        compiler_params=pltpu.CompilerParams(
            dimension_semantics=("parallel", "parallel"), vmem_limit_bytes=VMEM_LIMIT),
        name="ln_inproj_rotary",
    )(x2, g, b, w, bias, cos_t, sa_t, sb_t)


def _rotary_tables(seq):
    half = ROPE_DIM // 2
    inv_freq = np.power(ROPE_THETA, -np.arange(0, ROPE_DIM, 2, dtype=np.float64) / ROPE_DIM)
    ang = np.arange(seq, dtype=np.float64)[:, None] * inv_freq[None, :]
    cos, sin = np.cos(ang), np.sin(ang)
    ones = np.ones((seq, HEAD_DIM - ROPE_DIM))
    zeros = np.zeros((seq, HEAD_DIM - ROPE_DIM))
    zh = np.zeros((seq, half))
    cos_h = np.concatenate([cos, cos, ones], axis=1)
    sa_h = np.concatenate([zh, sin, zeros], axis=1)
    sb_h = np.concatenate([-sin, zh, zeros], axis=1)
    tile = lambda a: jnp.asarray(np.tile(a, (1, HEADS_PER_VREG)), dtype=jnp.float32)
    return tile(cos_h), tile(sa_h), tile(sb_h)


def _stack_heads(q):
    lane = lax.broadcasted_iota(jnp.int32, q.shape, 1)
    zero = jnp.zeros_like(q)
    return jnp.concatenate([jnp.where(lane < HEAD_DIM, q, zero),
                            jnp.where(lane >= HEAD_DIM, q, zero)], axis=0)


def _unstack_heads(o2):
    m = o2.shape[0] // 2
    lane = lax.broadcasted_iota(jnp.int32, (m, LANES), 1)
    return jnp.where(lane < HEAD_DIM, o2[:m], o2[m:])


_NT = (((1,), (1,)), ((), ()))


def _na_bias_kernel(rpb_ref, o_ref):
    w = lax.broadcasted_iota(jnp.int32, (GRID_W, LANES), 0)
    lane = lax.broadcasted_iota(jnp.int32, (GRID_W, LANES), 1)
    c = lane % GRID_W
    col_start = jnp.clip(w - NA_KW // 2, 0, GRID_W - NA_KW)
    inside = (c >= col_start) & (c < col_start + NA_KW)
    lower = lane < GRID_W
    n_dr = 2 * NA_KH - 1
    for e in range(HEADS_PER_VREG):
        lo, hi = [], []
        for dr in range(n_dr):
            x = jnp.broadcast_to(rpb_ref[0, e, dr:dr + 1, :], (GRID_W, LANES))
            lo.append(pltpu.roll(x, LANES - (NA_KW - 1), 1, stride=1, stride_axis=0))
            hi.append(pltpu.roll(x, GRID_W - (NA_KW - 1), 1, stride=1, stride_axis=0))
        for v in range(NA_KH):
            for kk in range(NA_KH // HEADS_PER_VREG):
                dr_a = 2 * kk - v + NA_KH - 1
                tile = jnp.where(inside, jnp.where(lower, lo[dr_a], hi[dr_a + 1]), MASK_VALUE)
                o_ref[v, 0, e * GRID_W:(e + 1) * GRID_W, kk * LANES:(kk + 1) * LANES] = tile


def _na_bias_table(rpb):
    n_pairs = NA_HEADS // HEADS_PER_VREG
    n_dr, n_dc = rpb.shape[1], rpb.shape[2]
    dr_pad = -(-n_dr // 8) * 8
    rp = jnp.pad(rpb, ((0, 0), (0, dr_pad - n_dr), (0, LANES - n_dc)))
    rp = rp.reshape(n_pairs, HEADS_PER_VREG, dr_pad, LANES)
    m2, win = HEADS_PER_VREG * GRID_W, NA_KH * GRID_W
    return pl.pallas_call(
        _na_bias_kernel,
        grid=(n_pairs,),
        in_specs=[pl.BlockSpec((1, HEADS_PER_VREG, dr_pad, LANES), lambda j: (j, 0, 0, 0))],
        out_specs=pl.BlockSpec((NA_KH, 1, m2, win), lambda j: (0, j, 0, 0)),
        out_shape=jax.ShapeDtypeStruct((NA_KH, n_pairs, m2, win), jnp.float32),
        compiler_params=pltpu.CompilerParams(dimension_semantics=("parallel",)),
        name="na_bias_table",
    )(rp)


def _na_kernel(q_ref, k_ref, v_ref, tbl_ref, o_ref, s_scr, p_scr, l_scr, *,
               rows_per_step, n_rows):
    i = pl.program_id(2)
    win = NA_KH * GRID_W

    def window(t):
        r = i * rows_per_step + t
        rs = jnp.clip(r - NA_KH // 2, 0, n_rows - NA_KH)
        return r - rs, pl.multiple_of(rs * GRID_W, GRID_W)

    for t in range(rows_per_step):
        variant, start = window(t)
        lhs = _stack_heads(q_ref[0, t * GRID_W:(t + 1) * GRID_W, :])
        kw = k_ref[0, pl.ds(start, win), :]
        s = lax.dot_general(lhs, kw, _NT, preferred_element_type=jnp.float32)
        s_scr[t] = s + tbl_ref[variant, 0]
    for t in range(rows_per_step):
        s = s_scr[t]
        m = jnp.max(s, axis=-1, keepdims=True)
        p = jnp.exp(s - m)
        l_scr[t] = jnp.broadcast_to(jnp.sum(p, axis=-1, keepdims=True), l_scr.shape[1:])
        p_scr[t] = p.astype(p_scr.dtype)
    for t in range(rows_per_step):
        _, start = window(t)
        vw = v_ref[0, pl.ds(start, win), :]
        o2 = jnp.dot(p_scr[t], vw, preferred_element_type=jnp.float32)
        o_ref[0, t * GRID_W:(t + 1) * GRID_W, :] = (
            _unstack_heads(o2 / l_scr[t]).astype(o_ref.dtype))


def _na_attention(qa, ka, va, tbl, rows_per_step):
    batch, seq, _ = qa.shape
    n_rows = seq // GRID_W
    n_pairs = NA_HEADS // HEADS_PER_VREG
    tq = rows_per_step * GRID_W
    m2, win = HEADS_PER_VREG * GRID_W, NA_KH * GRID_W
    kern = functools.partial(_na_kernel, rows_per_step=rows_per_step, n_rows=n_rows)
    return pl.pallas_call(
        kern,
        grid=(batch, n_pairs, n_rows // rows_per_step),
        in_specs=[
            pl.BlockSpec((1, tq, LANES), lambda b, j, i: (b, i, j)),
            pl.BlockSpec((1, seq, LANES), lambda b, j, i: (b, 0, j)),
            pl.BlockSpec((1, seq, LANES), lambda b, j, i: (b, 0, j)),
            pl.BlockSpec((NA_KH, 1, m2, win), lambda b, j, i: (0, j, 0, 0)),
        ],
        out_specs=pl.BlockSpec((1, tq, LANES), lambda b, j, i: (b, i, j)),
        out_shape=jax.ShapeDtypeStruct(qa.shape, jnp.bfloat16),
        scratch_shapes=[
            pltpu.VMEM((rows_per_step, m2, win), jnp.float32),
            pltpu.VMEM((rows_per_step, m2, win), jnp.bfloat16),
            pltpu.VMEM((rows_per_step, m2, LANES), jnp.float32),
        ],
        compiler_params=pltpu.CompilerParams(
            dimension_semantics=("parallel", "parallel", "parallel"),
            vmem_limit_bytes=VMEM_LIMIT),
        name="na_attention",
    )(qa, ka, va, tbl)


def _swa_mask_table():
    qi = np.arange(SWA_BLOCK)[:, None]
    kc = np.arange(3 * SWA_BLOCK)[None, :]
    tbl = [np.where(np.abs(kc - d * SWA_BLOCK - qi) <= SWA_WINDOW, 0.0, MASK_VALUE)
           for d in range(3)]
    return jnp.asarray(np.stack(tbl), dtype=jnp.float32)


def _swa_kernel(sink_ref, q_ref, k_ref, v_ref, mask_ref, o_ref, s_scr, p_scr, l_scr, *,
                blocks_per_step, seq):
    j = pl.program_id(1)
    n0 = pl.program_id(2) * blocks_per_step
    blk = SWA_BLOCK
    win = 3 * blk
    row = lax.broadcasted_iota(jnp.int32, (2 * blk, 1), 0)
    sink_col = jnp.where(row < blk, sink_ref[2 * j], sink_ref[2 * j + 1])

    def window(u):
        n = n0 + u
        start = jnp.clip((n - 1) * blk, 0, seq - win)
        return n - start // blk, pl.multiple_of(start, blk)

    for u in range(blocks_per_step):
        variant, start = window(u)
        lhs = _stack_heads(q_ref[0, u * blk:(u + 1) * blk, :])
        kw = k_ref[0, pl.ds(start, win), :]
        s = lax.dot_general(lhs, kw, _NT, preferred_element_type=jnp.float32)
        mask = mask_ref[variant]
        s_scr[u] = s + jnp.concatenate([mask, mask], axis=0)
    for u in range(blocks_per_step):
        s = s_scr[u]
        m = jnp.maximum(jnp.max(s, axis=-1, keepdims=True), sink_col)
        e = jnp.exp(s - m)
        denom = jnp.sum(e, axis=-1, keepdims=True) + jnp.exp(sink_col - m)
        l_scr[u] = jnp.broadcast_to(denom, l_scr.shape[1:])
        p_scr[u] = e.astype(p_scr.dtype)
    for u in range(blocks_per_step):
        _, start = window(u)
        vw = v_ref[0, pl.ds(start, win), :]
        o2 = jnp.dot(p_scr[u], vw, preferred_element_type=jnp.float32)
        o_ref[0, u * blk:(u + 1) * blk, :] = _unstack_heads(o2 / l_scr[u]).astype(o_ref.dtype)


def _swa_attention(qb, kb, vb, sink, blocks_per_step):
    batch, seq, _ = qb.shape
    n_pairs = SWA_Q_HEADS // HEADS_PER_VREG
    pairs_per_group = n_pairs // SWA_KV_HEADS
    tq = blocks_per_step * SWA_BLOCK
    m2, win = HEADS_PER_VREG * SWA_BLOCK, 3 * SWA_BLOCK
    kern = functools.partial(_swa_kernel, blocks_per_step=blocks_per_step, seq=seq)
    return pl.pallas_call(
        kern,
        grid=(batch, n_pairs, seq // tq),
        in_specs=[
            pl.BlockSpec(memory_space=pltpu.SMEM),
            pl.BlockSpec((1, tq, LANES), lambda b, j, n: (b, n, j)),
            pl.BlockSpec((1, seq, LANES), lambda b, j, n: (b, 0, j // pairs_per_group)),
            pl.BlockSpec((1, seq, LANES), lambda b, j, n: (b, 0, j // pairs_per_group)),
            _const_spec((3, SWA_BLOCK, 3 * SWA_BLOCK)),
        ],
        out_specs=pl.BlockSpec((1, tq, LANES), lambda b, j, n: (b, n, j)),
        out_shape=jax.ShapeDtypeStruct(qb.shape, jnp.bfloat16),
        scratch_shapes=[
            pltpu.VMEM((blocks_per_step, m2, win), jnp.float32),
            pltpu.VMEM((blocks_per_step, m2, win), jnp.bfloat16),
            pltpu.VMEM((blocks_per_step, m2, LANES), jnp.float32),
        ],
        compiler_params=pltpu.CompilerParams(
            dimension_semantics=("parallel", "parallel", "parallel"),
            vmem_limit_bytes=VMEM_LIMIT),
        name="swa_attention",
    )(sink, qb, kb, vb, _swa_mask_table())


def _mix_kernel(x_ref, g0_ref, b0_ref, na_ref, swa_ref, wg_ref, bin_ref, wna_ref, wswa_ref,
                wout_ref, bout_ref, g1_ref, b1_ref, o_ref, *, sub_rows):
    for r0 in range(0, x_ref.shape[0], sub_rows):
        rows = slice(r0, r0 + sub_rows)
        h = _layer_norm(x_ref[rows, :], g0_ref[...], b0_ref[...])
        gates = (jnp.dot(h.astype(jnp.bfloat16), wg_ref[...],
                         preferred_element_type=jnp.float32) + bin_ref[:, _C_G:_C_END])
        y_na = jnp.dot(na_ref[rows, :], wna_ref[...], preferred_element_type=jnp.float32)
        y_swa = jnp.dot(swa_ref[rows, :], wswa_ref[...], preferred_element_type=jnp.float32)
        mixed = (jax.nn.sigmoid(gates[:, :D_MODEL]) * y_na
                 + jax.nn.sigmoid(gates[:, D_MODEL:]) * y_swa)
        attn = jnp.dot(mixed.astype(jnp.bfloat16), wout_ref[...],
                       preferred_element_type=jnp.float32) + bout_ref[...]
        o_ref[rows, :] = _layer_norm(DEEPNORM_ALPHA * h + attn, g1_ref[...], b1_ref[...])


def _mix(x2, g0, b0, na, swa, wg, b_in, wna, wswa, wout, bout, g1, b1, tm, sub_rows):
    t = x2.shape[0]
    tok = lambda i: (i, 0)
    vec = _const_spec((1, D_MODEL))
    return pl.pallas_call(
        functools.partial(_mix_kernel, sub_rows=sub_rows),
        grid=(t // tm,),
        in_specs=[
            pl.BlockSpec((tm, D_MODEL), tok), vec, vec,
            pl.BlockSpec((tm, NA_WIDTH), tok), pl.BlockSpec((tm, SWA_Q_WIDTH), tok),
            _const_spec((D_MODEL, 2 * D_MODEL)), _const_spec((1, _C_END)),
            _const_spec((NA_WIDTH, D_MODEL)), _const_spec((SWA_Q_WIDTH, D_MODEL)),
            _const_spec((D_MODEL, D_MODEL)), vec, vec, vec,
        ],
        out_specs=pl.BlockSpec((tm, D_MODEL), tok),
        out_shape=jax.ShapeDtypeStruct((t, D_MODEL), jnp.float32),
        compiler_params=pltpu.CompilerParams(
            dimension_semantics=("parallel",), vmem_limit_bytes=VMEM_LIMIT),
        name="mix_outproj_ln",
    )(x2, g0, b0, na, swa, wg, b_in, wna, wswa, wout, bout, g1, b1)


def _ffn_kernel(h_ref, w1_ref, b1_ref, w2_ref, b2_ref, g_ref, b_ref, o_ref, *,
                ff_chunk, sub_rows):
    for r0 in range(0, h_ref.shape[0], sub_rows):
        rows = slice(r0, r0 + sub_rows)
        h = h_ref[rows, :]
        hb = h.astype(jnp.bfloat16)
        acc = jnp.zeros(h.shape, jnp.float32)
        for c in range(D_FF // ff_chunk):
            sl = slice(c * ff_chunk, (c + 1) * ff_chunk)
            u = jnp.dot(hb, w1_ref[:, sl], preferred_element_type=jnp.float32) + b1_ref[:, sl]
            u = jnp.square(jnp.maximum(u, 0.0))
            acc = acc + jnp.dot(u.astype(jnp.bfloat16), w2_ref[sl, :],
                                preferred_element_type=jnp.float32)
        o_ref[rows, :] = _layer_norm(DEEPNORM_ALPHA * h + acc + b2_ref[...],
                                     g_ref[...], b_ref[...])


def _ffn(h1, w1, b1, w2, b2, g, b, tm, sub_rows, ff_chunk):
    t = h1.shape[0]
    tok = lambda i: (i, 0)
    vec = _const_spec((1, D_MODEL))
    return pl.pallas_call(
        functools.partial(_ffn_kernel, ff_chunk=ff_chunk, sub_rows=sub_rows),
        grid=(t // tm,),
        in_specs=[
            pl.BlockSpec((tm, D_MODEL), tok),
            _const_spec((D_MODEL, D_FF)), _const_spec((1, D_FF)),
            _const_spec((D_FF, D_MODEL)), vec, vec, vec,
        ],
        out_specs=pl.BlockSpec((tm, D_MODEL), tok),
        out_shape=jax.ShapeDtypeStruct((t, D_MODEL), jnp.float32),
        compiler_params=pltpu.CompilerParams(
            dimension_semantics=("parallel",), vmem_limit_bytes=VMEM_LIMIT),
        name="ffn_ln",
    )(h1, w1, b1, w2, b2, g, b)


def kernel(x, ln0_g, ln0_b, w_in, b_in, na_rpb, swa_sink, w_branch_na, w_branch_swa,
           w_out, b_out, ln1_g, ln1_b, w_ff1, b_ff1, w_ff2, b_ff2, ln2_g, ln2_b):
    batch, seq, d = x.shape
    assert d == D_MODEL and seq % (NA_KH * GRID_W) == 0 and seq >= 3 * SWA_BLOCK
    assert w_in.shape == (DEPTH, D_MODEL, _C_END) and DEPTH == 1
    bf = jnp.bfloat16
    row = lambda v: v.reshape(1, -1)
    x2 = x.reshape(batch * seq, d)
    b_in_row = row(b_in[0])

    cos_t, sa_t, sb_t = _rotary_tables(seq)
    qa, ka, va, qb, kb, vb = _inproj(
        x2, row(ln0_g), row(ln0_b), w_in[0, :, :_C_G].astype(bf), b_in_row,
        cos_t, sa_t, sb_t, batch, seq, tm=1024, sub_rows=256)

    three = lambda a: a.reshape(batch, seq, a.shape[-1])
    o_na = _na_attention(three(qa), three(ka), three(va), _na_bias_table(na_rpb[0]),
                         rows_per_step=16)
    o_swa = _swa_attention(three(qb), three(kb), three(vb), swa_sink[0], blocks_per_step=8)

    h1 = _mix(x2, row(ln0_g), row(ln0_b), o_na.reshape(batch * seq, -1),
              o_swa.reshape(batch * seq, -1), w_in[0, :, _C_G:].astype(bf), b_in_row,
              w_branch_na[0].astype(bf), w_branch_swa[0].astype(bf), w_out[0].astype(bf),
              row(b_out[0]), row(ln1_g[0]), row(ln1_b[0]), tm=1024, sub_rows=256)
    out = _ffn(h1, w_ff1[0].astype(bf), row(b_ff1[0]), w_ff2[0].astype(bf), row(b_ff2[0]),
               row(ln2_g[0]), row(ln2_b[0]), tm=1024, sub_rows=256, ff_chunk=1024)
    return out.reshape(batch, seq, d)
```

```python
import functools

import numpy as np
import jax
import jax.numpy as jnp
from jax import lax
from jax.experimental import pallas as pl
from jax.experimental.pallas import tpu as pltpu

D_MODEL = 1024
GRID_W = 64
HEAD_DIM = 64
NA_HEADS = 8
NA_KH = 8
NA_KW = 16
NA_WIDTH = NA_HEADS * HEAD_DIM
SWA_Q_HEADS = 8
SWA_KV_HEADS = 2
SWA_WINDOW = 128
SWA_BLOCK = 128
SWA_Q_WIDTH = SWA_Q_HEADS * HEAD_DIM
SWA_KV_WIDTH = SWA_KV_HEADS * HEAD_DIM
ROPE_THETA = 500000.0
ROPE_DIM = HEAD_DIM // 4
D_FF = 4 * D_MODEL
DEPTH = 1
DEEPNORM_ALPHA = (2.0 * DEPTH) ** 0.25
LN_EPS = 1e-5
MASK_VALUE = -1e30
ATTN_SCALE = HEAD_DIM ** -0.5

LANES = 128
HEADS_PER_VREG = LANES // HEAD_DIM
VMEM_LIMIT = 56 * 1024 * 1024

_C_QA = 0
_C_KA = _C_QA + NA_WIDTH
_C_VA = _C_KA + NA_WIDTH
_C_QB = _C_VA + NA_WIDTH
_C_KB = _C_QB + SWA_Q_WIDTH
_C_VB = _C_KB + SWA_KV_WIDTH
_C_G = _C_VB + SWA_KV_WIDTH
_C_END = _C_G + 2 * D_MODEL
_KV_DUP = SWA_KV_HEADS * LANES


def _const_spec(shape):
    nd = len(shape)
    return pl.BlockSpec(shape, lambda *_: (0,) * nd, pipeline_mode=pl.Buffered(1))


def _layer_norm(x, g, b):
    mu = jnp.mean(x, axis=-1, keepdims=True)
    xc = x - mu
    var = jnp.mean(xc * xc, axis=-1, keepdims=True)
    return xc * lax.rsqrt(var + LN_EPS) * g + b


def _dup_heads(y):
    lane = lax.broadcasted_iota(jnp.int32, y.shape, 1)
    swapped = pltpu.roll(y, HEAD_DIM, 1)
    return jnp.concatenate([jnp.where(lane < HEAD_DIM, y, swapped),
                            jnp.where(lane >= HEAD_DIM, y, swapped)], axis=1)


def _inproj_kernel(x_ref, g_ref, b_ref, w_ref, bias_ref, cos_ref, sa_ref, sb_ref,
                   qa_ref, ka_ref, va_ref, qb_ref, kb_ref, vb_ref, *, sub_rows):
    half = ROPE_DIM // 2

    for r0 in range(0, x_ref.shape[0], sub_rows):
        rows = slice(r0, r0 + sub_rows)
        h = _layer_norm(x_ref[rows, :], g_ref[...], b_ref[...])
        hb = h.astype(jnp.bfloat16)

        def proj(c0, c1):
            return (jnp.dot(hb, w_ref[:, c0:c1], preferred_element_type=jnp.float32)
                    + bias_ref[:, c0:c1])

        def rotary(y):
            cos, sa, sb = cos_ref[rows, :], sa_ref[rows, :], sb_ref[rows, :]
            outs = []
            for c in range(y.shape[1] // LANES):
                yc = y[:, c * LANES:(c + 1) * LANES]
                outs.append(yc * cos
                            + pltpu.roll(yc, half, 1) * sa
                            + pltpu.roll(yc, LANES - half, 1) * sb)
            return outs[0] if len(outs) == 1 else jnp.concatenate(outs, axis=1)

        qa_ref[rows, :] = (proj(_C_QA, _C_KA) * ATTN_SCALE).astype(qa_ref.dtype)
        ka_ref[rows, :] = proj(_C_KA, _C_VA).astype(ka_ref.dtype)
        va_ref[rows, :] = proj(_C_VA, _C_QB).astype(va_ref.dtype)
        qb_ref[rows, :] = (rotary(proj(_C_QB, _C_KB)) * ATTN_SCALE).astype(qb_ref.dtype)
        kv = proj(_C_KB, _C_G)
        kb_ref[rows, :] = _dup_heads(rotary(kv[:, :SWA_KV_WIDTH])).astype(kb_ref.dtype)
        vb_ref[rows, :] = _dup_heads(kv[:, SWA_KV_WIDTH:]).astype(vb_ref.dtype)


def _inproj(x2, g, b, w, bias, cos_t, sa_t, sb_t, batch, seq, tm, sub_rows):
    t = batch * seq
    nblk = seq // tm
    tok = lambda bi, i: (bi * nblk + i, 0)
    pos = lambda bi, i: (i, 0)
    bf = jnp.bfloat16
    out_shape = (
        jax.ShapeDtypeStruct((t, NA_WIDTH), bf), jax.ShapeDtypeStruct((t, NA_WIDTH), bf),
        jax.ShapeDtypeStruct((t, NA_WIDTH), bf), jax.ShapeDtypeStruct((t, SWA_Q_WIDTH), bf),
        jax.ShapeDtypeStruct((t, _KV_DUP), bf), jax.ShapeDtypeStruct((t, _KV_DUP), bf),
    )
    return pl.pallas_call(
        functools.partial(_inproj_kernel, sub_rows=sub_rows),
        grid=(batch, nblk),
        in_specs=[
            pl.BlockSpec((tm, D_MODEL), tok),
            _const_spec((1, D_MODEL)), _const_spec((1, D_MODEL)),
            _const_spec((D_MODEL, _C_G)), _const_spec((1, _C_END)),
            pl.BlockSpec((tm, LANES), pos), pl.BlockSpec((tm, LANES), pos),
            pl.BlockSpec((tm, LANES), pos),
        ],
        out_specs=(
            pl.BlockSpec((tm, NA_WIDTH), tok), pl.BlockSpec((tm, NA_WIDTH), tok),
            pl.BlockSpec((tm, NA_WIDTH), tok), pl.BlockSpec((tm, SWA_Q_WIDTH), tok),
            pl.BlockSpec((tm, _KV_DUP), tok), pl.BlockSpec((tm, _KV_DUP), tok),
        ),
        out_shape=out_shape,
        compiler_params=pltpu.CompilerParams(
            dimension_semantics=("parallel", "parallel"), vmem_limit_bytes=VMEM_LIMIT),
        name="ln_inproj_rotary",
    )(x2, g, b, w, bias, cos_t, sa_t, sb_t)


def _rotary_tables(seq):
    half = ROPE_DIM // 2
    inv_freq = np.power(ROPE_THETA, -np.arange(0, ROPE_DIM, 2, dtype=np.float64) / ROPE_DIM)
    ang = np.arange(seq, dtype=np.float64)[:, None] * inv_freq[None, :]
    cos, sin = np.cos(ang), np.sin(ang)
    ones = np.ones((seq, HEAD_DIM - ROPE_DIM))
    zeros = np.zeros((seq, HEAD_DIM - ROPE_DIM))
    zh = np.zeros((seq, half))
    cos_h = np.concatenate([cos, cos, ones], axis=1)
    sa_h = np.concatenate([zh, sin, zeros], axis=1)
    sb_h = np.concatenate([-sin, zh, zeros], axis=1)
    tile = lambda a: jnp.asarray(np.tile(a, (1, HEADS_PER_VREG)), dtype=jnp.float32)
    return tile(cos_h), tile(sa_h), tile(sb_h)


def _stack_heads(q):
    lane = lax.broadcasted_iota(jnp.int32, q.shape, 1)
    zero = jnp.zeros_like(q)
    return jnp.concatenate([jnp.where(lane < HEAD_DIM, q, zero),
                            jnp.where(lane >= HEAD_DIM, q, zero)], axis=0)


def _unstack_heads(o2):
    m = o2.shape[0] // 2
    lane = lax.broadcasted_iota(jnp.int32, (m, LANES), 1)
    return jnp.where(lane < HEAD_DIM, o2[:m], o2[m:])


_NT = (((1,), (1,)), ((), ()))


def _na_bias_kernel(rpb_ref, o_ref):
    w = lax.broadcasted_iota(jnp.int32, (GRID_W, LANES), 0)
    lane = lax.broadcasted_iota(jnp.int32, (GRID_W, LANES), 1)
    c = lane % GRID_W
    col_start = jnp.clip(w - NA_KW // 2, 0, GRID_W - NA_KW)
    inside = (c >= col_start) & (c < col_start + NA_KW)
    lower = lane < GRID_W
    n_dr = 2 * NA_KH - 1
    for e in range(HEADS_PER_VREG):
        lo, hi = [], []
        for dr in range(n_dr):
            x = jnp.broadcast_to(rpb_ref[0, e, dr:dr + 1, :], (GRID_W, LANES))
            lo.append(pltpu.roll(x, LANES - (NA_KW - 1), 1, stride=1, stride_axis=0))
            hi.append(pltpu.roll(x, GRID_W - (NA_KW - 1), 1, stride=1, stride_axis=0))
        for v in range(NA_KH):
            for kk in range(NA_KH // HEADS_PER_VREG):
                dr_a = 2 * kk - v + NA_KH - 1
                tile = jnp.where(inside, jnp.where(lower, lo[dr_a], hi[dr_a + 1]), MASK_VALUE)
                o_ref[v, 0, e * GRID_W:(e + 1) * GRID_W, kk * LANES:(kk + 1) * LANES] = tile


def _na_bias_table(rpb):
    n_pairs = NA_HEADS // HEADS_PER_VREG
    n_dr, n_dc = rpb.shape[1], rpb.shape[2]
    dr_pad = -(-n_dr // 8) * 8
    rp = jnp.pad(rpb, ((0, 0), (0, dr_pad - n_dr), (0, LANES - n_dc)))
    rp = rp.reshape(n_pairs, HEADS_PER_VREG, dr_pad, LANES)
    m2, win = HEADS_PER_VREG * GRID_W, NA_KH * GRID_W
    return pl.pallas_call(
        _na_bias_kernel,
        grid=(n_pairs,),
        in_specs=[pl.BlockSpec((1, HEADS_PER_VREG, dr_pad, LANES), lambda j: (j, 0, 0, 0))],
        out_specs=pl.BlockSpec((NA_KH, 1, m2, win), lambda j: (0, j, 0, 0)),
        out_shape=jax.ShapeDtypeStruct((NA_KH, n_pairs, m2, win), jnp.float32),
        compiler_params=pltpu.CompilerParams(dimension_semantics=("parallel",)),
        name="na_bias_table",
    )(rp)


def _na_kernel(q_ref, k_ref, v_ref, tbl_ref, o_ref, s_scr, p_scr, l_scr, *,
               rows_per_step, n_rows):
    i = pl.program_id(2)
    win = NA_KH * GRID_W

    def window(t):
        r = i * rows_per_step + t
        rs = jnp.clip(r - NA_KH // 2, 0, n_rows - NA_KH)
        return r - rs, pl.multiple_of(rs * GRID_W, GRID_W)

    for t in range(rows_per_step):
        variant, start = window(t)
        lhs = _stack_heads(q_ref[0, t * GRID_W:(t + 1) * GRID_W, :])
        kw = k_ref[0, pl.ds(start, win), :]
        s = lax.dot_general(lhs, kw, _NT, preferred_element_type=jnp.float32)
        s_scr[t] = s + tbl_ref[variant, 0]
    for t in range(rows_per_step):
        s = s_scr[t]
        m = jnp.max(s, axis=-1, keepdims=True)
        p = jnp.exp(s - m)
        l_scr[t] = jnp.broadcast_to(jnp.sum(p, axis=-1, keepdims=True), l_scr.shape[1:])
        p_scr[t] = p.astype(p_scr.dtype)
    for t in range(rows_per_step):
        _, start = window(t)
        vw = v_ref[0, pl.ds(start, win), :]
        o2 = jnp.dot(p_scr[t], vw, preferred_element_type=jnp.float32)
        o_ref[0, t * GRID_W:(t + 1) * GRID_W, :] = (
            _unstack_heads(o2 / l_scr[t]).astype(o_ref.dtype))


def _na_attention(qa, ka, va, tbl, rows_per_step):
    batch, seq, _ = qa.shape
    n_rows = seq // GRID_W
    n_pairs = NA_HEADS // HEADS_PER_VREG
    tq = rows_per_step * GRID_W
    m2, win = HEADS_PER_VREG * GRID_W, NA_KH * GRID_W
    kern = functools.partial(_na_kernel, rows_per_step=rows_per_step, n_rows=n_rows)
    return pl.pallas_call(
        kern,
        grid=(batch, n_pairs, n_rows // rows_per_step),
        in_specs=[
            pl.BlockSpec((1, tq, LANES), lambda b, j, i: (b, i, j)),
            pl.BlockSpec((1, seq, LANES), lambda b, j, i: (b, 0, j)),
            pl.BlockSpec((1, seq, LANES), lambda b, j, i: (b, 0, j)),
            pl.BlockSpec((NA_KH, 1, m2, win), lambda b, j, i: (0, j, 0, 0)),
        ],
        out_specs=pl.BlockSpec((1, tq, LANES), lambda b, j, i: (b, i, j)),
        out_shape=jax.ShapeDtypeStruct(qa.shape, jnp.bfloat16),
        scratch_shapes=[
            pltpu.VMEM((rows_per_step, m2, win), jnp.float32),
            pltpu.VMEM((rows_per_step, m2, win), jnp.bfloat16),
            pltpu.VMEM((rows_per_step, m2, LANES), jnp.float32),
        ],
        compiler_params=pltpu.CompilerParams(
            dimension_semantics=("parallel", "parallel", "parallel"),
            vmem_limit_bytes=VMEM_LIMIT),
        name="na_attention",
    )(qa, ka, va, tbl)


def _swa_mask_table():
    qi = np.arange(SWA_BLOCK)[:, None]
    kc = np.arange(3 * SWA_BLOCK)[None, :]
    tbl = [np.where(np.abs(kc - d * SWA_BLOCK - qi) <= SWA_WINDOW, 0.0, MASK_VALUE)
           for d in range(3)]
    return jnp.asarray(np.stack(tbl), dtype=jnp.float32)


def _swa_kernel(sink_ref, q_ref, k_ref, v_ref, mask_ref, o_ref, s_scr, p_scr, l_scr, *,
                blocks_per_step, seq):
    j = pl.program_id(1)
    n0 = pl.program_id(2) * blocks_per_step
    blk = SWA_BLOCK
    win = 3 * blk
    row = lax.broadcasted_iota(jnp.int32, (2 * blk, 1), 0)
    sink_col = jnp.where(row < blk, sink_ref[2 * j], sink_ref[2 * j + 1])

    def window(u):
        n = n0 + u
        start = jnp.clip((n - 1) * blk, 0, seq - win)
        return n - start // blk, pl.multiple_of(start, blk)

    for u in range(blocks_per_step):
        variant, start = window(u)
        lhs = _stack_heads(q_ref[0, u * blk:(u + 1) * blk, :])
        kw = k_ref[0, pl.ds(start, win), :]
        s = lax.dot_general(lhs, kw, _NT, preferred_element_type=jnp.float32)
        mask = mask_ref[variant]
        s_scr[u] = s + jnp.concatenate([mask, mask], axis=0)
    for u in range(blocks_per_step):
        s = s_scr[u]
        m = jnp.maximum(jnp.max(s, axis=-1, keepdims=True), sink_col)
        e = jnp.exp(s - m)
        denom = jnp.sum(e, axis=-1, keepdims=True) + jnp.exp(sink_col - m)
        l_scr[u] = jnp.broadcast_to(denom, l_scr.shape[1:])
        p_scr[u] = e.astype(p_scr.dtype)
    for u in range(blocks_per_step):
        _, start = window(u)
        vw = v_ref[0, pl.ds(start, win), :]
        o2 = jnp.dot(p_scr[u], vw, preferred_element_type=jnp.float32)
        o_ref[0, u * blk:(u + 1) * blk, :] = _unstack_heads(o2 / l_scr[u]).astype(o_ref.dtype)


def _swa_attention(qb, kb, vb, sink, blocks_per_step):
    batch, seq, _ = qb.shape
    n_pairs = SWA_Q_HEADS // HEADS_PER_VREG
    pairs_per_group = n_pairs // SWA_KV_HEADS
    tq = blocks_per_step * SWA_BLOCK
    m2, win = HEADS_PER_VREG * SWA_BLOCK, 3 * SWA_BLOCK
    kern = functools.partial(_swa_kernel, blocks_per_step=blocks_per_step, seq=seq)
    return pl.pallas_call(
        kern,
        grid=(batch, n_pairs, seq // tq),
        in_specs=[
            pl.BlockSpec(memory_space=pltpu.SMEM),
            pl.BlockSpec((1, tq, LANES), lambda b, j, n: (b, n, j)),
            pl.BlockSpec((1, seq, LANES), lambda b, j, n: (b, 0, j // pairs_per_group)),
            pl.BlockSpec((1, seq, LANES), lambda b, j, n: (b, 0, j // pairs_per_group)),
            _const_spec((3, SWA_BLOCK, 3 * SWA_BLOCK)),
        ],
        out_specs=pl.BlockSpec((1, tq, LANES), lambda b, j, n: (b, n, j)),
        out_shape=jax.ShapeDtypeStruct(qb.shape, jnp.bfloat16),
        scratch_shapes=[
            pltpu.VMEM((blocks_per_step, m2, win), jnp.float32),
            pltpu.VMEM((blocks_per_step, m2, win), jnp.bfloat16),
            pltpu.VMEM((blocks_per_step, m2, LANES), jnp.float32),
        ],
        compiler_params=pltpu.CompilerParams(
            dimension_semantics=("parallel", "parallel", "parallel"),
            vmem_limit_bytes=VMEM_LIMIT),
        name="swa_attention",
    )(sink, qb, kb, vb, _swa_mask_table())


def _tail_kernel(x_ref, g0_ref, b0_ref, na_ref, swa_ref, wg_ref, bin_ref, wna_ref, wswa_ref,
                 wout_ref, bout_ref, g1_ref, b1_ref, w1_ref, bf1_ref, w2_ref, bf2_ref,
                 g2_ref, b2_ref, o_ref, *, sub_rows, ff_chunk):
    for r0 in range(0, x_ref.shape[0], sub_rows):
        rows = slice(r0, r0 + sub_rows)
        h = _layer_norm(x_ref[rows, :], g0_ref[...], b0_ref[...])
        gates = (jnp.dot(h.astype(jnp.bfloat16), wg_ref[...],
                         preferred_element_type=jnp.float32) + bin_ref[:, _C_G:_C_END])
        y_na = jnp.dot(na_ref[rows, :], wna_ref[...], preferred_element_type=jnp.float32)
        y_swa = jnp.dot(swa_ref[rows, :], wswa_ref[...], preferred_element_type=jnp.float32)
        mixed = (jax.nn.sigmoid(gates[:, :D_MODEL]) * y_na
                 + jax.nn.sigmoid(gates[:, D_MODEL:]) * y_swa)
        attn = jnp.dot(mixed.astype(jnp.bfloat16), wout_ref[...],
                       preferred_element_type=jnp.float32) + bout_ref[...]
        h = _layer_norm(DEEPNORM_ALPHA * h + attn, g1_ref[...], b1_ref[...])

        hb = h.astype(jnp.bfloat16)
        acc = jnp.zeros(h.shape, jnp.float32)
        for c in range(D_FF // ff_chunk):
            sl = slice(c * ff_chunk, (c + 1) * ff_chunk)
            u = jnp.dot(hb, w1_ref[:, sl], preferred_element_type=jnp.float32) + bf1_ref[:, sl]
            u = jnp.square(jnp.maximum(u, 0.0))
            acc = acc + jnp.dot(u.astype(jnp.bfloat16), w2_ref[sl, :],
                                preferred_element_type=jnp.float32)
        o_ref[rows, :] = _layer_norm(DEEPNORM_ALPHA * h + acc + bf2_ref[...],
                                     g2_ref[...], b2_ref[...])


def _tail(x2, g0, b0, na, swa, wg, b_in, wna, wswa, wout, bout, g1, b1, w1, bf1, w2, bf2,
          g2, b2, tm, sub_rows, ff_chunk):
    t = x2.shape[0]
    tok = lambda i: (i, 0)
    vec = _const_spec((1, D_MODEL))
    return pl.pallas_call(
        functools.partial(_tail_kernel, sub_rows=sub_rows, ff_chunk=ff_chunk),
        grid=(t // tm,),
        in_specs=[
            pl.BlockSpec((tm, D_MODEL), tok), vec, vec,
            pl.BlockSpec((tm, NA_WIDTH), tok), pl.BlockSpec((tm, SWA_Q_WIDTH), tok),
            _const_spec((D_MODEL, 2 * D_MODEL)), _const_spec((1, _C_END)),
            _const_spec((NA_WIDTH, D_MODEL)), _const_spec((SWA_Q_WIDTH, D_MODEL)),
            _const_spec((D_MODEL, D_MODEL)), vec, vec, vec,
            _const_spec((D_MODEL, D_FF)), _const_spec((1, D_FF)),
            _const_spec((D_FF, D_MODEL)), vec, vec, vec,
        ],
        out_specs=pl.BlockSpec((tm, D_MODEL), tok),
        out_shape=jax.ShapeDtypeStruct((t, D_MODEL), jnp.float32),
        compiler_params=pltpu.CompilerParams(
            dimension_semantics=("parallel",), vmem_limit_bytes=VMEM_LIMIT),
        name="mix_ffn",
    )(x2, g0, b0, na, swa, wg, b_in, wna, wswa, wout, bout, g1, b1, w1, bf1, w2, bf2, g2, b2)


def kernel(x, ln0_g, ln0_b, w_in, b_in, na_rpb, swa_sink, w_branch_na, w_branch_swa,
           w_out, b_out, ln1_g, ln1_b, w_ff1, b_ff1, w_ff2, b_ff2, ln2_g, ln2_b):
    batch, seq, d = x.shape
    assert d == D_MODEL and seq % (NA_KH * GRID_W) == 0 and seq >= 3 * SWA_BLOCK
    assert w_in.shape == (DEPTH, D_MODEL, _C_END) and DEPTH == 1
    bf = jnp.bfloat16
    row = lambda v: v.reshape(1, -1)
    x2 = x.reshape(batch * seq, d)
    b_in_row = row(b_in[0])

    cos_t, sa_t, sb_t = _rotary_tables(seq)
    qa, ka, va, qb, kb, vb = _inproj(
        x2, row(ln0_g), row(ln0_b), w_in[0, :, :_C_G].astype(bf), b_in_row,
        cos_t, sa_t, sb_t, batch, seq, tm=1024, sub_rows=256)

    three = lambda a: a.reshape(batch, seq, a.shape[-1])
    o_na = _na_attention(three(qa), three(ka), three(va), _na_bias_table(na_rpb[0]),
                         rows_per_step=16)
    o_swa = _swa_attention(three(qb), three(kb), three(vb), swa_sink[0], blocks_per_step=8)

    out = _tail(x2, row(ln0_g), row(ln0_b), o_na.reshape(batch * seq, -1),
                o_swa.reshape(batch * seq, -1), w_in[0, :, _C_G:].astype(bf), b_in_row,
                w_branch_na[0].astype(bf), w_branch_swa[0].astype(bf), w_out[0].astype(bf),
                row(b_out[0]), row(ln1_g[0]), row(ln1_b[0]),
                w_ff1[0].astype(bf), row(b_ff1[0]), w_ff2[0].astype(bf), row(b_ff2[0]),
                row(ln2_g[0]), row(ln2_b[0]), tm=512, sub_rows=256, ff_chunk=1024)
    return out.reshape(batch, seq, d)
```

```python
import functools

import numpy as np
import jax
import jax.numpy as jnp
from jax import lax
from jax.experimental import pallas as pl
from jax.experimental.pallas import tpu as pltpu

D_MODEL = 1024
GRID_W = 64
HEAD_DIM = 64
NA_HEADS = 8
NA_KH = 8
NA_KW = 16
NA_WIDTH = NA_HEADS * HEAD_DIM
SWA_Q_HEADS = 8
SWA_KV_HEADS = 2
SWA_WINDOW = 128
SWA_BLOCK = 128
SWA_Q_WIDTH = SWA_Q_HEADS * HEAD_DIM
SWA_KV_WIDTH = SWA_KV_HEADS * HEAD_DIM
ROPE_THETA = 500000.0
ROPE_DIM = HEAD_DIM // 4
D_FF = 4 * D_MODEL
DEPTH = 1
DEEPNORM_ALPHA = (2.0 * DEPTH) ** 0.25
LN_EPS = 1e-5
MASK_VALUE = -1e30
ATTN_SCALE = HEAD_DIM ** -0.5

LANES = 128
HEADS_PER_VREG = LANES // HEAD_DIM
VMEM_LIMIT = 56 * 1024 * 1024

_C_QA = 0
_C_KA = _C_QA + NA_WIDTH
_C_VA = _C_KA + NA_WIDTH
_C_QB = _C_VA + NA_WIDTH
_C_KB = _C_QB + SWA_Q_WIDTH
_C_VB = _C_KB + SWA_KV_WIDTH
_C_G = _C_VB + SWA_KV_WIDTH
_C_END = _C_G + 2 * D_MODEL
_KV_DUP = SWA_KV_HEADS * LANES


def _const_spec(shape):
    nd = len(shape)
    return pl.BlockSpec(shape, lambda *_: (0,) * nd, pipeline_mode=pl.Buffered(1))


def _layer_norm(x, g, b):
    mu = jnp.mean(x, axis=-1, keepdims=True)
    xc = x - mu
    var = jnp.mean(xc * xc, axis=-1, keepdims=True)
    return xc * lax.rsqrt(var + LN_EPS) * g + b


def _dup_heads(y):
    lane = lax.broadcasted_iota(jnp.int32, y.shape, 1)
    swapped = pltpu.roll(y, HEAD_DIM, 1)
    return jnp.concatenate([jnp.where(lane < HEAD_DIM, y, swapped),
                            jnp.where(lane >= HEAD_DIM, y, swapped)], axis=1)


def _inproj_kernel(x_ref, g_ref, b_ref, w_ref, bias_ref, cos_ref, sa_ref, sb_ref,
                   qa_ref, ka_ref, va_ref, qb_ref, kb_ref, vb_ref, *, sub_rows):
    half = ROPE_DIM // 2

    for r0 in range(0, x_ref.shape[0], sub_rows):
        rows = slice(r0, r0 + sub_rows)
        h = _layer_norm(x_ref[rows, :], g_ref[...], b_ref[...])
        hb = h.astype(jnp.bfloat16)

        def proj(c0, c1):
            return (jnp.dot(hb, w_ref[:, c0:c1], preferred_element_type=jnp.float32)
                    + bias_ref[:, c0:c1])

        def rotary(y):
            cos, sa, sb = cos_ref[rows, :], sa_ref[rows, :], sb_ref[rows, :]
            outs = []
            for c in range(y.shape[1] // LANES):
                yc = y[:, c * LANES:(c + 1) * LANES]
                outs.append(yc * cos
                            + pltpu.roll(yc, half, 1) * sa
                            + pltpu.roll(yc, LANES - half, 1) * sb)
            return outs[0] if len(outs) == 1 else jnp.concatenate(outs, axis=1)

        qa_ref[rows, :] = (proj(_C_QA, _C_KA) * ATTN_SCALE).astype(qa_ref.dtype)
        ka_ref[rows, :] = proj(_C_KA, _C_VA).astype(ka_ref.dtype)
        va_ref[rows, :] = proj(_C_VA, _C_QB).astype(va_ref.dtype)
        qb_ref[rows, :] = (rotary(proj(_C_QB, _C_KB)) * ATTN_SCALE).astype(qb_ref.dtype)
        kv = proj(_C_KB, _C_G)
        kb_ref[rows, :] = _dup_heads(rotary(kv[:, :SWA_KV_WIDTH])).astype(kb_ref.dtype)
        vb_ref[rows, :] = _dup_heads(kv[:, SWA_KV_WIDTH:]).astype(vb_ref.dtype)


def _inproj(x2, g, b, w, bias, cos_t, sa_t, sb_t, batch, seq, tm, sub_rows):
    t = batch * seq
    nblk = seq // tm
    tok = lambda bi, i: (bi * nblk + i, 0)
    pos = lambda bi, i: (i, 0)
    bf = jnp.bfloat16
    out_shape = (
        jax.ShapeDtypeStruct((t, NA_WIDTH), bf), jax.ShapeDtypeStruct((t, NA_WIDTH), bf),
        jax.ShapeDtypeStruct((t, NA_WIDTH), bf), jax.ShapeDtypeStruct((t, SWA_Q_WIDTH), bf),
        jax.ShapeDtypeStruct((t, _KV_DUP), bf), jax.ShapeDtypeStruct((t, _KV_DUP), bf),
    )
    return pl.pallas_call(
        functools.partial(_inproj_kernel, sub_rows=sub_rows),
        grid=(batch, nblk),
        in_specs=[
            pl.BlockSpec((tm, D_MODEL), tok),
            _const_spec((1, D_MODEL)), _const_spec((1, D_MODEL)),
            _const_spec((D_MODEL, _C_G)), _const_spec((1, _C_END)),
            pl.BlockSpec((tm, LANES), pos), pl.BlockSpec((tm, LANES), pos),
            pl.BlockSpec((tm, LANES), pos),
        ],
        out_specs=(
            pl.BlockSpec((tm, NA_WIDTH), tok), pl.BlockSpec((tm, NA_WIDTH), tok),
            pl.BlockSpec((tm, NA_WIDTH), tok), pl.BlockSpec((tm, SWA_Q_WIDTH), tok),
            pl.BlockSpec((tm, _KV_DUP), tok), pl.BlockSpec((tm, _KV_DUP), tok),
        ),
        out_shape=out_shape,
        compiler_params=pltpu.CompilerParams(
            dimension_semantics=("parallel", "parallel"), vmem_limit_bytes=VMEM_LIMIT),
        name="ln_inproj_rotary",
    )(x2, g, b, w, bias, cos_t, sa_t, sb_t)


def _rotary_tables(seq):
    half = ROPE_DIM // 2
    inv_freq = np.power(ROPE_THETA, -np.arange(0, ROPE_DIM, 2, dtype=np.float64) / ROPE_DIM)
    ang = np.arange(seq, dtype=np.float64)[:, None] * inv_freq[None, :]
    cos, sin = np.cos(ang), np.sin(ang)
    ones = np.ones((seq, HEAD_DIM - ROPE_DIM))
    zeros = np.zeros((seq, HEAD_DIM - ROPE_DIM))
    zh = np.zeros((seq, half))
    cos_h = np.concatenate([cos, cos, ones], axis=1)
    sa_h = np.concatenate([zh, sin, zeros], axis=1)
    sb_h = np.concatenate([-sin, zh, zeros], axis=1)
    tile = lambda a: jnp.asarray(np.tile(a, (1, HEADS_PER_VREG)), dtype=jnp.float32)
    return tile(cos_h), tile(sa_h), tile(sb_h)


def _stack_heads(q):
    lane = lax.broadcasted_iota(jnp.int32, q.shape, 1)
    zero = jnp.zeros_like(q)
    return jnp.concatenate([jnp.where(lane < HEAD_DIM, q, zero),
                            jnp.where(lane >= HEAD_DIM, q, zero)], axis=0)


def _unstack_heads(o2):
    m = o2.shape[0] // 2
    lane = lax.broadcasted_iota(jnp.int32, (m, LANES), 1)
    return jnp.where(lane < HEAD_DIM, o2[:m], o2[m:])


_NT = (((1,), (1,)), ((), ()))


def _na_bias_kernel(rpb_ref, o_ref):
    w = lax.broadcasted_iota(jnp.int32, (GRID_W, LANES), 0)
    lane = lax.broadcasted_iota(jnp.int32, (GRID_W, LANES), 1)
    c = lane % GRID_W
    col_start = jnp.clip(w - NA_KW // 2, 0, GRID_W - NA_KW)
    inside = (c >= col_start) & (c < col_start + NA_KW)
    lower = lane < GRID_W
    n_dr = 2 * NA_KH - 1
    for e in range(HEADS_PER_VREG):
        lo, hi = [], []
        for dr in range(n_dr):
            x = jnp.broadcast_to(rpb_ref[0, e, dr:dr + 1, :], (GRID_W, LANES))
            lo.append(pltpu.roll(x, LANES - (NA_KW - 1), 1, stride=1, stride_axis=0))
            hi.append(pltpu.roll(x, GRID_W - (NA_KW - 1), 1, stride=1, stride_axis=0))
        for v in range(NA_KH):
            for kk in range(NA_KH // HEADS_PER_VREG):
                dr_a = 2 * kk - v + NA_KH - 1
                tile = jnp.where(inside, jnp.where(lower, lo[dr_a], hi[dr_a + 1]), MASK_VALUE)
                o_ref[v, 0, e * GRID_W:(e + 1) * GRID_W, kk * LANES:(kk + 1) * LANES] = tile


def _na_bias_table(rpb):
    n_pairs = NA_HEADS // HEADS_PER_VREG
    n_dr, n_dc = rpb.shape[1], rpb.shape[2]
    dr_pad = -(-n_dr // 8) * 8
    rp = jnp.pad(rpb, ((0, 0), (0, dr_pad - n_dr), (0, LANES - n_dc)))
    rp = rp.reshape(n_pairs, HEADS_PER_VREG, dr_pad, LANES)
    m2, win = HEADS_PER_VREG * GRID_W, NA_KH * GRID_W
    return pl.pallas_call(
        _na_bias_kernel,
        grid=(n_pairs,),
        in_specs=[pl.BlockSpec((1, HEADS_PER_VREG, dr_pad, LANES), lambda j: (j, 0, 0, 0))],
        out_specs=pl.BlockSpec((NA_KH, 1, m2, win), lambda j: (0, j, 0, 0)),
        out_shape=jax.ShapeDtypeStruct((NA_KH, n_pairs, m2, win), jnp.float32),
        compiler_params=pltpu.CompilerParams(dimension_semantics=("parallel",)),
        name="na_bias_table",
    )(rp)


def _na_kernel(q_ref, k_ref, v_ref, tbl_ref, o_ref, s_scr, *, rows_per_step, n_rows):
    i = pl.program_id(2)
    win = NA_KH * GRID_W

    def window(t):
        r = i * rows_per_step + t
        rs = jnp.clip(r - NA_KH // 2, 0, n_rows - NA_KH)
        return r - rs, pl.multiple_of(rs * GRID_W, GRID_W)

    for t in range(rows_per_step):
        variant, start = window(t)
        lhs = _stack_heads(q_ref[0, t * GRID_W:(t + 1) * GRID_W, :])
        kw = k_ref[0, pl.ds(start, win), :]
        s = lax.dot_general(lhs, kw, _NT, preferred_element_type=jnp.float32)
        s_scr[t] = s + tbl_ref[variant, 0]
    ones = jnp.ones((win, LANES), jnp.bfloat16)
    for t in range(rows_per_step):
        s = s_scr[t]
        m = jnp.max(s, axis=-1, keepdims=True)
        p = jnp.exp(s - m)
        _, start = window(t)
        vw = jnp.concatenate([v_ref[0, pl.ds(start, win), :], ones], axis=1)
        o2 = jnp.dot(p.astype(jnp.bfloat16), vw, preferred_element_type=jnp.float32)
        o2 = o2[:, :LANES] / o2[:, LANES:]
        o_ref[0, t * GRID_W:(t + 1) * GRID_W, :] = _unstack_heads(o2).astype(o_ref.dtype)


def _na_attention(qa, ka, va, tbl, rows_per_step):
    batch, seq, _ = qa.shape
    n_rows = seq // GRID_W
    n_pairs = NA_HEADS // HEADS_PER_VREG
    tq = rows_per_step * GRID_W
    m2, win = HEADS_PER_VREG * GRID_W, NA_KH * GRID_W
    kern = functools.partial(_na_kernel, rows_per_step=rows_per_step, n_rows=n_rows)
    return pl.pallas_call(
        kern,
        grid=(batch, n_pairs, n_rows // rows_per_step),
        in_specs=[
            pl.BlockSpec((1, tq, LANES), lambda b, j, i: (b, i, j)),
            pl.BlockSpec((1, seq, LANES), lambda b, j, i: (b, 0, j)),
            pl.BlockSpec((1, seq, LANES), lambda b, j, i: (b, 0, j)),
            pl.BlockSpec((NA_KH, 1, m2, win), lambda b, j, i: (0, j, 0, 0)),
        ],
        out_specs=pl.BlockSpec((1, tq, LANES), lambda b, j, i: (b, i, j)),
        out_shape=jax.ShapeDtypeStruct(qa.shape, jnp.bfloat16),
        scratch_shapes=[pltpu.VMEM((rows_per_step, m2, win), jnp.float32)],
        compiler_params=pltpu.CompilerParams(
            dimension_semantics=("parallel", "parallel", "parallel"),
            vmem_limit_bytes=VMEM_LIMIT),
        name="na_attention",
    )(qa, ka, va, tbl)


def _swa_mask_table():
    qi = np.arange(SWA_BLOCK)[:, None]
    kc = np.arange(3 * SWA_BLOCK)[None, :]
    tbl = [np.where(np.abs(kc - d * SWA_BLOCK - qi) <= SWA_WINDOW, 0.0, MASK_VALUE)
           for d in range(3)]
    return jnp.asarray(np.stack(tbl), dtype=jnp.float32)


def _swa_kernel(sink_ref, q_ref, k_ref, v_ref, mask_ref, o_ref, s_scr, *,
                blocks_per_step, seq):
    j = pl.program_id(1)
    n0 = pl.program_id(2) * blocks_per_step
    blk = SWA_BLOCK
    win = 3 * blk
    row = lax.broadcasted_iota(jnp.int32, (2 * blk, 1), 0)
    sink_col = jnp.where(row < blk, sink_ref[2 * j], sink_ref[2 * j + 1])

    def window(u):
        n = n0 + u
        start = jnp.clip((n - 1) * blk, 0, seq - win)
        return n - start // blk, pl.multiple_of(start, blk)

    for u in range(blocks_per_step):
        variant, start = window(u)
        lhs = _stack_heads(q_ref[0, u * blk:(u + 1) * blk, :])
        kw = k_ref[0, pl.ds(start, win), :]
        s = lax.dot_general(lhs, kw, _NT, preferred_element_type=jnp.float32)
        mask = mask_ref[variant]
        s_scr[u] = s + jnp.concatenate([mask, mask], axis=0)
    ones = jnp.ones((win, LANES), jnp.bfloat16)
    for u in range(blocks_per_step):
        s = s_scr[u]
        m = jnp.maximum(jnp.max(s, axis=-1, keepdims=True), sink_col)
        e = jnp.exp(s - m)
        _, start = window(u)
        vw = jnp.concatenate([v_ref[0, pl.ds(start, win), :], ones], axis=1)
        o2 = jnp.dot(e.astype(jnp.bfloat16), vw, preferred_element_type=jnp.float32)
        o2 = o2[:, :LANES] / (o2[:, LANES:] + jnp.exp(sink_col - m))
        o_ref[0, u * blk:(u + 1) * blk, :] = _unstack_heads(o2).astype(o_ref.dtype)


def _swa_attention(qb, kb, vb, sink, blocks_per_step):
    batch, seq, _ = qb.shape
    n_pairs = SWA_Q_HEADS // HEADS_PER_VREG
    pairs_per_group = n_pairs // SWA_KV_HEADS
    tq = blocks_per_step * SWA_BLOCK
    m2, win = HEADS_PER_VREG * SWA_BLOCK, 3 * SWA_BLOCK
    kern = functools.partial(_swa_kernel, blocks_per_step=blocks_per_step, seq=seq)
    return pl.pallas_call(
        kern,
        grid=(batch, n_pairs, seq // tq),
        in_specs=[
            pl.BlockSpec(memory_space=pltpu.SMEM),
            pl.BlockSpec((1, tq, LANES), lambda b, j, n: (b, n, j)),
            pl.BlockSpec((1, seq, LANES), lambda b, j, n: (b, 0, j // pairs_per_group)),
            pl.BlockSpec((1, seq, LANES), lambda b, j, n: (b, 0, j // pairs_per_group)),
            _const_spec((3, SWA_BLOCK, 3 * SWA_BLOCK)),
        ],
        out_specs=pl.BlockSpec((1, tq, LANES), lambda b, j, n: (b, n, j)),
        out_shape=jax.ShapeDtypeStruct(qb.shape, jnp.bfloat16),
        scratch_shapes=[pltpu.VMEM((blocks_per_step, m2, win), jnp.float32)],
        compiler_params=pltpu.CompilerParams(
            dimension_semantics=("parallel", "parallel", "parallel"),
            vmem_limit_bytes=VMEM_LIMIT),
        name="swa_attention",
    )(sink, qb, kb, vb, _swa_mask_table())


def _tail_kernel(x_ref, g0_ref, b0_ref, na_ref, swa_ref, wg_ref, bin_ref, wna_ref, wswa_ref,
                 wout_ref, bout_ref, g1_ref, b1_ref, w1_ref, bf1_ref, w2_ref, bf2_ref,
                 g2_ref, b2_ref, o_ref, *, sub_rows, ff_chunk):
    for r0 in range(0, x_ref.shape[0], sub_rows):
        rows = slice(r0, r0 + sub_rows)
        h = _layer_norm(x_ref[rows, :], g0_ref[...], b0_ref[...])
        gates = (jnp.dot(h.astype(jnp.bfloat16), wg_ref[...],
                         preferred_element_type=jnp.float32) + bin_ref[:, _C_G:_C_END])
        y_na = jnp.dot(na_ref[rows, :], wna_ref[...], preferred_element_type=jnp.float32)
        y_swa = jnp.dot(swa_ref[rows, :], wswa_ref[...], preferred_element_type=jnp.float32)
        mixed = (jax.nn.sigmoid(gates[:, :D_MODEL]) * y_na
                 + jax.nn.sigmoid(gates[:, D_MODEL:]) * y_swa)
        attn = jnp.dot(mixed.astype(jnp.bfloat16), wout_ref[...],
                       preferred_element_type=jnp.float32) + bout_ref[...]
        h = _layer_norm(DEEPNORM_ALPHA * h + attn, g1_ref[...], b1_ref[...])

        hb = h.astype(jnp.bfloat16)
        acc = jnp.zeros(h.shape, jnp.float32)
        for c in range(D_FF // ff_chunk):
            sl = slice(c * ff_chunk, (c + 1) * ff_chunk)
            u = jnp.dot(hb, w1_ref[:, sl], preferred_element_type=jnp.float32) + bf1_ref[:, sl]
            u = jnp.square(jnp.maximum(u, 0.0))
            acc = acc + jnp.dot(u.astype(jnp.bfloat16), w2_ref[sl, :],
                                preferred_element_type=jnp.float32)
        o_ref[rows, :] = _layer_norm(DEEPNORM_ALPHA * h + acc + bf2_ref[...],
                                     g2_ref[...], b2_ref[...])


def _tail(x2, g0, b0, na, swa, wg, b_in, wna, wswa, wout, bout, g1, b1, w1, bf1, w2, bf2,
          g2, b2, tm, sub_rows, ff_chunk):
    t = x2.shape[0]
    tok = lambda i: (i, 0)
    vec = _const_spec((1, D_MODEL))
    return pl.pallas_call(
        functools.partial(_tail_kernel, sub_rows=sub_rows, ff_chunk=ff_chunk),
        grid=(t // tm,),
        in_specs=[
            pl.BlockSpec((tm, D_MODEL), tok), vec, vec,
            pl.BlockSpec((tm, NA_WIDTH), tok), pl.BlockSpec((tm, SWA_Q_WIDTH), tok),
            _const_spec((D_MODEL, 2 * D_MODEL)), _const_spec((1, _C_END)),
            _const_spec((NA_WIDTH, D_MODEL)), _const_spec((SWA_Q_WIDTH, D_MODEL)),
            _const_spec((D_MODEL, D_MODEL)), vec, vec, vec,
            _const_spec((D_MODEL, D_FF)), _const_spec((1, D_FF)),
            _const_spec((D_FF, D_MODEL)), vec, vec, vec,
        ],
        out_specs=pl.BlockSpec((tm, D_MODEL), tok),
        out_shape=jax.ShapeDtypeStruct((t, D_MODEL), jnp.float32),
        compiler_params=pltpu.CompilerParams(
            dimension_semantics=("parallel",), vmem_limit_bytes=VMEM_LIMIT),
        name="mix_ffn",
    )(x2, g0, b0, na, swa, wg, b_in, wna, wswa, wout, bout, g1, b1, w1, bf1, w2, bf2, g2, b2)


def kernel(x, ln0_g, ln0_b, w_in, b_in, na_rpb, swa_sink, w_branch_na, w_branch_swa,
           w_out, b_out, ln1_g, ln1_b, w_ff1, b_ff1, w_ff2, b_ff2, ln2_g, ln2_b):
    batch, seq, d = x.shape
    assert d == D_MODEL and seq % (NA_KH * GRID_W) == 0 and seq >= 3 * SWA_BLOCK
    assert w_in.shape == (DEPTH, D_MODEL, _C_END) and DEPTH == 1
    bf = jnp.bfloat16
    row = lambda v: v.reshape(1, -1)
    x2 = x.reshape(batch * seq, d)
    b_in_row = row(b_in[0])

    cos_t, sa_t, sb_t = _rotary_tables(seq)
    qa, ka, va, qb, kb, vb = _inproj(
        x2, row(ln0_g), row(ln0_b), w_in[0, :, :_C_G].astype(bf), b_in_row,
        cos_t, sa_t, sb_t, batch, seq, tm=1024, sub_rows=256)

    three = lambda a: a.reshape(batch, seq, a.shape[-1])
    o_na = _na_attention(three(qa), three(ka), three(va), _na_bias_table(na_rpb[0]),
                         rows_per_step=16)
    o_swa = _swa_attention(three(qb), three(kb), three(vb), swa_sink[0], blocks_per_step=8)

    out = _tail(x2, row(ln0_g), row(ln0_b), o_na.reshape(batch * seq, -1),
                o_swa.reshape(batch * seq, -1), w_in[0, :, _C_G:].astype(bf), b_in_row,
                w_branch_na[0].astype(bf), w_branch_swa[0].astype(bf), w_out[0].astype(bf),
                row(b_out[0]), row(ln1_g[0]), row(ln1_b[0]),
                w_ff1[0].astype(bf), row(b_ff1[0]), w_ff2[0].astype(bf), row(b_ff2[0]),
                row(ln2_g[0]), row(ln2_b[0]), tm=512, sub_rows=256, ff_chunk=1024)
    return out.reshape(batch, seq, d)
```

```python
import functools

import numpy as np
import jax
import jax.numpy as jnp
from jax import lax
from jax.experimental import pallas as pl
from jax.experimental.pallas import tpu as pltpu

D_MODEL = 1024
GRID_W = 64
HEAD_DIM = 64
NA_HEADS = 8
NA_KH = 8
NA_KW = 16
NA_WIDTH = NA_HEADS * HEAD_DIM
SWA_Q_HEADS = 8
SWA_KV_HEADS = 2
SWA_WINDOW = 128
SWA_BLOCK = 128
SWA_Q_WIDTH = SWA_Q_HEADS * HEAD_DIM
SWA_KV_WIDTH = SWA_KV_HEADS * HEAD_DIM
ROPE_THETA = 500000.0
ROPE_DIM = HEAD_DIM // 4
D_FF = 4 * D_MODEL
DEPTH = 1
DEEPNORM_ALPHA = (2.0 * DEPTH) ** 0.25
LN_EPS = 1e-5
MASK_VALUE = -1e30
ATTN_SCALE = HEAD_DIM ** -0.5

LANES = 128
HEADS_PER_VREG = LANES // HEAD_DIM
VMEM_LIMIT = 56 * 1024 * 1024

_C_QA = 0
_C_KA = _C_QA + NA_WIDTH
_C_VA = _C_KA + NA_WIDTH
_C_QB = _C_VA + NA_WIDTH
_C_KB = _C_QB + SWA_Q_WIDTH
_C_VB = _C_KB + SWA_KV_WIDTH
_C_G = _C_VB + SWA_KV_WIDTH
_C_END = _C_G + 2 * D_MODEL
_KV_DUP = SWA_KV_HEADS * LANES


def _const_spec(shape):
    nd = len(shape)
    return pl.BlockSpec(shape, lambda *_: (0,) * nd, pipeline_mode=pl.Buffered(1))


def _layer_norm(x, g, b):
    mu = jnp.mean(x, axis=-1, keepdims=True)
    xc = x - mu
    var = jnp.mean(xc * xc, axis=-1, keepdims=True)
    return xc * lax.rsqrt(var + LN_EPS) * g + b


def _dup_heads(y):
    lane = lax.broadcasted_iota(jnp.int32, y.shape, 1)
    swapped = pltpu.roll(y, HEAD_DIM, 1)
    return jnp.concatenate([jnp.where(lane < HEAD_DIM, y, swapped),
                            jnp.where(lane >= HEAD_DIM, y, swapped)], axis=1)


def _inproj_kernel(x_ref, g_ref, b_ref, w_ref, bias_ref, cos_ref, sa_ref, sb_ref,
                   qa_ref, ka_ref, va_ref, qb_ref, kb_ref, vb_ref, *, sub_rows):
    half = ROPE_DIM // 2

    for r0 in range(0, x_ref.shape[0], sub_rows):
        rows = slice(r0, r0 + sub_rows)
        h = _layer_norm(x_ref[rows, :], g_ref[...], b_ref[...])
        hb = h.astype(jnp.bfloat16)

        def proj(c0, c1):
            return (jnp.dot(hb, w_ref[:, c0:c1], preferred_element_type=jnp.float32)
                    + bias_ref[:, c0:c1])

        def rotary(y):
            cos, sa, sb = cos_ref[rows, :], sa_ref[rows, :], sb_ref[rows, :]
            outs = []
            for c in range(y.shape[1] // LANES):
                yc = y[:, c * LANES:(c + 1) * LANES]
                outs.append(yc * cos
                            + pltpu.roll(yc, half, 1) * sa
                            + pltpu.roll(yc, LANES - half, 1) * sb)
            return outs[0] if len(outs) == 1 else jnp.concatenate(outs, axis=1)

        qa_ref[rows, :] = (proj(_C_QA, _C_KA) * ATTN_SCALE).astype(qa_ref.dtype)
        ka_ref[rows, :] = proj(_C_KA, _C_VA).astype(ka_ref.dtype)
        va_ref[rows, :] = proj(_C_VA, _C_QB).astype(va_ref.dtype)
        qb_ref[rows, :] = (rotary(proj(_C_QB, _C_KB)) * ATTN_SCALE).astype(qb_ref.dtype)
        kv = proj(_C_KB, _C_G)
        kb_ref[rows, :] = _dup_heads(rotary(kv[:, :SWA_KV_WIDTH])).astype(kb_ref.dtype)
        vb_ref[rows, :] = _dup_heads(kv[:, SWA_KV_WIDTH:]).astype(vb_ref.dtype)


def _inproj(x2, g, b, w, bias, cos_t, sa_t, sb_t, batch, seq, tm, sub_rows):
    t = batch * seq
    nblk = seq // tm
    tok = lambda bi, i: (bi * nblk + i, 0)
    pos = lambda bi, i: (i, 0)
    bf = jnp.bfloat16
    out_shape = (
        jax.ShapeDtypeStruct((t, NA_WIDTH), bf), jax.ShapeDtypeStruct((t, NA_WIDTH), bf),
        jax.ShapeDtypeStruct((t, NA_WIDTH), bf), jax.ShapeDtypeStruct((t, SWA_Q_WIDTH), bf),
        jax.ShapeDtypeStruct((t, _KV_DUP), bf), jax.ShapeDtypeStruct((t, _KV_DUP), bf),
    )
    return pl.pallas_call(
        functools.partial(_inproj_kernel, sub_rows=sub_rows),
        grid=(batch, nblk),
        in_specs=[
            pl.BlockSpec((tm, D_MODEL), tok),
            _const_spec((1, D_MODEL)), _const_spec((1, D_MODEL)),
            _const_spec((D_MODEL, _C_G)), _const_spec((1, _C_END)),
            pl.BlockSpec((tm, LANES), pos), pl.BlockSpec((tm, LANES), pos),
            pl.BlockSpec((tm, LANES), pos),
        ],
        out_specs=(
            pl.BlockSpec((tm, NA_WIDTH), tok), pl.BlockSpec((tm, NA_WIDTH), tok),
            pl.BlockSpec((tm, NA_WIDTH), tok), pl.BlockSpec((tm, SWA_Q_WIDTH), tok),
            pl.BlockSpec((tm, _KV_DUP), tok), pl.BlockSpec((tm, _KV_DUP), tok),
        ),
        out_shape=out_shape,
        compiler_params=pltpu.CompilerParams(
            dimension_semantics=("parallel", "parallel"), vmem_limit_bytes=VMEM_LIMIT),
        name="ln_inproj_rotary",
    )(x2, g, b, w, bias, cos_t, sa_t, sb_t)


def _rotary_tables(seq):
    half = ROPE_DIM // 2
    inv_freq = np.power(ROPE_THETA, -np.arange(0, ROPE_DIM, 2, dtype=np.float64) / ROPE_DIM)
    ang = np.arange(seq, dtype=np.float64)[:, None] * inv_freq[None, :]
    cos, sin = np.cos(ang), np.sin(ang)
    ones = np.ones((seq, HEAD_DIM - ROPE_DIM))
    zeros = np.zeros((seq, HEAD_DIM - ROPE_DIM))
    zh = np.zeros((seq, half))
    cos_h = np.concatenate([cos, cos, ones], axis=1)
    sa_h = np.concatenate([zh, sin, zeros], axis=1)
    sb_h = np.concatenate([-sin, zh, zeros], axis=1)
    tile = lambda a: jnp.asarray(np.tile(a, (1, HEADS_PER_VREG)), dtype=jnp.float32)
    return tile(cos_h), tile(sa_h), tile(sb_h)


def _stack_heads(q):
    lane = lax.broadcasted_iota(jnp.int32, q.shape, 1)
    zero = jnp.zeros_like(q)
    return jnp.concatenate([jnp.where(lane < HEAD_DIM, q, zero),
                            jnp.where(lane >= HEAD_DIM, q, zero)], axis=0)


def _unstack_heads(o2):
    m = o2.shape[0] // 2
    lane = lax.broadcasted_iota(jnp.int32, (m, LANES), 1)
    return jnp.where(lane < HEAD_DIM, o2[:m], o2[m:])


_NT = (((1,), (1,)), ((), ()))


def _na_bias_kernel(rpb_ref, o_ref):
    w = lax.broadcasted_iota(jnp.int32, (GRID_W, LANES), 0)
    lane = lax.broadcasted_iota(jnp.int32, (GRID_W, LANES), 1)
    c = lane % GRID_W
    col_start = jnp.clip(w - NA_KW // 2, 0, GRID_W - NA_KW)
    inside = (c >= col_start) & (c < col_start + NA_KW)
    lower = lane < GRID_W
    n_dr = 2 * NA_KH - 1
    for e in range(HEADS_PER_VREG):
        lo, hi = [], []
        for dr in range(n_dr):
            x = jnp.broadcast_to(rpb_ref[0, e, dr:dr + 1, :], (GRID_W, LANES))
            lo.append(pltpu.roll(x, LANES - (NA_KW - 1), 1, stride=1, stride_axis=0))
            hi.append(pltpu.roll(x, GRID_W - (NA_KW - 1), 1, stride=1, stride_axis=0))
        for v in range(NA_KH):
            for kk in range(NA_KH // HEADS_PER_VREG):
                dr_a = 2 * kk - v + NA_KH - 1
                tile = jnp.where(inside, jnp.where(lower, lo[dr_a], hi[dr_a + 1]), MASK_VALUE)
                o_ref[v, 0, e * GRID_W:(e + 1) * GRID_W, kk * LANES:(kk + 1) * LANES] = tile


def _na_bias_table(rpb):
    n_pairs = NA_HEADS // HEADS_PER_VREG
    n_dr, n_dc = rpb.shape[1], rpb.shape[2]
    dr_pad = -(-n_dr // 8) * 8
    rp = jnp.pad(rpb, ((0, 0), (0, dr_pad - n_dr), (0, LANES - n_dc)))
    rp = rp.reshape(n_pairs, HEADS_PER_VREG, dr_pad, LANES)
    m2, win = HEADS_PER_VREG * GRID_W, NA_KH * GRID_W
    return pl.pallas_call(
        _na_bias_kernel,
        grid=(n_pairs,),
        in_specs=[pl.BlockSpec((1, HEADS_PER_VREG, dr_pad, LANES), lambda j: (j, 0, 0, 0))],
        out_specs=pl.BlockSpec((NA_KH, 1, m2, win), lambda j: (0, j, 0, 0)),
        out_shape=jax.ShapeDtypeStruct((NA_KH, n_pairs, m2, win), jnp.float32),
        compiler_params=pltpu.CompilerParams(dimension_semantics=("parallel",)),
        name="na_bias_table",
    )(rp)


def _na_kernel(q_ref, k_ref, v_ref, tbl_ref, o_ref, s_scr, *, rows_per_iter, n_rows):
    win = NA_KH * GRID_W
    ones = jnp.ones((win, LANES), jnp.bfloat16)

    def body(i, carry):
        def window(t):
            r = i * rows_per_iter + t
            rs = jnp.clip(r - NA_KH // 2, 0, n_rows - NA_KH)
            return r - rs, pl.multiple_of(rs * GRID_W, GRID_W)

        def q_rows(t):
            return pl.ds(pl.multiple_of((i * rows_per_iter + t) * GRID_W, GRID_W), GRID_W)

        for t in range(rows_per_iter):
            variant, start = window(t)
            lhs = _stack_heads(q_ref[0, q_rows(t), :])
            kw = k_ref[0, pl.ds(start, win), :]
            s = lax.dot_general(lhs, kw, _NT, preferred_element_type=jnp.float32)
            s_scr[t] = s + tbl_ref[variant, 0]
        for t in range(rows_per_iter):
            s = s_scr[t]
            m = jnp.max(s, axis=-1, keepdims=True)
            p = jnp.exp(s - m)
            _, start = window(t)
            vw = jnp.concatenate([v_ref[0, pl.ds(start, win), :], ones], axis=1)
            o2 = jnp.dot(p.astype(jnp.bfloat16), vw, preferred_element_type=jnp.float32)
            o2 = o2[:, :LANES] / o2[:, LANES:]
            o_ref[0, q_rows(t), :] = _unstack_heads(o2).astype(o_ref.dtype)
        return carry

    lax.fori_loop(0, n_rows // rows_per_iter, body, 0)


def _na_attention(qa, ka, va, tbl, rows_per_iter):
    batch, seq, _ = qa.shape
    n_rows = seq // GRID_W
    n_pairs = NA_HEADS // HEADS_PER_VREG
    m2, win = HEADS_PER_VREG * GRID_W, NA_KH * GRID_W
    kern = functools.partial(_na_kernel, rows_per_iter=rows_per_iter, n_rows=n_rows)
    seq_block = pl.BlockSpec((1, seq, LANES), lambda b, j: (b, 0, j))
    return pl.pallas_call(
        kern,
        grid=(batch, n_pairs),
        in_specs=[seq_block, seq_block, seq_block,
                  pl.BlockSpec((NA_KH, 1, m2, win), lambda b, j: (0, j, 0, 0))],
        out_specs=seq_block,
        out_shape=jax.ShapeDtypeStruct(qa.shape, jnp.bfloat16),
        scratch_shapes=[pltpu.VMEM((rows_per_iter, m2, win), jnp.float32)],
        compiler_params=pltpu.CompilerParams(
            dimension_semantics=("parallel", "parallel"), vmem_limit_bytes=VMEM_LIMIT),
        name="na_attention",
    )(qa, ka, va, tbl)


def _swa_mask_table():
    qi = np.arange(SWA_BLOCK)[:, None]
    kc = np.arange(3 * SWA_BLOCK)[None, :]
    tbl = [np.where(np.abs(kc - d * SWA_BLOCK - qi) <= SWA_WINDOW, 0.0, MASK_VALUE)
           for d in range(3)]
    return jnp.asarray(np.stack(tbl), dtype=jnp.float32)


def _swa_kernel(sink_ref, q_ref, k_ref, v_ref, mask_ref, o_ref, s_scr, *,
                blocks_per_iter, seq):
    j = pl.program_id(1)
    blk = SWA_BLOCK
    win = 3 * blk
    row = lax.broadcasted_iota(jnp.int32, (2 * blk, 1), 0)
    sink_col = jnp.where(row < blk, sink_ref[2 * j], sink_ref[2 * j + 1])
    ones = jnp.ones((win, LANES), jnp.bfloat16)

    def body(i, carry):
        def window(u):
            n = i * blocks_per_iter + u
            start = jnp.clip((n - 1) * blk, 0, seq - win)
            return n - start // blk, pl.multiple_of(start, blk)

        def q_rows(u):
            return pl.ds(pl.multiple_of((i * blocks_per_iter + u) * blk, blk), blk)

        for u in range(blocks_per_iter):
            variant, start = window(u)
            lhs = _stack_heads(q_ref[0, q_rows(u), :])
            kw = k_ref[0, pl.ds(start, win), :]
            s = lax.dot_general(lhs, kw, _NT, preferred_element_type=jnp.float32)
            mask = mask_ref[variant]
            s_scr[u] = s + jnp.concatenate([mask, mask], axis=0)
        for u in range(blocks_per_iter):
            s = s_scr[u]
            m = jnp.maximum(jnp.max(s, axis=-1, keepdims=True), sink_col)
            e = jnp.exp(s - m)
            _, start = window(u)
            vw = jnp.concatenate([v_ref[0, pl.ds(start, win), :], ones], axis=1)
            o2 = jnp.dot(e.astype(jnp.bfloat16), vw, preferred_element_type=jnp.float32)
            o2 = o2[:, :LANES] / (o2[:, LANES:] + jnp.exp(sink_col - m))
            o_ref[0, q_rows(u), :] = _unstack_heads(o2).astype(o_ref.dtype)
        return carry

    lax.fori_loop(0, seq // (blocks_per_iter * blk), body, 0)


def _swa_attention(qb, kb, vb, sink, blocks_per_iter):
    batch, seq, _ = qb.shape
    n_pairs = SWA_Q_HEADS // HEADS_PER_VREG
    pairs_per_group = n_pairs // SWA_KV_HEADS
    m2, win = HEADS_PER_VREG * SWA_BLOCK, 3 * SWA_BLOCK
    kern = functools.partial(_swa_kernel, blocks_per_iter=blocks_per_iter, seq=seq)
    q_block = pl.BlockSpec((1, seq, LANES), lambda b, j: (b, 0, j))
    kv_block = pl.BlockSpec((1, seq, LANES), lambda b, j: (b, 0, j // pairs_per_group))
    return pl.pallas_call(
        kern,
        grid=(batch, n_pairs),
        in_specs=[pl.BlockSpec(memory_space=pltpu.SMEM), q_block, kv_block, kv_block,
                  _const_spec((3, SWA_BLOCK, win))],
        out_specs=q_block,
        out_shape=jax.ShapeDtypeStruct(qb.shape, jnp.bfloat16),
        scratch_shapes=[pltpu.VMEM((blocks_per_iter, m2, win), jnp.float32)],
        compiler_params=pltpu.CompilerParams(
            dimension_semantics=("parallel", "parallel"), vmem_limit_bytes=VMEM_LIMIT),
        name="swa_attention",
    )(sink, qb, kb, vb, _swa_mask_table())


def _tail_kernel(x_ref, g0_ref, b0_ref, na_ref, swa_ref, wg_ref, bin_ref, wna_ref, wswa_ref,
                 wout_ref, bout_ref, g1_ref, b1_ref, w1_ref, bf1_ref, w2_ref, bf2_ref,
                 g2_ref, b2_ref, o_ref, *, sub_rows, ff_chunk):
    for r0 in range(0, x_ref.shape[0], sub_rows):
        rows = slice(r0, r0 + sub_rows)
        h = _layer_norm(x_ref[rows, :], g0_ref[...], b0_ref[...])
        gates = (jnp.dot(h.astype(jnp.bfloat16), wg_ref[...],
                         preferred_element_type=jnp.float32) + bin_ref[:, _C_G:_C_END])
        y_na = jnp.dot(na_ref[rows, :], wna_ref[...], preferred_element_type=jnp.float32)
        y_swa = jnp.dot(swa_ref[rows, :], wswa_ref[...], preferred_element_type=jnp.float32)
        mixed = (jax.nn.sigmoid(gates[:, :D_MODEL]) * y_na
                 + jax.nn.sigmoid(gates[:, D_MODEL:]) * y_swa)
        attn = jnp.dot(mixed.astype(jnp.bfloat16), wout_ref[...],
                       preferred_element_type=jnp.float32) + bout_ref[...]
        h = _layer_norm(DEEPNORM_ALPHA * h + attn, g1_ref[...], b1_ref[...])

        hb = h.astype(jnp.bfloat16)
        acc = jnp.zeros(h.shape, jnp.float32)
        for c in range(D_FF // ff_chunk):
            sl = slice(c * ff_chunk, (c + 1) * ff_chunk)
            u = jnp.dot(hb, w1_ref[:, sl], preferred_element_type=jnp.float32) + bf1_ref[:, sl]
            u = jnp.square(jnp.maximum(u, 0.0))
            acc = acc + jnp.dot(u.astype(jnp.bfloat16), w2_ref[sl, :],
                                preferred_element_type=jnp.float32)
        o_ref[rows, :] = _layer_norm(DEEPNORM_ALPHA * h + acc + bf2_ref[...],
                                     g2_ref[...], b2_ref[...])


def _tail(x2, g0, b0, na, swa, wg, b_in, wna, wswa, wout, bout, g1, b1, w1, bf1, w2, bf2,
          g2, b2, tm, sub_rows, ff_chunk):
    t = x2.shape[0]
    tok = lambda i: (i, 0)
    vec = _const_spec((1, D_MODEL))
    return pl.pallas_call(
        functools.partial(_tail_kernel, sub_rows=sub_rows, ff_chunk=ff_chunk),
        grid=(t // tm,),
        in_specs=[
            pl.BlockSpec((tm, D_MODEL), tok), vec, vec,
            pl.BlockSpec((tm, NA_WIDTH), tok), pl.BlockSpec((tm, SWA_Q_WIDTH), tok),
            _const_spec((D_MODEL, 2 * D_MODEL)), _const_spec((1, _C_END)),
            _const_spec((NA_WIDTH, D_MODEL)), _const_spec((SWA_Q_WIDTH, D_MODEL)),
            _const_spec((D_MODEL, D_MODEL)), vec, vec, vec,
            _const_spec((D_MODEL, D_FF)), _const_spec((1, D_FF)),
            _const_spec((D_FF, D_MODEL)), vec, vec, vec,
        ],
        out_specs=pl.BlockSpec((tm, D_MODEL), tok),
        out_shape=jax.ShapeDtypeStruct((t, D_MODEL), jnp.float32),
        compiler_params=pltpu.CompilerParams(
            dimension_semantics=("parallel",), vmem_limit_bytes=VMEM_LIMIT),
        name="mix_ffn",
    )(x2, g0, b0, na, swa, wg, b_in, wna, wswa, wout, bout, g1, b1, w1, bf1, w2, bf2, g2, b2)


def kernel(x, ln0_g, ln0_b, w_in, b_in, na_rpb, swa_sink, w_branch_na, w_branch_swa,
           w_out, b_out, ln1_g, ln1_b, w_ff1, b_ff1, w_ff2, b_ff2, ln2_g, ln2_b):
    batch, seq, d = x.shape
    assert d == D_MODEL and seq % (NA_KH * GRID_W) == 0 and seq >= 3 * SWA_BLOCK
    assert w_in.shape == (DEPTH, D_MODEL, _C_END) and DEPTH == 1
    bf = jnp.bfloat16
    row = lambda v: v.reshape(1, -1)
    x2 = x.reshape(batch * seq, d)
    b_in_row = row(b_in[0])

    cos_t, sa_t, sb_t = _rotary_tables(seq)
    qa, ka, va, qb, kb, vb = _inproj(
        x2, row(ln0_g), row(ln0_b), w_in[0, :, :_C_G].astype(bf), b_in_row,
        cos_t, sa_t, sb_t, batch, seq, tm=1024, sub_rows=256)

    three = lambda a: a.reshape(batch, seq, a.shape[-1])
    o_na = _na_attention(three(qa), three(ka), three(va), _na_bias_table(na_rpb[0]),
                         rows_per_iter=16)
    o_swa = _swa_attention(three(qb), three(kb), three(vb), swa_sink[0], blocks_per_iter=8)

    out = _tail(x2, row(ln0_g), row(ln0_b), o_na.reshape(batch * seq, -1),
                o_swa.reshape(batch * seq, -1), w_in[0, :, _C_G:].astype(bf), b_in_row,
                w_branch_na[0].astype(bf), w_branch_swa[0].astype(bf), w_out[0].astype(bf),
                row(b_out[0]), row(ln1_g[0]), row(ln1_b[0]),
                w_ff1[0].astype(bf), row(b_ff1[0]), w_ff2[0].astype(bf), row(b_ff2[0]),
                row(ln2_g[0]), row(ln2_b[0]), tm=512, sub_rows=256, ff_chunk=1024)
    return out.reshape(batch, seq, d)
```

```python
import functools

import numpy as np
import jax
import jax.numpy as jnp
from jax import lax
from jax.experimental import pallas as pl
from jax.experimental.pallas import tpu as pltpu

D_MODEL = 1024
GRID_W = 64
HEAD_DIM = 64
NA_HEADS = 8
NA_KH = 8
NA_KW = 16
NA_WIDTH = NA_HEADS * HEAD_DIM
SWA_Q_HEADS = 8
SWA_KV_HEADS = 2
SWA_WINDOW = 128
SWA_BLOCK = 128
SWA_Q_WIDTH = SWA_Q_HEADS * HEAD_DIM
SWA_KV_WIDTH = SWA_KV_HEADS * HEAD_DIM
ROPE_THETA = 500000.0
ROPE_DIM = HEAD_DIM // 4
D_FF = 4 * D_MODEL
DEPTH = 1
DEEPNORM_ALPHA = (2.0 * DEPTH) ** 0.25
LN_EPS = 1e-5
MASK_VALUE = -1e30
ATTN_SCALE = HEAD_DIM ** -0.5

LANES = 128
HEADS_PER_VREG = LANES // HEAD_DIM
VMEM_LIMIT = 56 * 1024 * 1024

_C_QA = 0
_C_KA = _C_QA + NA_WIDTH
_C_VA = _C_KA + NA_WIDTH
_C_QB = _C_VA + NA_WIDTH
_C_KB = _C_QB + SWA_Q_WIDTH
_C_VB = _C_KB + SWA_KV_WIDTH
_C_G = _C_VB + SWA_KV_WIDTH
_C_END = _C_G + 2 * D_MODEL
_KV_DUP = SWA_KV_HEADS * LANES


def _const_spec(shape):
    nd = len(shape)
    return pl.BlockSpec(shape, lambda *_: (0,) * nd, pipeline_mode=pl.Buffered(1))


def _layer_norm(x, g, b):
    mu = jnp.mean(x, axis=-1, keepdims=True)
    xc = x - mu
    var = jnp.mean(xc * xc, axis=-1, keepdims=True)
    return xc * lax.rsqrt(var + LN_EPS) * g + b


def _dup_heads(y):
    lane = lax.broadcasted_iota(jnp.int32, y.shape, 1)
    swapped = pltpu.roll(y, HEAD_DIM, 1)
    return jnp.concatenate([jnp.where(lane < HEAD_DIM, y, swapped),
                            jnp.where(lane >= HEAD_DIM, y, swapped)], axis=1)


def _inproj_kernel(x_ref, g_ref, b_ref, w_ref, bias_ref, cos_ref, sa_ref, sb_ref,
                   qa_ref, ka_ref, va_ref, qb_ref, kb_ref, vb_ref, *, sub_rows):
    half = ROPE_DIM // 2

    for r0 in range(0, x_ref.shape[0], sub_rows):
        rows = slice(r0, r0 + sub_rows)
        h = _layer_norm(x_ref[rows, :], g_ref[...], b_ref[...])
        hb = h.astype(jnp.bfloat16)

        def proj(c0, c1):
            return (jnp.dot(hb, w_ref[:, c0:c1], preferred_element_type=jnp.float32)
                    + bias_ref[:, c0:c1])

        def rotary(y):
            cos, sa, sb = cos_ref[rows, :], sa_ref[rows, :], sb_ref[rows, :]
            outs = []
            for c in range(y.shape[1] // LANES):
                yc = y[:, c * LANES:(c + 1) * LANES]
                outs.append(yc * cos
                            + pltpu.roll(yc, half, 1) * sa
                            + pltpu.roll(yc, LANES - half, 1) * sb)
            return outs[0] if len(outs) == 1 else jnp.concatenate(outs, axis=1)

        qa_ref[rows, :] = (proj(_C_QA, _C_KA) * ATTN_SCALE).astype(qa_ref.dtype)
        ka_ref[rows, :] = proj(_C_KA, _C_VA).astype(ka_ref.dtype)
        va_ref[rows, :] = proj(_C_VA, _C_QB).astype(va_ref.dtype)
        qb_ref[rows, :] = (rotary(proj(_C_QB, _C_KB)) * ATTN_SCALE).astype(qb_ref.dtype)
        kv = proj(_C_KB, _C_G)
        kb_ref[rows, :] = _dup_heads(rotary(kv[:, :SWA_KV_WIDTH])).astype(kb_ref.dtype)
        vb_ref[rows, :] = _dup_heads(kv[:, SWA_KV_WIDTH:]).astype(vb_ref.dtype)


def _inproj(x2, g, b, w, bias, cos_t, sa_t, sb_t, batch, seq, tm, sub_rows):
    t = batch * seq
    nblk = seq // tm
    tok = lambda bi, i: (bi * nblk + i, 0)
    pos = lambda bi, i: (i, 0)
    bf = jnp.bfloat16
    out_shape = (
        jax.ShapeDtypeStruct((t, NA_WIDTH), bf), jax.ShapeDtypeStruct((t, NA_WIDTH), bf),
        jax.ShapeDtypeStruct((t, NA_WIDTH), bf), jax.ShapeDtypeStruct((t, SWA_Q_WIDTH), bf),
        jax.ShapeDtypeStruct((t, _KV_DUP), bf), jax.ShapeDtypeStruct((t, _KV_DUP), bf),
    )
    return pl.pallas_call(
        functools.partial(_inproj_kernel, sub_rows=sub_rows),
        grid=(batch, nblk),
        in_specs=[
            pl.BlockSpec((tm, D_MODEL), tok),
            _const_spec((1, D_MODEL)), _const_spec((1, D_MODEL)),
            _const_spec((D_MODEL, _C_G)), _const_spec((1, _C_END)),
            pl.BlockSpec((tm, LANES), pos), pl.BlockSpec((tm, LANES), pos),
            pl.BlockSpec((tm, LANES), pos),
        ],
        out_specs=(
            pl.BlockSpec((tm, NA_WIDTH), tok), pl.BlockSpec((tm, NA_WIDTH), tok),
            pl.BlockSpec((tm, NA_WIDTH), tok), pl.BlockSpec((tm, SWA_Q_WIDTH), tok),
            pl.BlockSpec((tm, _KV_DUP), tok), pl.BlockSpec((tm, _KV_DUP), tok),
        ),
        out_shape=out_shape,
        compiler_params=pltpu.CompilerParams(
            dimension_semantics=("parallel", "parallel"), vmem_limit_bytes=VMEM_LIMIT),
        name="ln_inproj_rotary",
    )(x2, g, b, w, bias, cos_t, sa_t, sb_t)


def _rotary_tables(seq):
    half = ROPE_DIM // 2
    inv_freq = np.power(ROPE_THETA, -np.arange(0, ROPE_DIM, 2, dtype=np.float64) / ROPE_DIM)
    ang = np.arange(seq, dtype=np.float64)[:, None] * inv_freq[None, :]
    cos, sin = np.cos(ang), np.sin(ang)
    ones = np.ones((seq, HEAD_DIM - ROPE_DIM))
    zeros = np.zeros((seq, HEAD_DIM - ROPE_DIM))
    zh = np.zeros((seq, half))
    cos_h = np.concatenate([cos, cos, ones], axis=1)
    sa_h = np.concatenate([zh, sin, zeros], axis=1)
    sb_h = np.concatenate([-sin, zh, zeros], axis=1)
    tile = lambda a: jnp.asarray(np.tile(a, (1, HEADS_PER_VREG)), dtype=jnp.float32)
    return tile(cos_h), tile(sa_h), tile(sb_h)


def _stack_heads(q):
    lane = lax.broadcasted_iota(jnp.int32, q.shape, 1)
    zero = jnp.zeros_like(q)
    return jnp.concatenate([jnp.where(lane < HEAD_DIM, q, zero),
                            jnp.where(lane >= HEAD_DIM, q, zero)], axis=0)


def _unstack_heads(o2):
    m = o2.shape[0] // 2
    lane = lax.broadcasted_iota(jnp.int32, (m, LANES), 1)
    return jnp.where(lane < HEAD_DIM, o2[:m], o2[m:])


_NT = (((1,), (1,)), ((), ()))


def _na_bias_kernel(rpb_ref, o_ref):
    w = lax.broadcasted_iota(jnp.int32, (GRID_W, LANES), 0)
    lane = lax.broadcasted_iota(jnp.int32, (GRID_W, LANES), 1)
    c = lane % GRID_W
    col_start = jnp.clip(w - NA_KW // 2, 0, GRID_W - NA_KW)
    inside = (c >= col_start) & (c < col_start + NA_KW)
    lower = lane < GRID_W
    n_dr = 2 * NA_KH - 1
    for e in range(HEADS_PER_VREG):
        lo, hi = [], []
        for dr in range(n_dr):
            x = jnp.broadcast_to(rpb_ref[0, e, dr:dr + 1, :], (GRID_W, LANES))
            lo.append(pltpu.roll(x, LANES - (NA_KW - 1), 1, stride=1, stride_axis=0))
            hi.append(pltpu.roll(x, GRID_W - (NA_KW - 1), 1, stride=1, stride_axis=0))
        for v in range(NA_KH):
            for kk in range(NA_KH // HEADS_PER_VREG):
                dr_a = 2 * kk - v + NA_KH - 1
                tile = jnp.where(inside, jnp.where(lower, lo[dr_a], hi[dr_a + 1]), MASK_VALUE)
                o_ref[v, 0, e * GRID_W:(e + 1) * GRID_W, kk * LANES:(kk + 1) * LANES] = tile


def _na_bias_table(rpb):
    n_pairs = NA_HEADS // HEADS_PER_VREG
    n_dr, n_dc = rpb.shape[1], rpb.shape[2]
    dr_pad = -(-n_dr // 8) * 8
    rp = jnp.pad(rpb, ((0, 0), (0, dr_pad - n_dr), (0, LANES - n_dc)))
    rp = rp.reshape(n_pairs, HEADS_PER_VREG, dr_pad, LANES)
    m2, win = HEADS_PER_VREG * GRID_W, NA_KH * GRID_W
    return pl.pallas_call(
        _na_bias_kernel,
        grid=(n_pairs,),
        in_specs=[pl.BlockSpec((1, HEADS_PER_VREG, dr_pad, LANES), lambda j: (j, 0, 0, 0))],
        out_specs=pl.BlockSpec((NA_KH, 1, m2, win), lambda j: (0, j, 0, 0)),
        out_shape=jax.ShapeDtypeStruct((NA_KH, n_pairs, m2, win), jnp.float32),
        compiler_params=pltpu.CompilerParams(dimension_semantics=("parallel",)),
        name="na_bias_table",
    )(rp)


def _na_kernel(q_ref, k_ref, v_ref, tbl_ref, o_ref, s_scr, *, rows_per_iter, n_rows):
    win = NA_KH * GRID_W
    ones = jnp.ones((win, LANES), jnp.bfloat16)

    def body(i, carry):
        def window(t):
            r = i * rows_per_iter + t
            rs = jnp.clip(r - NA_KH // 2, 0, n_rows - NA_KH)
            return r - rs, pl.multiple_of(rs * GRID_W, GRID_W)

        def q_rows(t):
            return pl.ds(pl.multiple_of((i * rows_per_iter + t) * GRID_W, GRID_W), GRID_W)

        for t in range(rows_per_iter):
            variant, start = window(t)
            lhs = _stack_heads(q_ref[0, q_rows(t), :])
            kw = k_ref[0, pl.ds(start, win), :]
            s = lax.dot_general(lhs, kw, _NT, preferred_element_type=jnp.float32)
            s_scr[t] = s + tbl_ref[variant, 0]
        for t in range(rows_per_iter):
            s = s_scr[t]
            m = jnp.max(s, axis=-1, keepdims=True)
            p = jnp.exp(s - m)
            _, start = window(t)
            vw = jnp.concatenate([v_ref[0, pl.ds(start, win), :], ones], axis=1)
            o2 = jnp.dot(p.astype(jnp.bfloat16), vw, preferred_element_type=jnp.float32)
            o2 = o2[:, :LANES] / o2[:, LANES:]
            o_ref[0, q_rows(t), :] = _unstack_heads(o2).astype(o_ref.dtype)
        return carry

    lax.fori_loop(0, n_rows // rows_per_iter, body, 0)


def _na_attention(qa, ka, va, tbl, rows_per_iter):
    batch, seq, _ = qa.shape
    n_rows = seq // GRID_W
    n_pairs = NA_HEADS // HEADS_PER_VREG
    m2, win = HEADS_PER_VREG * GRID_W, NA_KH * GRID_W
    kern = functools.partial(_na_kernel, rows_per_iter=rows_per_iter, n_rows=n_rows)
    seq_block = pl.BlockSpec((1, seq, LANES), lambda b, j: (b, 0, j))
    return pl.pallas_call(
        kern,
        grid=(batch, n_pairs),
        in_specs=[seq_block, seq_block, seq_block,
                  pl.BlockSpec((NA_KH, 1, m2, win), lambda b, j: (0, j, 0, 0))],
        out_specs=seq_block,
        out_shape=jax.ShapeDtypeStruct(qa.shape, jnp.bfloat16),
        scratch_shapes=[pltpu.VMEM((rows_per_iter, m2, win), jnp.float32)],
        compiler_params=pltpu.CompilerParams(
            dimension_semantics=("parallel", "parallel"), vmem_limit_bytes=VMEM_LIMIT),
        name="na_attention",
    )(qa, ka, va, tbl)


def _swa_mask_table():
    qi = np.arange(SWA_BLOCK)[:, None]
    kc = np.arange(3 * SWA_BLOCK)[None, :]
    tbl = [np.where(np.abs(kc - d * SWA_BLOCK - qi) <= SWA_WINDOW, 0.0, MASK_VALUE)
           for d in range(3)]
    return jnp.asarray(np.stack(tbl), dtype=jnp.float32)


def _swa_kernel(sink_ref, q_ref, k_ref, v_ref, mask_ref, o_ref, s_scr, *,
                blocks_per_iter, seq):
    j = pl.program_id(1)
    blk = SWA_BLOCK
    win = 3 * blk
    row = lax.broadcasted_iota(jnp.int32, (2 * blk, 1), 0)
    sink_col = jnp.where(row < blk, sink_ref[2 * j], sink_ref[2 * j + 1])
    ones = jnp.ones((win, LANES), jnp.bfloat16)

    def body(i, carry):
        def window(u):
            n = i * blocks_per_iter + u
            start = jnp.clip((n - 1) * blk, 0, seq - win)
            return n - start // blk, pl.multiple_of(start, blk)

        def q_rows(u):
            return pl.ds(pl.multiple_of((i * blocks_per_iter + u) * blk, blk), blk)

        for u in range(blocks_per_iter):
            variant, start = window(u)
            lhs = _stack_heads(q_ref[0, q_rows(u), :])
            kw = k_ref[0, pl.ds(start, win), :]
            s = lax.dot_general(lhs, kw, _NT, preferred_element_type=jnp.float32)
            mask = mask_ref[variant]
            s_scr[u] = s + jnp.concatenate([mask, mask], axis=0)
        for u in range(blocks_per_iter):
            s = s_scr[u]
            m = jnp.maximum(jnp.max(s, axis=-1, keepdims=True), sink_col)
            e = jnp.exp(s - m)
            _, start = window(u)
            vw = jnp.concatenate([v_ref[0, pl.ds(start, win), :], ones], axis=1)
            o2 = jnp.dot(e.astype(jnp.bfloat16), vw, preferred_element_type=jnp.float32)
            o2 = o2[:, :LANES] / (o2[:, LANES:] + jnp.exp(sink_col - m))
            o_ref[0, q_rows(u), :] = _unstack_heads(o2).astype(o_ref.dtype)
        return carry

    lax.fori_loop(0, seq // (blocks_per_iter * blk), body, 0)


def _swa_attention(qb, kb, vb, sink, blocks_per_iter):
    batch, seq, _ = qb.shape
    n_pairs = SWA_Q_HEADS // HEADS_PER_VREG
    pairs_per_group = n_pairs // SWA_KV_HEADS
    m2, win = HEADS_PER_VREG * SWA_BLOCK, 3 * SWA_BLOCK
    kern = functools.partial(_swa_kernel, blocks_per_iter=blocks_per_iter, seq=seq)
    q_block = pl.BlockSpec((1, seq, LANES), lambda b, j: (b, 0, j))
    kv_block = pl.BlockSpec((1, seq, LANES), lambda b, j: (b, 0, j // pairs_per_group))
    return pl.pallas_call(
        kern,
        grid=(batch, n_pairs),
        in_specs=[pl.BlockSpec(memory_space=pltpu.SMEM), q_block, kv_block, kv_block,
                  _const_spec((3, SWA_BLOCK, win))],
        out_specs=q_block,
        out_shape=jax.ShapeDtypeStruct(qb.shape, jnp.bfloat16),
        scratch_shapes=[pltpu.VMEM((blocks_per_iter, m2, win), jnp.float32)],
        compiler_params=pltpu.CompilerParams(
            dimension_semantics=("parallel", "parallel"), vmem_limit_bytes=VMEM_LIMIT),
        name="swa_attention",
    )(sink, qb, kb, vb, _swa_mask_table())


def _tail_kernel(x_ref, g0_ref, b0_ref, na_ref, swa_ref, wg_ref, bin_ref, wna_ref, wswa_ref,
                 wout_ref, bout_ref, g1_ref, b1_ref, w1_ref, bf1_ref, w2_ref, bf2_ref,
                 g2_ref, b2_ref, o_ref, *, sub_rows, ff_chunk):
    for r0 in range(0, x_ref.shape[0], sub_rows):
        rows = slice(r0, r0 + sub_rows)
        h = _layer_norm(x_ref[rows, :], g0_ref[...], b0_ref[...])
        gates = (jnp.dot(h.astype(jnp.bfloat16), wg_ref[...],
                         preferred_element_type=jnp.float32) + bin_ref[:, _C_G:_C_END])
        y_na = jnp.dot(na_ref[rows, :], wna_ref[...], preferred_element_type=jnp.float32)
        y_swa = jnp.dot(swa_ref[rows, :], wswa_ref[...], preferred_element_type=jnp.float32)
        mixed = (jax.nn.sigmoid(gates[:, :D_MODEL]) * y_na
                 + jax.nn.sigmoid(gates[:, D_MODEL:]) * y_swa)
        attn = jnp.dot(mixed.astype(jnp.bfloat16), wout_ref[...],
                       preferred_element_type=jnp.float32) + bout_ref[...]
        h = _layer_norm(DEEPNORM_ALPHA * h + attn, g1_ref[...], b1_ref[...])

        hb = h.astype(jnp.bfloat16)
        acc = jnp.zeros(h.shape, jnp.float32)
        for c in range(D_FF // ff_chunk):
            sl = slice(c * ff_chunk, (c + 1) * ff_chunk)
            u = jnp.dot(hb, w1_ref[:, sl], preferred_element_type=jnp.float32) + bf1_ref[:, sl]
            u = jnp.square(jnp.maximum(u, 0.0))
            acc = acc + jnp.dot(u.astype(jnp.bfloat16), w2_ref[sl, :],
                                preferred_element_type=jnp.float32)
        o_ref[rows, :] = _layer_norm(DEEPNORM_ALPHA * h + acc + bf2_ref[...],
                                     g2_ref[...], b2_ref[...])


def _tail(x2, g0, b0, na, swa, wg, b_in, wna, wswa, wout, bout, g1, b1, w1, bf1, w2, bf2,
          g2, b2, tm, sub_rows, ff_chunk):
    t = x2.shape[0]
    tok = lambda i: (i, 0)
    vec = _const_spec((1, D_MODEL))
    return pl.pallas_call(
        functools.partial(_tail_kernel, sub_rows=sub_rows, ff_chunk=ff_chunk),
        grid=(t // tm,),
        in_specs=[
            pl.BlockSpec((tm, D_MODEL), tok), vec, vec,
            pl.BlockSpec((tm, NA_WIDTH), tok), pl.BlockSpec((tm, SWA_Q_WIDTH), tok),
            _const_spec((D_MODEL, 2 * D_MODEL)), _const_spec((1, _C_END)),
            _const_spec((NA_WIDTH, D_MODEL)), _const_spec((SWA_Q_WIDTH, D_MODEL)),
            _const_spec((D_MODEL, D_MODEL)), vec, vec, vec,
            _const_spec((D_MODEL, D_FF)), _const_spec((1, D_FF)),
            _const_spec((D_FF, D_MODEL)), vec, vec, vec,
        ],
        out_specs=pl.BlockSpec((tm, D_MODEL), tok),
        out_shape=jax.ShapeDtypeStruct((t, D_MODEL), jnp.float32),
        compiler_params=pltpu.CompilerParams(
            dimension_semantics=("parallel",), vmem_limit_bytes=VMEM_LIMIT),
        name="mix_ffn",
    )(x2, g0, b0, na, swa, wg, b_in, wna, wswa, wout, bout, g1, b1, w1, bf1, w2, bf2, g2, b2)


def kernel(x, ln0_g, ln0_b, w_in, b_in, na_rpb, swa_sink, w_branch_na, w_branch_swa,
           w_out, b_out, ln1_g, ln1_b, w_ff1, b_ff1, w_ff2, b_ff2, ln2_g, ln2_b):
    batch, seq, d = x.shape
    assert d == D_MODEL and seq % (NA_KH * GRID_W) == 0 and seq >= 3 * SWA_BLOCK
    assert w_in.shape == (DEPTH, D_MODEL, _C_END) and DEPTH == 1
    bf = jnp.bfloat16
    row = lambda v: v.reshape(1, -1)
    x2 = x.reshape(batch * seq, d)
    b_in_row = row(b_in[0])

    cos_t, sa_t, sb_t = _rotary_tables(seq)
    qa, ka, va, qb, kb, vb = _inproj(
        x2, row(ln0_g), row(ln0_b), w_in[0, :, :_C_G].astype(bf), b_in_row,
        cos_t, sa_t, sb_t, batch, seq, tm=1024, sub_rows=256)

    three = lambda a: a.reshape(batch, seq, a.shape[-1])
    o_na = _na_attention(three(qa), three(ka), three(va), _na_bias_table(na_rpb[0]),
                         rows_per_iter=16)
    o_swa = _swa_attention(three(qb), three(kb), three(vb), swa_sink[0], blocks_per_iter=8)

    out = _tail(x2, row(ln0_g), row(ln0_b), o_na.reshape(batch * seq, -1),
                o_swa.reshape(batch * seq, -1), w_in[0, :, _C_G:].astype(bf), b_in_row,
                w_branch_na[0].astype(bf), w_branch_swa[0].astype(bf), w_out[0].astype(bf),
                row(b_out[0]), row(ln1_g[0]), row(ln1_b[0]),
                w_ff1[0].astype(bf), row(b_ff1[0]), w_ff2[0].astype(bf), row(b_ff2[0]),
                row(ln2_g[0]), row(ln2_b[0]), tm=1024, sub_rows=256, ff_chunk=1024)
    return out.reshape(batch, seq, d)
```

```python
import functools

import numpy as np
import jax
import jax.numpy as jnp
from jax import lax
from jax.experimental import pallas as pl
from jax.experimental.pallas import tpu as pltpu

D_MODEL = 1024
GRID_W = 64
HEAD_DIM = 64
NA_HEADS = 8
NA_KH = 8
NA_KW = 16
NA_WIDTH = NA_HEADS * HEAD_DIM
SWA_Q_HEADS = 8
SWA_KV_HEADS = 2
SWA_WINDOW = 128
SWA_BLOCK = 128
SWA_Q_WIDTH = SWA_Q_HEADS * HEAD_DIM
SWA_KV_WIDTH = SWA_KV_HEADS * HEAD_DIM
ROPE_THETA = 500000.0
ROPE_DIM = HEAD_DIM // 4
D_FF = 4 * D_MODEL
DEPTH = 1
DEEPNORM_ALPHA = (2.0 * DEPTH) ** 0.25
LN_EPS = 1e-5
MASK_VALUE = -1e30
ATTN_SCALE = HEAD_DIM ** -0.5

LANES = 128
HEADS_PER_VREG = LANES // HEAD_DIM
VMEM_LIMIT = 56 * 1024 * 1024

_C_QA = 0
_C_KA = _C_QA + NA_WIDTH
_C_VA = _C_KA + NA_WIDTH
_C_QB = _C_VA + NA_WIDTH
_C_KB = _C_QB + SWA_Q_WIDTH
_C_VB = _C_KB + SWA_KV_WIDTH
_C_G = _C_VB + SWA_KV_WIDTH
_C_END = _C_G + 2 * D_MODEL
_KV_DUP = SWA_KV_HEADS * LANES


def _const_spec(shape):
    nd = len(shape)
    return pl.BlockSpec(shape, lambda *_: (0,) * nd, pipeline_mode=pl.Buffered(1))


def _layer_norm(x, g, b):
    mu = jnp.mean(x, axis=-1, keepdims=True)
    xc = x - mu
    var = jnp.mean(xc * xc, axis=-1, keepdims=True)
    return xc * lax.rsqrt(var + LN_EPS) * g + b


def _dup_heads(y):
    lane = lax.broadcasted_iota(jnp.int32, y.shape, 1)
    swapped = pltpu.roll(y, HEAD_DIM, 1)
    return jnp.concatenate([jnp.where(lane < HEAD_DIM, y, swapped),
                            jnp.where(lane >= HEAD_DIM, y, swapped)], axis=1)


def _inproj_kernel(x_ref, g_ref, b_ref, w_ref, bias_ref, cos_ref, sa_ref, sb_ref,
                   qa_ref, ka_ref, va_ref, qb_ref, kb_ref, vb_ref, *, sub_rows):
    half = ROPE_DIM // 2

    for r0 in range(0, x_ref.shape[0], sub_rows):
        rows = slice(r0, r0 + sub_rows)
        h = _layer_norm(x_ref[rows, :], g_ref[...], b_ref[...])
        hb = h.astype(jnp.bfloat16)

        def proj(c0, c1):
            return (jnp.dot(hb, w_ref[:, c0:c1], preferred_element_type=jnp.float32)
                    + bias_ref[:, c0:c1])

        def rotary(y):
            cos, sa, sb = cos_ref[rows, :], sa_ref[rows, :], sb_ref[rows, :]
            outs = []
            for c in range(y.shape[1] // LANES):
                yc = y[:, c * LANES:(c + 1) * LANES]
                outs.append(yc * cos
                            + pltpu.roll(yc, half, 1) * sa
                            + pltpu.roll(yc, LANES - half, 1) * sb)
            return outs[0] if len(outs) == 1 else jnp.concatenate(outs, axis=1)

        qa_ref[rows, :] = (proj(_C_QA, _C_KA) * ATTN_SCALE).astype(qa_ref.dtype)
        ka_ref[rows, :] = proj(_C_KA, _C_VA).astype(ka_ref.dtype)
        va_ref[rows, :] = proj(_C_VA, _C_QB).astype(va_ref.dtype)
        qb_ref[rows, :] = (rotary(proj(_C_QB, _C_KB)) * ATTN_SCALE).astype(qb_ref.dtype)
        kv = proj(_C_KB, _C_G)
        kb_ref[rows, :] = _dup_heads(rotary(kv[:, :SWA_KV_WIDTH])).astype(kb_ref.dtype)
        vb_ref[rows, :] = _dup_heads(kv[:, SWA_KV_WIDTH:]).astype(vb_ref.dtype)


def _inproj(x2, g, b, w, bias, cos_t, sa_t, sb_t, batch, seq, tm, sub_rows):
    t = batch * seq
    nblk = seq // tm
    tok = lambda bi, i: (bi * nblk + i, 0)
    pos = lambda bi, i: (i, 0)
    bf = jnp.bfloat16
    out_shape = (
        jax.ShapeDtypeStruct((t, NA_WIDTH), bf), jax.ShapeDtypeStruct((t, NA_WIDTH), bf),
        jax.ShapeDtypeStruct((t, NA_WIDTH), bf), jax.ShapeDtypeStruct((t, SWA_Q_WIDTH), bf),
        jax.ShapeDtypeStruct((t, _KV_DUP), bf), jax.ShapeDtypeStruct((t, _KV_DUP), bf),
    )
    return pl.pallas_call(
        functools.partial(_inproj_kernel, sub_rows=sub_rows),
        grid=(batch, nblk),
        in_specs=[
            pl.BlockSpec((tm, D_MODEL), tok),
            _const_spec((1, D_MODEL)), _const_spec((1, D_MODEL)),
            _const_spec((D_MODEL, _C_G)), _const_spec((1, _C_END)),
            pl.BlockSpec((tm, LANES), pos), pl.BlockSpec((tm, LANES), pos),
            pl.BlockSpec((tm, LANES), pos),
        ],
        out_specs=(
            pl.BlockSpec((tm, NA_WIDTH), tok), pl.BlockSpec((tm, NA_WIDTH), tok),
            pl.BlockSpec((tm, NA_WIDTH), tok), pl.BlockSpec((tm, SWA_Q_WIDTH), tok),
            pl.BlockSpec((tm, _KV_DUP), tok), pl.BlockSpec((tm, _KV_DUP), tok),
        ),
        out_shape=out_shape,
        compiler_params=pltpu.CompilerParams(
            dimension_semantics=("parallel", "parallel"), vmem_limit_bytes=VMEM_LIMIT),
        name="ln_inproj_rotary",
    )(x2, g, b, w, bias, cos_t, sa_t, sb_t)


def _rotary_tables(seq):
    half = ROPE_DIM // 2
    inv_freq = np.power(ROPE_THETA, -np.arange(0, ROPE_DIM, 2, dtype=np.float64) / ROPE_DIM)
    ang = np.arange(seq, dtype=np.float64)[:, None] * inv_freq[None, :]
    cos, sin = np.cos(ang), np.sin(ang)
    ones = np.ones((seq, HEAD_DIM - ROPE_DIM))
    zeros = np.zeros((seq, HEAD_DIM - ROPE_DIM))
    zh = np.zeros((seq, half))
    cos_h = np.concatenate([cos, cos, ones], axis=1)
    sa_h = np.concatenate([zh, sin, zeros], axis=1)
    sb_h = np.concatenate([-sin, zh, zeros], axis=1)
    tile = lambda a: jnp.asarray(np.tile(a, (1, HEADS_PER_VREG)), dtype=jnp.float32)
    return tile(cos_h), tile(sa_h), tile(sb_h)


def _stack_heads(q):
    lane = lax.broadcasted_iota(jnp.int32, q.shape, 1)
    zero = jnp.zeros_like(q)
    return jnp.concatenate([jnp.where(lane < HEAD_DIM, q, zero),
                            jnp.where(lane >= HEAD_DIM, q, zero)], axis=0)


def _unstack_heads(o2):
    m = o2.shape[0] // 2
    lane = lax.broadcasted_iota(jnp.int32, (m, LANES), 1)
    return jnp.where(lane < HEAD_DIM, o2[:m], o2[m:])


_NT = (((1,), (1,)), ((), ()))


def _na_bias_kernel(rpb_ref, o_ref):
    w = lax.broadcasted_iota(jnp.int32, (GRID_W, LANES), 0)
    lane = lax.broadcasted_iota(jnp.int32, (GRID_W, LANES), 1)
    c = lane % GRID_W
    col_start = jnp.clip(w - NA_KW // 2, 0, GRID_W - NA_KW)
    inside = (c >= col_start) & (c < col_start + NA_KW)
    lower = lane < GRID_W
    n_dr = 2 * NA_KH - 1
    for e in range(HEADS_PER_VREG):
        lo, hi = [], []
        for dr in range(n_dr):
            x = jnp.broadcast_to(rpb_ref[0, e, dr:dr + 1, :], (GRID_W, LANES))
            lo.append(pltpu.roll(x, LANES - (NA_KW - 1), 1, stride=1, stride_axis=0))
            hi.append(pltpu.roll(x, GRID_W - (NA_KW - 1), 1, stride=1, stride_axis=0))
        for v in range(NA_KH):
            for kk in range(NA_KH // HEADS_PER_VREG):
                dr_a = 2 * kk - v + NA_KH - 1
                tile = jnp.where(inside, jnp.where(lower, lo[dr_a], hi[dr_a + 1]), MASK_VALUE)
                o_ref[v, 0, e * GRID_W:(e + 1) * GRID_W, kk * LANES:(kk + 1) * LANES] = tile


def _na_bias_table(rpb):
    n_pairs = NA_HEADS // HEADS_PER_VREG
    n_dr, n_dc = rpb.shape[1], rpb.shape[2]
    dr_pad = -(-n_dr // 8) * 8
    rp = jnp.pad(rpb, ((0, 0), (0, dr_pad - n_dr), (0, LANES - n_dc)))
    rp = rp.reshape(n_pairs, HEADS_PER_VREG, dr_pad, LANES)
    m2, win = HEADS_PER_VREG * GRID_W, NA_KH * GRID_W
    return pl.pallas_call(
        _na_bias_kernel,
        grid=(n_pairs,),
        in_specs=[pl.BlockSpec((1, HEADS_PER_VREG, dr_pad, LANES), lambda j: (j, 0, 0, 0))],
        out_specs=pl.BlockSpec((NA_KH, 1, m2, win), lambda j: (0, j, 0, 0)),
        out_shape=jax.ShapeDtypeStruct((NA_KH, n_pairs, m2, win), jnp.float32),
        compiler_params=pltpu.CompilerParams(dimension_semantics=("parallel",)),
        name="na_bias_table",
    )(rp)


def _na_kernel(q_ref, k_ref, v_ref, tbl_ref, o_ref, s_scr, *, rows_per_iter, n_rows):
    win = NA_KH * GRID_W
    ones = jnp.ones((win, LANES), jnp.bfloat16)

    def body(i, carry):
        def window(t):
            r = i * rows_per_iter + t
            rs = jnp.clip(r - NA_KH // 2, 0, n_rows - NA_KH)
            return r - rs, pl.multiple_of(rs * GRID_W, GRID_W)

        def q_rows(t):
            return pl.ds(pl.multiple_of((i * rows_per_iter + t) * GRID_W, GRID_W), GRID_W)

        for t in range(rows_per_iter):
            variant, start = window(t)
            lhs = _stack_heads(q_ref[0, q_rows(t), :])
            kw = k_ref[0, pl.ds(start, win), :]
            s = lax.dot_general(lhs, kw, _NT, preferred_element_type=jnp.float32)
            s_scr[t] = s + tbl_ref[variant, 0]
        for t in range(rows_per_iter):
            s = s_scr[t]
            m = jnp.max(s, axis=-1, keepdims=True)
            p = jnp.exp(s - m)
            _, start = window(t)
            vw = jnp.concatenate([v_ref[0, pl.ds(start, win), :], ones], axis=1)
            o2 = jnp.dot(p.astype(jnp.bfloat16), vw, preferred_element_type=jnp.float32)
            o2 = o2[:, :LANES] / o2[:, LANES:]
            o_ref[0, q_rows(t), :] = _unstack_heads(o2).astype(o_ref.dtype)
        return carry

    lax.fori_loop(0, n_rows // rows_per_iter, body, 0)


def _na_attention(qa, ka, va, tbl, rows_per_iter):
    batch, seq, _ = qa.shape
    n_rows = seq // GRID_W
    n_pairs = NA_HEADS // HEADS_PER_VREG
    m2, win = HEADS_PER_VREG * GRID_W, NA_KH * GRID_W
    kern = functools.partial(_na_kernel, rows_per_iter=rows_per_iter, n_rows=n_rows)
    seq_block = pl.BlockSpec((1, seq, LANES), lambda b, j: (b, 0, j))
    return pl.pallas_call(
        kern,
        grid=(batch, n_pairs),
        in_specs=[seq_block, seq_block, seq_block,
                  pl.BlockSpec((NA_KH, 1, m2, win), lambda b, j: (0, j, 0, 0))],
        out_specs=seq_block,
        out_shape=jax.ShapeDtypeStruct(qa.shape, jnp.bfloat16),
        scratch_shapes=[pltpu.VMEM((rows_per_iter, m2, win), jnp.float32)],
        compiler_params=pltpu.CompilerParams(
            dimension_semantics=("parallel", "parallel"), vmem_limit_bytes=VMEM_LIMIT),
        name="na_attention",
    )(qa, ka, va, tbl)


def _swa_mask_table():
    qi = np.arange(SWA_BLOCK)[:, None]
    kc = np.arange(3 * SWA_BLOCK)[None, :]
    tbl = [np.where(np.abs(kc - d * SWA_BLOCK - qi) <= SWA_WINDOW, 0.0, MASK_VALUE)
           for d in range(3)]
    return jnp.asarray(np.stack(tbl), dtype=jnp.float32)


def _swa_kernel(sink_ref, q_ref, k_ref, v_ref, mask_ref, o_ref, s_scr, *,
                blocks_per_iter, seq):
    j = pl.program_id(1)
    blk = SWA_BLOCK
    win = 3 * blk
    row = lax.broadcasted_iota(jnp.int32, (2 * blk, 1), 0)
    sink_col = jnp.where(row < blk, sink_ref[2 * j], sink_ref[2 * j + 1])
    ones = jnp.ones((win, LANES), jnp.bfloat16)

    def body(i, carry):
        def window(u):
            n = i * blocks_per_iter + u
            start = jnp.clip((n - 1) * blk, 0, seq - win)
            return n - start // blk, pl.multiple_of(start, blk)

        def q_rows(u):
            return pl.ds(pl.multiple_of((i * blocks_per_iter + u) * blk, blk), blk)

        for u in range(blocks_per_iter):
            variant, start = window(u)
            lhs = _stack_heads(q_ref[0, q_rows(u), :])
            kw = k_ref[0, pl.ds(start, win), :]
            s = lax.dot_general(lhs, kw, _NT, preferred_element_type=jnp.float32)
            mask = mask_ref[variant]
            s_scr[u] = s + jnp.concatenate([mask, mask], axis=0)
        for u in range(blocks_per_iter):
            s = s_scr[u]
            m = jnp.maximum(jnp.max(s, axis=-1, keepdims=True), sink_col)
            e = jnp.exp(s - m)
            _, start = window(u)
            vw = jnp.concatenate([v_ref[0, pl.ds(start, win), :], ones], axis=1)
            o2 = jnp.dot(e.astype(jnp.bfloat16), vw, preferred_element_type=jnp.float32)
            o2 = o2[:, :LANES] / (o2[:, LANES:] + jnp.exp(sink_col - m))
            o_ref[0, q_rows(u), :] = _unstack_heads(o2).astype(o_ref.dtype)
        return carry

    lax.fori_loop(0, seq // (blocks_per_iter * blk), body, 0)


def _swa_attention(qb, kb, vb, sink, blocks_per_iter):
    batch, seq, _ = qb.shape
    n_pairs = SWA_Q_HEADS // HEADS_PER_VREG
    pairs_per_group = n_pairs // SWA_KV_HEADS
    m2, win = HEADS_PER_VREG * SWA_BLOCK, 3 * SWA_BLOCK
    kern = functools.partial(_swa_kernel, blocks_per_iter=blocks_per_iter, seq=seq)
    q_block = pl.BlockSpec((1, seq, LANES), lambda b, j: (b, 0, j))
    kv_block = pl.BlockSpec((1, seq, LANES), lambda b, j: (b, 0, j // pairs_per_group))
    return pl.pallas_call(
        kern,
        grid=(batch, n_pairs),
        in_specs=[pl.BlockSpec(memory_space=pltpu.SMEM), q_block, kv_block, kv_block,
                  _const_spec((3, SWA_BLOCK, win))],
        out_specs=q_block,
        out_shape=jax.ShapeDtypeStruct(qb.shape, jnp.bfloat16),
        scratch_shapes=[pltpu.VMEM((blocks_per_iter, m2, win), jnp.float32)],
        compiler_params=pltpu.CompilerParams(
            dimension_semantics=("parallel", "parallel"), vmem_limit_bytes=VMEM_LIMIT),
        name="swa_attention",
    )(sink, qb, kb, vb, _swa_mask_table())


W_CHUNK = D_MODEL


def _stage_weights_bf16(pieces, stage_ref, sem):
    def copy(k):
        src = pieces[k][0]
        dst = stage_ref.at[k % 2, pl.ds(0, src.shape[0]), :]
        return pltpu.make_async_copy(src, dst, sem.at[k % 2])

    copy(0).start()
    for k, (src, dst) in enumerate(pieces):
        if k + 1 < len(pieces):
            copy(k + 1).start()
        copy(k).wait()
        dst[...] = stage_ref[k % 2, 0:src.shape[0], :].astype(dst.dtype)


def _tail_kernel(x_ref, g0_ref, b0_ref, na_ref, swa_ref, bin_ref, bout_ref, g1_ref, b1_ref,
                 bf1_ref, bf2_ref, g2_ref, b2_ref,
                 win_hbm, wna_hbm, wswa_hbm, wout_hbm, w1_hbm, w2_hbm,
                 o_ref,
                 wg_ref, wna_ref, wswa_ref, wout_ref, w1_ref, w2_ref, stage_ref, sem,
                 *, sub_rows, ff_chunk):
    @pl.when(pl.program_id(0) == 0)
    def _():
        cols = lambda c: pl.ds(c * W_CHUNK, W_CHUNK)
        pieces = [(win_hbm.at[0, :, pl.ds(_C_G + c * W_CHUNK, W_CHUNK)], wg_ref.at[:, cols(c)])
                  for c in range(2 * D_MODEL // W_CHUNK)]
        pieces += [(wna_hbm.at[0], wna_ref), (wswa_hbm.at[0], wswa_ref),
                   (wout_hbm.at[0], wout_ref)]
        pieces += [(w1_hbm.at[0, :, cols(c)], w1_ref.at[:, cols(c)])
                   for c in range(D_FF // W_CHUNK)]
        pieces += [(w2_hbm.at[0, cols(c), :], w2_ref.at[cols(c), :])
                   for c in range(D_FF // W_CHUNK)]
        _stage_weights_bf16(pieces, stage_ref, sem)

    for r0 in range(0, x_ref.shape[0], sub_rows):
        rows = slice(r0, r0 + sub_rows)
        h = _layer_norm(x_ref[rows, :], g0_ref[...], b0_ref[...])
        gates = (jnp.dot(h.astype(jnp.bfloat16), wg_ref[...],
                         preferred_element_type=jnp.float32) + bin_ref[:, _C_G:_C_END])
        y_na = jnp.dot(na_ref[rows, :], wna_ref[...], preferred_element_type=jnp.float32)
        y_swa = jnp.dot(swa_ref[rows, :], wswa_ref[...], preferred_element_type=jnp.float32)
        mixed = (jax.nn.sigmoid(gates[:, :D_MODEL]) * y_na
                 + jax.nn.sigmoid(gates[:, D_MODEL:]) * y_swa)
        attn = jnp.dot(mixed.astype(jnp.bfloat16), wout_ref[...],
                       preferred_element_type=jnp.float32) + bout_ref[...]
        h = _layer_norm(DEEPNORM_ALPHA * h + attn, g1_ref[...], b1_ref[...])

        hb = h.astype(jnp.bfloat16)
        acc = jnp.zeros(h.shape, jnp.float32)
        for c in range(D_FF // ff_chunk):
            sl = slice(c * ff_chunk, (c + 1) * ff_chunk)
            u = jnp.dot(hb, w1_ref[:, sl], preferred_element_type=jnp.float32) + bf1_ref[:, sl]
            u = jnp.square(jnp.maximum(u, 0.0))
            acc = acc + jnp.dot(u.astype(jnp.bfloat16), w2_ref[sl, :],
                                preferred_element_type=jnp.float32)
        o_ref[rows, :] = _layer_norm(DEEPNORM_ALPHA * h + acc + bf2_ref[...],
                                     g2_ref[...], b2_ref[...])


def _tail(x2, g0, b0, na, swa, b_in, bout, g1, b1, bf1, bf2, g2, b2,
          w_in, w_na, w_swa, w_out, w_ff1, w_ff2, tm, sub_rows, ff_chunk):
    t = x2.shape[0]
    tok = lambda i: (i, 0)
    vec = _const_spec((1, D_MODEL))
    hbm = pl.BlockSpec(memory_space=pl.ANY)
    bf = jnp.bfloat16
    return pl.pallas_call(
        functools.partial(_tail_kernel, sub_rows=sub_rows, ff_chunk=ff_chunk),
        grid=(t // tm,),
        in_specs=[
            pl.BlockSpec((tm, D_MODEL), tok), vec, vec,
            pl.BlockSpec((tm, NA_WIDTH), tok), pl.BlockSpec((tm, SWA_Q_WIDTH), tok),
            _const_spec((1, _C_END)), vec, vec, vec,
            _const_spec((1, D_FF)), vec, vec, vec,
            hbm, hbm, hbm, hbm, hbm, hbm,
        ],
        out_specs=pl.BlockSpec((tm, D_MODEL), tok),
        out_shape=jax.ShapeDtypeStruct((t, D_MODEL), jnp.float32),
        scratch_shapes=[
            pltpu.VMEM((D_MODEL, 2 * D_MODEL), bf),
            pltpu.VMEM((NA_WIDTH, D_MODEL), bf), pltpu.VMEM((SWA_Q_WIDTH, D_MODEL), bf),
            pltpu.VMEM((D_MODEL, D_MODEL), bf),
            pltpu.VMEM((D_MODEL, D_FF), bf), pltpu.VMEM((D_FF, D_MODEL), bf),
            pltpu.VMEM((2, W_CHUNK, D_MODEL), jnp.float32),
            pltpu.SemaphoreType.DMA((2,)),
        ],
        compiler_params=pltpu.CompilerParams(
            dimension_semantics=("arbitrary",), vmem_limit_bytes=VMEM_LIMIT),
        name="mix_ffn",
    )(x2, g0, b0, na, swa, b_in, bout, g1, b1, bf1, bf2, g2, b2,
      w_in, w_na, w_swa, w_out, w_ff1, w_ff2)


def kernel(x, ln0_g, ln0_b, w_in, b_in, na_rpb, swa_sink, w_branch_na, w_branch_swa,
           w_out, b_out, ln1_g, ln1_b, w_ff1, b_ff1, w_ff2, b_ff2, ln2_g, ln2_b):
    batch, seq, d = x.shape
    assert d == D_MODEL and seq % (NA_KH * GRID_W) == 0 and seq >= 3 * SWA_BLOCK
    assert w_in.shape == (DEPTH, D_MODEL, _C_END) and DEPTH == 1
    bf = jnp.bfloat16
    row = lambda v: v.reshape(1, -1)
    x2 = x.reshape(batch * seq, d)
    b_in_row = row(b_in[0])

    cos_t, sa_t, sb_t = _rotary_tables(seq)
    qa, ka, va, qb, kb, vb = _inproj(
        x2, row(ln0_g), row(ln0_b), w_in[0, :, :_C_G].astype(bf), b_in_row,
        cos_t, sa_t, sb_t, batch, seq, tm=1024, sub_rows=256)

    three = lambda a: a.reshape(batch, seq, a.shape[-1])
    o_na = _na_attention(three(qa), three(ka), three(va), _na_bias_table(na_rpb[0]),
                         rows_per_iter=16)
    o_swa = _swa_attention(three(qb), three(kb), three(vb), swa_sink[0], blocks_per_iter=8)

    out = _tail(x2, row(ln0_g), row(ln0_b), o_na.reshape(batch * seq, -1),
                o_swa.reshape(batch * seq, -1), b_in_row, row(b_out[0]),
                row(ln1_g[0]), row(ln1_b[0]), row(b_ff1[0]), row(b_ff2[0]),
                row(ln2_g[0]), row(ln2_b[0]),
                w_in, w_branch_na, w_branch_swa, w_out, w_ff1, w_ff2,
                tm=512, sub_rows=256, ff_chunk=1024)
    return out.reshape(batch, seq, d)
```

```python
import functools

import numpy as np
import jax
import jax.numpy as jnp
from jax import lax
from jax.experimental import pallas as pl
from jax.experimental.pallas import tpu as pltpu

D_MODEL = 1024
GRID_W = 64
HEAD_DIM = 64
NA_HEADS = 8
NA_KH = 8
NA_KW = 16
NA_WIDTH = NA_HEADS * HEAD_DIM
SWA_Q_HEADS = 8
SWA_KV_HEADS = 2
SWA_WINDOW = 128
SWA_BLOCK = 128
SWA_Q_WIDTH = SWA_Q_HEADS * HEAD_DIM
SWA_KV_WIDTH = SWA_KV_HEADS * HEAD_DIM
ROPE_THETA = 500000.0
ROPE_DIM = HEAD_DIM // 4
D_FF = 4 * D_MODEL
DEPTH = 1
DEEPNORM_ALPHA = (2.0 * DEPTH) ** 0.25
LN_EPS = 1e-5
MASK_VALUE = -1e30
ATTN_SCALE = HEAD_DIM ** -0.5

LANES = 128
HEADS_PER_VREG = LANES // HEAD_DIM
VMEM_LIMIT = 56 * 1024 * 1024

_C_QA = 0
_C_KA = _C_QA + NA_WIDTH
_C_VA = _C_KA + NA_WIDTH
_C_QB = _C_VA + NA_WIDTH
_C_KB = _C_QB + SWA_Q_WIDTH
_C_VB = _C_KB + SWA_KV_WIDTH
_C_G = _C_VB + SWA_KV_WIDTH
_C_END = _C_G + 2 * D_MODEL
_KV_DUP = SWA_KV_HEADS * LANES


def _const_spec(shape):
    nd = len(shape)
    return pl.BlockSpec(shape, lambda *_: (0,) * nd, pipeline_mode=pl.Buffered(1))


def _layer_norm(x, g, b):
    mu = jnp.mean(x, axis=-1, keepdims=True)
    xc = x - mu
    var = jnp.mean(xc * xc, axis=-1, keepdims=True)
    return xc * lax.rsqrt(var + LN_EPS) * g + b


def _stage_weights_bf16(pieces, stage_ref, sem):
    def copy(k):
        src = pieces[k][0]
        dst = stage_ref.at[k % 2, pl.ds(0, src.shape[0]), :]
        return pltpu.make_async_copy(src, dst, sem.at[k % 2])

    copy(0).start()
    for k, (src, dst) in enumerate(pieces):
        if k + 1 < len(pieces):
            copy(k + 1).start()
        copy(k).wait()
        dst[...] = stage_ref[k % 2, 0:src.shape[0], :].astype(dst.dtype)


def _dup_heads(y):
    lane = lax.broadcasted_iota(jnp.int32, y.shape, 1)
    swapped = pltpu.roll(y, HEAD_DIM, 1)
    return jnp.concatenate([jnp.where(lane < HEAD_DIM, y, swapped),
                            jnp.where(lane >= HEAD_DIM, y, swapped)], axis=1)


QKV_CHUNK = _C_G // 3


def _inproj_kernel(x_ref, g_ref, b_ref, bias_ref, cos_ref, sa_ref, sb_ref, win_hbm,
                   qa_ref, ka_ref, va_ref, qb_ref, kb_ref, vb_ref,
                   w_ref, stage_ref, sem, *, sub_rows):
    half = ROPE_DIM // 2

    @pl.when((pl.program_id(0) == 0) & (pl.program_id(1) == 0))
    def _():
        cols = lambda c: pl.ds(c * QKV_CHUNK, QKV_CHUNK)
        _stage_weights_bf16([(win_hbm.at[0, :, cols(c)], w_ref.at[:, cols(c)])
                             for c in range(_C_G // QKV_CHUNK)], stage_ref, sem)

    for r0 in range(0, x_ref.shape[0], sub_rows):
        rows = slice(r0, r0 + sub_rows)
        h = _layer_norm(x_ref[rows, :], g_ref[...], b_ref[...])
        hb = h.astype(jnp.bfloat16)

        def proj(c0, c1):
            return (jnp.dot(hb, w_ref[:, c0:c1], preferred_element_type=jnp.float32)
                    + bias_ref[:, c0:c1])

        def rotary(y):
            cos, sa, sb = cos_ref[rows, :], sa_ref[rows, :], sb_ref[rows, :]
            outs = []
            for c in range(y.shape[1] // LANES):
                yc = y[:, c * LANES:(c + 1) * LANES]
                outs.append(yc * cos
                            + pltpu.roll(yc, half, 1) * sa
                            + pltpu.roll(yc, LANES - half, 1) * sb)
            return outs[0] if len(outs) == 1 else jnp.concatenate(outs, axis=1)

        qa_ref[rows, :] = (proj(_C_QA, _C_KA) * ATTN_SCALE).astype(qa_ref.dtype)
        ka_ref[rows, :] = proj(_C_KA, _C_VA).astype(ka_ref.dtype)
        va_ref[rows, :] = proj(_C_VA, _C_QB).astype(va_ref.dtype)
        qb_ref[rows, :] = (rotary(proj(_C_QB, _C_KB)) * ATTN_SCALE).astype(qb_ref.dtype)
        kv = proj(_C_KB, _C_G)
        kb_ref[rows, :] = _dup_heads(rotary(kv[:, :SWA_KV_WIDTH])).astype(kb_ref.dtype)
        vb_ref[rows, :] = _dup_heads(kv[:, SWA_KV_WIDTH:]).astype(vb_ref.dtype)


def _inproj(x2, g, b, bias, cos_t, sa_t, sb_t, w_in, batch, seq, tm, sub_rows):
    t = batch * seq
    nblk = seq // tm
    tok = lambda bi, i: (bi * nblk + i, 0)
    pos = lambda bi, i: (i, 0)
    bf = jnp.bfloat16
    out_shape = (
        jax.ShapeDtypeStruct((t, NA_WIDTH), bf), jax.ShapeDtypeStruct((t, NA_WIDTH), bf),
        jax.ShapeDtypeStruct((t, NA_WIDTH), bf), jax.ShapeDtypeStruct((t, SWA_Q_WIDTH), bf),
        jax.ShapeDtypeStruct((t, _KV_DUP), bf), jax.ShapeDtypeStruct((t, _KV_DUP), bf),
    )
    return pl.pallas_call(
        functools.partial(_inproj_kernel, sub_rows=sub_rows),
        grid=(batch, nblk),
        in_specs=[
            pl.BlockSpec((tm, D_MODEL), tok),
            _const_spec((1, D_MODEL)), _const_spec((1, D_MODEL)),
            _const_spec((1, _C_END)),
            pl.BlockSpec((tm, LANES), pos), pl.BlockSpec((tm, LANES), pos),
            pl.BlockSpec((tm, LANES), pos),
            pl.BlockSpec(memory_space=pl.ANY),
        ],
        out_specs=(
            pl.BlockSpec((tm, NA_WIDTH), tok), pl.BlockSpec((tm, NA_WIDTH), tok),
            pl.BlockSpec((tm, NA_WIDTH), tok), pl.BlockSpec((tm, SWA_Q_WIDTH), tok),
            pl.BlockSpec((tm, _KV_DUP), tok), pl.BlockSpec((tm, _KV_DUP), tok),
        ),
        out_shape=out_shape,
        scratch_shapes=[
            pltpu.VMEM((D_MODEL, _C_G), bf),
            pltpu.VMEM((2, D_MODEL, QKV_CHUNK), jnp.float32),
            pltpu.SemaphoreType.DMA((2,)),
        ],
        compiler_params=pltpu.CompilerParams(
            dimension_semantics=("arbitrary", "arbitrary"), vmem_limit_bytes=VMEM_LIMIT),
        name="ln_inproj_rotary",
    )(x2, g, b, bias, cos_t, sa_t, sb_t, w_in)


def _rotary_tables(seq):
    half = ROPE_DIM // 2
    inv_freq = np.power(ROPE_THETA, -np.arange(0, ROPE_DIM, 2, dtype=np.float64) / ROPE_DIM)
    ang = np.arange(seq, dtype=np.float64)[:, None] * inv_freq[None, :]
    cos, sin = np.cos(ang), np.sin(ang)
    ones = np.ones((seq, HEAD_DIM - ROPE_DIM))
    zeros = np.zeros((seq, HEAD_DIM - ROPE_DIM))
    zh = np.zeros((seq, half))
    cos_h = np.concatenate([cos, cos, ones], axis=1)
    sa_h = np.concatenate([zh, sin, zeros], axis=1)
    sb_h = np.concatenate([-sin, zh, zeros], axis=1)
    tile = lambda a: jnp.asarray(np.tile(a, (1, HEADS_PER_VREG)), dtype=jnp.float32)
    return tile(cos_h), tile(sa_h), tile(sb_h)


def _stack_heads(q):
    lane = lax.broadcasted_iota(jnp.int32, q.shape, 1)
    zero = jnp.zeros_like(q)
    return jnp.concatenate([jnp.where(lane < HEAD_DIM, q, zero),
                            jnp.where(lane >= HEAD_DIM, q, zero)], axis=0)


def _unstack_heads(o2):
    m = o2.shape[0] // 2
    lane = lax.broadcasted_iota(jnp.int32, (m, LANES), 1)
    return jnp.where(lane < HEAD_DIM, o2[:m], o2[m:])


_NT = (((1,), (1,)), ((), ()))


def _na_bias_kernel(rpb_ref, o_ref):
    w = lax.broadcasted_iota(jnp.int32, (GRID_W, LANES), 0)
    lane = lax.broadcasted_iota(jnp.int32, (GRID_W, LANES), 1)
    c = lane % GRID_W
    col_start = jnp.clip(w - NA_KW // 2, 0, GRID_W - NA_KW)
    inside = (c >= col_start) & (c < col_start + NA_KW)
    lower = lane < GRID_W
    n_dr = 2 * NA_KH - 1
    for e in range(HEADS_PER_VREG):
        lo, hi = [], []
        for dr in range(n_dr):
            x = jnp.broadcast_to(rpb_ref[0, e, dr:dr + 1, :], (GRID_W, LANES))
            lo.append(pltpu.roll(x, LANES - (NA_KW - 1), 1, stride=1, stride_axis=0))
            hi.append(pltpu.roll(x, GRID_W - (NA_KW - 1), 1, stride=1, stride_axis=0))
        for v in range(NA_KH):
            for kk in range(NA_KH // HEADS_PER_VREG):
                dr_a = 2 * kk - v + NA_KH - 1
                tile = jnp.where(inside, jnp.where(lower, lo[dr_a], hi[dr_a + 1]), MASK_VALUE)
                o_ref[v, 0, e * GRID_W:(e + 1) * GRID_W, kk * LANES:(kk + 1) * LANES] = tile


def _na_bias_table(rpb):
    n_pairs = NA_HEADS // HEADS_PER_VREG
    n_dr, n_dc = rpb.shape[1], rpb.shape[2]
    dr_pad = -(-n_dr // 8) * 8
    rp = jnp.pad(rpb, ((0, 0), (0, dr_pad - n_dr), (0, LANES - n_dc)))
    rp = rp.reshape(n_pairs, HEADS_PER_VREG, dr_pad, LANES)
    m2, win = HEADS_PER_VREG * GRID_W, NA_KH * GRID_W
    return pl.pallas_call(
        _na_bias_kernel,
        grid=(n_pairs,),
        in_specs=[pl.BlockSpec((1, HEADS_PER_VREG, dr_pad, LANES), lambda j: (j, 0, 0, 0))],
        out_specs=pl.BlockSpec((NA_KH, 1, m2, win), lambda j: (0, j, 0, 0)),
        out_shape=jax.ShapeDtypeStruct((NA_KH, n_pairs, m2, win), jnp.float32),
        compiler_params=pltpu.CompilerParams(dimension_semantics=("parallel",)),
        name="na_bias_table",
    )(rp)


def _na_kernel(q_ref, k_ref, v_ref, tbl_ref, o_ref, s_scr, *, rows_per_iter, n_rows):
    win = NA_KH * GRID_W
    ones = jnp.ones((win, LANES), jnp.bfloat16)

    def body(i, carry):
        def window(t):
            r = i * rows_per_iter + t
            rs = jnp.clip(r - NA_KH // 2, 0, n_rows - NA_KH)
            return r - rs, pl.multiple_of(rs * GRID_W, GRID_W)

        def q_rows(t):
            return pl.ds(pl.multiple_of((i * rows_per_iter + t) * GRID_W, GRID_W), GRID_W)

        for t in range(rows_per_iter):
            variant, start = window(t)
            lhs = _stack_heads(q_ref[0, q_rows(t), :])
            kw = k_ref[0, pl.ds(start, win), :]
            s = lax.dot_general(lhs, kw, _NT, preferred_element_type=jnp.float32)
            s_scr[t] = s + tbl_ref[variant, 0]
        for t in range(rows_per_iter):
            s = s_scr[t]
            m = jnp.max(s, axis=-1, keepdims=True)
            p = jnp.exp(s - m)
            _, start = window(t)
            vw = jnp.concatenate([v_ref[0, pl.ds(start, win), :], ones], axis=1)
            o2 = jnp.dot(p.astype(jnp.bfloat16), vw, preferred_element_type=jnp.float32)
            o2 = o2[:, :LANES] / o2[:, LANES:]
            o_ref[0, q_rows(t), :] = _unstack_heads(o2).astype(o_ref.dtype)
        return carry

    lax.fori_loop(0, n_rows // rows_per_iter, body, 0)


def _na_attention(qa, ka, va, tbl, rows_per_iter):
    batch, seq, _ = qa.shape
    n_rows = seq // GRID_W
    n_pairs = NA_HEADS // HEADS_PER_VREG
    m2, win = HEADS_PER_VREG * GRID_W, NA_KH * GRID_W
    kern = functools.partial(_na_kernel, rows_per_iter=rows_per_iter, n_rows=n_rows)
    seq_block = pl.BlockSpec((1, seq, LANES), lambda b, j: (b, 0, j))
    return pl.pallas_call(
        kern,
        grid=(batch, n_pairs),
        in_specs=[seq_block, seq_block, seq_block,
                  pl.BlockSpec((NA_KH, 1, m2, win), lambda b, j: (0, j, 0, 0))],
        out_specs=seq_block,
        out_shape=jax.ShapeDtypeStruct(qa.shape, jnp.bfloat16),
        scratch_shapes=[pltpu.VMEM((rows_per_iter, m2, win), jnp.float32)],
        compiler_params=pltpu.CompilerParams(
            dimension_semantics=("parallel", "parallel"), vmem_limit_bytes=VMEM_LIMIT),
        name="na_attention",
    )(qa, ka, va, tbl)


def _swa_mask_table():
    qi = np.arange(SWA_BLOCK)[:, None]
    kc = np.arange(3 * SWA_BLOCK)[None, :]
    tbl = [np.where(np.abs(kc - d * SWA_BLOCK - qi) <= SWA_WINDOW, 0.0, MASK_VALUE)
           for d in range(3)]
    return jnp.asarray(np.stack(tbl), dtype=jnp.float32)


def _swa_kernel(sink_ref, q_ref, k_ref, v_ref, mask_ref, o_ref, s_scr, *,
                blocks_per_iter, seq):
    j = pl.program_id(1)
    blk = SWA_BLOCK
    win = 3 * blk
    row = lax.broadcasted_iota(jnp.int32, (2 * blk, 1), 0)
    sink_col = jnp.where(row < blk, sink_ref[2 * j], sink_ref[2 * j + 1])
    ones = jnp.ones((win, LANES), jnp.bfloat16)

    def body(i, carry):
        def window(u):
            n = i * blocks_per_iter + u
            start = jnp.clip((n - 1) * blk, 0, seq - win)
            return n - start // blk, pl.multiple_of(start, blk)

        def q_rows(u):
            return pl.ds(pl.multiple_of((i * blocks_per_iter + u) * blk, blk), blk)

        for u in range(blocks_per_iter):
            variant, start = window(u)
            lhs = _stack_heads(q_ref[0, q_rows(u), :])
            kw = k_ref[0, pl.ds(start, win), :]
            s = lax.dot_general(lhs, kw, _NT, preferred_element_type=jnp.float32)
            mask = mask_ref[variant]
            s_scr[u] = s + jnp.concatenate([mask, mask], axis=0)
        for u in range(blocks_per_iter):
            s = s_scr[u]
            m = jnp.maximum(jnp.max(s, axis=-1, keepdims=True), sink_col)
            e = jnp.exp(s - m)
            _, start = window(u)
            vw = jnp.concatenate([v_ref[0, pl.ds(start, win), :], ones], axis=1)
            o2 = jnp.dot(e.astype(jnp.bfloat16), vw, preferred_element_type=jnp.float32)
            o2 = o2[:, :LANES] / (o2[:, LANES:] + jnp.exp(sink_col - m))
            o_ref[0, q_rows(u), :] = _unstack_heads(o2).astype(o_ref.dtype)
        return carry

    lax.fori_loop(0, seq // (blocks_per_iter * blk), body, 0)


def _swa_attention(qb, kb, vb, sink, blocks_per_iter):
    batch, seq, _ = qb.shape
    n_pairs = SWA_Q_HEADS // HEADS_PER_VREG
    pairs_per_group = n_pairs // SWA_KV_HEADS
    m2, win = HEADS_PER_VREG * SWA_BLOCK, 3 * SWA_BLOCK
    kern = functools.partial(_swa_kernel, blocks_per_iter=blocks_per_iter, seq=seq)
    q_block = pl.BlockSpec((1, seq, LANES), lambda b, j: (b, 0, j))
    kv_block = pl.BlockSpec((1, seq, LANES), lambda b, j: (b, 0, j // pairs_per_group))
    return pl.pallas_call(
        kern,
        grid=(batch, n_pairs),
        in_specs=[pl.BlockSpec(memory_space=pltpu.SMEM), q_block, kv_block, kv_block,
                  _const_spec((3, SWA_BLOCK, win))],
        out_specs=q_block,
        out_shape=jax.ShapeDtypeStruct(qb.shape, jnp.bfloat16),
        scratch_shapes=[pltpu.VMEM((blocks_per_iter, m2, win), jnp.float32)],
        compiler_params=pltpu.CompilerParams(
            dimension_semantics=("parallel", "parallel"), vmem_limit_bytes=VMEM_LIMIT),
        name="swa_attention",
    )(sink, qb, kb, vb, _swa_mask_table())


W_CHUNK = D_MODEL


def _tail_kernel(x_ref, g0_ref, b0_ref, na_ref, swa_ref, bin_ref, bout_ref, g1_ref, b1_ref,
                 bf1_ref, bf2_ref, g2_ref, b2_ref,
                 win_hbm, wna_hbm, wswa_hbm, wout_hbm, w1_hbm, w2_hbm,
                 o_ref,
                 wg_ref, wna_ref, wswa_ref, wout_ref, w1_ref, w2_ref, stage_ref, sem,
                 *, sub_rows, ff_chunk):
    @pl.when(pl.program_id(0) == 0)
    def _():
        cols = lambda c: pl.ds(c * W_CHUNK, W_CHUNK)
        pieces = [(win_hbm.at[0, :, pl.ds(_C_G + c * W_CHUNK, W_CHUNK)], wg_ref.at[:, cols(c)])
                  for c in range(2 * D_MODEL // W_CHUNK)]
        pieces += [(wna_hbm.at[0], wna_ref), (wswa_hbm.at[0], wswa_ref),
                   (wout_hbm.at[0], wout_ref)]
        pieces += [(w1_hbm.at[0, :, cols(c)], w1_ref.at[:, cols(c)])
                   for c in range(D_FF // W_CHUNK)]
        pieces += [(w2_hbm.at[0, cols(c), :], w2_ref.at[cols(c), :])
                   for c in range(D_FF // W_CHUNK)]
        _stage_weights_bf16(pieces, stage_ref, sem)

    for r0 in range(0, x_ref.shape[0], sub_rows):
        rows = slice(r0, r0 + sub_rows)
        h = _layer_norm(x_ref[rows, :], g0_ref[...], b0_ref[...])
        gates = (jnp.dot(h.astype(jnp.bfloat16), wg_ref[...],
                         preferred_element_type=jnp.float32) + bin_ref[:, _C_G:_C_END])
        y_na = jnp.dot(na_ref[rows, :], wna_ref[...], preferred_element_type=jnp.float32)
        y_swa = jnp.dot(swa_ref[rows, :], wswa_ref[...], preferred_element_type=jnp.float32)
        mixed = (jax.nn.sigmoid(gates[:, :D_MODEL]) * y_na
                 + jax.nn.sigmoid(gates[:, D_MODEL:]) * y_swa)
        attn = jnp.dot(mixed.astype(jnp.bfloat16), wout_ref[...],
                       preferred_element_type=jnp.float32) + bout_ref[...]
        h = _layer_norm(DEEPNORM_ALPHA * h + attn, g1_ref[...], b1_ref[...])

        hb = h.astype(jnp.bfloat16)
        acc = jnp.zeros(h.shape, jnp.float32)
        for c in range(D_FF // ff_chunk):
            sl = slice(c * ff_chunk, (c + 1) * ff_chunk)
            u = jnp.dot(hb, w1_ref[:, sl], preferred_element_type=jnp.float32) + bf1_ref[:, sl]
            u = jnp.square(jnp.maximum(u, 0.0))
            acc = acc + jnp.dot(u.astype(jnp.bfloat16), w2_ref[sl, :],
                                preferred_element_type=jnp.float32)
        o_ref[rows, :] = _layer_norm(DEEPNORM_ALPHA * h + acc + bf2_ref[...],
                                     g2_ref[...], b2_ref[...])


def _tail(x2, g0, b0, na, swa, b_in, bout, g1, b1, bf1, bf2, g2, b2,
          w_in, w_na, w_swa, w_out, w_ff1, w_ff2, tm, sub_rows, ff_chunk):
    t = x2.shape[0]
    tok = lambda i: (i, 0)
    vec = _const_spec((1, D_MODEL))
    hbm = pl.BlockSpec(memory_space=pl.ANY)
    bf = jnp.bfloat16
    return pl.pallas_call(
        functools.partial(_tail_kernel, sub_rows=sub_rows, ff_chunk=ff_chunk),
        grid=(t // tm,),
        in_specs=[
            pl.BlockSpec((tm, D_MODEL), tok), vec, vec,
            pl.BlockSpec((tm, NA_WIDTH), tok), pl.BlockSpec((tm, SWA_Q_WIDTH), tok),
            _const_spec((1, _C_END)), vec, vec, vec,
            _const_spec((1, D_FF)), vec, vec, vec,
            hbm, hbm, hbm, hbm, hbm, hbm,
        ],
        out_specs=pl.BlockSpec((tm, D_MODEL), tok),
        out_shape=jax.ShapeDtypeStruct((t, D_MODEL), jnp.float32),
        scratch_shapes=[
            pltpu.VMEM((D_MODEL, 2 * D_MODEL), bf),
            pltpu.VMEM((NA_WIDTH, D_MODEL), bf), pltpu.VMEM((SWA_Q_WIDTH, D_MODEL), bf),
            pltpu.VMEM((D_MODEL, D_MODEL), bf),
            pltpu.VMEM((D_MODEL, D_FF), bf), pltpu.VMEM((D_FF, D_MODEL), bf),
            pltpu.VMEM((2, W_CHUNK, D_MODEL), jnp.float32),
            pltpu.SemaphoreType.DMA((2,)),
        ],
        compiler_params=pltpu.CompilerParams(
            dimension_semantics=("arbitrary",), vmem_limit_bytes=VMEM_LIMIT),
        name="mix_ffn",
    )(x2, g0, b0, na, swa, b_in, bout, g1, b1, bf1, bf2, g2, b2,
      w_in, w_na, w_swa, w_out, w_ff1, w_ff2)


def kernel(x, ln0_g, ln0_b, w_in, b_in, na_rpb, swa_sink, w_branch_na, w_branch_swa,
           w_out, b_out, ln1_g, ln1_b, w_ff1, b_ff1, w_ff2, b_ff2, ln2_g, ln2_b):
    batch, seq, d = x.shape
    assert d == D_MODEL and seq % (NA_KH * GRID_W) == 0 and seq >= 3 * SWA_BLOCK
    assert w_in.shape == (DEPTH, D_MODEL, _C_END) and DEPTH == 1
    row = lambda v: v.reshape(1, -1)
    x2 = x.reshape(batch * seq, d)
    b_in_row = row(b_in[0])

    cos_t, sa_t, sb_t = _rotary_tables(seq)
    qa, ka, va, qb, kb, vb = _inproj(
        x2, row(ln0_g), row(ln0_b), b_in_row, cos_t, sa_t, sb_t, w_in,
        batch, seq, tm=1024, sub_rows=256)

    three = lambda a: a.reshape(batch, seq, a.shape[-1])
    o_na = _na_attention(three(qa), three(ka), three(va), _na_bias_table(na_rpb[0]),
                         rows_per_iter=16)
    o_swa = _swa_attention(three(qb), three(kb), three(vb), swa_sink[0], blocks_per_iter=8)

    out = _tail(x2, row(ln0_g), row(ln0_b), o_na.reshape(batch * seq, -1),
                o_swa.reshape(batch * seq, -1), b_in_row, row(b_out[0]),
                row(ln1_g[0]), row(ln1_b[0]), row(b_ff1[0]), row(b_ff2[0]),
                row(ln2_g[0]), row(ln2_b[0]),
                w_in, w_branch_na, w_branch_swa, w_out, w_ff1, w_ff2,
                tm=512, sub_rows=256, ff_chunk=1024)
    return out.reshape(batch, seq, d)
```

```python
import functools

import numpy as np
import jax
import jax.numpy as jnp
from jax import lax
from jax.experimental import pallas as pl
from jax.experimental.pallas import tpu as pltpu

D_MODEL = 1024
GRID_W = 64
HEAD_DIM = 64
NA_HEADS = 8
NA_KH = 8
NA_KW = 16
NA_WIDTH = NA_HEADS * HEAD_DIM
SWA_Q_HEADS = 8
SWA_KV_HEADS = 2
SWA_WINDOW = 128
SWA_BLOCK = 128
SWA_Q_WIDTH = SWA_Q_HEADS * HEAD_DIM
SWA_KV_WIDTH = SWA_KV_HEADS * HEAD_DIM
ROPE_THETA = 500000.0
ROPE_DIM = HEAD_DIM // 4
D_FF = 4 * D_MODEL
DEPTH = 1
DEEPNORM_ALPHA = (2.0 * DEPTH) ** 0.25
LN_EPS = 1e-5
MASK_VALUE = -1e30
ATTN_SCALE = HEAD_DIM ** -0.5

LANES = 128
HEADS_PER_VREG = LANES // HEAD_DIM
VMEM_LIMIT = 56 * 1024 * 1024

_C_QA = 0
_C_KA = _C_QA + NA_WIDTH
_C_VA = _C_KA + NA_WIDTH
_C_QB = _C_VA + NA_WIDTH
_C_KB = _C_QB + SWA_Q_WIDTH
_C_VB = _C_KB + SWA_KV_WIDTH
_C_G = _C_VB + SWA_KV_WIDTH
_C_END = _C_G + 2 * D_MODEL
_KV_DUP = SWA_KV_HEADS * LANES


def _const_spec(shape):
    nd = len(shape)
    return pl.BlockSpec(shape, lambda *_: (0,) * nd, pipeline_mode=pl.Buffered(1))


def _layer_norm(x, g, b):
    mu = jnp.mean(x, axis=-1, keepdims=True)
    xc = x - mu
    var = jnp.mean(xc * xc, axis=-1, keepdims=True)
    return xc * lax.rsqrt(var + LN_EPS) * g + b


def _stage_weights_bf16(pieces, stage_ref, sem):
    def copy(k):
        src = pieces[k][0]
        dst = stage_ref.at[k % 2, pl.ds(0, src.shape[0]), :]
        return pltpu.make_async_copy(src, dst, sem.at[k % 2])

    copy(0).start()
    for k, (src, dst) in enumerate(pieces):
        if k + 1 < len(pieces):
            copy(k + 1).start()
        copy(k).wait()
        dst[...] = stage_ref[k % 2, 0:src.shape[0], :].astype(dst.dtype)


def _dup_heads(y):
    lane = lax.broadcasted_iota(jnp.int32, y.shape, 1)
    swapped = pltpu.roll(y, HEAD_DIM, 1)
    return jnp.concatenate([jnp.where(lane < HEAD_DIM, y, swapped),
                            jnp.where(lane >= HEAD_DIM, y, swapped)], axis=1)


QKV_CHUNK = _C_G // 3


def _inproj_kernel(x_ref, g_ref, b_ref, bias_ref, cos_ref, sa_ref, sb_ref, win_hbm,
                   qa_ref, ka_ref, va_ref, qb_ref, kb_ref, vb_ref,
                   w_ref, stage_ref, sem, *, sub_rows):
    half = ROPE_DIM // 2

    @pl.when((pl.program_id(0) == 0) & (pl.program_id(1) == 0))
    def _():
        cols = lambda c: pl.ds(c * QKV_CHUNK, QKV_CHUNK)
        _stage_weights_bf16([(win_hbm.at[0, :, cols(c)], w_ref.at[:, cols(c)])
                             for c in range(_C_G // QKV_CHUNK)], stage_ref, sem)

    for r0 in range(0, x_ref.shape[0], sub_rows):
        rows = slice(r0, r0 + sub_rows)
        h = _layer_norm(x_ref[rows, :], g_ref[...], b_ref[...])
        hb = h.astype(jnp.bfloat16)

        def proj(c0, c1):
            return (jnp.dot(hb, w_ref[:, c0:c1], preferred_element_type=jnp.float32)
                    + bias_ref[:, c0:c1])

        def rotary(y):
            cos, sa, sb = cos_ref[rows, :], sa_ref[rows, :], sb_ref[rows, :]
            outs = []
            for c in range(y.shape[1] // LANES):
                yc = y[:, c * LANES:(c + 1) * LANES]
                outs.append(yc * cos
                            + pltpu.roll(yc, half, 1) * sa
                            + pltpu.roll(yc, LANES - half, 1) * sb)
            return outs[0] if len(outs) == 1 else jnp.concatenate(outs, axis=1)

        qa_ref[rows, :] = (proj(_C_QA, _C_KA) * ATTN_SCALE).astype(qa_ref.dtype)
        ka_ref[rows, :] = proj(_C_KA, _C_VA).astype(ka_ref.dtype)
        va_ref[rows, :] = proj(_C_VA, _C_QB).astype(va_ref.dtype)
        qb_ref[rows, :] = (rotary(proj(_C_QB, _C_KB)) * ATTN_SCALE).astype(qb_ref.dtype)
        kv = proj(_C_KB, _C_G)
        kb_ref[rows, :] = _dup_heads(rotary(kv[:, :SWA_KV_WIDTH])).astype(kb_ref.dtype)
        vb_ref[rows, :] = _dup_heads(kv[:, SWA_KV_WIDTH:]).astype(vb_ref.dtype)


def _inproj(x2, g, b, bias, cos_t, sa_t, sb_t, w_in, batch, seq, tm, sub_rows):
    t = batch * seq
    nblk = seq // tm
    tok = lambda bi, i: (bi * nblk + i, 0)
    pos = lambda bi, i: (i, 0)
    bf = jnp.bfloat16
    out_shape = (
        jax.ShapeDtypeStruct((t, NA_WIDTH), bf), jax.ShapeDtypeStruct((t, NA_WIDTH), bf),
        jax.ShapeDtypeStruct((t, NA_WIDTH), bf), jax.ShapeDtypeStruct((t, SWA_Q_WIDTH), bf),
        jax.ShapeDtypeStruct((t, _KV_DUP), bf), jax.ShapeDtypeStruct((t, _KV_DUP), bf),
    )
    return pl.pallas_call(
        functools.partial(_inproj_kernel, sub_rows=sub_rows),
        grid=(batch, nblk),
        in_specs=[
            pl.BlockSpec((tm, D_MODEL), tok),
            _const_spec((1, D_MODEL)), _const_spec((1, D_MODEL)),
            _const_spec((1, _C_END)),
            pl.BlockSpec((tm, LANES), pos), pl.BlockSpec((tm, LANES), pos),
            pl.BlockSpec((tm, LANES), pos),
            pl.BlockSpec(memory_space=pl.ANY),
        ],
        out_specs=(
            pl.BlockSpec((tm, NA_WIDTH), tok), pl.BlockSpec((tm, NA_WIDTH), tok),
            pl.BlockSpec((tm, NA_WIDTH), tok), pl.BlockSpec((tm, SWA_Q_WIDTH), tok),
            pl.BlockSpec((tm, _KV_DUP), tok), pl.BlockSpec((tm, _KV_DUP), tok),
        ),
        out_shape=out_shape,
        scratch_shapes=[
            pltpu.VMEM((D_MODEL, _C_G), bf),
            pltpu.VMEM((2, D_MODEL, QKV_CHUNK), jnp.float32),
            pltpu.SemaphoreType.DMA((2,)),
        ],
        compiler_params=pltpu.CompilerParams(
            dimension_semantics=("arbitrary", "arbitrary"), vmem_limit_bytes=VMEM_LIMIT),
        name="ln_inproj_rotary",
    )(x2, g, b, bias, cos_t, sa_t, sb_t, w_in)


def _rotary_tables(seq):
    half = ROPE_DIM // 2
    inv_freq = np.power(ROPE_THETA, -np.arange(0, ROPE_DIM, 2, dtype=np.float64) / ROPE_DIM)
    ang = np.arange(seq, dtype=np.float64)[:, None] * inv_freq[None, :]
    cos, sin = np.cos(ang), np.sin(ang)
    ones = np.ones((seq, HEAD_DIM - ROPE_DIM))
    zeros = np.zeros((seq, HEAD_DIM - ROPE_DIM))
    zh = np.zeros((seq, half))
    cos_h = np.concatenate([cos, cos, ones], axis=1)
    sa_h = np.concatenate([zh, sin, zeros], axis=1)
    sb_h = np.concatenate([-sin, zh, zeros], axis=1)
    tile = lambda a: jnp.asarray(np.tile(a, (1, HEADS_PER_VREG)), dtype=jnp.float32)
    return tile(cos_h), tile(sa_h), tile(sb_h)


def _stack_heads(q):
    lane = lax.broadcasted_iota(jnp.int32, q.shape, 1)
    zero = jnp.zeros_like(q)
    return jnp.concatenate([jnp.where(lane < HEAD_DIM, q, zero),
                            jnp.where(lane >= HEAD_DIM, q, zero)], axis=0)


def _unstack_heads(o2):
    m = o2.shape[0] // 2
    lane = lax.broadcasted_iota(jnp.int32, (m, LANES), 1)
    return jnp.where(lane < HEAD_DIM, o2[:m], o2[m:])


_NT = (((1,), (1,)), ((), ()))


def _na_bias_kernel(rpb_ref, o_ref):
    w = lax.broadcasted_iota(jnp.int32, (GRID_W, LANES), 0)
    lane = lax.broadcasted_iota(jnp.int32, (GRID_W, LANES), 1)
    c = lane % GRID_W
    col_start = jnp.clip(w - NA_KW // 2, 0, GRID_W - NA_KW)
    inside = (c >= col_start) & (c < col_start + NA_KW)
    lower = lane < GRID_W
    n_dr = 2 * NA_KH - 1
    for e in range(HEADS_PER_VREG):
        lo, hi = [], []
        for dr in range(n_dr):
            x = jnp.broadcast_to(rpb_ref[0, e, dr:dr + 1, :], (GRID_W, LANES))
            lo.append(pltpu.roll(x, LANES - (NA_KW - 1), 1, stride=1, stride_axis=0))
            hi.append(pltpu.roll(x, GRID_W - (NA_KW - 1), 1, stride=1, stride_axis=0))
        for v in range(NA_KH):
            for kk in range(NA_KH // HEADS_PER_VREG):
                dr_a = 2 * kk - v + NA_KH - 1
                tile = jnp.where(inside, jnp.where(lower, lo[dr_a], hi[dr_a + 1]), MASK_VALUE)
                o_ref[v, 0, e * GRID_W:(e + 1) * GRID_W, kk * LANES:(kk + 1) * LANES] = tile


def _na_bias_table(rpb):
    n_pairs = NA_HEADS // HEADS_PER_VREG
    n_dr, n_dc = rpb.shape[1], rpb.shape[2]
    dr_pad = -(-n_dr // 8) * 8
    rp = jnp.pad(rpb, ((0, 0), (0, dr_pad - n_dr), (0, LANES - n_dc)))
    rp = rp.reshape(n_pairs, HEADS_PER_VREG, dr_pad, LANES)
    m2, win = HEADS_PER_VREG * GRID_W, NA_KH * GRID_W
    return pl.pallas_call(
        _na_bias_kernel,
        grid=(n_pairs,),
        in_specs=[pl.BlockSpec((1, HEADS_PER_VREG, dr_pad, LANES), lambda j: (j, 0, 0, 0))],
        out_specs=pl.BlockSpec((NA_KH, 1, m2, win), lambda j: (0, j, 0, 0)),
        out_shape=jax.ShapeDtypeStruct((NA_KH, n_pairs, m2, win), jnp.float32),
        compiler_params=pltpu.CompilerParams(dimension_semantics=("parallel",)),
        name="na_bias_table",
    )(rp)


def _na_kernel(q_ref, k_ref, v_ref, tbl_ref, o_ref, s_scr, *, rows_per_iter, n_rows):
    win = NA_KH * GRID_W
    ones = jnp.ones((win, LANES), jnp.bfloat16)

    def body(i, carry):
        def window(t):
            r = i * rows_per_iter + t
            rs = jnp.clip(r - NA_KH // 2, 0, n_rows - NA_KH)
            return r - rs, pl.multiple_of(rs * GRID_W, GRID_W)

        def q_rows(t):
            return pl.ds(pl.multiple_of((i * rows_per_iter + t) * GRID_W, GRID_W), GRID_W)

        for t in range(rows_per_iter):
            variant, start = window(t)
            lhs = _stack_heads(q_ref[0, q_rows(t), :])
            kw = k_ref[0, pl.ds(start, win), :]
            s = lax.dot_general(lhs, kw, _NT, preferred_element_type=jnp.float32)
            s_scr[t] = s + tbl_ref[variant, 0]
        for t in range(rows_per_iter):
            s = s_scr[t]
            m = jnp.max(s, axis=-1, keepdims=True)
            p = jnp.exp(s - m)
            _, start = window(t)
            vw = jnp.concatenate([v_ref[0, pl.ds(start, win), :], ones], axis=1)
            o2 = jnp.dot(p.astype(jnp.bfloat16), vw, preferred_element_type=jnp.float32)
            o2 = o2[:, :LANES] / o2[:, LANES:]
            o_ref[0, q_rows(t), :] = _unstack_heads(o2).astype(o_ref.dtype)
        return carry

    lax.fori_loop(0, n_rows // rows_per_iter, body, 0)


def _na_attention(qa, ka, va, tbl, rows_per_iter):
    batch, seq, _ = qa.shape
    n_rows = seq // GRID_W
    n_pairs = NA_HEADS // HEADS_PER_VREG
    m2, win = HEADS_PER_VREG * GRID_W, NA_KH * GRID_W
    kern = functools.partial(_na_kernel, rows_per_iter=rows_per_iter, n_rows=n_rows)
    seq_block = pl.BlockSpec((1, seq, LANES), lambda b, j: (b, 0, j))
    return pl.pallas_call(
        kern,
        grid=(batch, n_pairs),
        in_specs=[seq_block, seq_block, seq_block,
                  pl.BlockSpec((NA_KH, 1, m2, win), lambda b, j: (0, j, 0, 0))],
        out_specs=seq_block,
        out_shape=jax.ShapeDtypeStruct(qa.shape, jnp.bfloat16),
        scratch_shapes=[pltpu.VMEM((rows_per_iter, m2, win), jnp.float32)],
        compiler_params=pltpu.CompilerParams(
            dimension_semantics=("parallel", "parallel"), vmem_limit_bytes=VMEM_LIMIT),
        name="na_attention",
    )(qa, ka, va, tbl)


def _swa_mask_table():
    qi = np.arange(SWA_BLOCK)[:, None]
    kc = np.arange(3 * SWA_BLOCK)[None, :]
    tbl = [np.where(np.abs(kc - d * SWA_BLOCK - qi) <= SWA_WINDOW, 0.0, MASK_VALUE)
           for d in range(3)]
    return jnp.asarray(np.stack(tbl), dtype=jnp.float32)


def _swa_kernel(sink_ref, q_ref, k_ref, v_ref, mask_ref, o_ref, s_scr, *,
                blocks_per_iter, seq):
    j = pl.program_id(1)
    blk = SWA_BLOCK
    win = 3 * blk
    row = lax.broadcasted_iota(jnp.int32, (2 * blk, 1), 0)
    sink_col = jnp.where(row < blk, sink_ref[2 * j], sink_ref[2 * j + 1])
    ones = jnp.ones((win, LANES), jnp.bfloat16)

    def body(i, carry):
        def window(u):
            n = i * blocks_per_iter + u
            start = jnp.clip((n - 1) * blk, 0, seq - win)
            return n - start // blk, pl.multiple_of(start, blk)

        def q_rows(u):
            return pl.ds(pl.multiple_of((i * blocks_per_iter + u) * blk, blk), blk)

        for u in range(blocks_per_iter):
            variant, start = window(u)
            lhs = _stack_heads(q_ref[0, q_rows(u), :])
            kw = k_ref[0, pl.ds(start, win), :]
            s = lax.dot_general(lhs, kw, _NT, preferred_element_type=jnp.float32)
            mask = mask_ref[variant]
            s_scr[u] = s + jnp.concatenate([mask, mask], axis=0)
        for u in range(blocks_per_iter):
            s = s_scr[u]
            m = jnp.maximum(jnp.max(s, axis=-1, keepdims=True), sink_col)
            e = jnp.exp(s - m)
            _, start = window(u)
            vw = jnp.concatenate([v_ref[0, pl.ds(start, win), :], ones], axis=1)
            o2 = jnp.dot(e.astype(jnp.bfloat16), vw, preferred_element_type=jnp.float32)
            o2 = o2[:, :LANES] / (o2[:, LANES:] + jnp.exp(sink_col - m))
            o_ref[0, q_rows(u), :] = _unstack_heads(o2).astype(o_ref.dtype)
        return carry

    lax.fori_loop(0, seq // (blocks_per_iter * blk), body, 0)


def _swa_attention(qb, kb, vb, sink, blocks_per_iter):
    batch, seq, _ = qb.shape
    n_pairs = SWA_Q_HEADS // HEADS_PER_VREG
    pairs_per_group = n_pairs // SWA_KV_HEADS
    m2, win = HEADS_PER_VREG * SWA_BLOCK, 3 * SWA_BLOCK
    kern = functools.partial(_swa_kernel, blocks_per_iter=blocks_per_iter, seq=seq)
    q_block = pl.BlockSpec((1, seq, LANES), lambda b, j: (b, 0, j))
    kv_block = pl.BlockSpec((1, seq, LANES), lambda b, j: (b, 0, j // pairs_per_group))
    return pl.pallas_call(
        kern,
        grid=(batch, n_pairs),
        in_specs=[pl.BlockSpec(memory_space=pltpu.SMEM), q_block, kv_block, kv_block,
                  _const_spec((3, SWA_BLOCK, win))],
        out_specs=q_block,
        out_shape=jax.ShapeDtypeStruct(qb.shape, jnp.bfloat16),
        scratch_shapes=[pltpu.VMEM((blocks_per_iter, m2, win), jnp.float32)],
        compiler_params=pltpu.CompilerParams(
            dimension_semantics=("parallel", "parallel"), vmem_limit_bytes=VMEM_LIMIT),
        name="swa_attention",
    )(sink, qb, kb, vb, _swa_mask_table())


W_CHUNK = D_MODEL


def _tail_kernel(x_ref, g0_ref, b0_ref, na_ref, swa_ref, bin_ref, bout_ref, g1_ref, b1_ref,
                 bf1_ref, bf2_ref, g2_ref, b2_ref,
                 win_hbm, wna_hbm, wswa_hbm, wout_hbm, w1_hbm, w2_hbm,
                 o_ref,
                 wg_ref, wna_ref, wswa_ref, wout_ref, w1_ref, w2_ref, stage_ref, sem,
                 *, sub_rows, ff_chunk):
    @pl.when(pl.program_id(0) == 0)
    def _():
        cols = lambda c: pl.ds(c * W_CHUNK, W_CHUNK)
        pieces = [(win_hbm.at[0, :, pl.ds(_C_G + c * W_CHUNK, W_CHUNK)], wg_ref.at[:, cols(c)])
                  for c in range(2 * D_MODEL // W_CHUNK)]
        pieces += [(wna_hbm.at[0], wna_ref), (wswa_hbm.at[0], wswa_ref),
                   (wout_hbm.at[0], wout_ref)]
        pieces += [(w1_hbm.at[0, :, cols(c)], w1_ref.at[:, cols(c)])
                   for c in range(D_FF // W_CHUNK)]
        pieces += [(w2_hbm.at[0, cols(c), :], w2_ref.at[cols(c), :])
                   for c in range(D_FF // W_CHUNK)]
        _stage_weights_bf16(pieces, stage_ref, sem)

    h1 = []
    for r0 in range(0, x_ref.shape[0], sub_rows):
        rows = slice(r0, r0 + sub_rows)
        h = _layer_norm(x_ref[rows, :], g0_ref[...], b0_ref[...])
        gates = (jnp.dot(h.astype(jnp.bfloat16), wg_ref[...],
                         preferred_element_type=jnp.float32) + bin_ref[:, _C_G:_C_END])
        y_na = jnp.dot(na_ref[rows, :], wna_ref[...], preferred_element_type=jnp.float32)
        y_swa = jnp.dot(swa_ref[rows, :], wswa_ref[...], preferred_element_type=jnp.float32)
        mixed = (jax.nn.sigmoid(gates[:, :D_MODEL]) * y_na
                 + jax.nn.sigmoid(gates[:, D_MODEL:]) * y_swa)
        attn = jnp.dot(mixed.astype(jnp.bfloat16), wout_ref[...],
                       preferred_element_type=jnp.float32) + bout_ref[...]
        h1.append(_layer_norm(DEEPNORM_ALPHA * h + attn, g1_ref[...], b1_ref[...]))

    h = jnp.concatenate(h1, axis=0)
    hb = h.astype(jnp.bfloat16)
    acc = jnp.zeros(h.shape, jnp.float32)
    n_chunks = D_FF // ff_chunk
    for c in range(n_chunks):
        sl = slice(c * ff_chunk, (c + 1) * ff_chunk)
        u = jnp.dot(hb, w1_ref[:, sl], preferred_element_type=jnp.float32) + bf1_ref[:, sl]
        u = jnp.square(jnp.maximum(u, 0.0)).astype(jnp.bfloat16)
        if c + 1 < n_chunks:
            acc = acc + jnp.dot(u, w2_ref[sl, :], preferred_element_type=jnp.float32)
        else:
            for r0 in range(0, h.shape[0], sub_rows):
                rows = slice(r0, r0 + sub_rows)
                z = acc[rows] + jnp.dot(u[rows], w2_ref[sl, :],
                                        preferred_element_type=jnp.float32)
                o_ref[rows, :] = _layer_norm(DEEPNORM_ALPHA * h[rows] + z + bf2_ref[...],
                                             g2_ref[...], b2_ref[...])


def _tail(x2, g0, b0, na, swa, b_in, bout, g1, b1, bf1, bf2, g2, b2,
          w_in, w_na, w_swa, w_out, w_ff1, w_ff2, tm, sub_rows, ff_chunk):
    t = x2.shape[0]
    tok = lambda i: (i, 0)
    vec = _const_spec((1, D_MODEL))
    hbm = pl.BlockSpec(memory_space=pl.ANY)
    bf = jnp.bfloat16
    return pl.pallas_call(
        functools.partial(_tail_kernel, sub_rows=sub_rows, ff_chunk=ff_chunk),
        grid=(t // tm,),
        in_specs=[
            pl.BlockSpec((tm, D_MODEL), tok), vec, vec,
            pl.BlockSpec((tm, NA_WIDTH), tok), pl.BlockSpec((tm, SWA_Q_WIDTH), tok),
            _const_spec((1, _C_END)), vec, vec, vec,
            _const_spec((1, D_FF)), vec, vec, vec,
            hbm, hbm, hbm, hbm, hbm, hbm,
        ],
        out_specs=pl.BlockSpec((tm, D_MODEL), tok),
        out_shape=jax.ShapeDtypeStruct((t, D_MODEL), jnp.float32),
        scratch_shapes=[
            pltpu.VMEM((D_MODEL, 2 * D_MODEL), bf),
            pltpu.VMEM((NA_WIDTH, D_MODEL), bf), pltpu.VMEM((SWA_Q_WIDTH, D_MODEL), bf),
            pltpu.VMEM((D_MODEL, D_MODEL), bf),
            pltpu.VMEM((D_MODEL, D_FF), bf), pltpu.VMEM((D_FF, D_MODEL), bf),
            pltpu.VMEM((2, W_CHUNK, D_MODEL), jnp.float32),
            pltpu.SemaphoreType.DMA((2,)),
        ],
        compiler_params=pltpu.CompilerParams(
            dimension_semantics=("arbitrary",), vmem_limit_bytes=VMEM_LIMIT),
        name="mix_ffn",
    )(x2, g0, b0, na, swa, b_in, bout, g1, b1, bf1, bf2, g2, b2,
      w_in, w_na, w_swa, w_out, w_ff1, w_ff2)


def kernel(x, ln0_g, ln0_b, w_in, b_in, na_rpb, swa_sink, w_branch_na, w_branch_swa,
           w_out, b_out, ln1_g, ln1_b, w_ff1, b_ff1, w_ff2, b_ff2, ln2_g, ln2_b):
    batch, seq, d = x.shape
    assert d == D_MODEL and seq % (NA_KH * GRID_W) == 0 and seq >= 3 * SWA_BLOCK
    assert w_in.shape == (DEPTH, D_MODEL, _C_END) and DEPTH == 1
    row = lambda v: v.reshape(1, -1)
    x2 = x.reshape(batch * seq, d)
    b_in_row = row(b_in[0])

    cos_t, sa_t, sb_t = _rotary_tables(seq)
    qa, ka, va, qb, kb, vb = _inproj(
        x2, row(ln0_g), row(ln0_b), b_in_row, cos_t, sa_t, sb_t, w_in,
        batch, seq, tm=1024, sub_rows=256)

    three = lambda a: a.reshape(batch, seq, a.shape[-1])
    o_na = _na_attention(three(qa), three(ka), three(va), _na_bias_table(na_rpb[0]),
                         rows_per_iter=16)
    o_swa = _swa_attention(three(qb), three(kb), three(vb), swa_sink[0], blocks_per_iter=8)

    out = _tail(x2, row(ln0_g), row(ln0_b), o_na.reshape(batch * seq, -1),
                o_swa.reshape(batch * seq, -1), b_in_row, row(b_out[0]),
                row(ln1_g[0]), row(ln1_b[0]), row(b_ff1[0]), row(b_ff2[0]),
                row(ln2_g[0]), row(ln2_b[0]),
                w_in, w_branch_na, w_branch_swa, w_out, w_ff1, w_ff2,
                tm=512, sub_rows=256, ff_chunk=1024)
    return out.reshape(batch, seq, d)
```

```python
import functools

import numpy as np
import jax
import jax.numpy as jnp
from jax import lax
from jax.experimental import pallas as pl
from jax.experimental.pallas import tpu as pltpu

D_MODEL = 1024
GRID_W = 64
HEAD_DIM = 64
NA_HEADS = 8
NA_KH = 8
NA_KW = 16
NA_WIDTH = NA_HEADS * HEAD_DIM
SWA_Q_HEADS = 8
SWA_KV_HEADS = 2
SWA_WINDOW = 128
SWA_BLOCK = 128
SWA_Q_WIDTH = SWA_Q_HEADS * HEAD_DIM
SWA_KV_WIDTH = SWA_KV_HEADS * HEAD_DIM
ROPE_THETA = 500000.0
ROPE_DIM = HEAD_DIM // 4
D_FF = 4 * D_MODEL
DEPTH = 1
DEEPNORM_ALPHA = (2.0 * DEPTH) ** 0.25
LN_EPS = 1e-5
MASK_VALUE = -1e30
ATTN_SCALE = HEAD_DIM ** -0.5

LANES = 128
HEADS_PER_VREG = LANES // HEAD_DIM
VMEM_LIMIT = 56 * 1024 * 1024

_C_QA = 0
_C_KA = _C_QA + NA_WIDTH
_C_VA = _C_KA + NA_WIDTH
_C_QB = _C_VA + NA_WIDTH
_C_KB = _C_QB + SWA_Q_WIDTH
_C_VB = _C_KB + SWA_KV_WIDTH
_C_G = _C_VB + SWA_KV_WIDTH
_C_END = _C_G + 2 * D_MODEL
_KV_DUP = SWA_KV_HEADS * LANES


def _const_spec(shape):
    nd = len(shape)
    return pl.BlockSpec(shape, lambda *_: (0,) * nd, pipeline_mode=pl.Buffered(1))


def _layer_norm(x, g, b):
    mu = jnp.mean(x, axis=-1, keepdims=True)
    xc = x - mu
    var = jnp.mean(xc * xc, axis=-1, keepdims=True)
    return xc * lax.rsqrt(var + LN_EPS) * g + b


def _stage_weights_bf16(pieces, stage_ref, sem):
    def copy(k):
        src = pieces[k][0]
        dst = stage_ref.at[k % 2, pl.ds(0, src.shape[0]), :]
        return pltpu.make_async_copy(src, dst, sem.at[k % 2])

    copy(0).start()
    for k, (src, dst) in enumerate(pieces):
        if k + 1 < len(pieces):
            copy(k + 1).start()
        copy(k).wait()
        dst[...] = stage_ref[k % 2, 0:src.shape[0], :].astype(dst.dtype)


def _dup_heads(y):
    lane = lax.broadcasted_iota(jnp.int32, y.shape, 1)
    swapped = pltpu.roll(y, HEAD_DIM, 1)
    return jnp.concatenate([jnp.where(lane < HEAD_DIM, y, swapped),
                            jnp.where(lane >= HEAD_DIM, y, swapped)], axis=1)


QKV_CHUNK = _C_G // 3


def _inproj_kernel(xn_ref, x0_ref, g_ref, b_ref, bias_ref, cos_ref, sa_ref, sb_ref, win_hbm,
                   qa_ref, ka_ref, va_ref, qb_ref, kb_ref, vb_ref,
                   w_ref, stage_ref, sem, hb_even, hb_odd, *, ln_rows, dot_rows):
    half = ROPE_DIM // 2
    i = pl.program_id(0)

    def normalise(x_ref, hb_ref):
        for r0 in range(0, x_ref.shape[0], ln_rows):
            rows = slice(r0, r0 + ln_rows)
            hb_ref[rows, :] = _layer_norm(x_ref[rows, :], g_ref[...],
                                          b_ref[...]).astype(hb_ref.dtype)

    @pl.when(i == 0)
    def _():
        cols = lambda c: pl.ds(c * QKV_CHUNK, QKV_CHUNK)
        _stage_weights_bf16([(win_hbm.at[0, :, cols(c)], w_ref.at[:, cols(c)])
                             for c in range(_C_G // QKV_CHUNK)], stage_ref, sem)
        normalise(x0_ref, hb_even)

    def step(hb_ref, hb_next_ref):
        normalise(xn_ref, hb_next_ref)

        for r0 in range(0, hb_ref.shape[0], dot_rows):
            rows = slice(r0, r0 + dot_rows)

            def proj(c0, c1):
                return (jnp.dot(hb_ref[rows, :], w_ref[:, c0:c1],
                                preferred_element_type=jnp.float32) + bias_ref[:, c0:c1])

            def rotary(y):
                cos, sa, sb = cos_ref[rows, :], sa_ref[rows, :], sb_ref[rows, :]
                outs = []
                for c in range(y.shape[1] // LANES):
                    yc = y[:, c * LANES:(c + 1) * LANES]
                    outs.append(yc * cos
                                + pltpu.roll(yc, half, 1) * sa
                                + pltpu.roll(yc, LANES - half, 1) * sb)
                return outs[0] if len(outs) == 1 else jnp.concatenate(outs, axis=1)

            qa_ref[rows, :] = (proj(_C_QA, _C_KA) * ATTN_SCALE).astype(qa_ref.dtype)
            ka_ref[rows, :] = proj(_C_KA, _C_VA).astype(ka_ref.dtype)
            va_ref[rows, :] = proj(_C_VA, _C_QB).astype(va_ref.dtype)
            qb_ref[rows, :] = (rotary(proj(_C_QB, _C_KB)) * ATTN_SCALE).astype(qb_ref.dtype)
            kv = proj(_C_KB, _C_G)
            kb_ref[rows, :] = _dup_heads(rotary(kv[:, :SWA_KV_WIDTH])).astype(kb_ref.dtype)
            vb_ref[rows, :] = _dup_heads(kv[:, SWA_KV_WIDTH:]).astype(vb_ref.dtype)

    pl.when(i % 2 == 0)(lambda: step(hb_even, hb_odd))
    pl.when(i % 2 == 1)(lambda: step(hb_odd, hb_even))


def _inproj(x2, g, b, bias, cos_t, sa_t, sb_t, w_in, seq, tm, ln_rows, dot_rows):
    t = x2.shape[0]
    n_tiles, nblk = t // tm, seq // tm
    tok = lambda i: (i, 0)
    pos = lambda i: (i % nblk, 0)
    bf = jnp.bfloat16
    out_shape = (
        jax.ShapeDtypeStruct((t, NA_WIDTH), bf), jax.ShapeDtypeStruct((t, NA_WIDTH), bf),
        jax.ShapeDtypeStruct((t, NA_WIDTH), bf), jax.ShapeDtypeStruct((t, SWA_Q_WIDTH), bf),
        jax.ShapeDtypeStruct((t, _KV_DUP), bf), jax.ShapeDtypeStruct((t, _KV_DUP), bf),
    )
    return pl.pallas_call(
        functools.partial(_inproj_kernel, ln_rows=ln_rows, dot_rows=dot_rows),
        grid=(n_tiles,),
        in_specs=[
            pl.BlockSpec((tm, D_MODEL), lambda i: (jnp.minimum(i + 1, n_tiles - 1), 0)),
            _const_spec((tm, D_MODEL)),
            _const_spec((1, D_MODEL)), _const_spec((1, D_MODEL)),
            _const_spec((1, _C_END)),
            pl.BlockSpec((tm, LANES), pos), pl.BlockSpec((tm, LANES), pos),
            pl.BlockSpec((tm, LANES), pos),
            pl.BlockSpec(memory_space=pl.ANY),
        ],
        out_specs=(
            pl.BlockSpec((tm, NA_WIDTH), tok), pl.BlockSpec((tm, NA_WIDTH), tok),
            pl.BlockSpec((tm, NA_WIDTH), tok), pl.BlockSpec((tm, SWA_Q_WIDTH), tok),
            pl.BlockSpec((tm, _KV_DUP), tok), pl.BlockSpec((tm, _KV_DUP), tok),
        ),
        out_shape=out_shape,
        scratch_shapes=[
            pltpu.VMEM((D_MODEL, _C_G), bf),
            pltpu.VMEM((2, D_MODEL, QKV_CHUNK), jnp.float32),
            pltpu.SemaphoreType.DMA((2,)),
            pltpu.VMEM((tm, D_MODEL), bf), pltpu.VMEM((tm, D_MODEL), bf),
        ],
        compiler_params=pltpu.CompilerParams(
            dimension_semantics=("arbitrary",), vmem_limit_bytes=VMEM_LIMIT),
        name="ln_inproj_rotary",
    )(x2, x2, g, b, bias, cos_t, sa_t, sb_t, w_in)


def _rotary_tables(seq):
    half = ROPE_DIM // 2
    inv_freq = np.power(ROPE_THETA, -np.arange(0, ROPE_DIM, 2, dtype=np.float64) / ROPE_DIM)
    ang = np.arange(seq, dtype=np.float64)[:, None] * inv_freq[None, :]
    cos, sin = np.cos(ang), np.sin(ang)
    ones = np.ones((seq, HEAD_DIM - ROPE_DIM))
    zeros = np.zeros((seq, HEAD_DIM - ROPE_DIM))
    zh = np.zeros((seq, half))
    cos_h = np.concatenate([cos, cos, ones], axis=1)
    sa_h = np.concatenate([zh, sin, zeros], axis=1)
    sb_h = np.concatenate([-sin, zh, zeros], axis=1)
    tile = lambda a: jnp.asarray(np.tile(a, (1, HEADS_PER_VREG)), dtype=jnp.float32)
    return tile(cos_h), tile(sa_h), tile(sb_h)


def _stack_heads(q):
    lane = lax.broadcasted_iota(jnp.int32, q.shape, 1)
    zero = jnp.zeros_like(q)
    return jnp.concatenate([jnp.where(lane < HEAD_DIM, q, zero),
                            jnp.where(lane >= HEAD_DIM, q, zero)], axis=0)


def _unstack_heads(o2):
    m = o2.shape[0] // 2
    lane = lax.broadcasted_iota(jnp.int32, (m, LANES), 1)
    return jnp.where(lane < HEAD_DIM, o2[:m], o2[m:])


_NT = (((1,), (1,)), ((), ()))


def _na_bias_kernel(rpb_ref, o_ref):
    w = lax.broadcasted_iota(jnp.int32, (GRID_W, LANES), 0)
    lane = lax.broadcasted_iota(jnp.int32, (GRID_W, LANES), 1)
    c = lane % GRID_W
    col_start = jnp.clip(w - NA_KW // 2, 0, GRID_W - NA_KW)
    inside = (c >= col_start) & (c < col_start + NA_KW)
    lower = lane < GRID_W
    n_dr = 2 * NA_KH - 1
    for e in range(HEADS_PER_VREG):
        lo, hi = [], []
        for dr in range(n_dr):
            x = jnp.broadcast_to(rpb_ref[0, e, dr:dr + 1, :], (GRID_W, LANES))
            lo.append(pltpu.roll(x, LANES - (NA_KW - 1), 1, stride=1, stride_axis=0))
            hi.append(pltpu.roll(x, GRID_W - (NA_KW - 1), 1, stride=1, stride_axis=0))
        for v in range(NA_KH):
            for kk in range(NA_KH // HEADS_PER_VREG):
                dr_a = 2 * kk - v + NA_KH - 1
                tile = jnp.where(inside, jnp.where(lower, lo[dr_a], hi[dr_a + 1]), MASK_VALUE)
                o_ref[v, 0, e * GRID_W:(e + 1) * GRID_W, kk * LANES:(kk + 1) * LANES] = tile


def _na_bias_table(rpb):
    n_pairs = NA_HEADS // HEADS_PER_VREG
    n_dr, n_dc = rpb.shape[1], rpb.shape[2]
    dr_pad = -(-n_dr // 8) * 8
    rp = jnp.pad(rpb, ((0, 0), (0, dr_pad - n_dr), (0, LANES - n_dc)))
    rp = rp.reshape(n_pairs, HEADS_PER_VREG, dr_pad, LANES)
    m2, win = HEADS_PER_VREG * GRID_W, NA_KH * GRID_W
    return pl.pallas_call(
        _na_bias_kernel,
        grid=(n_pairs,),
        in_specs=[pl.BlockSpec((1, HEADS_PER_VREG, dr_pad, LANES), lambda j: (j, 0, 0, 0))],
        out_specs=pl.BlockSpec((NA_KH, 1, m2, win), lambda j: (0, j, 0, 0)),
        out_shape=jax.ShapeDtypeStruct((NA_KH, n_pairs, m2, win), jnp.float32),
        compiler_params=pltpu.CompilerParams(dimension_semantics=("parallel",)),
        name="na_bias_table",
    )(rp)


def _na_kernel(q_ref, k_ref, v_ref, tbl_ref, o_ref, s_scr, *, rows_per_iter, n_rows):
    win = NA_KH * GRID_W
    ones = jnp.ones((win, LANES), jnp.bfloat16)

    def body(i, carry):
        def window(t):
            r = i * rows_per_iter + t
            rs = jnp.clip(r - NA_KH // 2, 0, n_rows - NA_KH)
            return r - rs, pl.multiple_of(rs * GRID_W, GRID_W)

        def q_rows(t):
            return pl.ds(pl.multiple_of((i * rows_per_iter + t) * GRID_W, GRID_W), GRID_W)

        for t in range(rows_per_iter):
            variant, start = window(t)
            lhs = _stack_heads(q_ref[0, q_rows(t), :])
            kw = k_ref[0, pl.ds(start, win), :]
            s = lax.dot_general(lhs, kw, _NT, preferred_element_type=jnp.float32)
            s_scr[t] = s + tbl_ref[variant, 0]
        for t in range(rows_per_iter):
            s = s_scr[t]
            m = jnp.max(s, axis=-1, keepdims=True)
            p = jnp.exp(s - m)
            _, start = window(t)
            vw = jnp.concatenate([v_ref[0, pl.ds(start, win), :], ones], axis=1)
            o2 = jnp.dot(p.astype(jnp.bfloat16), vw, preferred_element_type=jnp.float32)
            o2 = o2[:, :LANES] / o2[:, LANES:]
            o_ref[0, q_rows(t), :] = _unstack_heads(o2).astype(o_ref.dtype)
        return carry

    lax.fori_loop(0, n_rows // rows_per_iter, body, 0)


def _na_attention(qa, ka, va, tbl, rows_per_iter):
    batch, seq, _ = qa.shape
    n_rows = seq // GRID_W
    n_pairs = NA_HEADS // HEADS_PER_VREG
    m2, win = HEADS_PER_VREG * GRID_W, NA_KH * GRID_W
    kern = functools.partial(_na_kernel, rows_per_iter=rows_per_iter, n_rows=n_rows)
    seq_block = pl.BlockSpec((1, seq, LANES), lambda b, j: (b, 0, j))
    return pl.pallas_call(
        kern,
        grid=(batch, n_pairs),
        in_specs=[seq_block, seq_block, seq_block,
                  pl.BlockSpec((NA_KH, 1, m2, win), lambda b, j: (0, j, 0, 0))],
        out_specs=seq_block,
        out_shape=jax.ShapeDtypeStruct(qa.shape, jnp.bfloat16),
        scratch_shapes=[pltpu.VMEM((rows_per_iter, m2, win), jnp.float32)],
        compiler_params=pltpu.CompilerParams(
            dimension_semantics=("parallel", "parallel"), vmem_limit_bytes=VMEM_LIMIT),
        name="na_attention",
    )(qa, ka, va, tbl)


def _swa_mask_table():
    qi = np.arange(SWA_BLOCK)[:, None]
    kc = np.arange(3 * SWA_BLOCK)[None, :]
    tbl = [np.where(np.abs(kc - d * SWA_BLOCK - qi) <= SWA_WINDOW, 0.0, MASK_VALUE)
           for d in range(3)]
    return jnp.asarray(np.stack(tbl), dtype=jnp.float32)


def _swa_kernel(sink_ref, q_ref, k_ref, v_ref, mask_ref, o_ref, s_scr, *,
                blocks_per_iter, seq):
    j = pl.program_id(1)
    blk = SWA_BLOCK
    win = 3 * blk
    row = lax.broadcasted_iota(jnp.int32, (2 * blk, 1), 0)
    sink_col = jnp.where(row < blk, sink_ref[2 * j], sink_ref[2 * j + 1])
    ones = jnp.ones((win, LANES), jnp.bfloat16)

    def body(i, carry):
        def window(u):
            n = i * blocks_per_iter + u
            start = jnp.clip((n - 1) * blk, 0, seq - win)
            return n - start // blk, pl.multiple_of(start, blk)

        def q_rows(u):
            return pl.ds(pl.multiple_of((i * blocks_per_iter + u) * blk, blk), blk)

        for u in range(blocks_per_iter):
            variant, start = window(u)
            lhs = _stack_heads(q_ref[0, q_rows(u), :])
            kw = k_ref[0, pl.ds(start, win), :]
            s = lax.dot_general(lhs, kw, _NT, preferred_element_type=jnp.float32)
            mask = mask_ref[variant]
            s_scr[u] = s + jnp.concatenate([mask, mask], axis=0)
        for u in range(blocks_per_iter):
            s = s_scr[u]
            m = jnp.maximum(jnp.max(s, axis=-1, keepdims=True), sink_col)
            e = jnp.exp(s - m)
            _, start = window(u)
            vw = jnp.concatenate([v_ref[0, pl.ds(start, win), :], ones], axis=1)
            o2 = jnp.dot(e.astype(jnp.bfloat16), vw, preferred_element_type=jnp.float32)
            o2 = o2[:, :LANES] / (o2[:, LANES:] + jnp.exp(sink_col - m))
            o_ref[0, q_rows(u), :] = _unstack_heads(o2).astype(o_ref.dtype)
        return carry

    lax.fori_loop(0, seq // (blocks_per_iter * blk), body, 0)


def _swa_attention(qb, kb, vb, sink, blocks_per_iter):
    batch, seq, _ = qb.shape
    n_pairs = SWA_Q_HEADS // HEADS_PER_VREG
    pairs_per_group = n_pairs // SWA_KV_HEADS
    m2, win = HEADS_PER_VREG * SWA_BLOCK, 3 * SWA_BLOCK
    kern = functools.partial(_swa_kernel, blocks_per_iter=blocks_per_iter, seq=seq)
    q_block = pl.BlockSpec((1, seq, LANES), lambda b, j: (b, 0, j))
    kv_block = pl.BlockSpec((1, seq, LANES), lambda b, j: (b, 0, j // pairs_per_group))
    return pl.pallas_call(
        kern,
        grid=(batch, n_pairs),
        in_specs=[pl.BlockSpec(memory_space=pltpu.SMEM), q_block, kv_block, kv_block,
                  _const_spec((3, SWA_BLOCK, win))],
        out_specs=q_block,
        out_shape=jax.ShapeDtypeStruct(qb.shape, jnp.bfloat16),
        scratch_shapes=[pltpu.VMEM((blocks_per_iter, m2, win), jnp.float32)],
        compiler_params=pltpu.CompilerParams(
            dimension_semantics=("parallel", "parallel"), vmem_limit_bytes=VMEM_LIMIT),
        name="swa_attention",
    )(sink, qb, kb, vb, _swa_mask_table())


W_CHUNK = D_MODEL


def _tail_kernel(x_ref, g0_ref, b0_ref, na_ref, swa_ref, bin_ref, bout_ref, g1_ref, b1_ref,
                 bf1_ref, bf2_ref, g2_ref, b2_ref,
                 win_hbm, wna_hbm, wswa_hbm, wout_hbm, w1_hbm, w2_hbm,
                 o_ref,
                 wg_ref, wna_ref, wswa_ref, wout_ref, w1_ref, w2_ref, stage_ref, sem,
                 *, sub_rows, ff_chunk):
    @pl.when(pl.program_id(0) == 0)
    def _():
        cols = lambda c: pl.ds(c * W_CHUNK, W_CHUNK)
        pieces = [(win_hbm.at[0, :, pl.ds(_C_G + c * W_CHUNK, W_CHUNK)], wg_ref.at[:, cols(c)])
                  for c in range(2 * D_MODEL // W_CHUNK)]
        pieces += [(wna_hbm.at[0], wna_ref), (wswa_hbm.at[0], wswa_ref),
                   (wout_hbm.at[0], wout_ref)]
        pieces += [(w1_hbm.at[0, :, cols(c)], w1_ref.at[:, cols(c)])
                   for c in range(D_FF // W_CHUNK)]
        pieces += [(w2_hbm.at[0, cols(c), :], w2_ref.at[cols(c), :])
                   for c in range(D_FF // W_CHUNK)]
        _stage_weights_bf16(pieces, stage_ref, sem)

    h1 = []
    for r0 in range(0, x_ref.shape[0], sub_rows):
        rows = slice(r0, r0 + sub_rows)
        h = _layer_norm(x_ref[rows, :], g0_ref[...], b0_ref[...])
        gates = (jnp.dot(h.astype(jnp.bfloat16), wg_ref[...],
                         preferred_element_type=jnp.float32) + bin_ref[:, _C_G:_C_END])
        y_na = jnp.dot(na_ref[rows, :], wna_ref[...], preferred_element_type=jnp.float32)
        y_swa = jnp.dot(swa_ref[rows, :], wswa_ref[...], preferred_element_type=jnp.float32)
        mixed = (jax.nn.sigmoid(gates[:, :D_MODEL]) * y_na
                 + jax.nn.sigmoid(gates[:, D_MODEL:]) * y_swa)
        attn = jnp.dot(mixed.astype(jnp.bfloat16), wout_ref[...],
                       preferred_element_type=jnp.float32) + bout_ref[...]
        h1.append(_layer_norm(DEEPNORM_ALPHA * h + attn, g1_ref[...], b1_ref[...]))

    h = jnp.concatenate(h1, axis=0)
    hb = h.astype(jnp.bfloat16)
    acc = jnp.zeros(h.shape, jnp.float32)
    n_chunks = D_FF // ff_chunk
    for c in range(n_chunks):
        sl = slice(c * ff_chunk, (c + 1) * ff_chunk)
        u = jnp.dot(hb, w1_ref[:, sl], preferred_element_type=jnp.float32) + bf1_ref[:, sl]
        u = jnp.square(jnp.maximum(u, 0.0)).astype(jnp.bfloat16)
        if c + 1 < n_chunks:
            acc = acc + jnp.dot(u, w2_ref[sl, :], preferred_element_type=jnp.float32)
        else:
            for r0 in range(0, h.shape[0], sub_rows):
                rows = slice(r0, r0 + sub_rows)
                z = acc[rows] + jnp.dot(u[rows], w2_ref[sl, :],
                                        preferred_element_type=jnp.float32)
                o_ref[rows, :] = _layer_norm(DEEPNORM_ALPHA * h[rows] + z + bf2_ref[...],
                                             g2_ref[...], b2_ref[...])


def _tail(x2, g0, b0, na, swa, b_in, bout, g1, b1, bf1, bf2, g2, b2,
          w_in, w_na, w_swa, w_out, w_ff1, w_ff2, tm, sub_rows, ff_chunk):
    t = x2.shape[0]
    tok = lambda i: (i, 0)
    vec = _const_spec((1, D_MODEL))
    hbm = pl.BlockSpec(memory_space=pl.ANY)
    bf = jnp.bfloat16
    return pl.pallas_call(
        functools.partial(_tail_kernel, sub_rows=sub_rows, ff_chunk=ff_chunk),
        grid=(t // tm,),
        in_specs=[
            pl.BlockSpec((tm, D_MODEL), tok), vec, vec,
            pl.BlockSpec((tm, NA_WIDTH), tok), pl.BlockSpec((tm, SWA_Q_WIDTH), tok),
            _const_spec((1, _C_END)), vec, vec, vec,
            _const_spec((1, D_FF)), vec, vec, vec,
            hbm, hbm, hbm, hbm, hbm, hbm,
        ],
        out_specs=pl.BlockSpec((tm, D_MODEL), tok),
        out_shape=jax.ShapeDtypeStruct((t, D_MODEL), jnp.float32),
        scratch_shapes=[
            pltpu.VMEM((D_MODEL, 2 * D_MODEL), bf),
            pltpu.VMEM((NA_WIDTH, D_MODEL), bf), pltpu.VMEM((SWA_Q_WIDTH, D_MODEL), bf),
            pltpu.VMEM((D_MODEL, D_MODEL), bf),
            pltpu.VMEM((D_MODEL, D_FF), bf), pltpu.VMEM((D_FF, D_MODEL), bf),
            pltpu.VMEM((2, W_CHUNK, D_MODEL), jnp.float32),
            pltpu.SemaphoreType.DMA((2,)),
        ],
        compiler_params=pltpu.CompilerParams(
            dimension_semantics=("arbitrary",), vmem_limit_bytes=VMEM_LIMIT),
        name="mix_ffn",
    )(x2, g0, b0, na, swa, b_in, bout, g1, b1, bf1, bf2, g2, b2,
      w_in, w_na, w_swa, w_out, w_ff1, w_ff2)


def kernel(x, ln0_g, ln0_b, w_in, b_in, na_rpb, swa_sink, w_branch_na, w_branch_swa,
           w_out, b_out, ln1_g, ln1_b, w_ff1, b_ff1, w_ff2, b_ff2, ln2_g, ln2_b):
    batch, seq, d = x.shape
    assert d == D_MODEL and seq % (NA_KH * GRID_W) == 0 and seq >= 3 * SWA_BLOCK
    assert w_in.shape == (DEPTH, D_MODEL, _C_END) and DEPTH == 1
    row = lambda v: v.reshape(1, -1)
    x2 = x.reshape(batch * seq, d)
    b_in_row = row(b_in[0])

    cos_t, sa_t, sb_t = _rotary_tables(seq)
    qa, ka, va, qb, kb, vb = _inproj(
        x2, row(ln0_g), row(ln0_b), b_in_row, cos_t, sa_t, sb_t, w_in,
        seq, tm=1024, ln_rows=256, dot_rows=256)

    three = lambda a: a.reshape(batch, seq, a.shape[-1])
    o_na = _na_attention(three(qa), three(ka), three(va), _na_bias_table(na_rpb[0]),
                         rows_per_iter=16)
    o_swa = _swa_attention(three(qb), three(kb), three(vb), swa_sink[0], blocks_per_iter=8)

    out = _tail(x2, row(ln0_g), row(ln0_b), o_na.reshape(batch * seq, -1),
                o_swa.reshape(batch * seq, -1), b_in_row, row(b_out[0]),
                row(ln1_g[0]), row(ln1_b[0]), row(b_ff1[0]), row(b_ff2[0]),
                row(ln2_g[0]), row(ln2_b[0]),
                w_in, w_branch_na, w_branch_swa, w_out, w_ff1, w_ff2,
                tm=512, sub_rows=256, ff_chunk=1024)
    return out.reshape(batch, seq, d)
```

```python
import functools

import numpy as np
import jax
import jax.numpy as jnp
from jax import lax
from jax.experimental import pallas as pl
from jax.experimental.pallas import tpu as pltpu

D_MODEL = 1024
GRID_W = 64
HEAD_DIM = 64
NA_HEADS = 8
NA_KH = 8
NA_KW = 16
NA_WIDTH = NA_HEADS * HEAD_DIM
SWA_Q_HEADS = 8
SWA_KV_HEADS = 2
SWA_WINDOW = 128
SWA_BLOCK = 128
SWA_Q_WIDTH = SWA_Q_HEADS * HEAD_DIM
SWA_KV_WIDTH = SWA_KV_HEADS * HEAD_DIM
ROPE_THETA = 500000.0
ROPE_DIM = HEAD_DIM // 4
D_FF = 4 * D_MODEL
DEPTH = 1
DEEPNORM_ALPHA = (2.0 * DEPTH) ** 0.25
LN_EPS = 1e-5
MASK_VALUE = -1e30
ATTN_SCALE = HEAD_DIM ** -0.5

LANES = 128
SUBLANES = 8
HEADS_PER_VREG = LANES // HEAD_DIM
VMEM_BYTES = 64 * 1024 * 1024
VMEM_TEMP_BYTES = 12 * 1024 * 1024
VMEM_LIMIT = 56 * 1024 * 1024

INPROJ_TILE_ROWS = 1024
TAIL_TILE_ROWS = 512
SUB_TILE_ROWS = 256
FF_CHUNK = 1024
NA_ROWS_PER_ITER = 16
SWA_BLOCKS_PER_ITER = 8

_C_QA = 0
_C_KA = _C_QA + NA_WIDTH
_C_VA = _C_KA + NA_WIDTH
_C_QB = _C_VA + NA_WIDTH
_C_KB = _C_QB + SWA_Q_WIDTH
_C_VB = _C_KB + SWA_KV_WIDTH
_C_G = _C_VB + SWA_KV_WIDTH
_C_END = _C_G + 2 * D_MODEL
_KV_DUP = SWA_KV_HEADS * LANES

_BF16_BYTES = 2
_F32_BYTES = 4


def _vmem_limit(resident_bytes):
    assert resident_bytes + VMEM_TEMP_BYTES <= VMEM_LIMIT < VMEM_BYTES, resident_bytes
    return VMEM_LIMIT


def _const_spec(shape):
    nd = len(shape)
    return pl.BlockSpec(shape, lambda *_: (0,) * nd, pipeline_mode=pl.Buffered(1))


def _layer_norm(x, g, b):
    mu = jnp.mean(x, axis=-1, keepdims=True)
    xc = x - mu
    var = jnp.mean(xc * xc, axis=-1, keepdims=True)
    return xc * lax.rsqrt(var + LN_EPS) * g + b


def _stage_weights_bf16(pieces, stage_ref, sem):
    def copy(k):
        src = pieces[k][0]
        dst = stage_ref.at[k % 2, pl.ds(0, src.shape[0]), :]
        return pltpu.make_async_copy(src, dst, sem.at[k % 2])

    copy(0).start()
    for k, (src, dst) in enumerate(pieces):
        if k + 1 < len(pieces):
            copy(k + 1).start()
        copy(k).wait()
        dst[...] = stage_ref[k % 2, 0:src.shape[0], :].astype(dst.dtype)


def _dup_heads(y):
    lane = lax.broadcasted_iota(jnp.int32, y.shape, 1)
    swapped = pltpu.roll(y, HEAD_DIM, 1)
    return jnp.concatenate([jnp.where(lane < HEAD_DIM, y, swapped),
                            jnp.where(lane >= HEAD_DIM, y, swapped)], axis=1)


QKV_CHUNK = _C_G // 3


def _inproj_kernel(x_ref, g_ref, b_ref, bias_ref, cos_ref, sa_ref, sb_ref, win_hbm,
                   qa_ref, ka_ref, va_ref, qb_ref, kb_ref, vb_ref,
                   w_ref, stage_ref, sem):
    half = ROPE_DIM // 2

    @pl.when((pl.program_id(0) == 0) & (pl.program_id(1) == 0))
    def _():
        cols = lambda c: pl.ds(c * QKV_CHUNK, QKV_CHUNK)
        _stage_weights_bf16([(win_hbm.at[0, :, cols(c)], w_ref.at[:, cols(c)])
                             for c in range(_C_G // QKV_CHUNK)], stage_ref, sem)

    for r0 in range(0, x_ref.shape[0], SUB_TILE_ROWS):
        rows = slice(r0, r0 + SUB_TILE_ROWS)
        h = _layer_norm(x_ref[rows, :], g_ref[...], b_ref[...])
        hb = h.astype(jnp.bfloat16)

        def proj(c0, c1):
            return (jnp.dot(hb, w_ref[:, c0:c1], preferred_element_type=jnp.float32)
                    + bias_ref[:, c0:c1])

        def rotary(y):
            cos, sa, sb = cos_ref[rows, :], sa_ref[rows, :], sb_ref[rows, :]
            outs = []
            for c in range(y.shape[1] // LANES):
                yc = y[:, c * LANES:(c + 1) * LANES]
                outs.append(yc * cos
                            + pltpu.roll(yc, half, 1) * sa
                            + pltpu.roll(yc, LANES - half, 1) * sb)
            return outs[0] if len(outs) == 1 else jnp.concatenate(outs, axis=1)

        qa_ref[rows, :] = (proj(_C_QA, _C_KA) * ATTN_SCALE).astype(qa_ref.dtype)
        ka_ref[rows, :] = proj(_C_KA, _C_VA).astype(ka_ref.dtype)
        va_ref[rows, :] = proj(_C_VA, _C_QB).astype(va_ref.dtype)
        qb_ref[rows, :] = (rotary(proj(_C_QB, _C_KB)) * ATTN_SCALE).astype(qb_ref.dtype)
        kv = proj(_C_KB, _C_G)
        kb_ref[rows, :] = _dup_heads(rotary(kv[:, :SWA_KV_WIDTH])).astype(kb_ref.dtype)
        vb_ref[rows, :] = _dup_heads(kv[:, SWA_KV_WIDTH:]).astype(vb_ref.dtype)


def _inproj(x2, g, b, bias, cos_t, sa_t, sb_t, w_in, batch, seq):
    t = batch * seq
    tm = INPROJ_TILE_ROWS
    nblk = seq // tm
    tok = lambda bi, i: (bi * nblk + i, 0)
    pos = lambda bi, i: (i, 0)
    bf = jnp.bfloat16
    out_widths = (NA_WIDTH, NA_WIDTH, NA_WIDTH, SWA_Q_WIDTH, _KV_DUP, _KV_DUP)
    resident = (2 * tm * D_MODEL * _F32_BYTES + 2 * tm * sum(out_widths) * _BF16_BYTES
                + 2 * 3 * tm * LANES * _F32_BYTES + D_MODEL * _C_G * _BF16_BYTES
                + 2 * D_MODEL * QKV_CHUNK * _F32_BYTES)
    return pl.pallas_call(
        _inproj_kernel,
        grid=(batch, nblk),
        in_specs=[
            pl.BlockSpec((tm, D_MODEL), tok),
            _const_spec((1, D_MODEL)), _const_spec((1, D_MODEL)),
            _const_spec((1, _C_END)),
            pl.BlockSpec((tm, LANES), pos), pl.BlockSpec((tm, LANES), pos),
            pl.BlockSpec((tm, LANES), pos),
            pl.BlockSpec(memory_space=pl.ANY),
        ],
        out_specs=tuple(pl.BlockSpec((tm, w), tok) for w in out_widths),
        out_shape=tuple(jax.ShapeDtypeStruct((t, w), bf) for w in out_widths),
        scratch_shapes=[
            pltpu.VMEM((D_MODEL, _C_G), bf),
            pltpu.VMEM((2, D_MODEL, QKV_CHUNK), jnp.float32),
            pltpu.SemaphoreType.DMA((2,)),
        ],
        compiler_params=pltpu.CompilerParams(
            dimension_semantics=("arbitrary", "arbitrary"),
            vmem_limit_bytes=_vmem_limit(resident)),
        name="ln_inproj_rotary",
    )(x2, g, b, bias, cos_t, sa_t, sb_t, w_in)


def _rotary_tables(seq):
    half = ROPE_DIM // 2
    inv_freq = np.power(ROPE_THETA, -np.arange(0, ROPE_DIM, 2, dtype=np.float64) / ROPE_DIM)
    ang = np.arange(seq, dtype=np.float64)[:, None] * inv_freq[None, :]
    cos, sin = np.cos(ang), np.sin(ang)
    ones = np.ones((seq, HEAD_DIM - ROPE_DIM))
    zeros = np.zeros((seq, HEAD_DIM - ROPE_DIM))
    zh = np.zeros((seq, half))
    cos_h = np.concatenate([cos, cos, ones], axis=1)
    sa_h = np.concatenate([zh, sin, zeros], axis=1)
    sb_h = np.concatenate([-sin, zh, zeros], axis=1)
    tile = lambda a: jnp.asarray(np.tile(a, (1, HEADS_PER_VREG)), dtype=jnp.float32)
    return tile(cos_h), tile(sa_h), tile(sb_h)


def _stack_heads(q):
    lane = lax.broadcasted_iota(jnp.int32, q.shape, 1)
    zero = jnp.zeros_like(q)
    return jnp.concatenate([jnp.where(lane < HEAD_DIM, q, zero),
                            jnp.where(lane >= HEAD_DIM, q, zero)], axis=0)


def _unstack_heads(o2):
    m = o2.shape[0] // 2
    lane = lax.broadcasted_iota(jnp.int32, (m, LANES), 1)
    return jnp.where(lane < HEAD_DIM, o2[:m], o2[m:])


_NT = (((1,), (1,)), ((), ()))


def _na_bias_kernel(rpb_ref, o_ref):
    w = lax.broadcasted_iota(jnp.int32, (GRID_W, LANES), 0)
    lane = lax.broadcasted_iota(jnp.int32, (GRID_W, LANES), 1)
    c = lane % GRID_W
    col_start = jnp.clip(w - NA_KW // 2, 0, GRID_W - NA_KW)
    inside = (c >= col_start) & (c < col_start + NA_KW)
    lower = lane < GRID_W
    n_dr = 2 * NA_KH - 1
    for e in range(HEADS_PER_VREG):
        lo, hi = [], []
        for dr in range(n_dr):
            x = jnp.broadcast_to(rpb_ref[0, e, dr:dr + 1, :], (GRID_W, LANES))
            lo.append(pltpu.roll(x, LANES - (NA_KW - 1), 1, stride=1, stride_axis=0))
            hi.append(pltpu.roll(x, GRID_W - (NA_KW - 1), 1, stride=1, stride_axis=0))
        for v in range(NA_KH):
            for kk in range(NA_KH // HEADS_PER_VREG):
                dr_a = 2 * kk - v + NA_KH - 1
                tile = jnp.where(inside, jnp.where(lower, lo[dr_a], hi[dr_a + 1]), MASK_VALUE)
                o_ref[v, 0, e * GRID_W:(e + 1) * GRID_W, kk * LANES:(kk + 1) * LANES] = tile


def _na_bias_table(rpb):
    n_pairs = NA_HEADS // HEADS_PER_VREG
    n_dr, n_dc = rpb.shape[1], rpb.shape[2]
    dr_pad = pl.cdiv(n_dr, SUBLANES) * SUBLANES
    rp = jnp.pad(rpb, ((0, 0), (0, dr_pad - n_dr), (0, LANES - n_dc)))
    rp = rp.reshape(n_pairs, HEADS_PER_VREG, dr_pad, LANES)
    m2, win = HEADS_PER_VREG * GRID_W, NA_KH * GRID_W
    return pl.pallas_call(
        _na_bias_kernel,
        grid=(n_pairs,),
        in_specs=[pl.BlockSpec((1, HEADS_PER_VREG, dr_pad, LANES), lambda j: (j, 0, 0, 0))],
        out_specs=pl.BlockSpec((NA_KH, 1, m2, win), lambda j: (0, j, 0, 0)),
        out_shape=jax.ShapeDtypeStruct((NA_KH, n_pairs, m2, win), jnp.float32),
        compiler_params=pltpu.CompilerParams(dimension_semantics=("parallel",)),
        name="na_bias_table",
    )(rp)


def _na_kernel(q_ref, k_ref, v_ref, tbl_ref, o_ref, s_scr, *, n_rows):
    win = NA_KH * GRID_W
    ones = jnp.ones((win, LANES), jnp.bfloat16)

    def body(i, carry):
        def window(t):
            r = i * NA_ROWS_PER_ITER + t
            rs = jnp.clip(r - NA_KH // 2, 0, n_rows - NA_KH)
            return r - rs, pl.multiple_of(rs * GRID_W, GRID_W)

        def q_rows(t):
            return pl.ds(pl.multiple_of((i * NA_ROWS_PER_ITER + t) * GRID_W, GRID_W), GRID_W)

        for t in range(NA_ROWS_PER_ITER):
            variant, start = window(t)
            lhs = _stack_heads(q_ref[0, q_rows(t), :])
            kw = k_ref[0, pl.ds(start, win), :]
            s = lax.dot_general(lhs, kw, _NT, preferred_element_type=jnp.float32)
            s_scr[t] = s + tbl_ref[variant, 0]
        for t in range(NA_ROWS_PER_ITER):
            s = s_scr[t]
            m = jnp.max(s, axis=-1, keepdims=True)
            p = jnp.exp(s - m)
            _, start = window(t)
            vw = jnp.concatenate([v_ref[0, pl.ds(start, win), :], ones], axis=1)
            o2 = jnp.dot(p.astype(jnp.bfloat16), vw, preferred_element_type=jnp.float32)
            o2 = o2[:, :LANES] / o2[:, LANES:]
            o_ref[0, q_rows(t), :] = _unstack_heads(o2).astype(o_ref.dtype)
        return carry

    lax.fori_loop(0, n_rows // NA_ROWS_PER_ITER, body, 0)


def _na_attention(qa, ka, va, tbl):
    batch, seq, _ = qa.shape
    n_rows = seq // GRID_W
    n_pairs = NA_HEADS // HEADS_PER_VREG
    m2, win = HEADS_PER_VREG * GRID_W, NA_KH * GRID_W
    seq_block = pl.BlockSpec((1, seq, LANES), lambda b, j: (b, 0, j))
    resident = (2 * 4 * seq * LANES * _BF16_BYTES + 2 * NA_KH * m2 * win * _F32_BYTES
                + NA_ROWS_PER_ITER * m2 * win * _F32_BYTES)
    return pl.pallas_call(
        functools.partial(_na_kernel, n_rows=n_rows),
        grid=(batch, n_pairs),
        in_specs=[seq_block, seq_block, seq_block,
                  pl.BlockSpec((NA_KH, 1, m2, win), lambda b, j: (0, j, 0, 0))],
        out_specs=seq_block,
        out_shape=jax.ShapeDtypeStruct(qa.shape, jnp.bfloat16),
        scratch_shapes=[pltpu.VMEM((NA_ROWS_PER_ITER, m2, win), jnp.float32)],
        compiler_params=pltpu.CompilerParams(
            dimension_semantics=("parallel", "parallel"),
            vmem_limit_bytes=_vmem_limit(resident)),
        name="na_attention",
    )(qa, ka, va, tbl)


def _swa_mask_table():
    qi = np.arange(SWA_BLOCK)[:, None]
    kc = np.arange(3 * SWA_BLOCK)[None, :]
    tbl = [np.where(np.abs(kc - d * SWA_BLOCK - qi) <= SWA_WINDOW, 0.0, MASK_VALUE)
           for d in range(3)]
    return jnp.asarray(np.stack(tbl), dtype=jnp.float32)


def _swa_kernel(sink_ref, q_ref, k_ref, v_ref, mask_ref, o_ref, s_scr, *, seq):
    j = pl.program_id(1)
    blk = SWA_BLOCK
    win = 3 * blk
    row = lax.broadcasted_iota(jnp.int32, (2 * blk, 1), 0)
    sink_col = jnp.where(row < blk, sink_ref[2 * j], sink_ref[2 * j + 1])
    ones = jnp.ones((win, LANES), jnp.bfloat16)

    def body(i, carry):
        def window(u):
            n = i * SWA_BLOCKS_PER_ITER + u
            start = jnp.clip((n - 1) * blk, 0, seq - win)
            return n - start // blk, pl.multiple_of(start, blk)

        def q_rows(u):
            return pl.ds(pl.multiple_of((i * SWA_BLOCKS_PER_ITER + u) * blk, blk), blk)

        for u in range(SWA_BLOCKS_PER_ITER):
            variant, start = window(u)
            lhs = _stack_heads(q_ref[0, q_rows(u), :])
            kw = k_ref[0, pl.ds(start, win), :]
            s = lax.dot_general(lhs, kw, _NT, preferred_element_type=jnp.float32)
            mask = mask_ref[variant]
            s_scr[u] = s + jnp.concatenate([mask, mask], axis=0)
        for u in range(SWA_BLOCKS_PER_ITER):
            s = s_scr[u]
            m = jnp.maximum(jnp.max(s, axis=-1, keepdims=True), sink_col)
            e = jnp.exp(s - m)
            _, start = window(u)
            vw = jnp.concatenate([v_ref[0, pl.ds(start, win), :], ones], axis=1)
            o2 = jnp.dot(e.astype(jnp.bfloat16), vw, preferred_element_type=jnp.float32)
            o2 = o2[:, :LANES] / (o2[:, LANES:] + jnp.exp(sink_col - m))
            o_ref[0, q_rows(u), :] = _unstack_heads(o2).astype(o_ref.dtype)
        return carry

    lax.fori_loop(0, seq // (SWA_BLOCKS_PER_ITER * blk), body, 0)


def _swa_attention(qb, kb, vb, sink):
    batch, seq, _ = qb.shape
    n_pairs = SWA_Q_HEADS // HEADS_PER_VREG
    pairs_per_group = n_pairs // SWA_KV_HEADS
    m2, win = HEADS_PER_VREG * SWA_BLOCK, 3 * SWA_BLOCK
    q_block = pl.BlockSpec((1, seq, LANES), lambda b, j: (b, 0, j))
    kv_block = pl.BlockSpec((1, seq, LANES), lambda b, j: (b, 0, j // pairs_per_group))
    resident = (2 * 4 * seq * LANES * _BF16_BYTES + 3 * SWA_BLOCK * win * _F32_BYTES
                + SWA_BLOCKS_PER_ITER * m2 * win * _F32_BYTES)
    return pl.pallas_call(
        functools.partial(_swa_kernel, seq=seq),
        grid=(batch, n_pairs),
        in_specs=[pl.BlockSpec(memory_space=pltpu.SMEM), q_block, kv_block, kv_block,
                  _const_spec((3, SWA_BLOCK, win))],
        out_specs=q_block,
        out_shape=jax.ShapeDtypeStruct(qb.shape, jnp.bfloat16),
        scratch_shapes=[pltpu.VMEM((SWA_BLOCKS_PER_ITER, m2, win), jnp.float32)],
        compiler_params=pltpu.CompilerParams(
            dimension_semantics=("parallel", "parallel"),
            vmem_limit_bytes=_vmem_limit(resident)),
        name="swa_attention",
    )(sink, qb, kb, vb, _swa_mask_table())


W_CHUNK = D_MODEL


def _tail_kernel(x_ref, g0_ref, b0_ref, na_ref, swa_ref, bin_ref, bout_ref, g1_ref, b1_ref,
                 bf1_ref, bf2_ref, g2_ref, b2_ref,
                 win_hbm, wna_hbm, wswa_hbm, wout_hbm, w1_hbm, w2_hbm,
                 o_ref,
                 wg_ref, wna_ref, wswa_ref, wout_ref, w1_ref, w2_ref, stage_ref, sem):
    @pl.when(pl.program_id(0) == 0)
    def _():
        cols = lambda c: pl.ds(c * W_CHUNK, W_CHUNK)
        pieces = [(win_hbm.at[0, :, pl.ds(_C_G + c * W_CHUNK, W_CHUNK)], wg_ref.at[:, cols(c)])
                  for c in range(2 * D_MODEL // W_CHUNK)]
        pieces += [(wna_hbm.at[0], wna_ref), (wswa_hbm.at[0], wswa_ref),
                   (wout_hbm.at[0], wout_ref)]
        pieces += [(w1_hbm.at[0, :, cols(c)], w1_ref.at[:, cols(c)])
                   for c in range(D_FF // W_CHUNK)]
        pieces += [(w2_hbm.at[0, cols(c), :], w2_ref.at[cols(c), :])
                   for c in range(D_FF // W_CHUNK)]
        _stage_weights_bf16(pieces, stage_ref, sem)

    h1 = []
    for r0 in range(0, x_ref.shape[0], SUB_TILE_ROWS):
        rows = slice(r0, r0 + SUB_TILE_ROWS)
        h = _layer_norm(x_ref[rows, :], g0_ref[...], b0_ref[...])
        gates = (jnp.dot(h.astype(jnp.bfloat16), wg_ref[...],
                         preferred_element_type=jnp.float32) + bin_ref[:, _C_G:_C_END])
        y_na = jnp.dot(na_ref[rows, :], wna_ref[...], preferred_element_type=jnp.float32)
        y_swa = jnp.dot(swa_ref[rows, :], wswa_ref[...], preferred_element_type=jnp.float32)
        mixed = (jax.nn.sigmoid(gates[:, :D_MODEL]) * y_na
                 + jax.nn.sigmoid(gates[:, D_MODEL:]) * y_swa)
        attn = jnp.dot(mixed.astype(jnp.bfloat16), wout_ref[...],
                       preferred_element_type=jnp.float32) + bout_ref[...]
        h1.append(_layer_norm(DEEPNORM_ALPHA * h + attn, g1_ref[...], b1_ref[...]))

    h = jnp.concatenate(h1, axis=0)
    hb = h.astype(jnp.bfloat16)
    acc = jnp.zeros(h.shape, jnp.float32)
    n_chunks = D_FF // FF_CHUNK
    for c in range(n_chunks):
        sl = slice(c * FF_CHUNK, (c + 1) * FF_CHUNK)
        u = jnp.dot(hb, w1_ref[:, sl], preferred_element_type=jnp.float32) + bf1_ref[:, sl]
        u = jnp.square(jnp.maximum(u, 0.0)).astype(jnp.bfloat16)
        if c + 1 < n_chunks:
            acc = acc + jnp.dot(u, w2_ref[sl, :], preferred_element_type=jnp.float32)
        else:
            for r0 in range(0, h.shape[0], SUB_TILE_ROWS):
                rows = slice(r0, r0 + SUB_TILE_ROWS)
                z = acc[rows] + jnp.dot(u[rows], w2_ref[sl, :],
                                        preferred_element_type=jnp.float32)
                o_ref[rows, :] = _layer_norm(DEEPNORM_ALPHA * h[rows] + z + bf2_ref[...],
                                             g2_ref[...], b2_ref[...])


def _tail(x2, g0, b0, na, swa, b_in, bout, g1, b1, bf1, bf2, g2, b2,
          w_in, w_na, w_swa, w_out, w_ff1, w_ff2):
    t = x2.shape[0]
    tm = TAIL_TILE_ROWS
    tok = lambda i: (i, 0)
    vec = _const_spec((1, D_MODEL))
    hbm = pl.BlockSpec(memory_space=pl.ANY)
    bf = jnp.bfloat16
    weight_elems = (D_MODEL * 2 * D_MODEL + (NA_WIDTH + SWA_Q_WIDTH) * D_MODEL
                    + D_MODEL * D_MODEL + 2 * D_MODEL * D_FF)
    resident = (weight_elems * _BF16_BYTES + 2 * W_CHUNK * D_MODEL * _F32_BYTES
                + 2 * 2 * tm * D_MODEL * _F32_BYTES
                + 2 * tm * (NA_WIDTH + SWA_Q_WIDTH) * _BF16_BYTES)
    return pl.pallas_call(
        _tail_kernel,
        grid=(t // tm,),
        in_specs=[
            pl.BlockSpec((tm, D_MODEL), tok), vec, vec,
            pl.BlockSpec((tm, NA_WIDTH), tok), pl.BlockSpec((tm, SWA_Q_WIDTH), tok),
            _const_spec((1, _C_END)), vec, vec, vec,
            _const_spec((1, D_FF)), vec, vec, vec,
            hbm, hbm, hbm, hbm, hbm, hbm,
        ],
        out_specs=pl.BlockSpec((tm, D_MODEL), tok),
        out_shape=jax.ShapeDtypeStruct((t, D_MODEL), jnp.float32),
        scratch_shapes=[
            pltpu.VMEM((D_MODEL, 2 * D_MODEL), bf),
            pltpu.VMEM((NA_WIDTH, D_MODEL), bf), pltpu.VMEM((SWA_Q_WIDTH, D_MODEL), bf),
            pltpu.VMEM((D_MODEL, D_MODEL), bf),
            pltpu.VMEM((D_MODEL, D_FF), bf), pltpu.VMEM((D_FF, D_MODEL), bf),
            pltpu.VMEM((2, W_CHUNK, D_MODEL), jnp.float32),
            pltpu.SemaphoreType.DMA((2,)),
        ],
        compiler_params=pltpu.CompilerParams(
            dimension_semantics=("arbitrary",), vmem_limit_bytes=_vmem_limit(resident)),
        name="mix_ffn",
    )(x2, g0, b0, na, swa, b_in, bout, g1, b1, bf1, bf2, g2, b2,
      w_in, w_na, w_swa, w_out, w_ff1, w_ff2)


def kernel(x, ln0_g, ln0_b, w_in, b_in, na_rpb, swa_sink, w_branch_na, w_branch_swa,
           w_out, b_out, ln1_g, ln1_b, w_ff1, b_ff1, w_ff2, b_ff2, ln2_g, ln2_b):
    batch, seq, d = x.shape
    assert d == D_MODEL and w_in.shape == (DEPTH, D_MODEL, _C_END) and DEPTH == 1
    assert seq % INPROJ_TILE_ROWS == 0 and (batch * seq) % TAIL_TILE_ROWS == 0
    assert seq % (NA_ROWS_PER_ITER * GRID_W) == 0 and seq // GRID_W >= NA_KH
    assert seq % (SWA_BLOCKS_PER_ITER * SWA_BLOCK) == 0 and seq >= 3 * SWA_BLOCK
    row = lambda v: v.reshape(1, -1)
    x2 = x.reshape(batch * seq, d)
    b_in_row = row(b_in[0])

    cos_t, sa_t, sb_t = _rotary_tables(seq)
    qa, ka, va, qb, kb, vb = _inproj(
        x2, row(ln0_g), row(ln0_b), b_in_row, cos_t, sa_t, sb_t, w_in, batch, seq)

    three = lambda a: a.reshape(batch, seq, a.shape[-1])
    o_na = _na_attention(three(qa), three(ka), three(va), _na_bias_table(na_rpb[0]))
    o_swa = _swa_attention(three(qb), three(kb), three(vb), swa_sink[0])

    return _tail(x2, row(ln0_g), row(ln0_b), o_na.reshape(batch * seq, -1),
                 o_swa.reshape(batch * seq, -1), b_in_row, row(b_out[0]),
                 row(ln1_g[0]), row(ln1_b[0]), row(b_ff1[0]), row(b_ff2[0]),
                 row(ln2_g[0]), row(ln2_b[0]),
                 w_in, w_branch_na, w_branch_swa, w_out, w_ff1, w_ff2).reshape(batch, seq, d)
```

```python
import functools

import numpy as np
import jax
import jax.numpy as jnp
from jax import lax
from jax.experimental import pallas as pl
from jax.experimental.pallas import tpu as pltpu

D_MODEL = 1024
GRID_W = 64
HEAD_DIM = 64
NA_HEADS = 8
NA_KH = 8
NA_KW = 16
NA_WIDTH = NA_HEADS * HEAD_DIM
SWA_Q_HEADS = 8
SWA_KV_HEADS = 2
SWA_WINDOW = 128
SWA_BLOCK = 128
SWA_Q_WIDTH = SWA_Q_HEADS * HEAD_DIM
SWA_KV_WIDTH = SWA_KV_HEADS * HEAD_DIM
ROPE_THETA = 500000.0
ROPE_DIM = HEAD_DIM // 4
D_FF = 4 * D_MODEL
DEPTH = 1
DEEPNORM_ALPHA = (2.0 * DEPTH) ** 0.25
LN_EPS = 1e-5
MASK_VALUE = -1e30
ATTN_SCALE = HEAD_DIM ** -0.5

LANES = 128
SUBLANES = 8
HEADS_PER_VREG = LANES // HEAD_DIM
VMEM_BYTES = 64 * 1024 * 1024
VMEM_TEMP_BYTES = 12 * 1024 * 1024
VMEM_LIMIT = 56 * 1024 * 1024

INPROJ_TILE_ROWS = 1024
TAIL_TILE_ROWS = 512
SUB_TILE_ROWS = 256
FF_CHUNK = 1024
NA_ROWS_PER_ITER = 16
SWA_BLOCKS_PER_ITER = 8

_C_QA = 0
_C_KA = _C_QA + NA_WIDTH
_C_VA = _C_KA + NA_WIDTH
_C_QB = _C_VA + NA_WIDTH
_C_KB = _C_QB + SWA_Q_WIDTH
_C_VB = _C_KB + SWA_KV_WIDTH
_C_G = _C_VB + SWA_KV_WIDTH
_C_END = _C_G + 2 * D_MODEL
_KV_DUP = SWA_KV_HEADS * LANES

_BF16_BYTES = 2
_F32_BYTES = 4


def _vmem_limit(resident_bytes):
    assert resident_bytes + VMEM_TEMP_BYTES <= VMEM_LIMIT < VMEM_BYTES, resident_bytes
    return VMEM_LIMIT


def _const_spec(shape):
    nd = len(shape)
    return pl.BlockSpec(shape, lambda *_: (0,) * nd, pipeline_mode=pl.Buffered(1))


def _layer_norm(x, g, b):
    mu = jnp.mean(x, axis=-1, keepdims=True)
    xc = x - mu
    var = jnp.mean(xc * xc, axis=-1, keepdims=True)
    return xc * lax.rsqrt(var + LN_EPS) * g + b


def _stage_weights_bf16(pieces, stage_ref, sem):
    def copy(k):
        src = pieces[k][0]
        dst = stage_ref.at[k % 2, pl.ds(0, src.shape[0]), :]
        return pltpu.make_async_copy(src, dst, sem.at[k % 2])

    copy(0).start()
    for k, (src, dst) in enumerate(pieces):
        if k + 1 < len(pieces):
            copy(k + 1).start()
        copy(k).wait()
        dst[...] = stage_ref[k % 2, 0:src.shape[0], :].astype(dst.dtype)


def _dup_heads(y):
    lane = lax.broadcasted_iota(jnp.int32, y.shape, 1)
    swapped = pltpu.roll(y, HEAD_DIM, 1)
    return jnp.concatenate([jnp.where(lane < HEAD_DIM, y, swapped),
                            jnp.where(lane >= HEAD_DIM, y, swapped)], axis=1)


QKV_CHUNK = _C_G // 3


def _inproj_kernel(x_ref, g_ref, b_ref, bias_ref, cos_ref, sa_ref, sb_ref, win_hbm,
                   qa_ref, ka_ref, va_ref, qb_ref, kb_ref, vb_ref,
                   w_ref, stage_ref, sem):
    half = ROPE_DIM // 2

    @pl.when((pl.program_id(0) == 0) & (pl.program_id(1) == 0))
    def _():
        cols = lambda c: pl.ds(c * QKV_CHUNK, QKV_CHUNK)
        _stage_weights_bf16([(win_hbm.at[0, :, cols(c)], w_ref.at[:, cols(c)])
                             for c in range(_C_G // QKV_CHUNK)], stage_ref, sem)

    for r0 in range(0, x_ref.shape[0], SUB_TILE_ROWS):
        rows = slice(r0, r0 + SUB_TILE_ROWS)
        h = _layer_norm(x_ref[rows, :], g_ref[...], b_ref[...])
        hb = h.astype(jnp.bfloat16)

        def proj(c0, c1):
            return (jnp.dot(hb, w_ref[:, c0:c1], preferred_element_type=jnp.float32)
                    + bias_ref[:, c0:c1])

        def rotary(y):
            cos, sa, sb = cos_ref[rows, :], sa_ref[rows, :], sb_ref[rows, :]
            outs = []
            for c in range(y.shape[1] // LANES):
                yc = y[:, c * LANES:(c + 1) * LANES]
                outs.append(yc * cos
                            + pltpu.roll(yc, half, 1) * sa
                            + pltpu.roll(yc, LANES - half, 1) * sb)
            return outs[0] if len(outs) == 1 else jnp.concatenate(outs, axis=1)

        qa_ref[rows, :] = (proj(_C_QA, _C_KA) * ATTN_SCALE).astype(qa_ref.dtype)
        ka_ref[rows, :] = proj(_C_KA, _C_VA).astype(ka_ref.dtype)
        va_ref[rows, :] = proj(_C_VA, _C_QB).astype(va_ref.dtype)
        qb_ref[rows, :] = (rotary(proj(_C_QB, _C_KB)) * ATTN_SCALE).astype(qb_ref.dtype)
        kv = proj(_C_KB, _C_G)
        kb_ref[rows, :] = _dup_heads(rotary(kv[:, :SWA_KV_WIDTH])).astype(kb_ref.dtype)
        vb_ref[rows, :] = _dup_heads(kv[:, SWA_KV_WIDTH:]).astype(vb_ref.dtype)


def _inproj(x2, g, b, bias, cos_t, sa_t, sb_t, w_in, batch, seq):
    t = batch * seq
    tm = INPROJ_TILE_ROWS
    nblk = seq // tm
    tok = lambda bi, i: (bi * nblk + i, 0)
    pos = lambda bi, i: (i, 0)
    bf = jnp.bfloat16
    out_widths = (NA_WIDTH, NA_WIDTH, NA_WIDTH, SWA_Q_WIDTH, _KV_DUP, _KV_DUP)
    resident = (2 * tm * D_MODEL * _F32_BYTES + 2 * tm * sum(out_widths) * _BF16_BYTES
                + 2 * 3 * tm * LANES * _F32_BYTES + D_MODEL * _C_G * _BF16_BYTES
                + 2 * D_MODEL * QKV_CHUNK * _F32_BYTES)
    return pl.pallas_call(
        _inproj_kernel,
        grid=(batch, nblk),
        in_specs=[
            pl.BlockSpec((tm, D_MODEL), tok),
            _const_spec((1, D_MODEL)), _const_spec((1, D_MODEL)),
            _const_spec((1, _C_END)),
            pl.BlockSpec((tm, LANES), pos), pl.BlockSpec((tm, LANES), pos),
            pl.BlockSpec((tm, LANES), pos),
            pl.BlockSpec(memory_space=pl.ANY),
        ],
        out_specs=tuple(pl.BlockSpec((tm, w), tok) for w in out_widths),
        out_shape=tuple(jax.ShapeDtypeStruct((t, w), bf) for w in out_widths),
        scratch_shapes=[
            pltpu.VMEM((D_MODEL, _C_G), bf),
            pltpu.VMEM((2, D_MODEL, QKV_CHUNK), jnp.float32),
            pltpu.SemaphoreType.DMA((2,)),
        ],
        compiler_params=pltpu.CompilerParams(
            dimension_semantics=("arbitrary", "arbitrary"),
            vmem_limit_bytes=_vmem_limit(resident)),
        name="ln_inproj_rotary",
    )(x2, g, b, bias, cos_t, sa_t, sb_t, w_in)


def _rotary_tables(seq):
    half = ROPE_DIM // 2
    inv_freq = np.power(ROPE_THETA, -np.arange(0, ROPE_DIM, 2, dtype=np.float64) / ROPE_DIM)
    ang = np.arange(seq, dtype=np.float64)[:, None] * inv_freq[None, :]
    cos, sin = np.cos(ang), np.sin(ang)
    ones = np.ones((seq, HEAD_DIM - ROPE_DIM))
    zeros = np.zeros((seq, HEAD_DIM - ROPE_DIM))
    zh = np.zeros((seq, half))
    cos_h = np.concatenate([cos, cos, ones], axis=1)
    sa_h = np.concatenate([zh, sin, zeros], axis=1)
    sb_h = np.concatenate([-sin, zh, zeros], axis=1)
    tile = lambda a: jnp.asarray(np.tile(a, (1, HEADS_PER_VREG)), dtype=jnp.float32)
    return tile(cos_h), tile(sa_h), tile(sb_h)


def _stack_heads(q):
    lane = lax.broadcasted_iota(jnp.int32, q.shape, 1)
    zero = jnp.zeros_like(q)
    return jnp.concatenate([jnp.where(lane < HEAD_DIM, q, zero),
                            jnp.where(lane >= HEAD_DIM, q, zero)], axis=0)


def _unstack_heads(o2):
    m = o2.shape[0] // 2
    lane = lax.broadcasted_iota(jnp.int32, (m, LANES), 1)
    return jnp.where(lane < HEAD_DIM, o2[:m], o2[m:])


_NT = (((1,), (1,)), ((), ()))


def _na_bias_kernel(rpb_ref, o_ref):
    w = lax.broadcasted_iota(jnp.int32, (GRID_W, LANES), 0)
    lane = lax.broadcasted_iota(jnp.int32, (GRID_W, LANES), 1)
    c = lane % GRID_W
    col_start = jnp.clip(w - NA_KW // 2, 0, GRID_W - NA_KW)
    inside = (c >= col_start) & (c < col_start + NA_KW)
    lower = lane < GRID_W
    n_dr = 2 * NA_KH - 1
    for e in range(HEADS_PER_VREG):
        lo, hi = [], []
        for dr in range(n_dr):
            x = jnp.broadcast_to(rpb_ref[0, e, dr:dr + 1, :], (GRID_W, LANES))
            lo.append(pltpu.roll(x, LANES - (NA_KW - 1), 1, stride=1, stride_axis=0))
            hi.append(pltpu.roll(x, GRID_W - (NA_KW - 1), 1, stride=1, stride_axis=0))
        for v in range(NA_KH):
            for kk in range(NA_KH // HEADS_PER_VREG):
                dr_a = 2 * kk - v + NA_KH - 1
                tile = jnp.where(inside, jnp.where(lower, lo[dr_a], hi[dr_a + 1]), MASK_VALUE)
                o_ref[v, 0, e * GRID_W:(e + 1) * GRID_W, kk * LANES:(kk + 1) * LANES] = tile


def _na_bias_table(rpb):
    n_pairs = NA_HEADS // HEADS_PER_VREG
    n_dr, n_dc = rpb.shape[1], rpb.shape[2]
    dr_pad = pl.cdiv(n_dr, SUBLANES) * SUBLANES
    rp = jnp.pad(rpb, ((0, 0), (0, dr_pad - n_dr), (0, LANES - n_dc)))
    rp = rp.reshape(n_pairs, HEADS_PER_VREG, dr_pad, LANES)
    m2, win = HEADS_PER_VREG * GRID_W, NA_KH * GRID_W
    return pl.pallas_call(
        _na_bias_kernel,
        grid=(n_pairs,),
        in_specs=[pl.BlockSpec((1, HEADS_PER_VREG, dr_pad, LANES), lambda j: (j, 0, 0, 0))],
        out_specs=pl.BlockSpec((NA_KH, 1, m2, win), lambda j: (0, j, 0, 0)),
        out_shape=jax.ShapeDtypeStruct((NA_KH, n_pairs, m2, win), jnp.float32),
        compiler_params=pltpu.CompilerParams(dimension_semantics=("parallel",)),
        name="na_bias_table",
    )(rp)


def _na_kernel(q_ref, k_ref, v_ref, tbl_ref, o_ref, s_scr, *, n_rows):
    win = NA_KH * GRID_W
    ones = jnp.ones((win, LANES), jnp.bfloat16)

    def body(i, carry):
        def window(t):
            r = i * NA_ROWS_PER_ITER + t
            rs = jnp.clip(r - NA_KH // 2, 0, n_rows - NA_KH)
            return r - rs, pl.multiple_of(rs * GRID_W, GRID_W)

        def q_rows(t):
            return pl.ds(pl.multiple_of((i * NA_ROWS_PER_ITER + t) * GRID_W, GRID_W), GRID_W)

        for t in range(NA_ROWS_PER_ITER):
            variant, start = window(t)
            lhs = _stack_heads(q_ref[0, q_rows(t), :])
            kw = k_ref[0, pl.ds(start, win), :]
            s = lax.dot_general(lhs, kw, _NT, preferred_element_type=jnp.float32)
            s_scr[t] = s + tbl_ref[variant, 0]
        for t in range(NA_ROWS_PER_ITER):
            s = s_scr[t]
            m = jnp.max(s, axis=-1, keepdims=True)
            p = jnp.exp(s - m)
            _, start = window(t)
            vw = jnp.concatenate([v_ref[0, pl.ds(start, win), :], ones], axis=1)
            o2 = jnp.dot(p.astype(jnp.bfloat16), vw, preferred_element_type=jnp.float32)
            o2 = o2[:, :LANES] / o2[:, LANES:]
            o_ref[0, q_rows(t), :] = _unstack_heads(o2).astype(o_ref.dtype)
        return carry

    lax.fori_loop(0, n_rows // NA_ROWS_PER_ITER, body, 0)


def _na_attention(qa, ka, va, tbl):
    batch, seq, _ = qa.shape
    n_rows = seq // GRID_W
    n_pairs = NA_HEADS // HEADS_PER_VREG
    m2, win = HEADS_PER_VREG * GRID_W, NA_KH * GRID_W
    seq_block = pl.BlockSpec((1, seq, LANES), lambda b, j: (b, 0, j))
    resident = (2 * 4 * seq * LANES * _BF16_BYTES + 2 * NA_KH * m2 * win * _F32_BYTES
                + NA_ROWS_PER_ITER * m2 * win * _F32_BYTES)
    return pl.pallas_call(
        functools.partial(_na_kernel, n_rows=n_rows),
        grid=(batch, n_pairs),
        in_specs=[seq_block, seq_block, seq_block,
                  pl.BlockSpec((NA_KH, 1, m2, win), lambda b, j: (0, j, 0, 0))],
        out_specs=seq_block,
        out_shape=jax.ShapeDtypeStruct(qa.shape, jnp.bfloat16),
        scratch_shapes=[pltpu.VMEM((NA_ROWS_PER_ITER, m2, win), jnp.float32)],
        compiler_params=pltpu.CompilerParams(
            dimension_semantics=("parallel", "parallel"),
            vmem_limit_bytes=_vmem_limit(resident)),
        name="na_attention",
    )(qa, ka, va, tbl)


def _swa_mask_table():
    qi = np.arange(SWA_BLOCK)[:, None]
    kc = np.arange(3 * SWA_BLOCK)[None, :]
    tbl = [np.where(np.abs(kc - d * SWA_BLOCK - qi) <= SWA_WINDOW, 0.0, MASK_VALUE)
           for d in range(3)]
    return jnp.asarray(np.stack(tbl), dtype=jnp.float32)


def _swa_kernel(sink_ref, q_ref, k_ref, v_ref, mask_ref, o_ref, s_scr, *, seq):
    j = pl.program_id(1)
    blk = SWA_BLOCK
    win = 3 * blk
    row = lax.broadcasted_iota(jnp.int32, (2 * blk, 1), 0)
    sink_col = jnp.where(row < blk, sink_ref[2 * j], sink_ref[2 * j + 1])
    ones = jnp.ones((win, LANES), jnp.bfloat16)

    def body(i, carry):
        def window(u):
            n = i * SWA_BLOCKS_PER_ITER + u
            start = jnp.clip((n - 1) * blk, 0, seq - win)
            return n - start // blk, pl.multiple_of(start, blk)

        def q_rows(u):
            return pl.ds(pl.multiple_of((i * SWA_BLOCKS_PER_ITER + u) * blk, blk), blk)

        for u in range(SWA_BLOCKS_PER_ITER):
            variant, start = window(u)
            lhs = _stack_heads(q_ref[0, q_rows(u), :])
            kw = k_ref[0, pl.ds(start, win), :]
            s = lax.dot_general(lhs, kw, _NT, preferred_element_type=jnp.float32)
            mask = mask_ref[variant]
            s_scr[u] = s + jnp.concatenate([mask, mask], axis=0)
        for u in range(SWA_BLOCKS_PER_ITER):
            s = s_scr[u]
            m = jnp.maximum(jnp.max(s, axis=-1, keepdims=True), sink_col)
            e = jnp.exp(s - m)
            _, start = window(u)
            vw = jnp.concatenate([v_ref[0, pl.ds(start, win), :], ones], axis=1)
            o2 = jnp.dot(e.astype(jnp.bfloat16), vw, preferred_element_type=jnp.float32)
            o2 = o2[:, :LANES] / (o2[:, LANES:] + jnp.exp(sink_col - m))
            o_ref[0, q_rows(u), :] = _unstack_heads(o2).astype(o_ref.dtype)
        return carry

    lax.fori_loop(0, seq // (SWA_BLOCKS_PER_ITER * blk), body, 0)


def _swa_attention(qb, kb, vb, sink):
    batch, seq, _ = qb.shape
    n_pairs = SWA_Q_HEADS // HEADS_PER_VREG
    pairs_per_group = n_pairs // SWA_KV_HEADS
    m2, win = HEADS_PER_VREG * SWA_BLOCK, 3 * SWA_BLOCK
    q_block = pl.BlockSpec((1, seq, LANES), lambda b, j: (b, 0, j))
    kv_block = pl.BlockSpec((1, seq, LANES), lambda b, j: (b, 0, j // pairs_per_group))
    resident = (2 * 4 * seq * LANES * _BF16_BYTES + 3 * SWA_BLOCK * win * _F32_BYTES
                + SWA_BLOCKS_PER_ITER * m2 * win * _F32_BYTES)
    return pl.pallas_call(
        functools.partial(_swa_kernel, seq=seq),
        grid=(batch, n_pairs),
        in_specs=[pl.BlockSpec(memory_space=pltpu.SMEM), q_block, kv_block, kv_block,
                  _const_spec((3, SWA_BLOCK, win))],
        out_specs=q_block,
        out_shape=jax.ShapeDtypeStruct(qb.shape, jnp.bfloat16),
        scratch_shapes=[pltpu.VMEM((SWA_BLOCKS_PER_ITER, m2, win), jnp.float32)],
        compiler_params=pltpu.CompilerParams(
            dimension_semantics=("parallel", "parallel"),
            vmem_limit_bytes=_vmem_limit(resident)),
        name="swa_attention",
    )(sink, qb, kb, vb, _swa_mask_table())


W_CHUNK = D_MODEL


def _tail_kernel(x_ref, g0_ref, b0_ref, na_ref, swa_ref, bin_ref, bout_ref, g1_ref, b1_ref,
                 bf1_ref, bf2_ref, g2_ref, b2_ref,
                 win_hbm, wna_hbm, wswa_hbm, wout_hbm, w1_hbm, w2_hbm,
                 o_ref,
                 wg_ref, wna_ref, wswa_ref, wout_ref, w1_ref, w2_ref, stage_ref, sem):
    @pl.when(pl.program_id(0) == 0)
    def _():
        cols = lambda c: pl.ds(c * W_CHUNK, W_CHUNK)
        pieces = [(win_hbm.at[0, :, pl.ds(_C_G + c * W_CHUNK, W_CHUNK)], wg_ref.at[:, cols(c)])
                  for c in range(2 * D_MODEL // W_CHUNK)]
        pieces += [(wna_hbm.at[0], wna_ref), (wswa_hbm.at[0], wswa_ref),
                   (wout_hbm.at[0], wout_ref)]
        pieces += [(w1_hbm.at[0, :, cols(c)], w1_ref.at[:, cols(c)])
                   for c in range(D_FF // W_CHUNK)]
        pieces += [(w2_hbm.at[0, cols(c), :], w2_ref.at[cols(c), :])
                   for c in range(D_FF // W_CHUNK)]
        _stage_weights_bf16(pieces, stage_ref, sem)

    h1 = []
    for r0 in range(0, x_ref.shape[0], SUB_TILE_ROWS):
        rows = slice(r0, r0 + SUB_TILE_ROWS)
        h = _layer_norm(x_ref[rows, :], g0_ref[...], b0_ref[...])
        gates = (jnp.dot(h.astype(jnp.bfloat16), wg_ref[...],
                         preferred_element_type=jnp.float32) + bin_ref[:, _C_G:_C_END])
        y_na = jnp.dot(na_ref[rows, :], wna_ref[...], preferred_element_type=jnp.float32)
        y_swa = jnp.dot(swa_ref[rows, :], wswa_ref[...], preferred_element_type=jnp.float32)
        mixed = (jax.nn.sigmoid(gates[:, :D_MODEL]) * y_na
                 + jax.nn.sigmoid(gates[:, D_MODEL:]) * y_swa)
        attn = jnp.dot(mixed.astype(jnp.bfloat16), wout_ref[...],
                       preferred_element_type=jnp.float32) + bout_ref[...]
        h1.append(_layer_norm(DEEPNORM_ALPHA * h + attn, g1_ref[...], b1_ref[...]))

    h = jnp.concatenate(h1, axis=0)
    hb = h.astype(jnp.bfloat16)
    acc = jnp.zeros(h.shape, jnp.float32)
    n_chunks = D_FF // FF_CHUNK
    for c in range(n_chunks):
        sl = slice(c * FF_CHUNK, (c + 1) * FF_CHUNK)
        if c == 0:
            u = jnp.concatenate(
                [jnp.dot(hs.astype(jnp.bfloat16), w1_ref[:, sl],
                         preferred_element_type=jnp.float32) for hs in h1], axis=0)
        else:
            u = jnp.dot(hb, w1_ref[:, sl], preferred_element_type=jnp.float32)
        u = jnp.square(jnp.maximum(u + bf1_ref[:, sl], 0.0)).astype(jnp.bfloat16)
        if c + 1 < n_chunks:
            acc = acc + jnp.dot(u, w2_ref[sl, :], preferred_element_type=jnp.float32)
        else:
            for r0 in range(0, h.shape[0], SUB_TILE_ROWS):
                rows = slice(r0, r0 + SUB_TILE_ROWS)
                z = acc[rows] + jnp.dot(u[rows], w2_ref[sl, :],
                                        preferred_element_type=jnp.float32)
                o_ref[rows, :] = _layer_norm(DEEPNORM_ALPHA * h[rows] + z + bf2_ref[...],
                                             g2_ref[...], b2_ref[...])


def _tail(x2, g0, b0, na, swa, b_in, bout, g1, b1, bf1, bf2, g2, b2,
          w_in, w_na, w_swa, w_out, w_ff1, w_ff2):
    t = x2.shape[0]
    tm = TAIL_TILE_ROWS
    tok = lambda i: (i, 0)
    vec = _const_spec((1, D_MODEL))
    hbm = pl.BlockSpec(memory_space=pl.ANY)
    bf = jnp.bfloat16
    weight_elems = (D_MODEL * 2 * D_MODEL + (NA_WIDTH + SWA_Q_WIDTH) * D_MODEL
                    + D_MODEL * D_MODEL + 2 * D_MODEL * D_FF)
    resident = (weight_elems * _BF16_BYTES + 2 * W_CHUNK * D_MODEL * _F32_BYTES
                + 2 * 2 * tm * D_MODEL * _F32_BYTES
                + 2 * tm * (NA_WIDTH + SWA_Q_WIDTH) * _BF16_BYTES)
    return pl.pallas_call(
        _tail_kernel,
        grid=(t // tm,),
        in_specs=[
            pl.BlockSpec((tm, D_MODEL), tok), vec, vec,
            pl.BlockSpec((tm, NA_WIDTH), tok), pl.BlockSpec((tm, SWA_Q_WIDTH), tok),
            _const_spec((1, _C_END)), vec, vec, vec,
            _const_spec((1, D_FF)), vec, vec, vec,
            hbm, hbm, hbm, hbm, hbm, hbm,
        ],
        out_specs=pl.BlockSpec((tm, D_MODEL), tok),
        out_shape=jax.ShapeDtypeStruct((t, D_MODEL), jnp.float32),
        scratch_shapes=[
            pltpu.VMEM((D_MODEL, 2 * D_MODEL), bf),
            pltpu.VMEM((NA_WIDTH, D_MODEL), bf), pltpu.VMEM((SWA_Q_WIDTH, D_MODEL), bf),
            pltpu.VMEM((D_MODEL, D_MODEL), bf),
            pltpu.VMEM((D_MODEL, D_FF), bf), pltpu.VMEM((D_FF, D_MODEL), bf),
            pltpu.VMEM((2, W_CHUNK, D_MODEL), jnp.float32),
            pltpu.SemaphoreType.DMA((2,)),
        ],
        compiler_params=pltpu.CompilerParams(
            dimension_semantics=("arbitrary",), vmem_limit_bytes=_vmem_limit(resident)),
        name="mix_ffn",
    )(x2, g0, b0, na, swa, b_in, bout, g1, b1, bf1, bf2, g2, b2,
      w_in, w_na, w_swa, w_out, w_ff1, w_ff2)


def kernel(x, ln0_g, ln0_b, w_in, b_in, na_rpb, swa_sink, w_branch_na, w_branch_swa,
           w_out, b_out, ln1_g, ln1_b, w_ff1, b_ff1, w_ff2, b_ff2, ln2_g, ln2_b):
    batch, seq, d = x.shape
    assert d == D_MODEL and w_in.shape == (DEPTH, D_MODEL, _C_END) and DEPTH == 1
    assert seq % INPROJ_TILE_ROWS == 0 and (batch * seq) % TAIL_TILE_ROWS == 0
    assert seq % (NA_ROWS_PER_ITER * GRID_W) == 0 and seq // GRID_W >= NA_KH
    assert seq % (SWA_BLOCKS_PER_ITER * SWA_BLOCK) == 0 and seq >= 3 * SWA_BLOCK
    row = lambda v: v.reshape(1, -1)
    x2 = x.reshape(batch * seq, d)
    b_in_row = row(b_in[0])

    cos_t, sa_t, sb_t = _rotary_tables(seq)
    qa, ka, va, qb, kb, vb = _inproj(
        x2, row(ln0_g), row(ln0_b), b_in_row, cos_t, sa_t, sb_t, w_in, batch, seq)

    three = lambda a: a.reshape(batch, seq, a.shape[-1])
    o_na = _na_attention(three(qa), three(ka), three(va), _na_bias_table(na_rpb[0]))
    o_swa = _swa_attention(three(qb), three(kb), three(vb), swa_sink[0])

    return _tail(x2, row(ln0_g), row(ln0_b), o_na.reshape(batch * seq, -1),
                 o_swa.reshape(batch * seq, -1), b_in_row, row(b_out[0]),
                 row(ln1_g[0]), row(ln1_b[0]), row(b_ff1[0]), row(b_ff2[0]),
                 row(ln2_g[0]), row(ln2_b[0]),
                 w_in, w_branch_na, w_branch_swa, w_out, w_ff1, w_ff2).reshape(batch, seq, d)
```

```python
import functools

import numpy as np
import jax
import jax.numpy as jnp
from jax import lax
from jax.experimental import pallas as pl
from jax.experimental.pallas import tpu as pltpu

D_MODEL = 1024
GRID_W = 64
HEAD_DIM = 64
NA_HEADS = 8
NA_KH = 8
NA_KW = 16
NA_WIDTH = NA_HEADS * HEAD_DIM
SWA_Q_HEADS = 8
SWA_KV_HEADS = 2
SWA_WINDOW = 128
SWA_BLOCK = 128
SWA_Q_WIDTH = SWA_Q_HEADS * HEAD_DIM
SWA_KV_WIDTH = SWA_KV_HEADS * HEAD_DIM
ROPE_THETA = 500000.0
ROPE_DIM = HEAD_DIM // 4
D_FF = 4 * D_MODEL
DEPTH = 1
DEEPNORM_ALPHA = (2.0 * DEPTH) ** 0.25
LN_EPS = 1e-5
MASK_VALUE = -1e30
ATTN_SCALE = HEAD_DIM ** -0.5

LANES = 128
SUBLANES = 8
HEADS_PER_VREG = LANES // HEAD_DIM
VMEM_BYTES = 64 * 1024 * 1024
VMEM_TEMP_BYTES = 12 * 1024 * 1024
VMEM_LIMIT = 56 * 1024 * 1024

INPROJ_TILE_ROWS = 1024
TAIL_TILE_ROWS = 512
SUB_TILE_ROWS = 256
FF_CHUNK = 1024
NA_ROWS_PER_ITER = 16
SWA_BLOCKS_PER_ITER = 8

_C_QA = 0
_C_KA = _C_QA + NA_WIDTH
_C_VA = _C_KA + NA_WIDTH
_C_QB = _C_VA + NA_WIDTH
_C_KB = _C_QB + SWA_Q_WIDTH
_C_VB = _C_KB + SWA_KV_WIDTH
_C_G = _C_VB + SWA_KV_WIDTH
_C_END = _C_G + 2 * D_MODEL
_KV_DUP = SWA_KV_HEADS * LANES

_BF16_BYTES = 2
_F32_BYTES = 4


def _vmem_limit(resident_bytes):
    assert resident_bytes + VMEM_TEMP_BYTES <= VMEM_LIMIT < VMEM_BYTES, resident_bytes
    return VMEM_LIMIT


def _const_spec(shape):
    nd = len(shape)
    return pl.BlockSpec(shape, lambda *_: (0,) * nd, pipeline_mode=pl.Buffered(1))


def _layer_norm(x, g, b):
    mu = jnp.mean(x, axis=-1, keepdims=True)
    xc = x - mu
    var = jnp.mean(xc * xc, axis=-1, keepdims=True)
    return xc * lax.rsqrt(var + LN_EPS) * g + b


def _stage_weights_bf16(pieces, stage_ref, sem):
    def copy(k):
        src = pieces[k][0]
        dst = stage_ref.at[k % 2, pl.ds(0, src.shape[0]), :]
        return pltpu.make_async_copy(src, dst, sem.at[k % 2])

    copy(0).start()
    for k, (src, dst) in enumerate(pieces):
        if k + 1 < len(pieces):
            copy(k + 1).start()
        copy(k).wait()
        dst[...] = stage_ref[k % 2, 0:src.shape[0], :].astype(dst.dtype)


def _dup_heads(y):
    lane = lax.broadcasted_iota(jnp.int32, y.shape, 1)
    swapped = pltpu.roll(y, HEAD_DIM, 1)
    return jnp.concatenate([jnp.where(lane < HEAD_DIM, y, swapped),
                            jnp.where(lane >= HEAD_DIM, y, swapped)], axis=1)


QKV_CHUNK = _C_G // 3


def _inproj_kernel(x_ref, g_ref, b_ref, bias_ref, cos_ref, sa_ref, sb_ref, win_hbm,
                   qa_ref, ka_ref, va_ref, qb_ref, kb_ref, vb_ref,
                   w_ref, stage_ref, sem):
    half = ROPE_DIM // 2

    @pl.when((pl.program_id(0) == 0) & (pl.program_id(1) == 0))
    def _():
        cols = lambda c: pl.ds(c * QKV_CHUNK, QKV_CHUNK)
        _stage_weights_bf16([(win_hbm.at[0, :, cols(c)], w_ref.at[:, cols(c)])
                             for c in range(_C_G // QKV_CHUNK)], stage_ref, sem)

    for r0 in range(0, x_ref.shape[0], SUB_TILE_ROWS):
        rows = slice(r0, r0 + SUB_TILE_ROWS)
        h = _layer_norm(x_ref[rows, :], g_ref[...], b_ref[...])
        hb = h.astype(jnp.bfloat16)

        def proj(c0, c1):
            return (jnp.dot(hb, w_ref[:, c0:c1], preferred_element_type=jnp.float32)
                    + bias_ref[:, c0:c1])

        def rotary(y):
            cos, sa, sb = cos_ref[rows, :], sa_ref[rows, :], sb_ref[rows, :]
            outs = []
            for c in range(y.shape[1] // LANES):
                yc = y[:, c * LANES:(c + 1) * LANES]
                outs.append(yc * cos
                            + pltpu.roll(yc, half, 1) * sa
                            + pltpu.roll(yc, LANES - half, 1) * sb)
            return outs[0] if len(outs) == 1 else jnp.concatenate(outs, axis=1)

        qa_ref[rows, :] = (proj(_C_QA, _C_KA) * ATTN_SCALE).astype(qa_ref.dtype)
        ka_ref[rows, :] = proj(_C_KA, _C_VA).astype(ka_ref.dtype)
        va_ref[rows, :] = proj(_C_VA, _C_QB).astype(va_ref.dtype)
        qb_ref[rows, :] = (rotary(proj(_C_QB, _C_KB)) * ATTN_SCALE).astype(qb_ref.dtype)
        kv = proj(_C_KB, _C_G)
        kb_ref[rows, :] = _dup_heads(rotary(kv[:, :SWA_KV_WIDTH])).astype(kb_ref.dtype)
        vb_ref[rows, :] = _dup_heads(kv[:, SWA_KV_WIDTH:]).astype(vb_ref.dtype)


def _inproj(x2, g, b, bias, cos_t, sa_t, sb_t, w_in, batch, seq):
    t = batch * seq
    tm = INPROJ_TILE_ROWS
    nblk = seq // tm
    tok = lambda bi, i: (bi * nblk + i, 0)
    pos = lambda bi, i: (i, 0)
    bf = jnp.bfloat16
    out_widths = (NA_WIDTH, NA_WIDTH, NA_WIDTH, SWA_Q_WIDTH, _KV_DUP, _KV_DUP)
    resident = (2 * tm * D_MODEL * _F32_BYTES + 2 * tm * sum(out_widths) * _BF16_BYTES
                + 2 * 3 * tm * LANES * _F32_BYTES + D_MODEL * _C_G * _BF16_BYTES
                + 2 * D_MODEL * QKV_CHUNK * _F32_BYTES)
    return pl.pallas_call(
        _inproj_kernel,
        grid=(batch, nblk),
        in_specs=[
            pl.BlockSpec((tm, D_MODEL), tok),
            _const_spec((1, D_MODEL)), _const_spec((1, D_MODEL)),
            _const_spec((1, _C_END)),
            pl.BlockSpec((tm, LANES), pos), pl.BlockSpec((tm, LANES), pos),
            pl.BlockSpec((tm, LANES), pos),
            pl.BlockSpec(memory_space=pl.ANY),
        ],
        out_specs=tuple(pl.BlockSpec((tm, w), tok) for w in out_widths),
        out_shape=tuple(jax.ShapeDtypeStruct((t, w), bf) for w in out_widths),
        scratch_shapes=[
            pltpu.VMEM((D_MODEL, _C_G), bf),
            pltpu.VMEM((2, D_MODEL, QKV_CHUNK), jnp.float32),
            pltpu.SemaphoreType.DMA((2,)),
        ],
        compiler_params=pltpu.CompilerParams(
            dimension_semantics=("arbitrary", "arbitrary"),
            vmem_limit_bytes=_vmem_limit(resident)),
        name="ln_inproj_rotary",
    )(x2, g, b, bias, cos_t, sa_t, sb_t, w_in)


def _rotary_tables(seq):
    half = ROPE_DIM // 2
    inv_freq = np.power(ROPE_THETA, -np.arange(0, ROPE_DIM, 2, dtype=np.float64) / ROPE_DIM)
    ang = np.arange(seq, dtype=np.float64)[:, None] * inv_freq[None, :]
    cos, sin = np.cos(ang), np.sin(ang)
    ones = np.ones((seq, HEAD_DIM - ROPE_DIM))
    zeros = np.zeros((seq, HEAD_DIM - ROPE_DIM))
    zh = np.zeros((seq, half))
    cos_h = np.concatenate([cos, cos, ones], axis=1)
    sa_h = np.concatenate([zh, sin, zeros], axis=1)
    sb_h = np.concatenate([-sin, zh, zeros], axis=1)
    tile = lambda a: jnp.asarray(np.tile(a, (1, HEADS_PER_VREG)), dtype=jnp.float32)
    return tile(cos_h), tile(sa_h), tile(sb_h)


def _stack_heads(q):
    lane = lax.broadcasted_iota(jnp.int32, q.shape, 1)
    zero = jnp.zeros_like(q)
    return jnp.concatenate([jnp.where(lane < HEAD_DIM, q, zero),
                            jnp.where(lane >= HEAD_DIM, q, zero)], axis=0)


def _unstack_heads(o2):
    m = o2.shape[0] // 2
    lane = lax.broadcasted_iota(jnp.int32, (m, LANES), 1)
    return jnp.where(lane < HEAD_DIM, o2[:m], o2[m:])


_NT = (((1,), (1,)), ((), ()))


def _na_bias_kernel(rpb_ref, o_ref):
    w = lax.broadcasted_iota(jnp.int32, (GRID_W, LANES), 0)
    lane = lax.broadcasted_iota(jnp.int32, (GRID_W, LANES), 1)
    c = lane % GRID_W
    col_start = jnp.clip(w - NA_KW // 2, 0, GRID_W - NA_KW)
    inside = (c >= col_start) & (c < col_start + NA_KW)
    lower = lane < GRID_W
    n_dr = 2 * NA_KH - 1
    for e in range(HEADS_PER_VREG):
        lo, hi = [], []
        for dr in range(n_dr):
            x = jnp.broadcast_to(rpb_ref[0, e, dr:dr + 1, :], (GRID_W, LANES))
            lo.append(pltpu.roll(x, LANES - (NA_KW - 1), 1, stride=1, stride_axis=0))
            hi.append(pltpu.roll(x, GRID_W - (NA_KW - 1), 1, stride=1, stride_axis=0))
        for v in range(NA_KH):
            for kk in range(NA_KH // HEADS_PER_VREG):
                dr_a = 2 * kk - v + NA_KH - 1
                tile = jnp.where(inside, jnp.where(lower, lo[dr_a], hi[dr_a + 1]), MASK_VALUE)
                o_ref[v, 0, e * GRID_W:(e + 1) * GRID_W, kk * LANES:(kk + 1) * LANES] = tile


def _na_bias_table(rpb):
    n_pairs = NA_HEADS // HEADS_PER_VREG
    n_dr, n_dc = rpb.shape[1], rpb.shape[2]
    dr_pad = pl.cdiv(n_dr, SUBLANES) * SUBLANES
    rp = jnp.pad(rpb, ((0, 0), (0, dr_pad - n_dr), (0, LANES - n_dc)))
    rp = rp.reshape(n_pairs, HEADS_PER_VREG, dr_pad, LANES)
    m2, win = HEADS_PER_VREG * GRID_W, NA_KH * GRID_W
    return pl.pallas_call(
        _na_bias_kernel,
        grid=(n_pairs,),
        in_specs=[pl.BlockSpec((1, HEADS_PER_VREG, dr_pad, LANES), lambda j: (j, 0, 0, 0))],
        out_specs=pl.BlockSpec((NA_KH, 1, m2, win), lambda j: (0, j, 0, 0)),
        out_shape=jax.ShapeDtypeStruct((NA_KH, n_pairs, m2, win), jnp.float32),
        compiler_params=pltpu.CompilerParams(dimension_semantics=("parallel",)),
        name="na_bias_table",
    )(rp)


def _na_kernel(q_ref, k_ref, v_ref, tbl_ref, o_ref, s_scr, *, n_rows):
    win = NA_KH * GRID_W
    ones = jnp.ones((win, LANES), jnp.bfloat16)

    def body(i, carry):
        def window(t):
            r = i * NA_ROWS_PER_ITER + t
            rs = jnp.clip(r - NA_KH // 2, 0, n_rows - NA_KH)
            return r - rs, pl.multiple_of(rs * GRID_W, GRID_W)

        def q_rows(t):
            return pl.ds(pl.multiple_of((i * NA_ROWS_PER_ITER + t) * GRID_W, GRID_W), GRID_W)

        for t in range(NA_ROWS_PER_ITER):
            variant, start = window(t)
            lhs = _stack_heads(q_ref[0, q_rows(t), :])
            kw = k_ref[0, pl.ds(start, win), :]
            s = lax.dot_general(lhs, kw, _NT, preferred_element_type=jnp.float32)
            s_scr[t] = s + tbl_ref[variant, 0]
        for t in range(NA_ROWS_PER_ITER):
            s = s_scr[t]
            m = jnp.max(s, axis=-1, keepdims=True)
            p = jnp.exp(s - m)
            _, start = window(t)
            vw = jnp.concatenate([v_ref[0, pl.ds(start, win), :], ones], axis=1)
            o2 = jnp.dot(p.astype(jnp.bfloat16), vw, preferred_element_type=jnp.float32)
            o2 = o2[:, :LANES] / o2[:, LANES:]
            o_ref[0, q_rows(t), :] = _unstack_heads(o2).astype(o_ref.dtype)
        return carry

    lax.fori_loop(0, n_rows // NA_ROWS_PER_ITER, body, 0)


def _swa_mask_table():
    qi = np.arange(SWA_BLOCK)[:, None]
    kc = np.arange(3 * SWA_BLOCK)[None, :]
    tbl = [np.where(np.abs(kc - d * SWA_BLOCK - qi) <= SWA_WINDOW, 0.0, MASK_VALUE)
           for d in range(3)]
    return jnp.asarray(np.stack(tbl), dtype=jnp.float32)


def _swa_kernel(sink_ref, q_ref, k_ref, v_ref, mask_ref, o_ref, s_scr, *, seq):
    j = pl.program_id(1)
    blk = SWA_BLOCK
    win = 3 * blk
    row = lax.broadcasted_iota(jnp.int32, (2 * blk, 1), 0)
    sink_col = jnp.where(row < blk, sink_ref[2 * j], sink_ref[2 * j + 1])
    ones = jnp.ones((win, LANES), jnp.bfloat16)

    def body(i, carry):
        def window(u):
            n = i * SWA_BLOCKS_PER_ITER + u
            start = jnp.clip((n - 1) * blk, 0, seq - win)
            return n - start // blk, pl.multiple_of(start, blk)

        def q_rows(u):
            return pl.ds(pl.multiple_of((i * SWA_BLOCKS_PER_ITER + u) * blk, blk), blk)

        for u in range(SWA_BLOCKS_PER_ITER):
            variant, start = window(u)
            lhs = _stack_heads(q_ref[0, q_rows(u), :])
            kw = k_ref[0, pl.ds(start, win), :]
            s = lax.dot_general(lhs, kw, _NT, preferred_element_type=jnp.float32)
            mask = mask_ref[variant]
            s_scr[u] = s + jnp.concatenate([mask, mask], axis=0)
        for u in range(SWA_BLOCKS_PER_ITER):
            s = s_scr[u]
            m = jnp.maximum(jnp.max(s, axis=-1, keepdims=True), sink_col)
            e = jnp.exp(s - m)
            _, start = window(u)
            vw = jnp.concatenate([v_ref[0, pl.ds(start, win), :], ones], axis=1)
            o2 = jnp.dot(e.astype(jnp.bfloat16), vw, preferred_element_type=jnp.float32)
            o2 = o2[:, :LANES] / (o2[:, LANES:] + jnp.exp(sink_col - m))
            o_ref[0, q_rows(u), :] = _unstack_heads(o2).astype(o_ref.dtype)
        return carry

    lax.fori_loop(0, seq // (SWA_BLOCKS_PER_ITER * blk), body, 0)


def _attention_kernel(sink_ref, qa_ref, ka_ref, va_ref, tbl_ref, qb_ref, kb_ref, vb_ref,
                      mask_ref, ona_ref, oswa_ref, na_scr, swa_scr, *, n_rows, seq):
    _na_kernel(qa_ref, ka_ref, va_ref, tbl_ref, ona_ref, na_scr, n_rows=n_rows)
    _swa_kernel(sink_ref, qb_ref, kb_ref, vb_ref, mask_ref, oswa_ref, swa_scr, seq=seq)


def _attention(qa, ka, va, tbl, qb, kb, vb, sink):
    batch, seq, _ = qa.shape
    n_rows = seq // GRID_W
    n_pairs = NA_HEADS // HEADS_PER_VREG
    assert n_pairs == SWA_Q_HEADS // HEADS_PER_VREG
    pairs_per_group = n_pairs // SWA_KV_HEADS
    na_m, na_win = HEADS_PER_VREG * GRID_W, NA_KH * GRID_W
    swa_m, swa_win = HEADS_PER_VREG * SWA_BLOCK, 3 * SWA_BLOCK
    pair_block = pl.BlockSpec((1, seq, LANES), lambda b, j: (b, 0, j))
    kv_block = pl.BlockSpec((1, seq, LANES), lambda b, j: (b, 0, j // pairs_per_group))
    resident = (2 * 8 * seq * LANES * _BF16_BYTES + 2 * NA_KH * na_m * na_win * _F32_BYTES
                + 3 * SWA_BLOCK * swa_win * _F32_BYTES
                + NA_ROWS_PER_ITER * na_m * na_win * _F32_BYTES
                + SWA_BLOCKS_PER_ITER * swa_m * swa_win * _F32_BYTES)
    out = jax.ShapeDtypeStruct(qa.shape, jnp.bfloat16)
    return pl.pallas_call(
        functools.partial(_attention_kernel, n_rows=n_rows, seq=seq),
        grid=(batch, n_pairs),
        in_specs=[pl.BlockSpec(memory_space=pltpu.SMEM),
                  pair_block, pair_block, pair_block,
                  pl.BlockSpec((NA_KH, 1, na_m, na_win), lambda b, j: (0, j, 0, 0)),
                  pair_block, kv_block, kv_block,
                  _const_spec((3, SWA_BLOCK, swa_win))],
        out_specs=(pair_block, pair_block),
        out_shape=(out, out),
        scratch_shapes=[pltpu.VMEM((NA_ROWS_PER_ITER, na_m, na_win), jnp.float32),
                        pltpu.VMEM((SWA_BLOCKS_PER_ITER, swa_m, swa_win), jnp.float32)],
        compiler_params=pltpu.CompilerParams(
            dimension_semantics=("parallel", "parallel"),
            vmem_limit_bytes=_vmem_limit(resident)),
        name="attention",
    )(sink, qa, ka, va, tbl, qb, kb, vb, _swa_mask_table())


W_CHUNK = D_MODEL


def _tail_kernel(x_ref, g0_ref, b0_ref, na_ref, swa_ref, bin_ref, bout_ref, g1_ref, b1_ref,
                 bf1_ref, bf2_ref, g2_ref, b2_ref,
                 win_hbm, wna_hbm, wswa_hbm, wout_hbm, w1_hbm, w2_hbm,
                 o_ref,
                 wg_ref, wna_ref, wswa_ref, wout_ref, w1_ref, w2_ref, stage_ref, sem):
    @pl.when(pl.program_id(0) == 0)
    def _():
        cols = lambda c: pl.ds(c * W_CHUNK, W_CHUNK)
        pieces = [(win_hbm.at[0, :, pl.ds(_C_G + c * W_CHUNK, W_CHUNK)], wg_ref.at[:, cols(c)])
                  for c in range(2 * D_MODEL // W_CHUNK)]
        pieces += [(wna_hbm.at[0], wna_ref), (wswa_hbm.at[0], wswa_ref),
                   (wout_hbm.at[0], wout_ref)]
        pieces += [(w1_hbm.at[0, :, cols(c)], w1_ref.at[:, cols(c)])
                   for c in range(D_FF // W_CHUNK)]
        pieces += [(w2_hbm.at[0, cols(c), :], w2_ref.at[cols(c), :])
                   for c in range(D_FF // W_CHUNK)]
        _stage_weights_bf16(pieces, stage_ref, sem)

    h1 = []
    for r0 in range(0, x_ref.shape[0], SUB_TILE_ROWS):
        rows = slice(r0, r0 + SUB_TILE_ROWS)
        h = _layer_norm(x_ref[rows, :], g0_ref[...], b0_ref[...])
        gates = (jnp.dot(h.astype(jnp.bfloat16), wg_ref[...],
                         preferred_element_type=jnp.float32) + bin_ref[:, _C_G:_C_END])
        y_na = jnp.dot(na_ref[rows, :], wna_ref[...], preferred_element_type=jnp.float32)
        y_swa = jnp.dot(swa_ref[rows, :], wswa_ref[...], preferred_element_type=jnp.float32)
        mixed = (jax.nn.sigmoid(gates[:, :D_MODEL]) * y_na
                 + jax.nn.sigmoid(gates[:, D_MODEL:]) * y_swa)
        attn = jnp.dot(mixed.astype(jnp.bfloat16), wout_ref[...],
                       preferred_element_type=jnp.float32) + bout_ref[...]
        h1.append(_layer_norm(DEEPNORM_ALPHA * h + attn, g1_ref[...], b1_ref[...]))

    h = jnp.concatenate(h1, axis=0)
    hb = h.astype(jnp.bfloat16)
    acc = jnp.zeros(h.shape, jnp.float32)
    n_chunks = D_FF // FF_CHUNK
    for c in range(n_chunks):
        sl = slice(c * FF_CHUNK, (c + 1) * FF_CHUNK)
        if c == 0:
            u = jnp.concatenate(
                [jnp.dot(hs.astype(jnp.bfloat16), w1_ref[:, sl],
                         preferred_element_type=jnp.float32) for hs in h1], axis=0)
        else:
            u = jnp.dot(hb, w1_ref[:, sl], preferred_element_type=jnp.float32)
        u = jnp.square(jnp.maximum(u + bf1_ref[:, sl], 0.0)).astype(jnp.bfloat16)
        if c + 1 < n_chunks:
            acc = acc + jnp.dot(u, w2_ref[sl, :], preferred_element_type=jnp.float32)
        else:
            for r0 in range(0, h.shape[0], SUB_TILE_ROWS):
                rows = slice(r0, r0 + SUB_TILE_ROWS)
                z = acc[rows] + jnp.dot(u[rows], w2_ref[sl, :],
                                        preferred_element_type=jnp.float32)
                o_ref[rows, :] = _layer_norm(DEEPNORM_ALPHA * h[rows] + z + bf2_ref[...],
                                             g2_ref[...], b2_ref[...])


def _tail(x2, g0, b0, na, swa, b_in, bout, g1, b1, bf1, bf2, g2, b2,
          w_in, w_na, w_swa, w_out, w_ff1, w_ff2):
    t = x2.shape[0]
    tm = TAIL_TILE_ROWS
    tok = lambda i: (i, 0)
    vec = _const_spec((1, D_MODEL))
    hbm = pl.BlockSpec(memory_space=pl.ANY)
    bf = jnp.bfloat16
    weight_elems = (D_MODEL * 2 * D_MODEL + (NA_WIDTH + SWA_Q_WIDTH) * D_MODEL
                    + D_MODEL * D_MODEL + 2 * D_MODEL * D_FF)
    resident = (weight_elems * _BF16_BYTES + 2 * W_CHUNK * D_MODEL * _F32_BYTES
                + 2 * 2 * tm * D_MODEL * _F32_BYTES
                + 2 * tm * (NA_WIDTH + SWA_Q_WIDTH) * _BF16_BYTES)
    return pl.pallas_call(
        _tail_kernel,
        grid=(t // tm,),
        in_specs=[
            pl.BlockSpec((tm, D_MODEL), tok), vec, vec,
            pl.BlockSpec((tm, NA_WIDTH), tok), pl.BlockSpec((tm, SWA_Q_WIDTH), tok),
            _const_spec((1, _C_END)), vec, vec, vec,
            _const_spec((1, D_FF)), vec, vec, vec,
            hbm, hbm, hbm, hbm, hbm, hbm,
        ],
        out_specs=pl.BlockSpec((tm, D_MODEL), tok),
        out_shape=jax.ShapeDtypeStruct((t, D_MODEL), jnp.float32),
        scratch_shapes=[
            pltpu.VMEM((D_MODEL, 2 * D_MODEL), bf),
            pltpu.VMEM((NA_WIDTH, D_MODEL), bf), pltpu.VMEM((SWA_Q_WIDTH, D_MODEL), bf),
            pltpu.VMEM((D_MODEL, D_MODEL), bf),
            pltpu.VMEM((D_MODEL, D_FF), bf), pltpu.VMEM((D_FF, D_MODEL), bf),
            pltpu.VMEM((2, W_CHUNK, D_MODEL), jnp.float32),
            pltpu.SemaphoreType.DMA((2,)),
        ],
        compiler_params=pltpu.CompilerParams(
            dimension_semantics=("arbitrary",), vmem_limit_bytes=_vmem_limit(resident)),
        name="mix_ffn",
    )(x2, g0, b0, na, swa, b_in, bout, g1, b1, bf1, bf2, g2, b2,
      w_in, w_na, w_swa, w_out, w_ff1, w_ff2)


def kernel(x, ln0_g, ln0_b, w_in, b_in, na_rpb, swa_sink, w_branch_na, w_branch_swa,
           w_out, b_out, ln1_g, ln1_b, w_ff1, b_ff1, w_ff2, b_ff2, ln2_g, ln2_b):
    batch, seq, d = x.shape
    assert d == D_MODEL and w_in.shape == (DEPTH, D_MODEL, _C_END) and DEPTH == 1
    assert seq % INPROJ_TILE_ROWS == 0 and (batch * seq) % TAIL_TILE_ROWS == 0
    assert seq % (NA_ROWS_PER_ITER * GRID_W) == 0 and seq // GRID_W >= NA_KH
    assert seq % (SWA_BLOCKS_PER_ITER * SWA_BLOCK) == 0 and seq >= 3 * SWA_BLOCK
    row = lambda v: v.reshape(1, -1)
    x2 = x.reshape(batch * seq, d)
    b_in_row = row(b_in[0])

    cos_t, sa_t, sb_t = _rotary_tables(seq)
    qa, ka, va, qb, kb, vb = _inproj(
        x2, row(ln0_g), row(ln0_b), b_in_row, cos_t, sa_t, sb_t, w_in, batch, seq)

    three = lambda a: a.reshape(batch, seq, a.shape[-1])
    o_na, o_swa = _attention(three(qa), three(ka), three(va), _na_bias_table(na_rpb[0]),
                             three(qb), three(kb), three(vb), swa_sink[0])

    return _tail(x2, row(ln0_g), row(ln0_b), o_na.reshape(batch * seq, -1),
                 o_swa.reshape(batch * seq, -1), b_in_row, row(b_out[0]),
                 row(ln1_g[0]), row(ln1_b[0]), row(b_ff1[0]), row(b_ff2[0]),
                 row(ln2_g[0]), row(ln2_b[0]),
                 w_in, w_branch_na, w_branch_swa, w_out, w_ff1, w_ff2).reshape(batch, seq, d)
```

```python
import functools

import numpy as np
import jax
import jax.numpy as jnp
from jax import lax
from jax.experimental import pallas as pl
from jax.experimental.pallas import tpu as pltpu

D_MODEL = 1024
GRID_W = 64
HEAD_DIM = 64
NA_HEADS = 8
NA_KH = 8
NA_KW = 16
NA_WIDTH = NA_HEADS * HEAD_DIM
SWA_Q_HEADS = 8
SWA_KV_HEADS = 2
SWA_WINDOW = 128
SWA_BLOCK = 128
SWA_Q_WIDTH = SWA_Q_HEADS * HEAD_DIM
SWA_KV_WIDTH = SWA_KV_HEADS * HEAD_DIM
ROPE_THETA = 500000.0
ROPE_DIM = HEAD_DIM // 4
D_FF = 4 * D_MODEL
DEPTH = 1
DEEPNORM_ALPHA = (2.0 * DEPTH) ** 0.25
LN_EPS = 1e-5
MASK_VALUE = -1e30
ATTN_SCALE = HEAD_DIM ** -0.5

LANES = 128
SUBLANES = 8
HEADS_PER_VREG = LANES // HEAD_DIM
VMEM_BYTES = 64 * 1024 * 1024
VMEM_TEMP_BYTES = 12 * 1024 * 1024
VMEM_LIMIT = 56 * 1024 * 1024

INPROJ_TILE_ROWS = 1024
TAIL_TILE_ROWS = 512
SUB_TILE_ROWS = 256
FF_CHUNK = 1024
NA_ROWS_PER_ITER = 16
SWA_BLOCKS_PER_ITER = 8

_C_QA = 0
_C_KA = _C_QA + NA_WIDTH
_C_VA = _C_KA + NA_WIDTH
_C_QB = _C_VA + NA_WIDTH
_C_KB = _C_QB + SWA_Q_WIDTH
_C_VB = _C_KB + SWA_KV_WIDTH
_C_G = _C_VB + SWA_KV_WIDTH
_C_END = _C_G + 2 * D_MODEL
_KV_DUP = SWA_KV_HEADS * LANES

_BF16_BYTES = 2
_F32_BYTES = 4


def _vmem_limit(resident_bytes):
    assert resident_bytes + VMEM_TEMP_BYTES <= VMEM_LIMIT < VMEM_BYTES, resident_bytes
    return VMEM_LIMIT


def _const_spec(shape):
    nd = len(shape)
    return pl.BlockSpec(shape, lambda *_: (0,) * nd, pipeline_mode=pl.Buffered(1))


def _layer_norm(x, g, b):
    mu = jnp.mean(x, axis=-1, keepdims=True)
    xc = x - mu
    var = jnp.mean(xc * xc, axis=-1, keepdims=True)
    return xc * lax.rsqrt(var + LN_EPS) * g + b


def _stage_weights_bf16(pieces, stage_ref, sem):
    def copy(k):
        src = pieces[k][0]
        dst = stage_ref.at[k % 2, pl.ds(0, src.shape[0]), :]
        return pltpu.make_async_copy(src, dst, sem.at[k % 2])

    copy(0).start()
    for k, (src, dst) in enumerate(pieces):
        if k + 1 < len(pieces):
            copy(k + 1).start()
        copy(k).wait()
        dst[...] = stage_ref[k % 2, 0:src.shape[0], :].astype(dst.dtype)


def _dup_heads(y):
    lane = lax.broadcasted_iota(jnp.int32, y.shape, 1)
    swapped = pltpu.roll(y, HEAD_DIM, 1)
    return jnp.concatenate([jnp.where(lane < HEAD_DIM, y, swapped),
                            jnp.where(lane >= HEAD_DIM, y, swapped)], axis=1)


QKV_CHUNK = _C_G // 3


def _inproj_kernel(x_ref, g_ref, b_ref, bias_ref, cos_ref, sa_ref, sb_ref, win_hbm,
                   qa_ref, ka_ref, va_ref, qb_ref, kb_ref, vb_ref,
                   w_ref, stage_ref, sem):
    half = ROPE_DIM // 2

    @pl.when((pl.program_id(0) == 0) & (pl.program_id(1) == 0))
    def _():
        cols = lambda c: pl.ds(c * QKV_CHUNK, QKV_CHUNK)
        _stage_weights_bf16([(win_hbm.at[0, :, cols(c)], w_ref.at[:, cols(c)])
                             for c in range(_C_G // QKV_CHUNK)], stage_ref, sem)

    for r0 in range(0, x_ref.shape[0], SUB_TILE_ROWS):
        rows = slice(r0, r0 + SUB_TILE_ROWS)
        h = _layer_norm(x_ref[rows, :], g_ref[...], b_ref[...])
        hb = h.astype(jnp.bfloat16)

        def proj(c0, c1):
            return (jnp.dot(hb, w_ref[:, c0:c1], preferred_element_type=jnp.float32)
                    + bias_ref[:, c0:c1])

        def rotary(y):
            cos, sa, sb = cos_ref[rows, :], sa_ref[rows, :], sb_ref[rows, :]
            outs = []
            for c in range(y.shape[1] // LANES):
                yc = y[:, c * LANES:(c + 1) * LANES]
                outs.append(yc * cos
                            + pltpu.roll(yc, half, 1) * sa
                            + pltpu.roll(yc, LANES - half, 1) * sb)
            return outs[0] if len(outs) == 1 else jnp.concatenate(outs, axis=1)

        qa_ref[rows, :] = (proj(_C_QA, _C_KA) * ATTN_SCALE).astype(qa_ref.dtype)
        ka_ref[rows, :] = proj(_C_KA, _C_VA).astype(ka_ref.dtype)
        va_ref[rows, :] = proj(_C_VA, _C_QB).astype(va_ref.dtype)
        qb_ref[rows, :] = (rotary(proj(_C_QB, _C_KB)) * ATTN_SCALE).astype(qb_ref.dtype)
        kv = proj(_C_KB, _C_G)
        kb_ref[rows, :] = _dup_heads(rotary(kv[:, :SWA_KV_WIDTH])).astype(kb_ref.dtype)
        vb_ref[rows, :] = _dup_heads(kv[:, SWA_KV_WIDTH:]).astype(vb_ref.dtype)


def _inproj(x2, g, b, bias, cos_t, sa_t, sb_t, w_in, batch, seq):
    t = batch * seq
    tm = INPROJ_TILE_ROWS
    nblk = seq // tm
    tok = lambda bi, i: (bi * nblk + i, 0)
    pos = lambda bi, i: (i, 0)
    bf = jnp.bfloat16
    out_widths = (NA_WIDTH, NA_WIDTH, NA_WIDTH, SWA_Q_WIDTH, _KV_DUP, _KV_DUP)
    resident = (2 * tm * D_MODEL * _F32_BYTES + 2 * tm * sum(out_widths) * _BF16_BYTES
                + 2 * 3 * tm * LANES * _F32_BYTES + D_MODEL * _C_G * _BF16_BYTES
                + 2 * D_MODEL * QKV_CHUNK * _F32_BYTES)
    return pl.pallas_call(
        _inproj_kernel,
        grid=(batch, nblk),
        in_specs=[
            pl.BlockSpec((tm, D_MODEL), tok),
            _const_spec((1, D_MODEL)), _const_spec((1, D_MODEL)),
            _const_spec((1, _C_END)),
            pl.BlockSpec((tm, LANES), pos), pl.BlockSpec((tm, LANES), pos),
            pl.BlockSpec((tm, LANES), pos),
            pl.BlockSpec(memory_space=pl.ANY),
        ],
        out_specs=tuple(pl.BlockSpec((tm, w), tok) for w in out_widths),
        out_shape=tuple(jax.ShapeDtypeStruct((t, w), bf) for w in out_widths),
        scratch_shapes=[
            pltpu.VMEM((D_MODEL, _C_G), bf),
            pltpu.VMEM((2, D_MODEL, QKV_CHUNK), jnp.float32),
            pltpu.SemaphoreType.DMA((2,)),
        ],
        compiler_params=pltpu.CompilerParams(
            dimension_semantics=("arbitrary", "arbitrary"),
            vmem_limit_bytes=_vmem_limit(resident)),
        name="ln_inproj_rotary",
    )(x2, g, b, bias, cos_t, sa_t, sb_t, w_in)


def _rotary_tables(seq):
    half = ROPE_DIM // 2
    inv_freq = np.power(ROPE_THETA, -np.arange(0, ROPE_DIM, 2, dtype=np.float64) / ROPE_DIM)
    ang = np.arange(seq, dtype=np.float64)[:, None] * inv_freq[None, :]
    cos, sin = np.cos(ang), np.sin(ang)
    ones = np.ones((seq, HEAD_DIM - ROPE_DIM))
    zeros = np.zeros((seq, HEAD_DIM - ROPE_DIM))
    zh = np.zeros((seq, half))
    cos_h = np.concatenate([cos, cos, ones], axis=1)
    sa_h = np.concatenate([zh, sin, zeros], axis=1)
    sb_h = np.concatenate([-sin, zh, zeros], axis=1)
    tile = lambda a: jnp.asarray(np.tile(a, (1, HEADS_PER_VREG)), dtype=jnp.float32)
    return tile(cos_h), tile(sa_h), tile(sb_h)


def _stack_heads(q):
    lane = lax.broadcasted_iota(jnp.int32, q.shape, 1)
    zero = jnp.zeros_like(q)
    return jnp.concatenate([jnp.where(lane < HEAD_DIM, q, zero),
                            jnp.where(lane >= HEAD_DIM, q, zero)], axis=0)


def _unstack_heads(o2):
    m = o2.shape[0] // 2
    lane = lax.broadcasted_iota(jnp.int32, (m, LANES), 1)
    return jnp.where(lane < HEAD_DIM, o2[:m], o2[m:])


_NT = (((1,), (1,)), ((), ()))


def _na_bias_kernel(rpb_ref, o_ref):
    w = lax.broadcasted_iota(jnp.int32, (GRID_W, LANES), 0)
    lane = lax.broadcasted_iota(jnp.int32, (GRID_W, LANES), 1)
    c = lane % GRID_W
    col_start = jnp.clip(w - NA_KW // 2, 0, GRID_W - NA_KW)
    inside = (c >= col_start) & (c < col_start + NA_KW)
    lower = lane < GRID_W
    n_dr = 2 * NA_KH - 1
    for e in range(HEADS_PER_VREG):
        lo, hi = [], []
        for dr in range(n_dr):
            x = jnp.broadcast_to(rpb_ref[0, e, dr:dr + 1, :], (GRID_W, LANES))
            lo.append(pltpu.roll(x, LANES - (NA_KW - 1), 1, stride=1, stride_axis=0))
            hi.append(pltpu.roll(x, GRID_W - (NA_KW - 1), 1, stride=1, stride_axis=0))
        for v in range(NA_KH):
            for kk in range(NA_KH // HEADS_PER_VREG):
                dr_a = 2 * kk - v + NA_KH - 1
                tile = jnp.where(inside, jnp.where(lower, lo[dr_a], hi[dr_a + 1]), MASK_VALUE)
                o_ref[v, 0, e * GRID_W:(e + 1) * GRID_W, kk * LANES:(kk + 1) * LANES] = tile


def _na_bias_table(rpb):
    n_pairs = NA_HEADS // HEADS_PER_VREG
    n_dr, n_dc = rpb.shape[1], rpb.shape[2]
    dr_pad = pl.cdiv(n_dr, SUBLANES) * SUBLANES
    rp = jnp.pad(rpb, ((0, 0), (0, dr_pad - n_dr), (0, LANES - n_dc)))
    rp = rp.reshape(n_pairs, HEADS_PER_VREG, dr_pad, LANES)
    m2, win = HEADS_PER_VREG * GRID_W, NA_KH * GRID_W
    return pl.pallas_call(
        _na_bias_kernel,
        grid=(n_pairs,),
        in_specs=[pl.BlockSpec((1, HEADS_PER_VREG, dr_pad, LANES), lambda j: (j, 0, 0, 0))],
        out_specs=pl.BlockSpec((NA_KH, 1, m2, win), lambda j: (0, j, 0, 0)),
        out_shape=jax.ShapeDtypeStruct((NA_KH, n_pairs, m2, win), jnp.float32),
        compiler_params=pltpu.CompilerParams(dimension_semantics=("parallel",)),
        name="na_bias_table",
    )(rp)


def _na_kernel(q_ref, k_ref, v_ref, tbl_ref, o_ref, s_scr, *, n_rows):
    win = NA_KH * GRID_W
    ones = jnp.ones((win, LANES), jnp.bfloat16)

    def body(i, carry):
        def window(t):
            r = i * NA_ROWS_PER_ITER + t
            rs = jnp.clip(r - NA_KH // 2, 0, n_rows - NA_KH)
            return r - rs, pl.multiple_of(rs * GRID_W, GRID_W)

        def q_rows(t):
            return pl.ds(pl.multiple_of((i * NA_ROWS_PER_ITER + t) * GRID_W, GRID_W), GRID_W)

        for t in range(NA_ROWS_PER_ITER):
            variant, start = window(t)
            lhs = _stack_heads(q_ref[0, q_rows(t), :])
            kw = k_ref[0, pl.ds(start, win), :]
            s = lax.dot_general(lhs, kw, _NT, preferred_element_type=jnp.float32)
            s_scr[t] = s + tbl_ref[variant, 0]
        for t in range(NA_ROWS_PER_ITER):
            s = s_scr[t]
            m = jnp.max(s, axis=-1, keepdims=True)
            p = jnp.exp(s - m)
            _, start = window(t)
            vw = jnp.concatenate([v_ref[0, pl.ds(start, win), :], ones], axis=1)
            o2 = jnp.dot(p.astype(jnp.bfloat16), vw, preferred_element_type=jnp.float32)
            o2 = o2[:, :LANES] / o2[:, LANES:]
            o_ref[0, q_rows(t), :] = _unstack_heads(o2).astype(o_ref.dtype)
        return carry

    lax.fori_loop(0, n_rows // NA_ROWS_PER_ITER, body, 0)


def _swa_mask_table():
    qi = np.arange(SWA_BLOCK)[:, None]
    kc = np.arange(3 * SWA_BLOCK)[None, :]
    tbl = [np.where(np.abs(kc - d * SWA_BLOCK - qi) <= SWA_WINDOW, 0.0, MASK_VALUE)
           for d in range(3)]
    return jnp.asarray(np.stack(tbl), dtype=jnp.float32)


def _swa_kernel(sink_ref, q_ref, k_ref, v_ref, mask_ref, o_ref, s_scr, *, seq):
    j = pl.program_id(1)
    blk = SWA_BLOCK
    win = 3 * blk
    row = lax.broadcasted_iota(jnp.int32, (2 * blk, 1), 0)
    sink_col = jnp.where(row < blk, sink_ref[2 * j], sink_ref[2 * j + 1])
    ones = jnp.ones((win, LANES), jnp.bfloat16)

    def body(i, carry):
        def window(u):
            n = i * SWA_BLOCKS_PER_ITER + u
            start = jnp.clip((n - 1) * blk, 0, seq - win)
            return n - start // blk, pl.multiple_of(start, blk)

        def q_rows(u):
            return pl.ds(pl.multiple_of((i * SWA_BLOCKS_PER_ITER + u) * blk, blk), blk)

        for u in range(SWA_BLOCKS_PER_ITER):
            variant, start = window(u)
            lhs = _stack_heads(q_ref[0, q_rows(u), :])
            kw = k_ref[0, pl.ds(start, win), :]
            s = lax.dot_general(lhs, kw, _NT, preferred_element_type=jnp.float32)
            mask = mask_ref[variant]
            s_scr[u] = s + jnp.concatenate([mask, mask], axis=0)
        for u in range(SWA_BLOCKS_PER_ITER):
            s = s_scr[u]
            m = jnp.maximum(jnp.max(s, axis=-1, keepdims=True), sink_col)
            e = jnp.exp(s - m)
            _, start = window(u)
            vw = jnp.concatenate([v_ref[0, pl.ds(start, win), :], ones], axis=1)
            o2 = jnp.dot(e.astype(jnp.bfloat16), vw, preferred_element_type=jnp.float32)
            o2 = o2[:, :LANES] / (o2[:, LANES:] + jnp.exp(sink_col - m))
            o_ref[0, q_rows(u), :] = _unstack_heads(o2).astype(o_ref.dtype)
        return carry

    lax.fori_loop(0, seq // (SWA_BLOCKS_PER_ITER * blk), body, 0)


def _attention_kernel(sink_ref, qa_ref, ka_ref, va_ref, tbl_ref, qb_ref, kb_ref, vb_ref,
                      mask_ref, ona_ref, oswa_ref, na_scr, swa_scr, *, n_rows, seq):
    _na_kernel(qa_ref, ka_ref, va_ref, tbl_ref, ona_ref, na_scr, n_rows=n_rows)
    _swa_kernel(sink_ref, qb_ref, kb_ref, vb_ref, mask_ref, oswa_ref, swa_scr, seq=seq)


def _attention(qa, ka, va, tbl, qb, kb, vb, sink):
    batch, seq, _ = qa.shape
    n_rows = seq // GRID_W
    n_pairs = NA_HEADS // HEADS_PER_VREG
    assert n_pairs == SWA_Q_HEADS // HEADS_PER_VREG
    pairs_per_group = n_pairs // SWA_KV_HEADS
    na_m, na_win = HEADS_PER_VREG * GRID_W, NA_KH * GRID_W
    swa_m, swa_win = HEADS_PER_VREG * SWA_BLOCK, 3 * SWA_BLOCK
    pair_block = pl.BlockSpec((1, seq, LANES), lambda b, j: (b, 0, j))
    kv_block = pl.BlockSpec((1, seq, LANES), lambda b, j: (b, 0, j // pairs_per_group))
    resident = (2 * 8 * seq * LANES * _BF16_BYTES + 2 * NA_KH * na_m * na_win * _F32_BYTES
                + 3 * SWA_BLOCK * swa_win * _F32_BYTES
                + NA_ROWS_PER_ITER * na_m * na_win * _F32_BYTES
                + SWA_BLOCKS_PER_ITER * swa_m * swa_win * _F32_BYTES)
    out = jax.ShapeDtypeStruct(qa.shape, jnp.bfloat16)
    return pl.pallas_call(
        functools.partial(_attention_kernel, n_rows=n_rows, seq=seq),
        grid=(batch, n_pairs),
        in_specs=[pl.BlockSpec(memory_space=pltpu.SMEM),
                  pair_block, pair_block, pair_block,
                  pl.BlockSpec((NA_KH, 1, na_m, na_win), lambda b, j: (0, j, 0, 0)),
                  pair_block, kv_block, kv_block,
                  _const_spec((3, SWA_BLOCK, swa_win))],
        out_specs=(pair_block, pair_block),
        out_shape=(out, out),
        scratch_shapes=[pltpu.VMEM((NA_ROWS_PER_ITER, na_m, na_win), jnp.float32),
                        pltpu.VMEM((SWA_BLOCKS_PER_ITER, swa_m, swa_win), jnp.float32)],
        compiler_params=pltpu.CompilerParams(
            dimension_semantics=("parallel", "parallel"),
            vmem_limit_bytes=_vmem_limit(resident)),
        name="attention",
    )(sink, qa, ka, va, tbl, qb, kb, vb, _swa_mask_table())


W_CHUNK = D_MODEL


def _tail_kernel(x_ref, g0_ref, b0_ref, na_ref, swa_ref, bin_ref, bout_ref, g1_ref, b1_ref,
                 bf1_ref, bf2_ref, g2_ref, b2_ref,
                 win_hbm, wna_hbm, wswa_hbm, wout_hbm, w1_hbm, w2_hbm,
                 o_ref,
                 wg_ref, wna_ref, wswa_ref, wout_ref, w1_ref, w2_ref, stage_ref, sem):
    assert FF_CHUNK == W_CHUNK
    cols = lambda c: pl.ds(c * W_CHUNK, W_CHUNK)
    w1_piece = lambda c: (w1_hbm.at[0, :, cols(c)], w1_ref.at[:, cols(c)])
    w2_piece = lambda c: (w2_hbm.at[0, cols(c), :], w2_ref.at[cols(c), :])

    def fetch(piece, slot):
        return pltpu.make_async_copy(piece[0], stage_ref.at[slot], sem.at[slot])

    def land(piece, slot, next_piece):
        fetch(piece, slot).wait()
        piece[1][...] = stage_ref[slot].astype(piece[1].dtype)
        if next_piece is not None:
            fetch(next_piece, slot).start()

    def tile(first_step):
        if first_step:
            pieces = [(win_hbm.at[0, :, pl.ds(_C_G + c * W_CHUNK, W_CHUNK)],
                       wg_ref.at[:, cols(c)]) for c in range(2 * D_MODEL // W_CHUNK)]
            pieces += [(wna_hbm.at[0], wna_ref), (wswa_hbm.at[0], wswa_ref),
                       (wout_hbm.at[0], wout_ref)]
            _stage_weights_bf16(pieces, stage_ref, sem)
            fetch(w1_piece(0), 0).start()
            fetch(w2_piece(0), 1).start()
        _tile_body(x_ref, g0_ref, b0_ref, na_ref, swa_ref, bin_ref, bout_ref, g1_ref, b1_ref,
                   bf1_ref, bf2_ref, g2_ref, b2_ref, o_ref,
                   wg_ref, wna_ref, wswa_ref, wout_ref, w1_ref, w2_ref,
                   before_chunk=(lambda c: stage_chunk(c)) if first_step else None)

    def stage_chunk(c):
        last = c + 1 == D_FF // FF_CHUNK
        land(w1_piece(c), 0, None if last else w1_piece(c + 1))
        land(w2_piece(c), 1, None if last else w2_piece(c + 1))

    pl.when(pl.program_id(0) == 0)(lambda: tile(first_step=True))
    pl.when(pl.program_id(0) != 0)(lambda: tile(first_step=False))


def _tile_body(x_ref, g0_ref, b0_ref, na_ref, swa_ref, bin_ref, bout_ref, g1_ref, b1_ref,
               bf1_ref, bf2_ref, g2_ref, b2_ref, o_ref,
               wg_ref, wna_ref, wswa_ref, wout_ref, w1_ref, w2_ref, before_chunk):
    h1 = []
    for r0 in range(0, x_ref.shape[0], SUB_TILE_ROWS):
        rows = slice(r0, r0 + SUB_TILE_ROWS)
        h = _layer_norm(x_ref[rows, :], g0_ref[...], b0_ref[...])
        gates = (jnp.dot(h.astype(jnp.bfloat16), wg_ref[...],
                         preferred_element_type=jnp.float32) + bin_ref[:, _C_G:_C_END])
        y_na = jnp.dot(na_ref[rows, :], wna_ref[...], preferred_element_type=jnp.float32)
        y_swa = jnp.dot(swa_ref[rows, :], wswa_ref[...], preferred_element_type=jnp.float32)
        mixed = (jax.nn.sigmoid(gates[:, :D_MODEL]) * y_na
                 + jax.nn.sigmoid(gates[:, D_MODEL:]) * y_swa)
        attn = jnp.dot(mixed.astype(jnp.bfloat16), wout_ref[...],
                       preferred_element_type=jnp.float32) + bout_ref[...]
        h1.append(_layer_norm(DEEPNORM_ALPHA * h + attn, g1_ref[...], b1_ref[...]))

    h = jnp.concatenate(h1, axis=0)
    hb = h.astype(jnp.bfloat16)
    acc = jnp.zeros(h.shape, jnp.float32)
    n_chunks = D_FF // FF_CHUNK
    for c in range(n_chunks):
        sl = slice(c * FF_CHUNK, (c + 1) * FF_CHUNK)
        if before_chunk is not None:
            before_chunk(c)
        if c == 0:
            u = jnp.concatenate(
                [jnp.dot(hs.astype(jnp.bfloat16), w1_ref[:, sl],
                         preferred_element_type=jnp.float32) for hs in h1], axis=0)
        else:
            u = jnp.dot(hb, w1_ref[:, sl], preferred_element_type=jnp.float32)
        u = jnp.square(jnp.maximum(u + bf1_ref[:, sl], 0.0)).astype(jnp.bfloat16)
        if c + 1 < n_chunks:
            acc = acc + jnp.dot(u, w2_ref[sl, :], preferred_element_type=jnp.float32)
        else:
            for r0 in range(0, h.shape[0], SUB_TILE_ROWS):
                rows = slice(r0, r0 + SUB_TILE_ROWS)
                z = acc[rows] + jnp.dot(u[rows], w2_ref[sl, :],
                                        preferred_element_type=jnp.float32)
                o_ref[rows, :] = _layer_norm(DEEPNORM_ALPHA * h[rows] + z + bf2_ref[...],
                                             g2_ref[...], b2_ref[...])


def _tail(x2, g0, b0, na, swa, b_in, bout, g1, b1, bf1, bf2, g2, b2,
          w_in, w_na, w_swa, w_out, w_ff1, w_ff2):
    t = x2.shape[0]
    tm = TAIL_TILE_ROWS
    tok = lambda i: (i, 0)
    vec = _const_spec((1, D_MODEL))
    hbm = pl.BlockSpec(memory_space=pl.ANY)
    bf = jnp.bfloat16
    weight_elems = (D_MODEL * 2 * D_MODEL + (NA_WIDTH + SWA_Q_WIDTH) * D_MODEL
                    + D_MODEL * D_MODEL + 2 * D_MODEL * D_FF)
    resident = (weight_elems * _BF16_BYTES + 2 * W_CHUNK * D_MODEL * _F32_BYTES
                + 2 * 2 * tm * D_MODEL * _F32_BYTES
                + 2 * tm * (NA_WIDTH + SWA_Q_WIDTH) * _BF16_BYTES)
    return pl.pallas_call(
        _tail_kernel,
        grid=(t // tm,),
        in_specs=[
            pl.BlockSpec((tm, D_MODEL), tok), vec, vec,
            pl.BlockSpec((tm, NA_WIDTH), tok), pl.BlockSpec((tm, SWA_Q_WIDTH), tok),
            _const_spec((1, _C_END)), vec, vec, vec,
            _const_spec((1, D_FF)), vec, vec, vec,
            hbm, hbm, hbm, hbm, hbm, hbm,
        ],
        out_specs=pl.BlockSpec((tm, D_MODEL), tok),
        out_shape=jax.ShapeDtypeStruct((t, D_MODEL), jnp.float32),
        scratch_shapes=[
            pltpu.VMEM((D_MODEL, 2 * D_MODEL), bf),
            pltpu.VMEM((NA_WIDTH, D_MODEL), bf), pltpu.VMEM((SWA_Q_WIDTH, D_MODEL), bf),
            pltpu.VMEM((D_MODEL, D_MODEL), bf),
            pltpu.VMEM((D_MODEL, D_FF), bf), pltpu.VMEM((D_FF, D_MODEL), bf),
            pltpu.VMEM((2, W_CHUNK, D_MODEL), jnp.float32),
            pltpu.SemaphoreType.DMA((2,)),
        ],
        compiler_params=pltpu.CompilerParams(
            dimension_semantics=("arbitrary",), vmem_limit_bytes=_vmem_limit(resident)),
        name="mix_ffn",
    )(x2, g0, b0, na, swa, b_in, bout, g1, b1, bf1, bf2, g2, b2,
      w_in, w_na, w_swa, w_out, w_ff1, w_ff2)


def kernel(x, ln0_g, ln0_b, w_in, b_in, na_rpb, swa_sink, w_branch_na, w_branch_swa,
           w_out, b_out, ln1_g, ln1_b, w_ff1, b_ff1, w_ff2, b_ff2, ln2_g, ln2_b):
    batch, seq, d = x.shape
    assert d == D_MODEL and w_in.shape == (DEPTH, D_MODEL, _C_END) and DEPTH == 1
    assert seq % INPROJ_TILE_ROWS == 0 and (batch * seq) % TAIL_TILE_ROWS == 0
    assert seq % (NA_ROWS_PER_ITER * GRID_W) == 0 and seq // GRID_W >= NA_KH
    assert seq % (SWA_BLOCKS_PER_ITER * SWA_BLOCK) == 0 and seq >= 3 * SWA_BLOCK
    row = lambda v: v.reshape(1, -1)
    x2 = x.reshape(batch * seq, d)
    b_in_row = row(b_in[0])

    cos_t, sa_t, sb_t = _rotary_tables(seq)
    qa, ka, va, qb, kb, vb = _inproj(
        x2, row(ln0_g), row(ln0_b), b_in_row, cos_t, sa_t, sb_t, w_in, batch, seq)

    three = lambda a: a.reshape(batch, seq, a.shape[-1])
    o_na, o_swa = _attention(three(qa), three(ka), three(va), _na_bias_table(na_rpb[0]),
                             three(qb), three(kb), three(vb), swa_sink[0])

    return _tail(x2, row(ln0_g), row(ln0_b), o_na.reshape(batch * seq, -1),
                 o_swa.reshape(batch * seq, -1), b_in_row, row(b_out[0]),
                 row(ln1_g[0]), row(ln1_b[0]), row(b_ff1[0]), row(b_ff2[0]),
                 row(ln2_g[0]), row(ln2_b[0]),
                 w_in, w_branch_na, w_branch_swa, w_out, w_ff1, w_ff2).reshape(batch, seq, d)
```

```python
import functools

import numpy as np
import jax
import jax.numpy as jnp
from jax import lax
from jax.experimental import pallas as pl
from jax.experimental.pallas import tpu as pltpu

D_MODEL = 1024
GRID_W = 64
HEAD_DIM = 64
NA_HEADS = 8
NA_KH = 8
NA_KW = 16
NA_WIDTH = NA_HEADS * HEAD_DIM
SWA_Q_HEADS = 8
SWA_KV_HEADS = 2
SWA_WINDOW = 128
SWA_BLOCK = 128
SWA_Q_WIDTH = SWA_Q_HEADS * HEAD_DIM
SWA_KV_WIDTH = SWA_KV_HEADS * HEAD_DIM
ROPE_THETA = 500000.0
ROPE_DIM = HEAD_DIM // 4
D_FF = 4 * D_MODEL
DEPTH = 1
DEEPNORM_ALPHA = (2.0 * DEPTH) ** 0.25
LN_EPS = 1e-5
MASK_VALUE = -1e30
ATTN_SCALE = HEAD_DIM ** -0.5

LANES = 128
SUBLANES = 8
HEADS_PER_VREG = LANES // HEAD_DIM
VMEM_BYTES = 64 * 1024 * 1024
VMEM_TEMP_BYTES = 12 * 1024 * 1024
VMEM_LIMIT = 56 * 1024 * 1024

INPROJ_TILE_ROWS = 1024
TAIL_TILE_ROWS = 512
SUB_TILE_ROWS = 256
FF_CHUNK = 1024
NA_ROWS_PER_ITER = 16
SWA_BLOCKS_PER_ITER = 8

_C_QA = 0
_C_KA = _C_QA + NA_WIDTH
_C_VA = _C_KA + NA_WIDTH
_C_QB = _C_VA + NA_WIDTH
_C_KB = _C_QB + SWA_Q_WIDTH
_C_VB = _C_KB + SWA_KV_WIDTH
_C_G = _C_VB + SWA_KV_WIDTH
_C_END = _C_G + 2 * D_MODEL
_KV_DUP = SWA_KV_HEADS * LANES

_BF16_BYTES = 2
_F32_BYTES = 4


def _vmem_limit(resident_bytes):
    assert resident_bytes + VMEM_TEMP_BYTES <= VMEM_LIMIT < VMEM_BYTES, resident_bytes
    return VMEM_LIMIT


def _const_spec(shape):
    nd = len(shape)
    return pl.BlockSpec(shape, lambda *_: (0,) * nd, pipeline_mode=pl.Buffered(1))


def _layer_norm(x, g, b):
    mu = jnp.mean(x, axis=-1, keepdims=True)
    xc = x - mu
    var = jnp.mean(xc * xc, axis=-1, keepdims=True)
    return xc * lax.rsqrt(var + LN_EPS) * g + b


def _stage_weights_bf16(pieces, stage_ref, sem):
    def copy(k):
        src = pieces[k][0]
        dst = stage_ref.at[k % 2, pl.ds(0, src.shape[0]), :]
        return pltpu.make_async_copy(src, dst, sem.at[k % 2])

    copy(0).start()
    for k, (src, dst) in enumerate(pieces):
        if k + 1 < len(pieces):
            copy(k + 1).start()
        copy(k).wait()
        dst[...] = stage_ref[k % 2, 0:src.shape[0], :].astype(dst.dtype)


def _dup_heads(y):
    lane = lax.broadcasted_iota(jnp.int32, y.shape, 1)
    swapped = pltpu.roll(y, HEAD_DIM, 1)
    return jnp.concatenate([jnp.where(lane < HEAD_DIM, y, swapped),
                            jnp.where(lane >= HEAD_DIM, y, swapped)], axis=1)


QKV_CHUNK = _C_G // 3


def _inproj_kernel(x_ref, g_ref, b_ref, bias_ref, cos_ref, sa_ref, sb_ref, win_hbm,
                   qa_ref, ka_ref, va_ref, qb_ref, kb_ref, vb_ref,
                   w_ref, stage_ref, sem):
    half = ROPE_DIM // 2

    @pl.when((pl.program_id(0) == 0) & (pl.program_id(1) == 0))
    def _():
        cols = lambda c: pl.ds(c * QKV_CHUNK, QKV_CHUNK)
        _stage_weights_bf16([(win_hbm.at[0, :, cols(c)], w_ref.at[:, cols(c)])
                             for c in range(_C_G // QKV_CHUNK)], stage_ref, sem)

    for r0 in range(0, x_ref.shape[0], SUB_TILE_ROWS):
        rows = slice(r0, r0 + SUB_TILE_ROWS)
        h = _layer_norm(x_ref[rows, :], g_ref[...], b_ref[...])
        hb = h.astype(jnp.bfloat16)

        def proj(c0, c1):
            return (jnp.dot(hb, w_ref[:, c0:c1], preferred_element_type=jnp.float32)
                    + bias_ref[:, c0:c1])

        def rotary(y):
            cos, sa, sb = cos_ref[rows, :], sa_ref[rows, :], sb_ref[rows, :]
            outs = []
            for c in range(y.shape[1] // LANES):
                yc = y[:, c * LANES:(c + 1) * LANES]
                outs.append(yc * cos
                            + pltpu.roll(yc, half, 1) * sa
                            + pltpu.roll(yc, LANES - half, 1) * sb)
            return outs[0] if len(outs) == 1 else jnp.concatenate(outs, axis=1)

        kv = proj(_C_KB, _C_G)
        kb_ref[rows, :] = _dup_heads(rotary(kv[:, :SWA_KV_WIDTH])).astype(kb_ref.dtype)
        vb_ref[rows, :] = _dup_heads(kv[:, SWA_KV_WIDTH:]).astype(vb_ref.dtype)
        qb_ref[rows, :] = (rotary(proj(_C_QB, _C_KB)) * ATTN_SCALE).astype(qb_ref.dtype)
        qa_ref[rows, :] = (proj(_C_QA, _C_KA) * ATTN_SCALE).astype(qa_ref.dtype)
        ka_ref[rows, :] = proj(_C_KA, _C_VA).astype(ka_ref.dtype)
        va_ref[rows, :] = proj(_C_VA, _C_QB).astype(va_ref.dtype)


def _inproj(x2, g, b, bias, cos_t, sa_t, sb_t, w_in, batch, seq):
    t = batch * seq
    tm = INPROJ_TILE_ROWS
    nblk = seq // tm
    tok = lambda bi, i: (bi * nblk + i, 0)
    pos = lambda bi, i: (i, 0)
    bf = jnp.bfloat16
    out_widths = (NA_WIDTH, NA_WIDTH, NA_WIDTH, SWA_Q_WIDTH, _KV_DUP, _KV_DUP)
    resident = (2 * tm * D_MODEL * _F32_BYTES + 2 * tm * sum(out_widths) * _BF16_BYTES
                + 2 * 3 * tm * LANES * _F32_BYTES + D_MODEL * _C_G * _BF16_BYTES
                + 2 * D_MODEL * QKV_CHUNK * _F32_BYTES)
    return pl.pallas_call(
        _inproj_kernel,
        grid=(batch, nblk),
        in_specs=[
            pl.BlockSpec((tm, D_MODEL), tok),
            _const_spec((1, D_MODEL)), _const_spec((1, D_MODEL)),
            _const_spec((1, _C_END)),
            pl.BlockSpec((tm, LANES), pos), pl.BlockSpec((tm, LANES), pos),
            pl.BlockSpec((tm, LANES), pos),
            pl.BlockSpec(memory_space=pl.ANY),
        ],
        out_specs=tuple(pl.BlockSpec((tm, w), tok) for w in out_widths),
        out_shape=tuple(jax.ShapeDtypeStruct((t, w), bf) for w in out_widths),
        scratch_shapes=[
            pltpu.VMEM((D_MODEL, _C_G), bf),
            pltpu.VMEM((2, D_MODEL, QKV_CHUNK), jnp.float32),
            pltpu.SemaphoreType.DMA((2,)),
        ],
        compiler_params=pltpu.CompilerParams(
            dimension_semantics=("arbitrary", "arbitrary"),
            vmem_limit_bytes=_vmem_limit(resident)),
        name="ln_inproj_rotary",
    )(x2, g, b, bias, cos_t, sa_t, sb_t, w_in)


def _rotary_tables(seq):
    half = ROPE_DIM // 2
    inv_freq = np.power(ROPE_THETA, -np.arange(0, ROPE_DIM, 2, dtype=np.float64) / ROPE_DIM)
    ang = np.arange(seq, dtype=np.float64)[:, None] * inv_freq[None, :]
    cos, sin = np.cos(ang), np.sin(ang)
    ones = np.ones((seq, HEAD_DIM - ROPE_DIM))
    zeros = np.zeros((seq, HEAD_DIM - ROPE_DIM))
    zh = np.zeros((seq, half))
    cos_h = np.concatenate([cos, cos, ones], axis=1)
    sa_h = np.concatenate([zh, sin, zeros], axis=1)
    sb_h = np.concatenate([-sin, zh, zeros], axis=1)
    tile = lambda a: jnp.asarray(np.tile(a, (1, HEADS_PER_VREG)), dtype=jnp.float32)
    return tile(cos_h), tile(sa_h), tile(sb_h)


def _stack_heads(q):
    lane = lax.broadcasted_iota(jnp.int32, q.shape, 1)
    zero = jnp.zeros_like(q)
    return jnp.concatenate([jnp.where(lane < HEAD_DIM, q, zero),
                            jnp.where(lane >= HEAD_DIM, q, zero)], axis=0)


def _unstack_heads(o2):
    m = o2.shape[0] // 2
    lane = lax.broadcasted_iota(jnp.int32, (m, LANES), 1)
    return jnp.where(lane < HEAD_DIM, o2[:m], o2[m:])


_NT = (((1,), (1,)), ((), ()))


def _na_bias_kernel(rpb_ref, o_ref):
    w = lax.broadcasted_iota(jnp.int32, (GRID_W, LANES), 0)
    lane = lax.broadcasted_iota(jnp.int32, (GRID_W, LANES), 1)
    c = lane % GRID_W
    col_start = jnp.clip(w - NA_KW // 2, 0, GRID_W - NA_KW)
    inside = (c >= col_start) & (c < col_start + NA_KW)
    lower = lane < GRID_W
    n_dr = 2 * NA_KH - 1
    for e in range(HEADS_PER_VREG):
        lo, hi = [], []
        for dr in range(n_dr):
            x = jnp.broadcast_to(rpb_ref[0, e, dr:dr + 1, :], (GRID_W, LANES))
            lo.append(pltpu.roll(x, LANES - (NA_KW - 1), 1, stride=1, stride_axis=0))
            hi.append(pltpu.roll(x, GRID_W - (NA_KW - 1), 1, stride=1, stride_axis=0))
        for v in range(NA_KH):
            for kk in range(NA_KH // HEADS_PER_VREG):
                dr_a = 2 * kk - v + NA_KH - 1
                tile = jnp.where(inside, jnp.where(lower, lo[dr_a], hi[dr_a + 1]), MASK_VALUE)
                o_ref[v, 0, e * GRID_W:(e + 1) * GRID_W, kk * LANES:(kk + 1) * LANES] = tile


def _na_bias_table(rpb):
    n_pairs = NA_HEADS // HEADS_PER_VREG
    n_dr, n_dc = rpb.shape[1], rpb.shape[2]
    dr_pad = pl.cdiv(n_dr, SUBLANES) * SUBLANES
    rp = jnp.pad(rpb, ((0, 0), (0, dr_pad - n_dr), (0, LANES - n_dc)))
    rp = rp.reshape(n_pairs, HEADS_PER_VREG, dr_pad, LANES)
    m2, win = HEADS_PER_VREG * GRID_W, NA_KH * GRID_W
    return pl.pallas_call(
        _na_bias_kernel,
        grid=(n_pairs,),
        in_specs=[pl.BlockSpec((1, HEADS_PER_VREG, dr_pad, LANES), lambda j: (j, 0, 0, 0))],
        out_specs=pl.BlockSpec((NA_KH, 1, m2, win), lambda j: (0, j, 0, 0)),
        out_shape=jax.ShapeDtypeStruct((NA_KH, n_pairs, m2, win), jnp.float32),
        compiler_params=pltpu.CompilerParams(dimension_semantics=("parallel",)),
        name="na_bias_table",
    )(rp)


def _na_kernel(q_ref, k_ref, v_ref, tbl_ref, o_ref, s_scr, *, n_rows):
    win = NA_KH * GRID_W
    ones = jnp.ones((win, LANES), jnp.bfloat16)

    def body(i, carry):
        def window(t):
            r = i * NA_ROWS_PER_ITER + t
            rs = jnp.clip(r - NA_KH // 2, 0, n_rows - NA_KH)
            return r - rs, pl.multiple_of(rs * GRID_W, GRID_W)

        def q_rows(t):
            return pl.ds(pl.multiple_of((i * NA_ROWS_PER_ITER + t) * GRID_W, GRID_W), GRID_W)

        for t in range(NA_ROWS_PER_ITER):
            variant, start = window(t)
            lhs = _stack_heads(q_ref[0, q_rows(t), :])
            kw = k_ref[0, pl.ds(start, win), :]
            s = lax.dot_general(lhs, kw, _NT, preferred_element_type=jnp.float32)
            s_scr[t] = s + tbl_ref[variant, 0]
        for t in range(NA_ROWS_PER_ITER):
            s = s_scr[t]
            m = jnp.max(s, axis=-1, keepdims=True)
            p = jnp.exp(s - m)
            _, start = window(t)
            vw = jnp.concatenate([v_ref[0, pl.ds(start, win), :], ones], axis=1)
            o2 = jnp.dot(p.astype(jnp.bfloat16), vw, preferred_element_type=jnp.float32)
            o2 = o2[:, :LANES] / o2[:, LANES:]
            o_ref[0, q_rows(t), :] = _unstack_heads(o2).astype(o_ref.dtype)
        return carry

    lax.fori_loop(0, n_rows // NA_ROWS_PER_ITER, body, 0)


def _swa_mask_table():
    qi = np.arange(SWA_BLOCK)[:, None]
    kc = np.arange(3 * SWA_BLOCK)[None, :]
    tbl = [np.where(np.abs(kc - d * SWA_BLOCK - qi) <= SWA_WINDOW, 0.0, MASK_VALUE)
           for d in range(3)]
    return jnp.asarray(np.stack(tbl), dtype=jnp.float32)


def _swa_kernel(sink_ref, q_ref, k_ref, v_ref, mask_ref, o_ref, s_scr, *, seq):
    j = pl.program_id(1)
    blk = SWA_BLOCK
    win = 3 * blk
    row = lax.broadcasted_iota(jnp.int32, (2 * blk, 1), 0)
    sink_col = jnp.where(row < blk, sink_ref[2 * j], sink_ref[2 * j + 1])
    ones = jnp.ones((win, LANES), jnp.bfloat16)

    def body(i, carry):
        def window(u):
            n = i * SWA_BLOCKS_PER_ITER + u
            start = jnp.clip((n - 1) * blk, 0, seq - win)
            return n - start // blk, pl.multiple_of(start, blk)

        def q_rows(u):
            return pl.ds(pl.multiple_of((i * SWA_BLOCKS_PER_ITER + u) * blk, blk), blk)

        for u in range(SWA_BLOCKS_PER_ITER):
            variant, start = window(u)
            lhs = _stack_heads(q_ref[0, q_rows(u), :])
            kw = k_ref[0, pl.ds(start, win), :]
            s = lax.dot_general(lhs, kw, _NT, preferred_element_type=jnp.float32)
            mask = mask_ref[variant]
            s_scr[u] = s + jnp.concatenate([mask, mask], axis=0)
        for u in range(SWA_BLOCKS_PER_ITER):
            s = s_scr[u]
            m = jnp.maximum(jnp.max(s, axis=-1, keepdims=True), sink_col)
            e = jnp.exp(s - m)
            _, start = window(u)
            vw = jnp.concatenate([v_ref[0, pl.ds(start, win), :], ones], axis=1)
            o2 = jnp.dot(e.astype(jnp.bfloat16), vw, preferred_element_type=jnp.float32)
            o2 = o2[:, :LANES] / (o2[:, LANES:] + jnp.exp(sink_col - m))
            o_ref[0, q_rows(u), :] = _unstack_heads(o2).astype(o_ref.dtype)
        return carry

    lax.fori_loop(0, seq // (SWA_BLOCKS_PER_ITER * blk), body, 0)


def _attention_kernel(sink_ref, qa_ref, ka_ref, va_ref, tbl_ref, qb_ref, kb_ref, vb_ref,
                      mask_ref, ona_ref, oswa_ref, na_scr, swa_scr, *, n_rows, seq):
    _na_kernel(qa_ref, ka_ref, va_ref, tbl_ref, ona_ref, na_scr, n_rows=n_rows)
    _swa_kernel(sink_ref, qb_ref, kb_ref, vb_ref, mask_ref, oswa_ref, swa_scr, seq=seq)


def _attention(qa, ka, va, tbl, qb, kb, vb, sink):
    batch, seq, _ = qa.shape
    n_rows = seq // GRID_W
    n_pairs = NA_HEADS // HEADS_PER_VREG
    assert n_pairs == SWA_Q_HEADS // HEADS_PER_VREG
    pairs_per_group = n_pairs // SWA_KV_HEADS
    na_m, na_win = HEADS_PER_VREG * GRID_W, NA_KH * GRID_W
    swa_m, swa_win = HEADS_PER_VREG * SWA_BLOCK, 3 * SWA_BLOCK
    pair_block = pl.BlockSpec((1, seq, LANES), lambda b, j: (b, 0, j))
    kv_block = pl.BlockSpec((1, seq, LANES), lambda b, j: (b, 0, j // pairs_per_group))
    resident = (2 * 8 * seq * LANES * _BF16_BYTES + 2 * NA_KH * na_m * na_win * _F32_BYTES
                + 3 * SWA_BLOCK * swa_win * _F32_BYTES
                + NA_ROWS_PER_ITER * na_m * na_win * _F32_BYTES
                + SWA_BLOCKS_PER_ITER * swa_m * swa_win * _F32_BYTES)
    out = jax.ShapeDtypeStruct(qa.shape, jnp.bfloat16)
    return pl.pallas_call(
        functools.partial(_attention_kernel, n_rows=n_rows, seq=seq),
        grid=(batch, n_pairs),
        in_specs=[pl.BlockSpec(memory_space=pltpu.SMEM),
                  pair_block, pair_block, pair_block,
                  pl.BlockSpec((NA_KH, 1, na_m, na_win), lambda b, j: (0, j, 0, 0)),
                  pair_block, kv_block, kv_block,
                  _const_spec((3, SWA_BLOCK, swa_win))],
        out_specs=(pair_block, pair_block),
        out_shape=(out, out),
        scratch_shapes=[pltpu.VMEM((NA_ROWS_PER_ITER, na_m, na_win), jnp.float32),
                        pltpu.VMEM((SWA_BLOCKS_PER_ITER, swa_m, swa_win), jnp.float32)],
        compiler_params=pltpu.CompilerParams(
            dimension_semantics=("parallel", "parallel"),
            vmem_limit_bytes=_vmem_limit(resident)),
        name="attention",
    )(sink, qa, ka, va, tbl, qb, kb, vb, _swa_mask_table())


W_CHUNK = D_MODEL


def _tail_kernel(x_ref, g0_ref, b0_ref, na_ref, swa_ref, bin_ref, bout_ref, g1_ref, b1_ref,
                 bf1_ref, bf2_ref, g2_ref, b2_ref,
                 win_hbm, wna_hbm, wswa_hbm, wout_hbm, w1_hbm, w2_hbm,
                 o_ref,
                 wg_ref, wna_ref, wswa_ref, wout_ref, w1_ref, w2_ref, stage_ref, sem):
    @pl.when(pl.program_id(0) == 0)
    def _():
        cols = lambda c: pl.ds(c * W_CHUNK, W_CHUNK)
        pieces = [(win_hbm.at[0, :, pl.ds(_C_G + c * W_CHUNK, W_CHUNK)], wg_ref.at[:, cols(c)])
                  for c in range(2 * D_MODEL // W_CHUNK)]
        pieces += [(wna_hbm.at[0], wna_ref), (wswa_hbm.at[0], wswa_ref),
                   (wout_hbm.at[0], wout_ref)]
        pieces += [(w1_hbm.at[0, :, cols(c)], w1_ref.at[:, cols(c)])
                   for c in range(D_FF // W_CHUNK)]
        pieces += [(w2_hbm.at[0, cols(c), :], w2_ref.at[cols(c), :])
                   for c in range(D_FF // W_CHUNK)]
        _stage_weights_bf16(pieces, stage_ref, sem)

    h1 = []
    for r0 in range(0, x_ref.shape[0], SUB_TILE_ROWS):
        rows = slice(r0, r0 + SUB_TILE_ROWS)
        h = _layer_norm(x_ref[rows, :], g0_ref[...], b0_ref[...])
        gates = (jnp.dot(h.astype(jnp.bfloat16), wg_ref[...],
                         preferred_element_type=jnp.float32) + bin_ref[:, _C_G:_C_END])
        y_na = jnp.dot(na_ref[rows, :], wna_ref[...], preferred_element_type=jnp.float32)
        y_swa = jnp.dot(swa_ref[rows, :], wswa_ref[...], preferred_element_type=jnp.float32)
        mixed = (jax.nn.sigmoid(gates[:, :D_MODEL]) * y_na
                 + jax.nn.sigmoid(gates[:, D_MODEL:]) * y_swa)
        attn = jnp.dot(mixed.astype(jnp.bfloat16), wout_ref[...],
                       preferred_element_type=jnp.float32) + bout_ref[...]
        h1.append(_layer_norm(DEEPNORM_ALPHA * h + attn, g1_ref[...], b1_ref[...]))

    h = jnp.concatenate(h1, axis=0)
    hb = h.astype(jnp.bfloat16)
    acc = jnp.zeros(h.shape, jnp.float32)
    n_chunks = D_FF // FF_CHUNK
    for c in range(n_chunks):
        sl = slice(c * FF_CHUNK, (c + 1) * FF_CHUNK)
        if c == 0:
            u = jnp.concatenate(
                [jnp.dot(hs.astype(jnp.bfloat16), w1_ref[:, sl],
                         preferred_element_type=jnp.float32) for hs in h1], axis=0)
        else:
            u = jnp.dot(hb, w1_ref[:, sl], preferred_element_type=jnp.float32)
        u = jnp.square(jnp.maximum(u + bf1_ref[:, sl], 0.0)).astype(jnp.bfloat16)
        if c + 1 < n_chunks:
            acc = acc + jnp.dot(u, w2_ref[sl, :], preferred_element_type=jnp.float32)
        else:
            for r0 in range(0, h.shape[0], SUB_TILE_ROWS):
                rows = slice(r0, r0 + SUB_TILE_ROWS)
                z = acc[rows] + jnp.dot(u[rows], w2_ref[sl, :],
                                        preferred_element_type=jnp.float32)
                o_ref[rows, :] = _layer_norm(DEEPNORM_ALPHA * h[rows] + z + bf2_ref[...],
                                             g2_ref[...], b2_ref[...])


def _tail(x2, g0, b0, na, swa, b_in, bout, g1, b1, bf1, bf2, g2, b2,
          w_in, w_na, w_swa, w_out, w_ff1, w_ff2):
    t = x2.shape[0]
    tm = TAIL_TILE_ROWS
    tok = lambda i: (i, 0)
    vec = _const_spec((1, D_MODEL))
    hbm = pl.BlockSpec(memory_space=pl.ANY)
    bf = jnp.bfloat16
    weight_elems = (D_MODEL * 2 * D_MODEL + (NA_WIDTH + SWA_Q_WIDTH) * D_MODEL
                    + D_MODEL * D_MODEL + 2 * D_MODEL * D_FF)
    resident = (weight_elems * _BF16_BYTES + 2 * W_CHUNK * D_MODEL * _F32_BYTES
                + 2 * 2 * tm * D_MODEL * _F32_BYTES
                + 2 * tm * (NA_WIDTH + SWA_Q_WIDTH) * _BF16_BYTES)
    return pl.pallas_call(
        _tail_kernel,
        grid=(t // tm,),
        in_specs=[
            pl.BlockSpec((tm, D_MODEL), tok), vec, vec,
            pl.BlockSpec((tm, NA_WIDTH), tok), pl.BlockSpec((tm, SWA_Q_WIDTH), tok),
            _const_spec((1, _C_END)), vec, vec, vec,
            _const_spec((1, D_FF)), vec, vec, vec,
            hbm, hbm, hbm, hbm, hbm, hbm,
        ],
        out_specs=pl.BlockSpec((tm, D_MODEL), tok),
        out_shape=jax.ShapeDtypeStruct((t, D_MODEL), jnp.float32),
        scratch_shapes=[
            pltpu.VMEM((D_MODEL, 2 * D_MODEL), bf),
            pltpu.VMEM((NA_WIDTH, D_MODEL), bf), pltpu.VMEM((SWA_Q_WIDTH, D_MODEL), bf),
            pltpu.VMEM((D_MODEL, D_MODEL), bf),
            pltpu.VMEM((D_MODEL, D_FF), bf), pltpu.VMEM((D_FF, D_MODEL), bf),
            pltpu.VMEM((2, W_CHUNK, D_MODEL), jnp.float32),
            pltpu.SemaphoreType.DMA((2,)),
        ],
        compiler_params=pltpu.CompilerParams(
            dimension_semantics=("arbitrary",), vmem_limit_bytes=_vmem_limit(resident)),
        name="mix_ffn",
    )(x2, g0, b0, na, swa, b_in, bout, g1, b1, bf1, bf2, g2, b2,
      w_in, w_na, w_swa, w_out, w_ff1, w_ff2)


def kernel(x, ln0_g, ln0_b, w_in, b_in, na_rpb, swa_sink, w_branch_na, w_branch_swa,
           w_out, b_out, ln1_g, ln1_b, w_ff1, b_ff1, w_ff2, b_ff2, ln2_g, ln2_b):
    batch, seq, d = x.shape
    assert d == D_MODEL and w_in.shape == (DEPTH, D_MODEL, _C_END) and DEPTH == 1
    assert seq % INPROJ_TILE_ROWS == 0 and (batch * seq) % TAIL_TILE_ROWS == 0
    assert seq % (NA_ROWS_PER_ITER * GRID_W) == 0 and seq // GRID_W >= NA_KH
    assert seq % (SWA_BLOCKS_PER_ITER * SWA_BLOCK) == 0 and seq >= 3 * SWA_BLOCK
    row = lambda v: v.reshape(1, -1)
    x2 = x.reshape(batch * seq, d)
    b_in_row = row(b_in[0])

    cos_t, sa_t, sb_t = _rotary_tables(seq)
    qa, ka, va, qb, kb, vb = _inproj(
        x2, row(ln0_g), row(ln0_b), b_in_row, cos_t, sa_t, sb_t, w_in, batch, seq)

    three = lambda a: a.reshape(batch, seq, a.shape[-1])
    o_na, o_swa = _attention(three(qa), three(ka), three(va), _na_bias_table(na_rpb[0]),
                             three(qb), three(kb), three(vb), swa_sink[0])

    return _tail(x2, row(ln0_g), row(ln0_b), o_na.reshape(batch * seq, -1),
                 o_swa.reshape(batch * seq, -1), b_in_row, row(b_out[0]),
                 row(ln1_g[0]), row(ln1_b[0]), row(b_ff1[0]), row(b_ff2[0]),
                 row(ln2_g[0]), row(ln2_b[0]),
                 w_in, w_branch_na, w_branch_swa, w_out, w_ff1, w_ff2).reshape(batch, seq, d)
```

```python
import functools

import numpy as np
import jax
import jax.numpy as jnp
from jax import lax
from jax.experimental import pallas as pl
from jax.experimental.pallas import tpu as pltpu

D_MODEL = 1024
GRID_W = 64
HEAD_DIM = 64
NA_HEADS = 8
NA_KH = 8
NA_KW = 16
NA_WIDTH = NA_HEADS * HEAD_DIM
SWA_Q_HEADS = 8
SWA_KV_HEADS = 2
SWA_WINDOW = 128
SWA_BLOCK = 128
SWA_Q_WIDTH = SWA_Q_HEADS * HEAD_DIM
SWA_KV_WIDTH = SWA_KV_HEADS * HEAD_DIM
ROPE_THETA = 500000.0
ROPE_DIM = HEAD_DIM // 4
D_FF = 4 * D_MODEL
DEPTH = 1
DEEPNORM_ALPHA = (2.0 * DEPTH) ** 0.25
LN_EPS = 1e-5
MASK_VALUE = -1e30
ATTN_SCALE = HEAD_DIM ** -0.5

LANES = 128
SUBLANES = 8
HEADS_PER_VREG = LANES // HEAD_DIM
VMEM_BYTES = 64 * 1024 * 1024
VMEM_TEMP_BYTES = 12 * 1024 * 1024
VMEM_LIMIT = 56 * 1024 * 1024

INPROJ_TILE_ROWS = 1024
TAIL_TILE_ROWS = 512
SUB_TILE_ROWS = 256
FF_CHUNK = 1024
NA_ROWS_PER_ITER = 16
SWA_BLOCKS_PER_ITER = 8

_C_QA = 0
_C_KA = _C_QA + NA_WIDTH
_C_VA = _C_KA + NA_WIDTH
_C_QB = _C_VA + NA_WIDTH
_C_KB = _C_QB + SWA_Q_WIDTH
_C_VB = _C_KB + SWA_KV_WIDTH
_C_G = _C_VB + SWA_KV_WIDTH
_C_END = _C_G + 2 * D_MODEL
_KV_DUP = SWA_KV_HEADS * LANES

_BF16_BYTES = 2
_F32_BYTES = 4


def _vmem_limit(resident_bytes):
    assert resident_bytes + VMEM_TEMP_BYTES <= VMEM_LIMIT < VMEM_BYTES, resident_bytes
    return VMEM_LIMIT


def _const_spec(shape):
    nd = len(shape)
    return pl.BlockSpec(shape, lambda *_: (0,) * nd, pipeline_mode=pl.Buffered(1))


def _layer_norm(x, g, b):
    mu = jnp.mean(x, axis=-1, keepdims=True)
    xc = x - mu
    var = jnp.mean(xc * xc, axis=-1, keepdims=True)
    return xc * lax.rsqrt(var + LN_EPS) * g + b


def _stage_weights_bf16(pieces, stage_ref, sem):
    def copy(k):
        src = pieces[k][0]
        dst = stage_ref.at[k % 2, pl.ds(0, src.shape[0]), :]
        return pltpu.make_async_copy(src, dst, sem.at[k % 2])

    copy(0).start()
    for k, (src, dst) in enumerate(pieces):
        if k + 1 < len(pieces):
            copy(k + 1).start()
        copy(k).wait()
        dst[...] = stage_ref[k % 2, 0:src.shape[0], :].astype(dst.dtype)


def _dup_heads(y):
    lane = lax.broadcasted_iota(jnp.int32, y.shape, 1)
    swapped = pltpu.roll(y, HEAD_DIM, 1)
    return jnp.concatenate([jnp.where(lane < HEAD_DIM, y, swapped),
                            jnp.where(lane >= HEAD_DIM, y, swapped)], axis=1)


QKV_CHUNK = _C_G // 3


def _inproj_kernel(x_ref, g_ref, b_ref, bias_ref, cos_ref, sa_ref, sb_ref, win_hbm,
                   qa_ref, ka_ref, va_ref, qb_ref, kb_ref, vb_ref,
                   w_ref, stage_ref, sem):
    half = ROPE_DIM // 2

    @pl.when((pl.program_id(0) == 0) & (pl.program_id(1) == 0))
    def _():
        cols = lambda c: pl.ds(c * QKV_CHUNK, QKV_CHUNK)
        _stage_weights_bf16([(win_hbm.at[0, :, cols(c)], w_ref.at[:, cols(c)])
                             for c in range(_C_G // QKV_CHUNK)], stage_ref, sem)

    for r0 in range(0, x_ref.shape[0], SUB_TILE_ROWS):
        rows = slice(r0, r0 + SUB_TILE_ROWS)
        h = _layer_norm(x_ref[rows, :], g_ref[...], b_ref[...])
        hb = h.astype(jnp.bfloat16)

        def proj(c0, c1):
            return (jnp.dot(hb, w_ref[:, c0:c1], preferred_element_type=jnp.float32)
                    + bias_ref[:, c0:c1])

        def rotary(y):
            cos, sa, sb = cos_ref[rows, :], sa_ref[rows, :], sb_ref[rows, :]
            outs = []
            for c in range(y.shape[1] // LANES):
                yc = y[:, c * LANES:(c + 1) * LANES]
                outs.append(yc * cos
                            + pltpu.roll(yc, half, 1) * sa
                            + pltpu.roll(yc, LANES - half, 1) * sb)
            return outs[0] if len(outs) == 1 else jnp.concatenate(outs, axis=1)

        kv = proj(_C_KB, _C_G)
        kb_ref[rows, :] = _dup_heads(rotary(kv[:, :SWA_KV_WIDTH])).astype(kb_ref.dtype)
        vb_ref[rows, :] = _dup_heads(kv[:, SWA_KV_WIDTH:]).astype(vb_ref.dtype)
        qb_ref[rows, :] = (rotary(proj(_C_QB, _C_KB)) * ATTN_SCALE).astype(qb_ref.dtype)
        qa_ref[rows, :] = (proj(_C_QA, _C_KA) * ATTN_SCALE).astype(qa_ref.dtype)
        ka_ref[rows, :] = proj(_C_KA, _C_VA).astype(ka_ref.dtype)
        va_ref[rows, :] = proj(_C_VA, _C_QB).astype(va_ref.dtype)


def _inproj(x2, g, b, bias, cos_t, sa_t, sb_t, w_in, batch, seq):
    t = batch * seq
    tm = INPROJ_TILE_ROWS
    nblk = seq // tm
    tok = lambda bi, i: (bi * nblk + i, 0)
    pos = lambda bi, i: (i, 0)
    bf = jnp.bfloat16
    out_widths = (NA_WIDTH, NA_WIDTH, NA_WIDTH, SWA_Q_WIDTH, _KV_DUP, _KV_DUP)
    resident = (2 * tm * D_MODEL * _F32_BYTES + 2 * tm * sum(out_widths) * _BF16_BYTES
                + 2 * 3 * tm * LANES * _F32_BYTES + D_MODEL * _C_G * _BF16_BYTES
                + 2 * D_MODEL * QKV_CHUNK * _F32_BYTES)
    return pl.pallas_call(
        _inproj_kernel,
        grid=(batch, nblk),
        in_specs=[
            pl.BlockSpec((tm, D_MODEL), tok),
            _const_spec((1, D_MODEL)), _const_spec((1, D_MODEL)),
            _const_spec((1, _C_END)),
            pl.BlockSpec((tm, LANES), pos), pl.BlockSpec((tm, LANES), pos),
            pl.BlockSpec((tm, LANES), pos),
            pl.BlockSpec(memory_space=pl.ANY),
        ],
        out_specs=tuple(pl.BlockSpec((tm, w), tok) for w in out_widths),
        out_shape=tuple(jax.ShapeDtypeStruct((t, w), bf) for w in out_widths),
        scratch_shapes=[
            pltpu.VMEM((D_MODEL, _C_G), bf),
            pltpu.VMEM((2, D_MODEL, QKV_CHUNK), jnp.float32),
            pltpu.SemaphoreType.DMA((2,)),
        ],
        compiler_params=pltpu.CompilerParams(
            dimension_semantics=("arbitrary", "arbitrary"),
            vmem_limit_bytes=_vmem_limit(resident)),
        name="ln_inproj_rotary",
    )(x2, g, b, bias, cos_t, sa_t, sb_t, w_in)


def _rotary_tables(seq):
    half = ROPE_DIM // 2
    inv_freq = np.power(ROPE_THETA, -np.arange(0, ROPE_DIM, 2, dtype=np.float64) / ROPE_DIM)
    ang = np.arange(seq, dtype=np.float64)[:, None] * inv_freq[None, :]
    cos, sin = np.cos(ang), np.sin(ang)
    ones = np.ones((seq, HEAD_DIM - ROPE_DIM))
    zeros = np.zeros((seq, HEAD_DIM - ROPE_DIM))
    zh = np.zeros((seq, half))
    cos_h = np.concatenate([cos, cos, ones], axis=1)
    sa_h = np.concatenate([zh, sin, zeros], axis=1)
    sb_h = np.concatenate([-sin, zh, zeros], axis=1)
    tile = lambda a: jnp.asarray(np.tile(a, (1, HEADS_PER_VREG)), dtype=jnp.float32)
    return tile(cos_h), tile(sa_h), tile(sb_h)


def _stack_heads(q):
    lane = lax.broadcasted_iota(jnp.int32, q.shape, 1)
    zero = jnp.zeros_like(q)
    return jnp.concatenate([jnp.where(lane < HEAD_DIM, q, zero),
                            jnp.where(lane >= HEAD_DIM, q, zero)], axis=0)


def _unstack_heads(o2):
    m = o2.shape[0] // 2
    lane = lax.broadcasted_iota(jnp.int32, (m, LANES), 1)
    return jnp.where(lane < HEAD_DIM, o2[:m], o2[m:])


_NT = (((1,), (1,)), ((), ()))


def _na_bias_kernel(rpb_ref, o_ref):
    w = lax.broadcasted_iota(jnp.int32, (GRID_W, LANES), 0)
    lane = lax.broadcasted_iota(jnp.int32, (GRID_W, LANES), 1)
    c = lane % GRID_W
    col_start = jnp.clip(w - NA_KW // 2, 0, GRID_W - NA_KW)
    inside = (c >= col_start) & (c < col_start + NA_KW)
    lower = lane < GRID_W
    n_dr = 2 * NA_KH - 1
    for e in range(HEADS_PER_VREG):
        lo, hi = [], []
        for dr in range(n_dr):
            x = jnp.broadcast_to(rpb_ref[0, e, dr:dr + 1, :], (GRID_W, LANES))
            lo.append(pltpu.roll(x, LANES - (NA_KW - 1), 1, stride=1, stride_axis=0))
            hi.append(pltpu.roll(x, GRID_W - (NA_KW - 1), 1, stride=1, stride_axis=0))
        for v in range(NA_KH):
            for kk in range(NA_KH // HEADS_PER_VREG):
                dr_a = 2 * kk - v + NA_KH - 1
                tile = jnp.where(inside, jnp.where(lower, lo[dr_a], hi[dr_a + 1]), MASK_VALUE)
                o_ref[v, 0, e * GRID_W:(e + 1) * GRID_W, kk * LANES:(kk + 1) * LANES] = tile


def _na_bias_table(rpb):
    n_pairs = NA_HEADS // HEADS_PER_VREG
    n_dr, n_dc = rpb.shape[1], rpb.shape[2]
    dr_pad = pl.cdiv(n_dr, SUBLANES) * SUBLANES
    rp = jnp.pad(rpb, ((0, 0), (0, dr_pad - n_dr), (0, LANES - n_dc)))
    rp = rp.reshape(n_pairs, HEADS_PER_VREG, dr_pad, LANES)
    m2, win = HEADS_PER_VREG * GRID_W, NA_KH * GRID_W
    return pl.pallas_call(
        _na_bias_kernel,
        grid=(n_pairs,),
        in_specs=[pl.BlockSpec((1, HEADS_PER_VREG, dr_pad, LANES), lambda j: (j, 0, 0, 0))],
        out_specs=pl.BlockSpec((NA_KH, 1, m2, win), lambda j: (0, j, 0, 0)),
        out_shape=jax.ShapeDtypeStruct((NA_KH, n_pairs, m2, win), jnp.float32),
        compiler_params=pltpu.CompilerParams(dimension_semantics=("parallel",)),
        name="na_bias_table",
    )(rp)


def _na_kernel(q_ref, k_ref, v_ref, tbl_ref, o_ref, s_scr, *, n_rows):
    win = NA_KH * GRID_W
    ones = jnp.ones((win, LANES), jnp.bfloat16)

    def body(i, carry):
        def window(t):
            r = i * NA_ROWS_PER_ITER + t
            rs = jnp.clip(r - NA_KH // 2, 0, n_rows - NA_KH)
            return r - rs, pl.multiple_of(rs * GRID_W, GRID_W)

        def q_rows(t):
            return pl.ds(pl.multiple_of((i * NA_ROWS_PER_ITER + t) * GRID_W, GRID_W), GRID_W)

        for t in range(NA_ROWS_PER_ITER):
            variant, start = window(t)
            lhs = _stack_heads(q_ref[0, q_rows(t), :])
            kw = k_ref[0, pl.ds(start, win), :]
            s = lax.dot_general(lhs, kw, _NT, preferred_element_type=jnp.float32)
            s_scr[t] = s + tbl_ref[variant, 0]
        for t in range(NA_ROWS_PER_ITER):
            s = s_scr[t]
            m = jnp.max(s, axis=-1, keepdims=True)
            p = jnp.exp(s - m)
            _, start = window(t)
            vw = jnp.concatenate([v_ref[0, pl.ds(start, win), :], ones], axis=1)
            o2 = jnp.dot(p.astype(jnp.bfloat16), vw, preferred_element_type=jnp.float32)
            o2 = o2[:, :LANES] / o2[:, LANES:]
            o_ref[0, q_rows(t), :] = _unstack_heads(o2).astype(o_ref.dtype)
        return carry

    lax.fori_loop(0, n_rows // NA_ROWS_PER_ITER, body, 0)


def _swa_mask_table():
    qi = np.arange(SWA_BLOCK)[:, None]
    kc = np.arange(3 * SWA_BLOCK)[None, :]
    tbl = [np.where(np.abs(kc - d * SWA_BLOCK - qi) <= SWA_WINDOW, 0.0, MASK_VALUE)
           for d in range(3)]
    return jnp.asarray(np.stack(tbl), dtype=jnp.float32)


def _swa_kernel(sink_ref, q_ref, k_ref, v_ref, mask_ref, o_ref, s_scr, *, seq):
    j = pl.program_id(1)
    blk = SWA_BLOCK
    win = 3 * blk
    row = lax.broadcasted_iota(jnp.int32, (2 * blk, 1), 0)
    sink_col = jnp.where(row < blk, sink_ref[2 * j], sink_ref[2 * j + 1])
    ones = jnp.ones((win, LANES), jnp.bfloat16)

    def body(i, carry):
        def window(u):
            n = i * SWA_BLOCKS_PER_ITER + u
            start = jnp.clip((n - 1) * blk, 0, seq - win)
            return n - start // blk, pl.multiple_of(start, blk)

        def q_rows(u):
            return pl.ds(pl.multiple_of((i * SWA_BLOCKS_PER_ITER + u) * blk, blk), blk)

        for u in range(SWA_BLOCKS_PER_ITER):
            variant, start = window(u)
            lhs = _stack_heads(q_ref[0, q_rows(u), :])
            kw = k_ref[0, pl.ds(start, win), :]
            s = lax.dot_general(lhs, kw, _NT, preferred_element_type=jnp.float32)
            mask = mask_ref[variant]
            s_scr[u] = s + jnp.concatenate([mask, mask], axis=0)
        for u in range(SWA_BLOCKS_PER_ITER):
            s = s_scr[u]
            m = jnp.maximum(jnp.max(s, axis=-1, keepdims=True), sink_col)
            e = jnp.exp(s - m)
            _, start = window(u)
            vw = jnp.concatenate([v_ref[0, pl.ds(start, win), :], ones], axis=1)
            o2 = jnp.dot(e.astype(jnp.bfloat16), vw, preferred_element_type=jnp.float32)
            o2 = o2[:, :LANES] / (o2[:, LANES:] + jnp.exp(sink_col - m))
            o_ref[0, q_rows(u), :] = _unstack_heads(o2).astype(o_ref.dtype)
        return carry

    lax.fori_loop(0, seq // (SWA_BLOCKS_PER_ITER * blk), body, 0)


def _attention_kernel(sink_ref, qa_ref, ka_ref, va_ref, tbl_ref, qb_ref, kb_ref, vb_ref,
                      mask_ref, ona_ref, oswa_ref, na_scr, swa_scr, *, n_rows, seq):
    _na_kernel(qa_ref, ka_ref, va_ref, tbl_ref, ona_ref, na_scr, n_rows=n_rows)
    _swa_kernel(sink_ref, qb_ref, kb_ref, vb_ref, mask_ref, oswa_ref, swa_scr, seq=seq)


def _attention(qa, ka, va, tbl, qb, kb, vb, sink):
    batch, seq, _ = qa.shape
    n_rows = seq // GRID_W
    n_pairs = NA_HEADS // HEADS_PER_VREG
    assert n_pairs == SWA_Q_HEADS // HEADS_PER_VREG
    pairs_per_group = n_pairs // SWA_KV_HEADS
    na_m, na_win = HEADS_PER_VREG * GRID_W, NA_KH * GRID_W
    swa_m, swa_win = HEADS_PER_VREG * SWA_BLOCK, 3 * SWA_BLOCK
    pair_block = pl.BlockSpec((1, seq, LANES), lambda b, j: (b, 0, j))
    kv_block = pl.BlockSpec((1, seq, LANES), lambda b, j: (b, 0, j // pairs_per_group))
    resident = (2 * 8 * seq * LANES * _BF16_BYTES + 2 * NA_KH * na_m * na_win * _F32_BYTES
                + 3 * SWA_BLOCK * swa_win * _F32_BYTES
                + NA_ROWS_PER_ITER * na_m * na_win * _F32_BYTES
                + SWA_BLOCKS_PER_ITER * swa_m * swa_win * _F32_BYTES)
    out = jax.ShapeDtypeStruct(qa.shape, jnp.bfloat16)
    return pl.pallas_call(
        functools.partial(_attention_kernel, n_rows=n_rows, seq=seq),
        grid=(batch, n_pairs),
        in_specs=[pl.BlockSpec(memory_space=pltpu.SMEM),
                  pair_block, pair_block, pair_block,
                  pl.BlockSpec((NA_KH, 1, na_m, na_win), lambda b, j: (0, j, 0, 0)),
                  pair_block, kv_block, kv_block,
                  _const_spec((3, SWA_BLOCK, swa_win))],
        out_specs=(pair_block, pair_block),
        out_shape=(out, out),
        scratch_shapes=[pltpu.VMEM((NA_ROWS_PER_ITER, na_m, na_win), jnp.float32),
                        pltpu.VMEM((SWA_BLOCKS_PER_ITER, swa_m, swa_win), jnp.float32)],
        compiler_params=pltpu.CompilerParams(
            dimension_semantics=("parallel", "parallel"),
            vmem_limit_bytes=_vmem_limit(resident)),
        name="attention",
    )(sink, qa, ka, va, tbl, qb, kb, vb, _swa_mask_table())


W_CHUNK = D_MODEL


def _tail_kernel(x_ref, g0_ref, b0_ref, na_ref, swa_ref, bin_ref, bout_ref, g1_ref, b1_ref,
                 bf1_ref, bf2_ref, g2_ref, b2_ref,
                 win_hbm, wna_hbm, wswa_hbm, wout_hbm, w1_hbm, w2_hbm,
                 o_ref,
                 wg_ref, wna_ref, wswa_ref, wout_ref, w1_ref, w2_ref, stage_ref, sem):
    @pl.when(pl.program_id(0) == 0)
    def _():
        cols = lambda c: pl.ds(c * W_CHUNK, W_CHUNK)
        pieces = [(win_hbm.at[0, :, pl.ds(_C_G + c * W_CHUNK, W_CHUNK)], wg_ref.at[:, cols(c)])
                  for c in range(2 * D_MODEL // W_CHUNK)]
        pieces += [(wna_hbm.at[0], wna_ref), (wswa_hbm.at[0], wswa_ref),
                   (wout_hbm.at[0], wout_ref)]
        pieces += [(w1_hbm.at[0, :, cols(c)], w1_ref.at[:, cols(c)])
                   for c in range(D_FF // W_CHUNK)]
        pieces += [(w2_hbm.at[0, cols(c), :], w2_ref.at[cols(c), :])
                   for c in range(D_FF // W_CHUNK)]
        _stage_weights_bf16(pieces, stage_ref, sem)

    h1 = []
    for r0 in range(0, x_ref.shape[0], SUB_TILE_ROWS):
        rows = slice(r0, r0 + SUB_TILE_ROWS)
        y_na = jnp.dot(na_ref[rows, :], wna_ref[...], preferred_element_type=jnp.float32)
        y_swa = jnp.dot(swa_ref[rows, :], wswa_ref[...], preferred_element_type=jnp.float32)
        h = _layer_norm(x_ref[rows, :], g0_ref[...], b0_ref[...])
        gates = (jnp.dot(h.astype(jnp.bfloat16), wg_ref[...],
                         preferred_element_type=jnp.float32) + bin_ref[:, _C_G:_C_END])
        mixed = (jax.nn.sigmoid(gates[:, :D_MODEL]) * y_na
                 + jax.nn.sigmoid(gates[:, D_MODEL:]) * y_swa)
        attn = jnp.dot(mixed.astype(jnp.bfloat16), wout_ref[...],
                       preferred_element_type=jnp.float32) + bout_ref[...]
        h1.append(_layer_norm(DEEPNORM_ALPHA * h + attn, g1_ref[...], b1_ref[...]))

    h = jnp.concatenate(h1, axis=0)
    hb = h.astype(jnp.bfloat16)
    acc = jnp.zeros(h.shape, jnp.float32)
    n_chunks = D_FF // FF_CHUNK
    for c in range(n_chunks):
        sl = slice(c * FF_CHUNK, (c + 1) * FF_CHUNK)
        if c == 0:
            u = jnp.concatenate(
                [jnp.dot(hs.astype(jnp.bfloat16), w1_ref[:, sl],
                         preferred_element_type=jnp.float32) for hs in h1], axis=0)
        else:
            u = jnp.dot(hb, w1_ref[:, sl], preferred_element_type=jnp.float32)
        u = jnp.square(jnp.maximum(u + bf1_ref[:, sl], 0.0)).astype(jnp.bfloat16)
        if c + 1 < n_chunks:
            acc = acc + jnp.dot(u, w2_ref[sl, :], preferred_element_type=jnp.float32)
        else:
            for r0 in range(0, h.shape[0], SUB_TILE_ROWS):
                rows = slice(r0, r0 + SUB_TILE_ROWS)
                z = acc[rows] + jnp.dot(u[rows], w2_ref[sl, :],
                                        preferred_element_type=jnp.float32)
                o_ref[rows, :] = _layer_norm(DEEPNORM_ALPHA * h[rows] + z + bf2_ref[...],
                                             g2_ref[...], b2_ref[...])


def _tail(x2, g0, b0, na, swa, b_in, bout, g1, b1, bf1, bf2, g2, b2,
          w_in, w_na, w_swa, w_out, w_ff1, w_ff2):
    t = x2.shape[0]
    tm = TAIL_TILE_ROWS
    tok = lambda i: (i, 0)
    vec = _const_spec((1, D_MODEL))
    hbm = pl.BlockSpec(memory_space=pl.ANY)
    bf = jnp.bfloat16
    weight_elems = (D_MODEL * 2 * D_MODEL + (NA_WIDTH + SWA_Q_WIDTH) * D_MODEL
                    + D_MODEL * D_MODEL + 2 * D_MODEL * D_FF)
    resident = (weight_elems * _BF16_BYTES + 2 * W_CHUNK * D_MODEL * _F32_BYTES
                + 2 * 2 * tm * D_MODEL * _F32_BYTES
                + 2 * tm * (NA_WIDTH + SWA_Q_WIDTH) * _BF16_BYTES)
    return pl.pallas_call(
        _tail_kernel,
        grid=(t // tm,),
        in_specs=[
            pl.BlockSpec((tm, D_MODEL), tok), vec, vec,
            pl.BlockSpec((tm, NA_WIDTH), tok), pl.BlockSpec((tm, SWA_Q_WIDTH), tok),
            _const_spec((1, _C_END)), vec, vec, vec,
            _const_spec((1, D_FF)), vec, vec, vec,
            hbm, hbm, hbm, hbm, hbm, hbm,
        ],
        out_specs=pl.BlockSpec((tm, D_MODEL), tok),
        out_shape=jax.ShapeDtypeStruct((t, D_MODEL), jnp.float32),
        scratch_shapes=[
            pltpu.VMEM((D_MODEL, 2 * D_MODEL), bf),
            pltpu.VMEM((NA_WIDTH, D_MODEL), bf), pltpu.VMEM((SWA_Q_WIDTH, D_MODEL), bf),
            pltpu.VMEM((D_MODEL, D_MODEL), bf),
            pltpu.VMEM((D_MODEL, D_FF), bf), pltpu.VMEM((D_FF, D_MODEL), bf),
            pltpu.VMEM((2, W_CHUNK, D_MODEL), jnp.float32),
            pltpu.SemaphoreType.DMA((2,)),
        ],
        compiler_params=pltpu.CompilerParams(
            dimension_semantics=("arbitrary",), vmem_limit_bytes=_vmem_limit(resident)),
        name="mix_ffn",
    )(x2, g0, b0, na, swa, b_in, bout, g1, b1, bf1, bf2, g2, b2,
      w_in, w_na, w_swa, w_out, w_ff1, w_ff2)


def kernel(x, ln0_g, ln0_b, w_in, b_in, na_rpb, swa_sink, w_branch_na, w_branch_swa,
           w_out, b_out, ln1_g, ln1_b, w_ff1, b_ff1, w_ff2, b_ff2, ln2_g, ln2_b):
    batch, seq, d = x.shape
    assert d == D_MODEL and w_in.shape == (DEPTH, D_MODEL, _C_END) and DEPTH == 1
    assert seq % INPROJ_TILE_ROWS == 0 and (batch * seq) % TAIL_TILE_ROWS == 0
    assert seq % (NA_ROWS_PER_ITER * GRID_W) == 0 and seq // GRID_W >= NA_KH
    assert seq % (SWA_BLOCKS_PER_ITER * SWA_BLOCK) == 0 and seq >= 3 * SWA_BLOCK
    row = lambda v: v.reshape(1, -1)
    x2 = x.reshape(batch * seq, d)
    b_in_row = row(b_in[0])

    cos_t, sa_t, sb_t = _rotary_tables(seq)
    qa, ka, va, qb, kb, vb = _inproj(
        x2, row(ln0_g), row(ln0_b), b_in_row, cos_t, sa_t, sb_t, w_in, batch, seq)

    three = lambda a: a.reshape(batch, seq, a.shape[-1])
    o_na, o_swa = _attention(three(qa), three(ka), three(va), _na_bias_table(na_rpb[0]),
                             three(qb), three(kb), three(vb), swa_sink[0])

    return _tail(x2, row(ln0_g), row(ln0_b), o_na.reshape(batch * seq, -1),
                 o_swa.reshape(batch * seq, -1), b_in_row, row(b_out[0]),
                 row(ln1_g[0]), row(ln1_b[0]), row(b_ff1[0]), row(b_ff2[0]),
                 row(ln2_g[0]), row(ln2_b[0]),
                 w_in, w_branch_na, w_branch_swa, w_out, w_ff1, w_ff2).reshape(batch, seq, d)
```

```python
import functools

import numpy as np
import jax
import jax.numpy as jnp
from jax import lax
from jax.experimental import pallas as pl
from jax.experimental.pallas import tpu as pltpu

D_MODEL = 1024
GRID_W = 64
HEAD_DIM = 64
NA_HEADS = 8
NA_KH = 8
NA_KW = 16
NA_WIDTH = NA_HEADS * HEAD_DIM
SWA_Q_HEADS = 8
SWA_KV_HEADS = 2
SWA_WINDOW = 128
SWA_BLOCK = 128
SWA_Q_WIDTH = SWA_Q_HEADS * HEAD_DIM
SWA_KV_WIDTH = SWA_KV_HEADS * HEAD_DIM
ROPE_THETA = 500000.0
ROPE_DIM = HEAD_DIM // 4
D_FF = 4 * D_MODEL
DEPTH = 1
DEEPNORM_ALPHA = (2.0 * DEPTH) ** 0.25
LN_EPS = 1e-5
MASK_VALUE = -1e30
ATTN_SCALE = HEAD_DIM ** -0.5

LANES = 128
SUBLANES = 8
HEADS_PER_VREG = LANES // HEAD_DIM
VMEM_BYTES = 64 * 1024 * 1024
VMEM_TEMP_BYTES = 12 * 1024 * 1024
VMEM_LIMIT = 56 * 1024 * 1024

INPROJ_TILE_ROWS = 1024
TAIL_TILE_ROWS = 512
SUB_TILE_ROWS = 256
FF_CHUNK = 1024
NA_ROWS_PER_ITER = 16
SWA_BLOCKS_PER_ITER = 8
NA_LOOKAHEAD = 3

_C_QA = 0
_C_KA = _C_QA + NA_WIDTH
_C_VA = _C_KA + NA_WIDTH
_C_QB = _C_VA + NA_WIDTH
_C_KB = _C_QB + SWA_Q_WIDTH
_C_VB = _C_KB + SWA_KV_WIDTH
_C_G = _C_VB + SWA_KV_WIDTH
_C_END = _C_G + 2 * D_MODEL
_KV_DUP = SWA_KV_HEADS * LANES

_BF16_BYTES = 2
_F32_BYTES = 4


def _vmem_limit(resident_bytes):
    assert resident_bytes + VMEM_TEMP_BYTES <= VMEM_LIMIT < VMEM_BYTES, resident_bytes
    return VMEM_LIMIT


def _const_spec(shape):
    nd = len(shape)
    return pl.BlockSpec(shape, lambda *_: (0,) * nd, pipeline_mode=pl.Buffered(1))


def _layer_norm(x, g, b):
    mu = jnp.mean(x, axis=-1, keepdims=True)
    xc = x - mu
    var = jnp.mean(xc * xc, axis=-1, keepdims=True)
    return xc * lax.rsqrt(var + LN_EPS) * g + b


def _stage_weights_bf16(pieces, stage_ref, sem):
    def copy(k):
        src = pieces[k][0]
        dst = stage_ref.at[k % 2, pl.ds(0, src.shape[0]), :]
        return pltpu.make_async_copy(src, dst, sem.at[k % 2])

    copy(0).start()
    for k, (src, dst) in enumerate(pieces):
        if k + 1 < len(pieces):
            copy(k + 1).start()
        copy(k).wait()
        dst[...] = stage_ref[k % 2, 0:src.shape[0], :].astype(dst.dtype)


def _dup_heads(y):
    lane = lax.broadcasted_iota(jnp.int32, y.shape, 1)
    swapped = pltpu.roll(y, HEAD_DIM, 1)
    return jnp.concatenate([jnp.where(lane < HEAD_DIM, y, swapped),
                            jnp.where(lane >= HEAD_DIM, y, swapped)], axis=1)


QKV_CHUNK = _C_G // 3


def _inproj_kernel(x_ref, g_ref, b_ref, bias_ref, cos_ref, sa_ref, sb_ref, win_hbm,
                   qa_ref, ka_ref, va_ref, qb_ref, kb_ref, vb_ref,
                   w_ref, stage_ref, sem):
    half = ROPE_DIM // 2

    @pl.when((pl.program_id(0) == 0) & (pl.program_id(1) == 0))
    def _():
        cols = lambda c: pl.ds(c * QKV_CHUNK, QKV_CHUNK)
        _stage_weights_bf16([(win_hbm.at[0, :, cols(c)], w_ref.at[:, cols(c)])
                             for c in range(_C_G // QKV_CHUNK)], stage_ref, sem)

    for r0 in range(0, x_ref.shape[0], SUB_TILE_ROWS):
        rows = slice(r0, r0 + SUB_TILE_ROWS)
        h = _layer_norm(x_ref[rows, :], g_ref[...], b_ref[...])
        hb = h.astype(jnp.bfloat16)

        def proj(c0, c1):
            return (jnp.dot(hb, w_ref[:, c0:c1], preferred_element_type=jnp.float32)
                    + bias_ref[:, c0:c1])

        def rotary(y):
            cos, sa, sb = cos_ref[rows, :], sa_ref[rows, :], sb_ref[rows, :]
            outs = []
            for c in range(y.shape[1] // LANES):
                yc = y[:, c * LANES:(c + 1) * LANES]
                outs.append(yc * cos
                            + pltpu.roll(yc, half, 1) * sa
                            + pltpu.roll(yc, LANES - half, 1) * sb)
            return outs[0] if len(outs) == 1 else jnp.concatenate(outs, axis=1)

        kv = proj(_C_KB, _C_G)
        kb_ref[rows, :] = _dup_heads(rotary(kv[:, :SWA_KV_WIDTH])).astype(kb_ref.dtype)
        vb_ref[rows, :] = _dup_heads(kv[:, SWA_KV_WIDTH:]).astype(vb_ref.dtype)
        qb_ref[rows, :] = (rotary(proj(_C_QB, _C_KB)) * ATTN_SCALE).astype(qb_ref.dtype)
        qa_ref[rows, :] = (proj(_C_QA, _C_KA) * ATTN_SCALE).astype(qa_ref.dtype)
        ka_ref[rows, :] = proj(_C_KA, _C_VA).astype(ka_ref.dtype)
        va_ref[rows, :] = proj(_C_VA, _C_QB).astype(va_ref.dtype)


def _inproj(x2, g, b, bias, cos_t, sa_t, sb_t, w_in, batch, seq):
    t = batch * seq
    tm = INPROJ_TILE_ROWS
    nblk = seq // tm
    tok = lambda bi, i: (bi * nblk + i, 0)
    pos = lambda bi, i: (i, 0)
    bf = jnp.bfloat16
    out_widths = (NA_WIDTH, NA_WIDTH, NA_WIDTH, SWA_Q_WIDTH, _KV_DUP, _KV_DUP)
    resident = (2 * tm * D_MODEL * _F32_BYTES + 2 * tm * sum(out_widths) * _BF16_BYTES
                + 2 * 3 * tm * LANES * _F32_BYTES + D_MODEL * _C_G * _BF16_BYTES
                + 2 * D_MODEL * QKV_CHUNK * _F32_BYTES)
    return pl.pallas_call(
        _inproj_kernel,
        grid=(batch, nblk),
        in_specs=[
            pl.BlockSpec((tm, D_MODEL), tok),
            _const_spec((1, D_MODEL)), _const_spec((1, D_MODEL)),
            _const_spec((1, _C_END)),
            pl.BlockSpec((tm, LANES), pos), pl.BlockSpec((tm, LANES), pos),
            pl.BlockSpec((tm, LANES), pos),
            pl.BlockSpec(memory_space=pl.ANY),
        ],
        out_specs=tuple(pl.BlockSpec((tm, w), tok) for w in out_widths),
        out_shape=tuple(jax.ShapeDtypeStruct((t, w), bf) for w in out_widths),
        scratch_shapes=[
            pltpu.VMEM((D_MODEL, _C_G), bf),
            pltpu.VMEM((2, D_MODEL, QKV_CHUNK), jnp.float32),
            pltpu.SemaphoreType.DMA((2,)),
        ],
        compiler_params=pltpu.CompilerParams(
            dimension_semantics=("arbitrary", "arbitrary"),
            vmem_limit_bytes=_vmem_limit(resident)),
        name="ln_inproj_rotary",
    )(x2, g, b, bias, cos_t, sa_t, sb_t, w_in)


def _rotary_tables(seq):
    half = ROPE_DIM // 2
    inv_freq = np.power(ROPE_THETA, -np.arange(0, ROPE_DIM, 2, dtype=np.float64) / ROPE_DIM)
    ang = np.arange(seq, dtype=np.float64)[:, None] * inv_freq[None, :]
    cos, sin = np.cos(ang), np.sin(ang)
    ones = np.ones((seq, HEAD_DIM - ROPE_DIM))
    zeros = np.zeros((seq, HEAD_DIM - ROPE_DIM))
    zh = np.zeros((seq, half))
    cos_h = np.concatenate([cos, cos, ones], axis=1)
    sa_h = np.concatenate([zh, sin, zeros], axis=1)
    sb_h = np.concatenate([-sin, zh, zeros], axis=1)
    tile = lambda a: jnp.asarray(np.tile(a, (1, HEADS_PER_VREG)), dtype=jnp.float32)
    return tile(cos_h), tile(sa_h), tile(sb_h)


def _stack_heads(q):
    lane = lax.broadcasted_iota(jnp.int32, q.shape, 1)
    zero = jnp.zeros_like(q)
    return jnp.concatenate([jnp.where(lane < HEAD_DIM, q, zero),
                            jnp.where(lane >= HEAD_DIM, q, zero)], axis=0)


def _unstack_heads(o2):
    m = o2.shape[0] // 2
    lane = lax.broadcasted_iota(jnp.int32, (m, LANES), 1)
    return jnp.where(lane < HEAD_DIM, o2[:m], o2[m:])


_NT = (((1,), (1,)), ((), ()))


def _na_bias_kernel(rpb_ref, o_ref):
    w = lax.broadcasted_iota(jnp.int32, (GRID_W, LANES), 0)
    lane = lax.broadcasted_iota(jnp.int32, (GRID_W, LANES), 1)
    c = lane % GRID_W
    col_start = jnp.clip(w - NA_KW // 2, 0, GRID_W - NA_KW)
    inside = (c >= col_start) & (c < col_start + NA_KW)
    lower = lane < GRID_W
    n_dr = 2 * NA_KH - 1
    for e in range(HEADS_PER_VREG):
        lo, hi = [], []
        for dr in range(n_dr):
            x = jnp.broadcast_to(rpb_ref[0, e, dr:dr + 1, :], (GRID_W, LANES))
            lo.append(pltpu.roll(x, LANES - (NA_KW - 1), 1, stride=1, stride_axis=0))
            hi.append(pltpu.roll(x, GRID_W - (NA_KW - 1), 1, stride=1, stride_axis=0))
        for v in range(NA_KH):
            for kk in range(NA_KH // HEADS_PER_VREG):
                dr_a = 2 * kk - v + NA_KH - 1
                tile = jnp.where(inside, jnp.where(lower, lo[dr_a], hi[dr_a + 1]), MASK_VALUE)
                o_ref[v, 0, e * GRID_W:(e + 1) * GRID_W, kk * LANES:(kk + 1) * LANES] = tile


def _na_bias_table(rpb):
    n_pairs = NA_HEADS // HEADS_PER_VREG
    n_dr, n_dc = rpb.shape[1], rpb.shape[2]
    dr_pad = pl.cdiv(n_dr, SUBLANES) * SUBLANES
    rp = jnp.pad(rpb, ((0, 0), (0, dr_pad - n_dr), (0, LANES - n_dc)))
    rp = rp.reshape(n_pairs, HEADS_PER_VREG, dr_pad, LANES)
    m2, win = HEADS_PER_VREG * GRID_W, NA_KH * GRID_W
    return pl.pallas_call(
        _na_bias_kernel,
        grid=(n_pairs,),
        in_specs=[pl.BlockSpec((1, HEADS_PER_VREG, dr_pad, LANES), lambda j: (j, 0, 0, 0))],
        out_specs=pl.BlockSpec((NA_KH, 1, m2, win), lambda j: (0, j, 0, 0)),
        out_shape=jax.ShapeDtypeStruct((NA_KH, n_pairs, m2, win), jnp.float32),
        compiler_params=pltpu.CompilerParams(dimension_semantics=("parallel",)),
        name="na_bias_table",
    )(rp)


def _na_kernel(q_ref, k_ref, v_ref, tbl_ref, o_ref, s_scr, *, n_rows):
    win = NA_KH * GRID_W
    ones = jnp.ones((win, LANES), jnp.bfloat16)

    def body(i, carry):
        def window(t):
            r = i * NA_ROWS_PER_ITER + t
            rs = jnp.clip(r - NA_KH // 2, 0, n_rows - NA_KH)
            return r - rs, pl.multiple_of(rs * GRID_W, GRID_W)

        def q_rows(t):
            return pl.ds(pl.multiple_of((i * NA_ROWS_PER_ITER + t) * GRID_W, GRID_W), GRID_W)

        def score(t):
            variant, start = window(t)
            lhs = _stack_heads(q_ref[0, q_rows(t), :])
            kw = k_ref[0, pl.ds(start, win), :]
            s = lax.dot_general(lhs, kw, _NT, preferred_element_type=jnp.float32)
            s_scr[t] = s + tbl_ref[variant, 0]

        for t in range(min(NA_LOOKAHEAD, NA_ROWS_PER_ITER)):
            score(t)
        for t in range(NA_ROWS_PER_ITER):
            if t + NA_LOOKAHEAD < NA_ROWS_PER_ITER:
                score(t + NA_LOOKAHEAD)
            s = s_scr[t]
            m = jnp.max(s, axis=-1, keepdims=True)
            p = jnp.exp(s - m)
            _, start = window(t)
            vw = jnp.concatenate([v_ref[0, pl.ds(start, win), :], ones], axis=1)
            o2 = jnp.dot(p.astype(jnp.bfloat16), vw, preferred_element_type=jnp.float32)
            o2 = o2[:, :LANES] / o2[:, LANES:]
            o_ref[0, q_rows(t), :] = _unstack_heads(o2).astype(o_ref.dtype)
        return carry

    lax.fori_loop(0, n_rows // NA_ROWS_PER_ITER, body, 0)


def _swa_mask_table():
    qi = np.arange(SWA_BLOCK)[:, None]
    kc = np.arange(3 * SWA_BLOCK)[None, :]
    tbl = [np.where(np.abs(kc - d * SWA_BLOCK - qi) <= SWA_WINDOW, 0.0, MASK_VALUE)
           for d in range(3)]
    return jnp.asarray(np.stack(tbl), dtype=jnp.float32)


def _swa_kernel(sink_ref, q_ref, k_ref, v_ref, mask_ref, o_ref, s_scr, *, seq):
    j = pl.program_id(1)
    blk = SWA_BLOCK
    win = 3 * blk
    row = lax.broadcasted_iota(jnp.int32, (2 * blk, 1), 0)
    sink_col = jnp.where(row < blk, sink_ref[2 * j], sink_ref[2 * j + 1])
    ones = jnp.ones((win, LANES), jnp.bfloat16)

    def body(i, carry):
        def window(u):
            n = i * SWA_BLOCKS_PER_ITER + u
            start = jnp.clip((n - 1) * blk, 0, seq - win)
            return n - start // blk, pl.multiple_of(start, blk)

        def q_rows(u):
            return pl.ds(pl.multiple_of((i * SWA_BLOCKS_PER_ITER + u) * blk, blk), blk)

        for u in range(SWA_BLOCKS_PER_ITER):
            variant, start = window(u)
            lhs = _stack_heads(q_ref[0, q_rows(u), :])
            kw = k_ref[0, pl.ds(start, win), :]
            s = lax.dot_general(lhs, kw, _NT, preferred_element_type=jnp.float32)
            mask = mask_ref[variant]
            s_scr[u] = s + jnp.concatenate([mask, mask], axis=0)
        for u in range(SWA_BLOCKS_PER_ITER):
            s = s_scr[u]
            m = jnp.maximum(jnp.max(s, axis=-1, keepdims=True), sink_col)
            e = jnp.exp(s - m)
            _, start = window(u)
            vw = jnp.concatenate([v_ref[0, pl.ds(start, win), :], ones], axis=1)
            o2 = jnp.dot(e.astype(jnp.bfloat16), vw, preferred_element_type=jnp.float32)
            o2 = o2[:, :LANES] / (o2[:, LANES:] + jnp.exp(sink_col - m))
            o_ref[0, q_rows(u), :] = _unstack_heads(o2).astype(o_ref.dtype)
        return carry

    lax.fori_loop(0, seq // (SWA_BLOCKS_PER_ITER * blk), body, 0)


def _attention_kernel(sink_ref, qa_ref, ka_ref, va_ref, tbl_ref, qb_ref, kb_ref, vb_ref,
                      mask_ref, ona_ref, oswa_ref, na_scr, swa_scr, *, n_rows, seq):
    _na_kernel(qa_ref, ka_ref, va_ref, tbl_ref, ona_ref, na_scr, n_rows=n_rows)
    _swa_kernel(sink_ref, qb_ref, kb_ref, vb_ref, mask_ref, oswa_ref, swa_scr, seq=seq)


def _attention(qa, ka, va, tbl, qb, kb, vb, sink):
    batch, seq, _ = qa.shape
    n_rows = seq // GRID_W
    n_pairs = NA_HEADS // HEADS_PER_VREG
    assert n_pairs == SWA_Q_HEADS // HEADS_PER_VREG
    pairs_per_group = n_pairs // SWA_KV_HEADS
    na_m, na_win = HEADS_PER_VREG * GRID_W, NA_KH * GRID_W
    swa_m, swa_win = HEADS_PER_VREG * SWA_BLOCK, 3 * SWA_BLOCK
    pair_block = pl.BlockSpec((1, seq, LANES), lambda b, j: (b, 0, j))
    kv_block = pl.BlockSpec((1, seq, LANES), lambda b, j: (b, 0, j // pairs_per_group))
    resident = (2 * 8 * seq * LANES * _BF16_BYTES + 2 * NA_KH * na_m * na_win * _F32_BYTES
                + 3 * SWA_BLOCK * swa_win * _F32_BYTES
                + NA_ROWS_PER_ITER * na_m * na_win * _F32_BYTES
                + SWA_BLOCKS_PER_ITER * swa_m * swa_win * _F32_BYTES)
    out = jax.ShapeDtypeStruct(qa.shape, jnp.bfloat16)
    return pl.pallas_call(
        functools.partial(_attention_kernel, n_rows=n_rows, seq=seq),
        grid=(batch, n_pairs),
        in_specs=[pl.BlockSpec(memory_space=pltpu.SMEM),
                  pair_block, pair_block, pair_block,
                  pl.BlockSpec((NA_KH, 1, na_m, na_win), lambda b, j: (0, j, 0, 0)),
                  pair_block, kv_block, kv_block,
                  _const_spec((3, SWA_BLOCK, swa_win))],
        out_specs=(pair_block, pair_block),
        out_shape=(out, out),
        scratch_shapes=[pltpu.VMEM((NA_ROWS_PER_ITER, na_m, na_win), jnp.float32),
                        pltpu.VMEM((SWA_BLOCKS_PER_ITER, swa_m, swa_win), jnp.float32)],
        compiler_params=pltpu.CompilerParams(
            dimension_semantics=("parallel", "parallel"),
            vmem_limit_bytes=_vmem_limit(resident)),
        name="attention",
    )(sink, qa, ka, va, tbl, qb, kb, vb, _swa_mask_table())


W_CHUNK = D_MODEL


def _tail_kernel(x_ref, g0_ref, b0_ref, na_ref, swa_ref, bin_ref, bout_ref, g1_ref, b1_ref,
                 bf1_ref, bf2_ref, g2_ref, b2_ref,
                 win_hbm, wna_hbm, wswa_hbm, wout_hbm, w1_hbm, w2_hbm,
                 o_ref,
                 wg_ref, wna_ref, wswa_ref, wout_ref, w1_ref, w2_ref, stage_ref, sem):
    @pl.when(pl.program_id(0) == 0)
    def _():
        cols = lambda c: pl.ds(c * W_CHUNK, W_CHUNK)
        pieces = [(win_hbm.at[0, :, pl.ds(_C_G + c * W_CHUNK, W_CHUNK)], wg_ref.at[:, cols(c)])
                  for c in range(2 * D_MODEL // W_CHUNK)]
        pieces += [(wna_hbm.at[0], wna_ref), (wswa_hbm.at[0], wswa_ref),
                   (wout_hbm.at[0], wout_ref)]
        pieces += [(w1_hbm.at[0, :, cols(c)], w1_ref.at[:, cols(c)])
                   for c in range(D_FF // W_CHUNK)]
        pieces += [(w2_hbm.at[0, cols(c), :], w2_ref.at[cols(c), :])
                   for c in range(D_FF // W_CHUNK)]
        _stage_weights_bf16(pieces, stage_ref, sem)

    h1 = []
    for r0 in range(0, x_ref.shape[0], SUB_TILE_ROWS):
        rows = slice(r0, r0 + SUB_TILE_ROWS)
        y_na = jnp.dot(na_ref[rows, :], wna_ref[...], preferred_element_type=jnp.float32)
        y_swa = jnp.dot(swa_ref[rows, :], wswa_ref[...], preferred_element_type=jnp.float32)
        h = _layer_norm(x_ref[rows, :], g0_ref[...], b0_ref[...])
        gates = (jnp.dot(h.astype(jnp.bfloat16), wg_ref[...],
                         preferred_element_type=jnp.float32) + bin_ref[:, _C_G:_C_END])
        mixed = (jax.nn.sigmoid(gates[:, :D_MODEL]) * y_na
                 + jax.nn.sigmoid(gates[:, D_MODEL:]) * y_swa)
        attn = jnp.dot(mixed.astype(jnp.bfloat16), wout_ref[...],
                       preferred_element_type=jnp.float32) + bout_ref[...]
        h1.append(_layer_norm(DEEPNORM_ALPHA * h + attn, g1_ref[...], b1_ref[...]))

    h = jnp.concatenate(h1, axis=0)
    hb = h.astype(jnp.bfloat16)
    acc = jnp.zeros(h.shape, jnp.float32)
    n_chunks = D_FF // FF_CHUNK
    for c in range(n_chunks):
        sl = slice(c * FF_CHUNK, (c + 1) * FF_CHUNK)
        if c == 0:
            u = jnp.concatenate(
                [jnp.dot(hs.astype(jnp.bfloat16), w1_ref[:, sl],
                         preferred_element_type=jnp.float32) for hs in h1], axis=0)
        else:
            u = jnp.dot(hb, w1_ref[:, sl], preferred_element_type=jnp.float32)
        u = jnp.square(jnp.maximum(u + bf1_ref[:, sl], 0.0)).astype(jnp.bfloat16)
        if c + 1 < n_chunks:
            acc = acc + jnp.dot(u, w2_ref[sl, :], preferred_element_type=jnp.float32)
        else:
            for r0 in range(0, h.shape[0], SUB_TILE_ROWS):
                rows = slice(r0, r0 + SUB_TILE_ROWS)
                z = acc[rows] + jnp.dot(u[rows], w2_ref[sl, :],
                                        preferred_element_type=jnp.float32)
                o_ref[rows, :] = _layer_norm(DEEPNORM_ALPHA * h[rows] + z + bf2_ref[...],
                                             g2_ref[...], b2_ref[...])


def _tail(x2, g0, b0, na, swa, b_in, bout, g1, b1, bf1, bf2, g2, b2,
          w_in, w_na, w_swa, w_out, w_ff1, w_ff2):
    t = x2.shape[0]
    tm = TAIL_TILE_ROWS
    tok = lambda i: (i, 0)
    vec = _const_spec((1, D_MODEL))
    hbm = pl.BlockSpec(memory_space=pl.ANY)
    bf = jnp.bfloat16
    weight_elems = (D_MODEL * 2 * D_MODEL + (NA_WIDTH + SWA_Q_WIDTH) * D_MODEL
                    + D_MODEL * D_MODEL + 2 * D_MODEL * D_FF)
    resident = (weight_elems * _BF16_BYTES + 2 * W_CHUNK * D_MODEL * _F32_BYTES
                + 2 * 2 * tm * D_MODEL * _F32_BYTES
                + 2 * tm * (NA_WIDTH + SWA_Q_WIDTH) * _BF16_BYTES)
    return pl.pallas_call(
        _tail_kernel,
        grid=(t // tm,),
        in_specs=[
            pl.BlockSpec((tm, D_MODEL), tok), vec, vec,
            pl.BlockSpec((tm, NA_WIDTH), tok), pl.BlockSpec((tm, SWA_Q_WIDTH), tok),
            _const_spec((1, _C_END)), vec, vec, vec,
            _const_spec((1, D_FF)), vec, vec, vec,
            hbm, hbm, hbm, hbm, hbm, hbm,
        ],
        out_specs=pl.BlockSpec((tm, D_MODEL), tok),
        out_shape=jax.ShapeDtypeStruct((t, D_MODEL), jnp.float32),
        scratch_shapes=[
            pltpu.VMEM((D_MODEL, 2 * D_MODEL), bf),
            pltpu.VMEM((NA_WIDTH, D_MODEL), bf), pltpu.VMEM((SWA_Q_WIDTH, D_MODEL), bf),
            pltpu.VMEM((D_MODEL, D_MODEL), bf),
            pltpu.VMEM((D_MODEL, D_FF), bf), pltpu.VMEM((D_FF, D_MODEL), bf),
            pltpu.VMEM((2, W_CHUNK, D_MODEL), jnp.float32),
            pltpu.SemaphoreType.DMA((2,)),
        ],
        compiler_params=pltpu.CompilerParams(
            dimension_semantics=("arbitrary",), vmem_limit_bytes=_vmem_limit(resident)),
        name="mix_ffn",
    )(x2, g0, b0, na, swa, b_in, bout, g1, b1, bf1, bf2, g2, b2,
      w_in, w_na, w_swa, w_out, w_ff1, w_ff2)


def kernel(x, ln0_g, ln0_b, w_in, b_in, na_rpb, swa_sink, w_branch_na, w_branch_swa,
           w_out, b_out, ln1_g, ln1_b, w_ff1, b_ff1, w_ff2, b_ff2, ln2_g, ln2_b):
    batch, seq, d = x.shape
    assert d == D_MODEL and w_in.shape == (DEPTH, D_MODEL, _C_END) and DEPTH == 1
    assert seq % INPROJ_TILE_ROWS == 0 and (batch * seq) % TAIL_TILE_ROWS == 0
    assert seq % (NA_ROWS_PER_ITER * GRID_W) == 0 and seq // GRID_W >= NA_KH
    assert seq % (SWA_BLOCKS_PER_ITER * SWA_BLOCK) == 0 and seq >= 3 * SWA_BLOCK
    row = lambda v: v.reshape(1, -1)
    x2 = x.reshape(batch * seq, d)
    b_in_row = row(b_in[0])

    cos_t, sa_t, sb_t = _rotary_tables(seq)
    qa, ka, va, qb, kb, vb = _inproj(
        x2, row(ln0_g), row(ln0_b), b_in_row, cos_t, sa_t, sb_t, w_in, batch, seq)

    three = lambda a: a.reshape(batch, seq, a.shape[-1])
    o_na, o_swa = _attention(three(qa), three(ka), three(va), _na_bias_table(na_rpb[0]),
                             three(qb), three(kb), three(vb), swa_sink[0])

    return _tail(x2, row(ln0_g), row(ln0_b), o_na.reshape(batch * seq, -1),
                 o_swa.reshape(batch * seq, -1), b_in_row, row(b_out[0]),
                 row(ln1_g[0]), row(ln1_b[0]), row(b_ff1[0]), row(b_ff2[0]),
                 row(ln2_g[0]), row(ln2_b[0]),
                 w_in, w_branch_na, w_branch_swa, w_out, w_ff1, w_ff2).reshape(batch, seq, d)
```

```python
import functools

import numpy as np
import jax
import jax.numpy as jnp
from jax import lax
from jax.experimental import pallas as pl
from jax.experimental.pallas import tpu as pltpu

D_MODEL = 1024
GRID_W = 64
HEAD_DIM = 64
NA_HEADS = 8
NA_KH = 8
NA_KW = 16
NA_WIDTH = NA_HEADS * HEAD_DIM
SWA_Q_HEADS = 8
SWA_KV_HEADS = 2
SWA_WINDOW = 128
SWA_BLOCK = 128
SWA_Q_WIDTH = SWA_Q_HEADS * HEAD_DIM
SWA_KV_WIDTH = SWA_KV_HEADS * HEAD_DIM
ROPE_THETA = 500000.0
ROPE_DIM = HEAD_DIM // 4
D_FF = 4 * D_MODEL
DEPTH = 1
DEEPNORM_ALPHA = (2.0 * DEPTH) ** 0.25
LN_EPS = 1e-5
MASK_VALUE = -1e30
ATTN_SCALE = HEAD_DIM ** -0.5

LANES = 128
SUBLANES = 8
HEADS_PER_VREG = LANES // HEAD_DIM
VMEM_BYTES = 64 * 1024 * 1024
VMEM_TEMP_BYTES = 12 * 1024 * 1024
VMEM_LIMIT = 56 * 1024 * 1024

INPROJ_TILE_ROWS = 1024
TAIL_TILE_ROWS = 512
SUB_TILE_ROWS = 256
FF_CHUNK = 1024
NA_ROWS_PER_ITER = 128
SWA_BLOCKS_PER_ITER = 8
NA_LOOKAHEAD = 3
NA_SCORE_SLOTS = NA_LOOKAHEAD + 1

_C_QA = 0
_C_KA = _C_QA + NA_WIDTH
_C_VA = _C_KA + NA_WIDTH
_C_QB = _C_VA + NA_WIDTH
_C_KB = _C_QB + SWA_Q_WIDTH
_C_VB = _C_KB + SWA_KV_WIDTH
_C_G = _C_VB + SWA_KV_WIDTH
_C_END = _C_G + 2 * D_MODEL
_KV_DUP = SWA_KV_HEADS * LANES

_BF16_BYTES = 2
_F32_BYTES = 4


def _vmem_limit(resident_bytes):
    assert resident_bytes + VMEM_TEMP_BYTES <= VMEM_LIMIT < VMEM_BYTES, resident_bytes
    return VMEM_LIMIT


def _const_spec(shape):
    nd = len(shape)
    return pl.BlockSpec(shape, lambda *_: (0,) * nd, pipeline_mode=pl.Buffered(1))


def _layer_norm(x, g, b):
    mu = jnp.mean(x, axis=-1, keepdims=True)
    xc = x - mu
    var = jnp.mean(xc * xc, axis=-1, keepdims=True)
    return xc * lax.rsqrt(var + LN_EPS) * g + b


def _stage_weights_bf16(pieces, stage_ref, sem):
    def copy(k):
        src = pieces[k][0]
        dst = stage_ref.at[k % 2, pl.ds(0, src.shape[0]), :]
        return pltpu.make_async_copy(src, dst, sem.at[k % 2])

    copy(0).start()
    for k, (src, dst) in enumerate(pieces):
        if k + 1 < len(pieces):
            copy(k + 1).start()
        copy(k).wait()
        dst[...] = stage_ref[k % 2, 0:src.shape[0], :].astype(dst.dtype)


def _dup_heads(y):
    lane = lax.broadcasted_iota(jnp.int32, y.shape, 1)
    swapped = pltpu.roll(y, HEAD_DIM, 1)
    return jnp.concatenate([jnp.where(lane < HEAD_DIM, y, swapped),
                            jnp.where(lane >= HEAD_DIM, y, swapped)], axis=1)


QKV_CHUNK = _C_G // 3


def _inproj_kernel(x_ref, g_ref, b_ref, bias_ref, cos_ref, sa_ref, sb_ref, win_hbm,
                   qa_ref, ka_ref, va_ref, qb_ref, kb_ref, vb_ref,
                   w_ref, stage_ref, sem):
    half = ROPE_DIM // 2

    @pl.when((pl.program_id(0) == 0) & (pl.program_id(1) == 0))
    def _():
        cols = lambda c: pl.ds(c * QKV_CHUNK, QKV_CHUNK)
        _stage_weights_bf16([(win_hbm.at[0, :, cols(c)], w_ref.at[:, cols(c)])
                             for c in range(_C_G // QKV_CHUNK)], stage_ref, sem)

    for r0 in range(0, x_ref.shape[0], SUB_TILE_ROWS):
        rows = slice(r0, r0 + SUB_TILE_ROWS)
        h = _layer_norm(x_ref[rows, :], g_ref[...], b_ref[...])
        hb = h.astype(jnp.bfloat16)

        def proj(c0, c1):
            return (jnp.dot(hb, w_ref[:, c0:c1], preferred_element_type=jnp.float32)
                    + bias_ref[:, c0:c1])

        def rotary(y):
            cos, sa, sb = cos_ref[rows, :], sa_ref[rows, :], sb_ref[rows, :]
            outs = []
            for c in range(y.shape[1] // LANES):
                yc = y[:, c * LANES:(c + 1) * LANES]
                outs.append(yc * cos
                            + pltpu.roll(yc, half, 1) * sa
                            + pltpu.roll(yc, LANES - half, 1) * sb)
            return outs[0] if len(outs) == 1 else jnp.concatenate(outs, axis=1)

        kv = proj(_C_KB, _C_G)
        kb_ref[rows, :] = _dup_heads(rotary(kv[:, :SWA_KV_WIDTH])).astype(kb_ref.dtype)
        vb_ref[rows, :] = _dup_heads(kv[:, SWA_KV_WIDTH:]).astype(vb_ref.dtype)
        qb_ref[rows, :] = (rotary(proj(_C_QB, _C_KB)) * ATTN_SCALE).astype(qb_ref.dtype)
        qa_ref[rows, :] = (proj(_C_QA, _C_KA) * ATTN_SCALE).astype(qa_ref.dtype)
        ka_ref[rows, :] = proj(_C_KA, _C_VA).astype(ka_ref.dtype)
        va_ref[rows, :] = proj(_C_VA, _C_QB).astype(va_ref.dtype)


def _inproj(x2, g, b, bias, cos_t, sa_t, sb_t, w_in, batch, seq):
    t = batch * seq
    tm = INPROJ_TILE_ROWS
    nblk = seq // tm
    tok = lambda bi, i: (bi * nblk + i, 0)
    pos = lambda bi, i: (i, 0)
    bf = jnp.bfloat16
    out_widths = (NA_WIDTH, NA_WIDTH, NA_WIDTH, SWA_Q_WIDTH, _KV_DUP, _KV_DUP)
    resident = (2 * tm * D_MODEL * _F32_BYTES + 2 * tm * sum(out_widths) * _BF16_BYTES
                + 2 * 3 * tm * LANES * _F32_BYTES + D_MODEL * _C_G * _BF16_BYTES
                + 2 * D_MODEL * QKV_CHUNK * _F32_BYTES)
    return pl.pallas_call(
        _inproj_kernel,
        grid=(batch, nblk),
        in_specs=[
            pl.BlockSpec((tm, D_MODEL), tok),
            _const_spec((1, D_MODEL)), _const_spec((1, D_MODEL)),
            _const_spec((1, _C_END)),
            pl.BlockSpec((tm, LANES), pos), pl.BlockSpec((tm, LANES), pos),
            pl.BlockSpec((tm, LANES), pos),
            pl.BlockSpec(memory_space=pl.ANY),
        ],
        out_specs=tuple(pl.BlockSpec((tm, w), tok) for w in out_widths),
        out_shape=tuple(jax.ShapeDtypeStruct((t, w), bf) for w in out_widths),
        scratch_shapes=[
            pltpu.VMEM((D_MODEL, _C_G), bf),
            pltpu.VMEM((2, D_MODEL, QKV_CHUNK), jnp.float32),
            pltpu.SemaphoreType.DMA((2,)),
        ],
        compiler_params=pltpu.CompilerParams(
            dimension_semantics=("arbitrary", "arbitrary"),
            vmem_limit_bytes=_vmem_limit(resident)),
        name="ln_inproj_rotary",
    )(x2, g, b, bias, cos_t, sa_t, sb_t, w_in)


def _rotary_tables(seq):
    half = ROPE_DIM // 2
    inv_freq = np.power(ROPE_THETA, -np.arange(0, ROPE_DIM, 2, dtype=np.float64) / ROPE_DIM)
    ang = np.arange(seq, dtype=np.float64)[:, None] * inv_freq[None, :]
    cos, sin = np.cos(ang), np.sin(ang)
    ones = np.ones((seq, HEAD_DIM - ROPE_DIM))
    zeros = np.zeros((seq, HEAD_DIM - ROPE_DIM))
    zh = np.zeros((seq, half))
    cos_h = np.concatenate([cos, cos, ones], axis=1)
    sa_h = np.concatenate([zh, sin, zeros], axis=1)
    sb_h = np.concatenate([-sin, zh, zeros], axis=1)
    tile = lambda a: jnp.asarray(np.tile(a, (1, HEADS_PER_VREG)), dtype=jnp.float32)
    return tile(cos_h), tile(sa_h), tile(sb_h)


def _stack_heads(q):
    lane = lax.broadcasted_iota(jnp.int32, q.shape, 1)
    zero = jnp.zeros_like(q)
    return jnp.concatenate([jnp.where(lane < HEAD_DIM, q, zero),
                            jnp.where(lane >= HEAD_DIM, q, zero)], axis=0)


def _unstack_heads(o2):
    m = o2.shape[0] // 2
    lane = lax.broadcasted_iota(jnp.int32, (m, LANES), 1)
    return jnp.where(lane < HEAD_DIM, o2[:m], o2[m:])


_NT = (((1,), (1,)), ((), ()))


def _na_bias_kernel(rpb_ref, o_ref):
    w = lax.broadcasted_iota(jnp.int32, (GRID_W, LANES), 0)
    lane = lax.broadcasted_iota(jnp.int32, (GRID_W, LANES), 1)
    c = lane % GRID_W
    col_start = jnp.clip(w - NA_KW // 2, 0, GRID_W - NA_KW)
    inside = (c >= col_start) & (c < col_start + NA_KW)
    lower = lane < GRID_W
    n_dr = 2 * NA_KH - 1
    for e in range(HEADS_PER_VREG):
        lo, hi = [], []
        for dr in range(n_dr):
            x = jnp.broadcast_to(rpb_ref[0, e, dr:dr + 1, :], (GRID_W, LANES))
            lo.append(pltpu.roll(x, LANES - (NA_KW - 1), 1, stride=1, stride_axis=0))
            hi.append(pltpu.roll(x, GRID_W - (NA_KW - 1), 1, stride=1, stride_axis=0))
        for v in range(NA_KH):
            for kk in range(NA_KH // HEADS_PER_VREG):
                dr_a = 2 * kk - v + NA_KH - 1
                tile = jnp.where(inside, jnp.where(lower, lo[dr_a], hi[dr_a + 1]), MASK_VALUE)
                o_ref[v, 0, e * GRID_W:(e + 1) * GRID_W, kk * LANES:(kk + 1) * LANES] = tile


def _na_bias_table(rpb):
    n_pairs = NA_HEADS // HEADS_PER_VREG
    n_dr, n_dc = rpb.shape[1], rpb.shape[2]
    dr_pad = pl.cdiv(n_dr, SUBLANES) * SUBLANES
    rp = jnp.pad(rpb, ((0, 0), (0, dr_pad - n_dr), (0, LANES - n_dc)))
    rp = rp.reshape(n_pairs, HEADS_PER_VREG, dr_pad, LANES)
    m2, win = HEADS_PER_VREG * GRID_W, NA_KH * GRID_W
    return pl.pallas_call(
        _na_bias_kernel,
        grid=(n_pairs,),
        in_specs=[pl.BlockSpec((1, HEADS_PER_VREG, dr_pad, LANES), lambda j: (j, 0, 0, 0))],
        out_specs=pl.BlockSpec((NA_KH, 1, m2, win), lambda j: (0, j, 0, 0)),
        out_shape=jax.ShapeDtypeStruct((NA_KH, n_pairs, m2, win), jnp.float32),
        compiler_params=pltpu.CompilerParams(dimension_semantics=("parallel",)),
        name="na_bias_table",
    )(rp)


def _na_kernel(q_ref, k_ref, v_ref, tbl_ref, o_ref, s_scr, *, n_rows):
    win = NA_KH * GRID_W
    ones = jnp.ones((win, LANES), jnp.bfloat16)

    def body(i, carry):
        def window(t):
            r = i * NA_ROWS_PER_ITER + t
            rs = jnp.clip(r - NA_KH // 2, 0, n_rows - NA_KH)
            return r - rs, pl.multiple_of(rs * GRID_W, GRID_W)

        def q_rows(t):
            return pl.ds(pl.multiple_of((i * NA_ROWS_PER_ITER + t) * GRID_W, GRID_W), GRID_W)

        def score(t):
            variant, start = window(t)
            lhs = _stack_heads(q_ref[0, q_rows(t), :])
            kw = k_ref[0, pl.ds(start, win), :]
            s = lax.dot_general(lhs, kw, _NT, preferred_element_type=jnp.float32)
            s_scr[t % NA_SCORE_SLOTS] = s + tbl_ref[variant, 0]

        for t in range(min(NA_LOOKAHEAD, NA_ROWS_PER_ITER)):
            score(t)
        for t in range(NA_ROWS_PER_ITER):
            if t + NA_LOOKAHEAD < NA_ROWS_PER_ITER:
                score(t + NA_LOOKAHEAD)
            s = s_scr[t % NA_SCORE_SLOTS]
            m = jnp.max(s, axis=-1, keepdims=True)
            p = jnp.exp(s - m)
            _, start = window(t)
            vw = jnp.concatenate([v_ref[0, pl.ds(start, win), :], ones], axis=1)
            o2 = jnp.dot(p.astype(jnp.bfloat16), vw, preferred_element_type=jnp.float32)
            o2 = o2[:, :LANES] / o2[:, LANES:]
            o_ref[0, q_rows(t), :] = _unstack_heads(o2).astype(o_ref.dtype)
        return carry

    lax.fori_loop(0, n_rows // NA_ROWS_PER_ITER, body, 0)


def _swa_mask_table():
    qi = np.arange(SWA_BLOCK)[:, None]
    kc = np.arange(3 * SWA_BLOCK)[None, :]
    tbl = [np.where(np.abs(kc - d * SWA_BLOCK - qi) <= SWA_WINDOW, 0.0, MASK_VALUE)
           for d in range(3)]
    return jnp.asarray(np.stack(tbl), dtype=jnp.float32)


def _swa_kernel(sink_ref, q_ref, k_ref, v_ref, mask_ref, o_ref, s_scr, *, seq):
    j = pl.program_id(1)
    blk = SWA_BLOCK
    win = 3 * blk
    row = lax.broadcasted_iota(jnp.int32, (2 * blk, 1), 0)
    sink_col = jnp.where(row < blk, sink_ref[2 * j], sink_ref[2 * j + 1])
    ones = jnp.ones((win, LANES), jnp.bfloat16)

    def body(i, carry):
        def window(u):
            n = i * SWA_BLOCKS_PER_ITER + u
            start = jnp.clip((n - 1) * blk, 0, seq - win)
            return n - start // blk, pl.multiple_of(start, blk)

        def q_rows(u):
            return pl.ds(pl.multiple_of((i * SWA_BLOCKS_PER_ITER + u) * blk, blk), blk)

        for u in range(SWA_BLOCKS_PER_ITER):
            variant, start = window(u)
            lhs = _stack_heads(q_ref[0, q_rows(u), :])
            kw = k_ref[0, pl.ds(start, win), :]
            s = lax.dot_general(lhs, kw, _NT, preferred_element_type=jnp.float32)
            mask = mask_ref[variant]
            s_scr[u] = s + jnp.concatenate([mask, mask], axis=0)
        for u in range(SWA_BLOCKS_PER_ITER):
            s = s_scr[u]
            m = jnp.maximum(jnp.max(s, axis=-1, keepdims=True), sink_col)
            e = jnp.exp(s - m)
            _, start = window(u)
            vw = jnp.concatenate([v_ref[0, pl.ds(start, win), :], ones], axis=1)
            o2 = jnp.dot(e.astype(jnp.bfloat16), vw, preferred_element_type=jnp.float32)
            o2 = o2[:, :LANES] / (o2[:, LANES:] + jnp.exp(sink_col - m))
            o_ref[0, q_rows(u), :] = _unstack_heads(o2).astype(o_ref.dtype)
        return carry

    lax.fori_loop(0, seq // (SWA_BLOCKS_PER_ITER * blk), body, 0)


def _attention_kernel(sink_ref, qa_ref, ka_ref, va_ref, tbl_ref, qb_ref, kb_ref, vb_ref,
                      mask_ref, ona_ref, oswa_ref, na_scr, swa_scr, *, n_rows, seq):
    _na_kernel(qa_ref, ka_ref, va_ref, tbl_ref, ona_ref, na_scr, n_rows=n_rows)
    _swa_kernel(sink_ref, qb_ref, kb_ref, vb_ref, mask_ref, oswa_ref, swa_scr, seq=seq)


def _attention(qa, ka, va, tbl, qb, kb, vb, sink):
    batch, seq, _ = qa.shape
    n_rows = seq // GRID_W
    n_pairs = NA_HEADS // HEADS_PER_VREG
    assert n_pairs == SWA_Q_HEADS // HEADS_PER_VREG
    pairs_per_group = n_pairs // SWA_KV_HEADS
    na_m, na_win = HEADS_PER_VREG * GRID_W, NA_KH * GRID_W
    swa_m, swa_win = HEADS_PER_VREG * SWA_BLOCK, 3 * SWA_BLOCK
    pair_block = pl.BlockSpec((1, seq, LANES), lambda b, j: (b, 0, j))
    kv_block = pl.BlockSpec((1, seq, LANES), lambda b, j: (b, 0, j // pairs_per_group))
    resident = (2 * 8 * seq * LANES * _BF16_BYTES + 2 * NA_KH * na_m * na_win * _F32_BYTES
                + 3 * SWA_BLOCK * swa_win * _F32_BYTES
                + NA_SCORE_SLOTS * na_m * na_win * _F32_BYTES
                + SWA_BLOCKS_PER_ITER * swa_m * swa_win * _F32_BYTES)
    out = jax.ShapeDtypeStruct(qa.shape, jnp.bfloat16)
    return pl.pallas_call(
        functools.partial(_attention_kernel, n_rows=n_rows, seq=seq),
        grid=(batch, n_pairs),
        in_specs=[pl.BlockSpec(memory_space=pltpu.SMEM),
                  pair_block, pair_block, pair_block,
                  pl.BlockSpec((NA_KH, 1, na_m, na_win), lambda b, j: (0, j, 0, 0)),
                  pair_block, kv_block, kv_block,
                  _const_spec((3, SWA_BLOCK, swa_win))],
        out_specs=(pair_block, pair_block),
        out_shape=(out, out),
        scratch_shapes=[pltpu.VMEM((NA_SCORE_SLOTS, na_m, na_win), jnp.float32),
                        pltpu.VMEM((SWA_BLOCKS_PER_ITER, swa_m, swa_win), jnp.float32)],
        compiler_params=pltpu.CompilerParams(
            dimension_semantics=("parallel", "parallel"),
            vmem_limit_bytes=_vmem_limit(resident)),
        name="attention",
    )(sink, qa, ka, va, tbl, qb, kb, vb, _swa_mask_table())


W_CHUNK = D_MODEL


def _tail_kernel(x_ref, g0_ref, b0_ref, na_ref, swa_ref, bin_ref, bout_ref, g1_ref, b1_ref,
                 bf1_ref, bf2_ref, g2_ref, b2_ref,
                 win_hbm, wna_hbm, wswa_hbm, wout_hbm, w1_hbm, w2_hbm,
                 o_ref,
                 wg_ref, wna_ref, wswa_ref, wout_ref, w1_ref, w2_ref, stage_ref, sem):
    @pl.when(pl.program_id(0) == 0)
    def _():
        cols = lambda c: pl.ds(c * W_CHUNK, W_CHUNK)
        pieces = [(win_hbm.at[0, :, pl.ds(_C_G + c * W_CHUNK, W_CHUNK)], wg_ref.at[:, cols(c)])
                  for c in range(2 * D_MODEL // W_CHUNK)]
        pieces += [(wna_hbm.at[0], wna_ref), (wswa_hbm.at[0], wswa_ref),
                   (wout_hbm.at[0], wout_ref)]
        pieces += [(w1_hbm.at[0, :, cols(c)], w1_ref.at[:, cols(c)])
                   for c in range(D_FF // W_CHUNK)]
        pieces += [(w2_hbm.at[0, cols(c), :], w2_ref.at[cols(c), :])
                   for c in range(D_FF // W_CHUNK)]
        _stage_weights_bf16(pieces, stage_ref, sem)

    h1 = []
    for r0 in range(0, x_ref.shape[0], SUB_TILE_ROWS):
        rows = slice(r0, r0 + SUB_TILE_ROWS)
        y_na = jnp.dot(na_ref[rows, :], wna_ref[...], preferred_element_type=jnp.float32)
        y_swa = jnp.dot(swa_ref[rows, :], wswa_ref[...], preferred_element_type=jnp.float32)
        h = _layer_norm(x_ref[rows, :], g0_ref[...], b0_ref[...])
        gates = (jnp.dot(h.astype(jnp.bfloat16), wg_ref[...],
                         preferred_element_type=jnp.float32) + bin_ref[:, _C_G:_C_END])
        mixed = (jax.nn.sigmoid(gates[:, :D_MODEL]) * y_na
                 + jax.nn.sigmoid(gates[:, D_MODEL:]) * y_swa)
        attn = jnp.dot(mixed.astype(jnp.bfloat16), wout_ref[...],
                       preferred_element_type=jnp.float32) + bout_ref[...]
        h1.append(_layer_norm(DEEPNORM_ALPHA * h + attn, g1_ref[...], b1_ref[...]))

    h = jnp.concatenate(h1, axis=0)
    hb = h.astype(jnp.bfloat16)
    acc = jnp.zeros(h.shape, jnp.float32)
    n_chunks = D_FF // FF_CHUNK
    for c in range(n_chunks):
        sl = slice(c * FF_CHUNK, (c + 1) * FF_CHUNK)
        if c == 0:
            u = jnp.concatenate(
                [jnp.dot(hs.astype(jnp.bfloat16), w1_ref[:, sl],
                         preferred_element_type=jnp.float32) for hs in h1], axis=0)
        else:
            u = jnp.dot(hb, w1_ref[:, sl], preferred_element_type=jnp.float32)
        u = jnp.square(jnp.maximum(u + bf1_ref[:, sl], 0.0)).astype(jnp.bfloat16)
        if c + 1 < n_chunks:
            acc = acc + jnp.dot(u, w2_ref[sl, :], preferred_element_type=jnp.float32)
        else:
            for r0 in range(0, h.shape[0], SUB_TILE_ROWS):
                rows = slice(r0, r0 + SUB_TILE_ROWS)
                z = acc[rows] + jnp.dot(u[rows], w2_ref[sl, :],
                                        preferred_element_type=jnp.float32)
                o_ref[rows, :] = _layer_norm(DEEPNORM_ALPHA * h[rows] + z + bf2_ref[...],
                                             g2_ref[...], b2_ref[...])


def _tail(x2, g0, b0, na, swa, b_in, bout, g1, b1, bf1, bf2, g2, b2,
          w_in, w_na, w_swa, w_out, w_ff1, w_ff2):
    t = x2.shape[0]
    tm = TAIL_TILE_ROWS
    tok = lambda i: (i, 0)
    vec = _const_spec((1, D_MODEL))
    hbm = pl.BlockSpec(memory_space=pl.ANY)
    bf = jnp.bfloat16
    weight_elems = (D_MODEL * 2 * D_MODEL + (NA_WIDTH + SWA_Q_WIDTH) * D_MODEL
                    + D_MODEL * D_MODEL + 2 * D_MODEL * D_FF)
    resident = (weight_elems * _BF16_BYTES + 2 * W_CHUNK * D_MODEL * _F32_BYTES
                + 2 * 2 * tm * D_MODEL * _F32_BYTES
                + 2 * tm * (NA_WIDTH + SWA_Q_WIDTH) * _BF16_BYTES)
    return pl.pallas_call(
        _tail_kernel,
        grid=(t // tm,),
        in_specs=[
            pl.BlockSpec((tm, D_MODEL), tok), vec, vec,
            pl.BlockSpec((tm, NA_WIDTH), tok), pl.BlockSpec((tm, SWA_Q_WIDTH), tok),
            _const_spec((1, _C_END)), vec, vec, vec,
            _const_spec((1, D_FF)), vec, vec, vec,
            hbm, hbm, hbm, hbm, hbm, hbm,
        ],
        out_specs=pl.BlockSpec((tm, D_MODEL), tok),
        out_shape=jax.ShapeDtypeStruct((t, D_MODEL), jnp.float32),
        scratch_shapes=[
            pltpu.VMEM((D_MODEL, 2 * D_MODEL), bf),
            pltpu.VMEM((NA_WIDTH, D_MODEL), bf), pltpu.VMEM((SWA_Q_WIDTH, D_MODEL), bf),
            pltpu.VMEM((D_MODEL, D_MODEL), bf),
            pltpu.VMEM((D_MODEL, D_FF), bf), pltpu.VMEM((D_FF, D_MODEL), bf),
            pltpu.VMEM((2, W_CHUNK, D_MODEL), jnp.float32),
            pltpu.SemaphoreType.DMA((2,)),
        ],
        compiler_params=pltpu.CompilerParams(
            dimension_semantics=("arbitrary",), vmem_limit_bytes=_vmem_limit(resident)),
        name="mix_ffn",
    )(x2, g0, b0, na, swa, b_in, bout, g1, b1, bf1, bf2, g2, b2,
      w_in, w_na, w_swa, w_out, w_ff1, w_ff2)


def kernel(x, ln0_g, ln0_b, w_in, b_in, na_rpb, swa_sink, w_branch_na, w_branch_swa,
           w_out, b_out, ln1_g, ln1_b, w_ff1, b_ff1, w_ff2, b_ff2, ln2_g, ln2_b):
    batch, seq, d = x.shape
    assert d == D_MODEL and w_in.shape == (DEPTH, D_MODEL, _C_END) and DEPTH == 1
    assert seq % INPROJ_TILE_ROWS == 0 and (batch * seq) % TAIL_TILE_ROWS == 0
    assert seq % (NA_ROWS_PER_ITER * GRID_W) == 0 and seq // GRID_W >= NA_KH
    assert seq % (SWA_BLOCKS_PER_ITER * SWA_BLOCK) == 0 and seq >= 3 * SWA_BLOCK
    row = lambda v: v.reshape(1, -1)
    x2 = x.reshape(batch * seq, d)
    b_in_row = row(b_in[0])

    cos_t, sa_t, sb_t = _rotary_tables(seq)
    qa, ka, va, qb, kb, vb = _inproj(
        x2, row(ln0_g), row(ln0_b), b_in_row, cos_t, sa_t, sb_t, w_in, batch, seq)

    three = lambda a: a.reshape(batch, seq, a.shape[-1])
    o_na, o_swa = _attention(three(qa), three(ka), three(va), _na_bias_table(na_rpb[0]),
                             three(qb), three(kb), three(vb), swa_sink[0])

    return _tail(x2, row(ln0_g), row(ln0_b), o_na.reshape(batch * seq, -1),
                 o_swa.reshape(batch * seq, -1), b_in_row, row(b_out[0]),
                 row(ln1_g[0]), row(ln1_b[0]), row(b_ff1[0]), row(b_ff2[0]),
                 row(ln2_g[0]), row(ln2_b[0]),
                 w_in, w_branch_na, w_branch_swa, w_out, w_ff1, w_ff2).reshape(batch, seq, d)
```

```python
import functools

import numpy as np
import jax
import jax.numpy as jnp
from jax import lax
from jax.experimental import pallas as pl
from jax.experimental.pallas import tpu as pltpu

D_MODEL = 1024
GRID_W = 64
HEAD_DIM = 64
NA_HEADS = 8
NA_KH = 8
NA_KW = 16
NA_WIDTH = NA_HEADS * HEAD_DIM
SWA_Q_HEADS = 8
SWA_KV_HEADS = 2
SWA_WINDOW = 128
SWA_BLOCK = 128
SWA_Q_WIDTH = SWA_Q_HEADS * HEAD_DIM
SWA_KV_WIDTH = SWA_KV_HEADS * HEAD_DIM
ROPE_THETA = 500000.0
ROPE_DIM = HEAD_DIM // 4
D_FF = 4 * D_MODEL
DEPTH = 1
DEEPNORM_ALPHA = (2.0 * DEPTH) ** 0.25
LN_EPS = 1e-5
MASK_VALUE = -1e30
ATTN_SCALE = HEAD_DIM ** -0.5

LANES = 128
SUBLANES = 8
HEADS_PER_VREG = LANES // HEAD_DIM
VMEM_BYTES = 64 * 1024 * 1024
VMEM_TEMP_BYTES = 12 * 1024 * 1024
VMEM_LIMIT = 56 * 1024 * 1024

INPROJ_TILE_ROWS = 1024
TAIL_TILE_ROWS = 512
SUB_TILE_ROWS = 256
FF_CHUNK = 1024
NA_ROWS_PER_ITER = 128
SWA_BLOCKS_PER_ITER = 32
NA_LOOKAHEAD = 3
NA_SCORE_SLOTS = NA_LOOKAHEAD + 1
SWA_LOOKAHEAD = 3
SWA_SCORE_SLOTS = SWA_LOOKAHEAD + 1

_C_QA = 0
_C_KA = _C_QA + NA_WIDTH
_C_VA = _C_KA + NA_WIDTH
_C_QB = _C_VA + NA_WIDTH
_C_KB = _C_QB + SWA_Q_WIDTH
_C_VB = _C_KB + SWA_KV_WIDTH
_C_G = _C_VB + SWA_KV_WIDTH
_C_END = _C_G + 2 * D_MODEL
_KV_DUP = SWA_KV_HEADS * LANES

_BF16_BYTES = 2
_F32_BYTES = 4


def _vmem_limit(resident_bytes):
    assert resident_bytes + VMEM_TEMP_BYTES <= VMEM_LIMIT < VMEM_BYTES, resident_bytes
    return VMEM_LIMIT


def _const_spec(shape):
    nd = len(shape)
    return pl.BlockSpec(shape, lambda *_: (0,) * nd, pipeline_mode=pl.Buffered(1))


def _layer_norm(x, g, b):
    mu = jnp.mean(x, axis=-1, keepdims=True)
    xc = x - mu
    var = jnp.mean(xc * xc, axis=-1, keepdims=True)
    return xc * lax.rsqrt(var + LN_EPS) * g + b


def _stage_weights_bf16(pieces, stage_ref, sem):
    def copy(k):
        src = pieces[k][0]
        dst = stage_ref.at[k % 2, pl.ds(0, src.shape[0]), :]
        return pltpu.make_async_copy(src, dst, sem.at[k % 2])

    copy(0).start()
    for k, (src, dst) in enumerate(pieces):
        if k + 1 < len(pieces):
            copy(k + 1).start()
        copy(k).wait()
        dst[...] = stage_ref[k % 2, 0:src.shape[0], :].astype(dst.dtype)


def _dup_heads(y):
    lane = lax.broadcasted_iota(jnp.int32, y.shape, 1)
    swapped = pltpu.roll(y, HEAD_DIM, 1)
    return jnp.concatenate([jnp.where(lane < HEAD_DIM, y, swapped),
                            jnp.where(lane >= HEAD_DIM, y, swapped)], axis=1)


QKV_CHUNK = _C_G // 3


def _inproj_kernel(x_ref, g_ref, b_ref, bias_ref, cos_ref, sa_ref, sb_ref, win_hbm,
                   qa_ref, ka_ref, va_ref, qb_ref, kb_ref, vb_ref,
                   w_ref, stage_ref, sem):
    half = ROPE_DIM // 2

    @pl.when((pl.program_id(0) == 0) & (pl.program_id(1) == 0))
    def _():
        cols = lambda c: pl.ds(c * QKV_CHUNK, QKV_CHUNK)
        _stage_weights_bf16([(win_hbm.at[0, :, cols(c)], w_ref.at[:, cols(c)])
                             for c in range(_C_G // QKV_CHUNK)], stage_ref, sem)

    for r0 in range(0, x_ref.shape[0], SUB_TILE_ROWS):
        rows = slice(r0, r0 + SUB_TILE_ROWS)
        h = _layer_norm(x_ref[rows, :], g_ref[...], b_ref[...])
        hb = h.astype(jnp.bfloat16)

        def proj(c0, c1):
            return (jnp.dot(hb, w_ref[:, c0:c1], preferred_element_type=jnp.float32)
                    + bias_ref[:, c0:c1])

        def rotary(y):
            cos, sa, sb = cos_ref[rows, :], sa_ref[rows, :], sb_ref[rows, :]
            outs = []
            for c in range(y.shape[1] // LANES):
                yc = y[:, c * LANES:(c + 1) * LANES]
                outs.append(yc * cos
                            + pltpu.roll(yc, half, 1) * sa
                            + pltpu.roll(yc, LANES - half, 1) * sb)
            return outs[0] if len(outs) == 1 else jnp.concatenate(outs, axis=1)

        kv = proj(_C_KB, _C_G)
        kb_ref[rows, :] = _dup_heads(rotary(kv[:, :SWA_KV_WIDTH])).astype(kb_ref.dtype)
        vb_ref[rows, :] = _dup_heads(kv[:, SWA_KV_WIDTH:]).astype(vb_ref.dtype)
        qb_ref[rows, :] = (rotary(proj(_C_QB, _C_KB)) * ATTN_SCALE).astype(qb_ref.dtype)
        qa_ref[rows, :] = (proj(_C_QA, _C_KA) * ATTN_SCALE).astype(qa_ref.dtype)
        ka_ref[rows, :] = proj(_C_KA, _C_VA).astype(ka_ref.dtype)
        va_ref[rows, :] = proj(_C_VA, _C_QB).astype(va_ref.dtype)


def _inproj(x2, g, b, bias, cos_t, sa_t, sb_t, w_in, batch, seq):
    t = batch * seq
    tm = INPROJ_TILE_ROWS
    nblk = seq // tm
    tok = lambda bi, i: (bi * nblk + i, 0)
    pos = lambda bi, i: (i, 0)
    bf = jnp.bfloat16
    out_widths = (NA_WIDTH, NA_WIDTH, NA_WIDTH, SWA_Q_WIDTH, _KV_DUP, _KV_DUP)
    resident = (2 * tm * D_MODEL * _F32_BYTES + 2 * tm * sum(out_widths) * _BF16_BYTES
                + 2 * 3 * tm * LANES * _F32_BYTES + D_MODEL * _C_G * _BF16_BYTES
                + 2 * D_MODEL * QKV_CHUNK * _F32_BYTES)
    return pl.pallas_call(
        _inproj_kernel,
        grid=(batch, nblk),
        in_specs=[
            pl.BlockSpec((tm, D_MODEL), tok),
            _const_spec((1, D_MODEL)), _const_spec((1, D_MODEL)),
            _const_spec((1, _C_END)),
            pl.BlockSpec((tm, LANES), pos), pl.BlockSpec((tm, LANES), pos),
            pl.BlockSpec((tm, LANES), pos),
            pl.BlockSpec(memory_space=pl.ANY),
        ],
        out_specs=tuple(pl.BlockSpec((tm, w), tok) for w in out_widths),
        out_shape=tuple(jax.ShapeDtypeStruct((t, w), bf) for w in out_widths),
        scratch_shapes=[
            pltpu.VMEM((D_MODEL, _C_G), bf),
            pltpu.VMEM((2, D_MODEL, QKV_CHUNK), jnp.float32),
            pltpu.SemaphoreType.DMA((2,)),
        ],
        compiler_params=pltpu.CompilerParams(
            dimension_semantics=("arbitrary", "arbitrary"),
            vmem_limit_bytes=_vmem_limit(resident)),
        name="ln_inproj_rotary",
    )(x2, g, b, bias, cos_t, sa_t, sb_t, w_in)


def _rotary_tables(seq):
    half = ROPE_DIM // 2
    inv_freq = np.power(ROPE_THETA, -np.arange(0, ROPE_DIM, 2, dtype=np.float64) / ROPE_DIM)
    ang = np.arange(seq, dtype=np.float64)[:, None] * inv_freq[None, :]
    cos, sin = np.cos(ang), np.sin(ang)
    ones = np.ones((seq, HEAD_DIM - ROPE_DIM))
    zeros = np.zeros((seq, HEAD_DIM - ROPE_DIM))
    zh = np.zeros((seq, half))
    cos_h = np.concatenate([cos, cos, ones], axis=1)
    sa_h = np.concatenate([zh, sin, zeros], axis=1)
    sb_h = np.concatenate([-sin, zh, zeros], axis=1)
    tile = lambda a: jnp.asarray(np.tile(a, (1, HEADS_PER_VREG)), dtype=jnp.float32)
    return tile(cos_h), tile(sa_h), tile(sb_h)


def _stack_heads(q):
    lane = lax.broadcasted_iota(jnp.int32, q.shape, 1)
    zero = jnp.zeros_like(q)
    return jnp.concatenate([jnp.where(lane < HEAD_DIM, q, zero),
                            jnp.where(lane >= HEAD_DIM, q, zero)], axis=0)


def _unstack_heads(o2):
    m = o2.shape[0] // 2
    lane = lax.broadcasted_iota(jnp.int32, (m, LANES), 1)
    return jnp.where(lane < HEAD_DIM, o2[:m], o2[m:])


_NT = (((1,), (1,)), ((), ()))


def _na_bias_kernel(rpb_ref, o_ref):
    w = lax.broadcasted_iota(jnp.int32, (GRID_W, LANES), 0)
    lane = lax.broadcasted_iota(jnp.int32, (GRID_W, LANES), 1)
    c = lane % GRID_W
    col_start = jnp.clip(w - NA_KW // 2, 0, GRID_W - NA_KW)
    inside = (c >= col_start) & (c < col_start + NA_KW)
    lower = lane < GRID_W
    n_dr = 2 * NA_KH - 1
    for e in range(HEADS_PER_VREG):
        lo, hi = [], []
        for dr in range(n_dr):
            x = jnp.broadcast_to(rpb_ref[0, e, dr:dr + 1, :], (GRID_W, LANES))
            lo.append(pltpu.roll(x, LANES - (NA_KW - 1), 1, stride=1, stride_axis=0))
            hi.append(pltpu.roll(x, GRID_W - (NA_KW - 1), 1, stride=1, stride_axis=0))
        for v in range(NA_KH):
            for kk in range(NA_KH // HEADS_PER_VREG):
                dr_a = 2 * kk - v + NA_KH - 1
                tile = jnp.where(inside, jnp.where(lower, lo[dr_a], hi[dr_a + 1]), MASK_VALUE)
                o_ref[v, 0, e * GRID_W:(e + 1) * GRID_W, kk * LANES:(kk + 1) * LANES] = tile


def _na_bias_table(rpb):
    n_pairs = NA_HEADS // HEADS_PER_VREG
    n_dr, n_dc = rpb.shape[1], rpb.shape[2]
    dr_pad = pl.cdiv(n_dr, SUBLANES) * SUBLANES
    rp = jnp.pad(rpb, ((0, 0), (0, dr_pad - n_dr), (0, LANES - n_dc)))
    rp = rp.reshape(n_pairs, HEADS_PER_VREG, dr_pad, LANES)
    m2, win = HEADS_PER_VREG * GRID_W, NA_KH * GRID_W
    return pl.pallas_call(
        _na_bias_kernel,
        grid=(n_pairs,),
        in_specs=[pl.BlockSpec((1, HEADS_PER_VREG, dr_pad, LANES), lambda j: (j, 0, 0, 0))],
        out_specs=pl.BlockSpec((NA_KH, 1, m2, win), lambda j: (0, j, 0, 0)),
        out_shape=jax.ShapeDtypeStruct((NA_KH, n_pairs, m2, win), jnp.float32),
        compiler_params=pltpu.CompilerParams(dimension_semantics=("parallel",)),
        name="na_bias_table",
    )(rp)


def _na_kernel(q_ref, k_ref, v_ref, tbl_ref, o_ref, s_scr, *, n_rows):
    win = NA_KH * GRID_W
    ones = jnp.ones((win, LANES), jnp.bfloat16)

    def body(i, carry):
        def window(t):
            r = i * NA_ROWS_PER_ITER + t
            rs = jnp.clip(r - NA_KH // 2, 0, n_rows - NA_KH)
            return r - rs, pl.multiple_of(rs * GRID_W, GRID_W)

        def q_rows(t):
            return pl.ds(pl.multiple_of((i * NA_ROWS_PER_ITER + t) * GRID_W, GRID_W), GRID_W)

        def score(t):
            variant, start = window(t)
            lhs = _stack_heads(q_ref[0, q_rows(t), :])
            kw = k_ref[0, pl.ds(start, win), :]
            s = lax.dot_general(lhs, kw, _NT, preferred_element_type=jnp.float32)
            s_scr[t % NA_SCORE_SLOTS] = s + tbl_ref[variant, 0]

        for t in range(min(NA_LOOKAHEAD, NA_ROWS_PER_ITER)):
            score(t)
        for t in range(NA_ROWS_PER_ITER):
            if t + NA_LOOKAHEAD < NA_ROWS_PER_ITER:
                score(t + NA_LOOKAHEAD)
            s = s_scr[t % NA_SCORE_SLOTS]
            m = jnp.max(s, axis=-1, keepdims=True)
            p = jnp.exp(s - m)
            _, start = window(t)
            vw = jnp.concatenate([v_ref[0, pl.ds(start, win), :], ones], axis=1)
            o2 = jnp.dot(p.astype(jnp.bfloat16), vw, preferred_element_type=jnp.float32)
            o2 = o2[:, :LANES] / o2[:, LANES:]
            o_ref[0, q_rows(t), :] = _unstack_heads(o2).astype(o_ref.dtype)
        return carry

    lax.fori_loop(0, n_rows // NA_ROWS_PER_ITER, body, 0)


def _swa_mask_table():
    qi = np.arange(SWA_BLOCK)[:, None]
    kc = np.arange(3 * SWA_BLOCK)[None, :]
    tbl = [np.where(np.abs(kc - d * SWA_BLOCK - qi) <= SWA_WINDOW, 0.0, MASK_VALUE)
           for d in range(3)]
    return jnp.asarray(np.stack(tbl), dtype=jnp.float32)


def _swa_kernel(sink_ref, q_ref, k_ref, v_ref, mask_ref, o_ref, s_scr, *, seq):
    j = pl.program_id(1)
    blk = SWA_BLOCK
    win = 3 * blk
    row = lax.broadcasted_iota(jnp.int32, (2 * blk, 1), 0)
    sink_col = jnp.where(row < blk, sink_ref[2 * j], sink_ref[2 * j + 1])
    ones = jnp.ones((win, LANES), jnp.bfloat16)

    def body(i, carry):
        def window(u):
            n = i * SWA_BLOCKS_PER_ITER + u
            start = jnp.clip((n - 1) * blk, 0, seq - win)
            return n - start // blk, pl.multiple_of(start, blk)

        def q_rows(u):
            return pl.ds(pl.multiple_of((i * SWA_BLOCKS_PER_ITER + u) * blk, blk), blk)

        def score(u):
            variant, start = window(u)
            lhs = _stack_heads(q_ref[0, q_rows(u), :])
            kw = k_ref[0, pl.ds(start, win), :]
            s = lax.dot_general(lhs, kw, _NT, preferred_element_type=jnp.float32)
            mask = mask_ref[variant]
            s_scr[u % SWA_SCORE_SLOTS] = s + jnp.concatenate([mask, mask], axis=0)

        for u in range(min(SWA_LOOKAHEAD, SWA_BLOCKS_PER_ITER)):
            score(u)
        for u in range(SWA_BLOCKS_PER_ITER):
            if u + SWA_LOOKAHEAD < SWA_BLOCKS_PER_ITER:
                score(u + SWA_LOOKAHEAD)
            s = s_scr[u % SWA_SCORE_SLOTS]
            m = jnp.maximum(jnp.max(s, axis=-1, keepdims=True), sink_col)
            e = jnp.exp(s - m)
            _, start = window(u)
            vw = jnp.concatenate([v_ref[0, pl.ds(start, win), :], ones], axis=1)
            o2 = jnp.dot(e.astype(jnp.bfloat16), vw, preferred_element_type=jnp.float32)
            o2 = o2[:, :LANES] / (o2[:, LANES:] + jnp.exp(sink_col - m))
            o_ref[0, q_rows(u), :] = _unstack_heads(o2).astype(o_ref.dtype)
        return carry

    lax.fori_loop(0, seq // (SWA_BLOCKS_PER_ITER * blk), body, 0)


def _attention_kernel(sink_ref, qa_ref, ka_ref, va_ref, tbl_ref, qb_ref, kb_ref, vb_ref,
                      mask_ref, ona_ref, oswa_ref, na_scr, swa_scr, *, n_rows, seq):
    _na_kernel(qa_ref, ka_ref, va_ref, tbl_ref, ona_ref, na_scr, n_rows=n_rows)
    _swa_kernel(sink_ref, qb_ref, kb_ref, vb_ref, mask_ref, oswa_ref, swa_scr, seq=seq)


def _attention(qa, ka, va, tbl, qb, kb, vb, sink):
    batch, seq, _ = qa.shape
    n_rows = seq // GRID_W
    n_pairs = NA_HEADS // HEADS_PER_VREG
    assert n_pairs == SWA_Q_HEADS // HEADS_PER_VREG
    pairs_per_group = n_pairs // SWA_KV_HEADS
    na_m, na_win = HEADS_PER_VREG * GRID_W, NA_KH * GRID_W
    swa_m, swa_win = HEADS_PER_VREG * SWA_BLOCK, 3 * SWA_BLOCK
    pair_block = pl.BlockSpec((1, seq, LANES), lambda b, j: (b, 0, j))
    kv_block = pl.BlockSpec((1, seq, LANES), lambda b, j: (b, 0, j // pairs_per_group))
    resident = (2 * 8 * seq * LANES * _BF16_BYTES + 2 * NA_KH * na_m * na_win * _F32_BYTES
                + 3 * SWA_BLOCK * swa_win * _F32_BYTES
                + NA_SCORE_SLOTS * na_m * na_win * _F32_BYTES
                + SWA_SCORE_SLOTS * swa_m * swa_win * _F32_BYTES)
    out = jax.ShapeDtypeStruct(qa.shape, jnp.bfloat16)
    return pl.pallas_call(
        functools.partial(_attention_kernel, n_rows=n_rows, seq=seq),
        grid=(batch, n_pairs),
        in_specs=[pl.BlockSpec(memory_space=pltpu.SMEM),
                  pair_block, pair_block, pair_block,
                  pl.BlockSpec((NA_KH, 1, na_m, na_win), lambda b, j: (0, j, 0, 0)),
                  pair_block, kv_block, kv_block,
                  _const_spec((3, SWA_BLOCK, swa_win))],
        out_specs=(pair_block, pair_block),
        out_shape=(out, out),
        scratch_shapes=[pltpu.VMEM((NA_SCORE_SLOTS, na_m, na_win), jnp.float32),
                        pltpu.VMEM((SWA_SCORE_SLOTS, swa_m, swa_win), jnp.float32)],
        compiler_params=pltpu.CompilerParams(
            dimension_semantics=("parallel", "parallel"),
            vmem_limit_bytes=_vmem_limit(resident)),
        name="attention",
    )(sink, qa, ka, va, tbl, qb, kb, vb, _swa_mask_table())


W_CHUNK = D_MODEL


def _tail_kernel(x_ref, g0_ref, b0_ref, na_ref, swa_ref, bin_ref, bout_ref, g1_ref, b1_ref,
                 bf1_ref, bf2_ref, g2_ref, b2_ref,
                 win_hbm, wna_hbm, wswa_hbm, wout_hbm, w1_hbm, w2_hbm,
                 o_ref,
                 wg_ref, wna_ref, wswa_ref, wout_ref, w1_ref, w2_ref, stage_ref, sem):
    @pl.when(pl.program_id(0) == 0)
    def _():
        cols = lambda c: pl.ds(c * W_CHUNK, W_CHUNK)
        pieces = [(win_hbm.at[0, :, pl.ds(_C_G + c * W_CHUNK, W_CHUNK)], wg_ref.at[:, cols(c)])
                  for c in range(2 * D_MODEL // W_CHUNK)]
        pieces += [(wna_hbm.at[0], wna_ref), (wswa_hbm.at[0], wswa_ref),
                   (wout_hbm.at[0], wout_ref)]
        pieces += [(w1_hbm.at[0, :, cols(c)], w1_ref.at[:, cols(c)])
                   for c in range(D_FF // W_CHUNK)]
        pieces += [(w2_hbm.at[0, cols(c), :], w2_ref.at[cols(c), :])
                   for c in range(D_FF // W_CHUNK)]
        _stage_weights_bf16(pieces, stage_ref, sem)

    h1 = []
    for r0 in range(0, x_ref.shape[0], SUB_TILE_ROWS):
        rows = slice(r0, r0 + SUB_TILE_ROWS)
        y_na = jnp.dot(na_ref[rows, :], wna_ref[...], preferred_element_type=jnp.float32)
        y_swa = jnp.dot(swa_ref[rows, :], wswa_ref[...], preferred_element_type=jnp.float32)
        h = _layer_norm(x_ref[rows, :], g0_ref[...], b0_ref[...])
        gates = (jnp.dot(h.astype(jnp.bfloat16), wg_ref[...],
                         preferred_element_type=jnp.float32) + bin_ref[:, _C_G:_C_END])
        mixed = (jax.nn.sigmoid(gates[:, :D_MODEL]) * y_na
                 + jax.nn.sigmoid(gates[:, D_MODEL:]) * y_swa)
        attn = jnp.dot(mixed.astype(jnp.bfloat16), wout_ref[...],
                       preferred_element_type=jnp.float32) + bout_ref[...]
        h1.append(_layer_norm(DEEPNORM_ALPHA * h + attn, g1_ref[...], b1_ref[...]))

    h = jnp.concatenate(h1, axis=0)
    hb = h.astype(jnp.bfloat16)
    acc = jnp.zeros(h.shape, jnp.float32)
    n_chunks = D_FF // FF_CHUNK
    for c in range(n_chunks):
        sl = slice(c * FF_CHUNK, (c + 1) * FF_CHUNK)
        if c == 0:
            u = jnp.concatenate(
                [jnp.dot(hs.astype(jnp.bfloat16), w1_ref[:, sl],
                         preferred_element_type=jnp.float32) for hs in h1], axis=0)
        else:
            u = jnp.dot(hb, w1_ref[:, sl], preferred_element_type=jnp.float32)
        u = jnp.square(jnp.maximum(u + bf1_ref[:, sl], 0.0)).astype(jnp.bfloat16)
        if c + 1 < n_chunks:
            acc = acc + jnp.dot(u, w2_ref[sl, :], preferred_element_type=jnp.float32)
        else:
            for r0 in range(0, h.shape[0], SUB_TILE_ROWS):
                rows = slice(r0, r0 + SUB_TILE_ROWS)
                z = acc[rows] + jnp.dot(u[rows], w2_ref[sl, :],
                                        preferred_element_type=jnp.float32)
                o_ref[rows, :] = _layer_norm(DEEPNORM_ALPHA * h[rows] + z + bf2_ref[...],
                                             g2_ref[...], b2_ref[...])


def _tail(x2, g0, b0, na, swa, b_in, bout, g1, b1, bf1, bf2, g2, b2,
          w_in, w_na, w_swa, w_out, w_ff1, w_ff2):
    t = x2.shape[0]
    tm = TAIL_TILE_ROWS
    tok = lambda i: (i, 0)
    vec = _const_spec((1, D_MODEL))
    hbm = pl.BlockSpec(memory_space=pl.ANY)
    bf = jnp.bfloat16
    weight_elems = (D_MODEL * 2 * D_MODEL + (NA_WIDTH + SWA_Q_WIDTH) * D_MODEL
                    + D_MODEL * D_MODEL + 2 * D_MODEL * D_FF)
    resident = (weight_elems * _BF16_BYTES + 2 * W_CHUNK * D_MODEL * _F32_BYTES
                + 2 * 2 * tm * D_MODEL * _F32_BYTES
                + 2 * tm * (NA_WIDTH + SWA_Q_WIDTH) * _BF16_BYTES)
    return pl.pallas_call(
        _tail_kernel,
        grid=(t // tm,),
        in_specs=[
            pl.BlockSpec((tm, D_MODEL), tok), vec, vec,
            pl.BlockSpec((tm, NA_WIDTH), tok), pl.BlockSpec((tm, SWA_Q_WIDTH), tok),
            _const_spec((1, _C_END)), vec, vec, vec,
            _const_spec((1, D_FF)), vec, vec, vec,
            hbm, hbm, hbm, hbm, hbm, hbm,
        ],
        out_specs=pl.BlockSpec((tm, D_MODEL), tok),
        out_shape=jax.ShapeDtypeStruct((t, D_MODEL), jnp.float32),
        scratch_shapes=[
            pltpu.VMEM((D_MODEL, 2 * D_MODEL), bf),
            pltpu.VMEM((NA_WIDTH, D_MODEL), bf), pltpu.VMEM((SWA_Q_WIDTH, D_MODEL), bf),
            pltpu.VMEM((D_MODEL, D_MODEL), bf),
            pltpu.VMEM((D_MODEL, D_FF), bf), pltpu.VMEM((D_FF, D_MODEL), bf),
            pltpu.VMEM((2, W_CHUNK, D_MODEL), jnp.float32),
            pltpu.SemaphoreType.DMA((2,)),
        ],
        compiler_params=pltpu.CompilerParams(
            dimension_semantics=("arbitrary",), vmem_limit_bytes=_vmem_limit(resident)),
        name="mix_ffn",
    )(x2, g0, b0, na, swa, b_in, bout, g1, b1, bf1, bf2, g2, b2,
      w_in, w_na, w_swa, w_out, w_ff1, w_ff2)


def kernel(x, ln0_g, ln0_b, w_in, b_in, na_rpb, swa_sink, w_branch_na, w_branch_swa,
           w_out, b_out, ln1_g, ln1_b, w_ff1, b_ff1, w_ff2, b_ff2, ln2_g, ln2_b):
    batch, seq, d = x.shape
    assert d == D_MODEL and w_in.shape == (DEPTH, D_MODEL, _C_END) and DEPTH == 1
    assert seq % INPROJ_TILE_ROWS == 0 and (batch * seq) % TAIL_TILE_ROWS == 0
    assert seq % (NA_ROWS_PER_ITER * GRID_W) == 0 and seq // GRID_W >= NA_KH
    assert seq % (SWA_BLOCKS_PER_ITER * SWA_BLOCK) == 0 and seq >= 3 * SWA_BLOCK
    row = lambda v: v.reshape(1, -1)
    x2 = x.reshape(batch * seq, d)
    b_in_row = row(b_in[0])

    cos_t, sa_t, sb_t = _rotary_tables(seq)
    qa, ka, va, qb, kb, vb = _inproj(
        x2, row(ln0_g), row(ln0_b), b_in_row, cos_t, sa_t, sb_t, w_in, batch, seq)

    three = lambda a: a.reshape(batch, seq, a.shape[-1])
    o_na, o_swa = _attention(three(qa), three(ka), three(va), _na_bias_table(na_rpb[0]),
                             three(qb), three(kb), three(vb), swa_sink[0])

    return _tail(x2, row(ln0_g), row(ln0_b), o_na.reshape(batch * seq, -1),
                 o_swa.reshape(batch * seq, -1), b_in_row, row(b_out[0]),
                 row(ln1_g[0]), row(ln1_b[0]), row(b_ff1[0]), row(b_ff2[0]),
                 row(ln2_g[0]), row(ln2_b[0]),
                 w_in, w_branch_na, w_branch_swa, w_out, w_ff1, w_ff2).reshape(batch, seq, d)
```

```python
import functools

import numpy as np
import jax
import jax.numpy as jnp
from jax import lax
from jax.experimental import pallas as pl
from jax.experimental.pallas import tpu as pltpu

D_MODEL = 1024
GRID_W = 64
HEAD_DIM = 64
NA_HEADS = 8
NA_KH = 8
NA_KW = 16
NA_WIDTH = NA_HEADS * HEAD_DIM
SWA_Q_HEADS = 8
SWA_KV_HEADS = 2
SWA_WINDOW = 128
SWA_BLOCK = 128
SWA_Q_WIDTH = SWA_Q_HEADS * HEAD_DIM
SWA_KV_WIDTH = SWA_KV_HEADS * HEAD_DIM
ROPE_THETA = 500000.0
ROPE_DIM = HEAD_DIM // 4
D_FF = 4 * D_MODEL
DEPTH = 1
DEEPNORM_ALPHA = (2.0 * DEPTH) ** 0.25
LN_EPS = 1e-5
MASK_VALUE = -1e30
LOG2E = 1.4426950408889634
ATTN_SCALE = HEAD_DIM ** -0.5 * LOG2E

LANES = 128
SUBLANES = 8
HEADS_PER_VREG = LANES // HEAD_DIM
VMEM_BYTES = 64 * 1024 * 1024
VMEM_TEMP_BYTES = 12 * 1024 * 1024
VMEM_LIMIT = 56 * 1024 * 1024

INPROJ_TILE_ROWS = 1024
TAIL_TILE_ROWS = 512
SUB_TILE_ROWS = 256
FF_CHUNK = 1024
NA_ROWS_PER_ITER = 128
SWA_BLOCKS_PER_ITER = 32
NA_LOOKAHEAD = 3
NA_SCORE_SLOTS = NA_LOOKAHEAD + 1
SWA_LOOKAHEAD = 3
SWA_SCORE_SLOTS = SWA_LOOKAHEAD + 1

_C_QA = 0
_C_KA = _C_QA + NA_WIDTH
_C_VA = _C_KA + NA_WIDTH
_C_QB = _C_VA + NA_WIDTH
_C_KB = _C_QB + SWA_Q_WIDTH
_C_VB = _C_KB + SWA_KV_WIDTH
_C_G = _C_VB + SWA_KV_WIDTH
_C_END = _C_G + 2 * D_MODEL
_KV_DUP = SWA_KV_HEADS * LANES

_BF16_BYTES = 2
_F32_BYTES = 4


def _vmem_limit(resident_bytes):
    assert resident_bytes + VMEM_TEMP_BYTES <= VMEM_LIMIT < VMEM_BYTES, resident_bytes
    return VMEM_LIMIT


def _const_spec(shape):
    nd = len(shape)
    return pl.BlockSpec(shape, lambda *_: (0,) * nd, pipeline_mode=pl.Buffered(1))


def _layer_norm(x, g, b):
    mu = jnp.mean(x, axis=-1, keepdims=True)
    xc = x - mu
    var = jnp.mean(xc * xc, axis=-1, keepdims=True)
    return xc * lax.rsqrt(var + LN_EPS) * g + b


def _stage_weights_bf16(pieces, stage_ref, sem):
    def copy(k):
        src = pieces[k][0]
        dst = stage_ref.at[k % 2, pl.ds(0, src.shape[0]), :]
        return pltpu.make_async_copy(src, dst, sem.at[k % 2])

    copy(0).start()
    for k, (src, dst) in enumerate(pieces):
        if k + 1 < len(pieces):
            copy(k + 1).start()
        copy(k).wait()
        dst[...] = stage_ref[k % 2, 0:src.shape[0], :].astype(dst.dtype)


def _dup_heads(y):
    lane = lax.broadcasted_iota(jnp.int32, y.shape, 1)
    swapped = pltpu.roll(y, HEAD_DIM, 1)
    return jnp.concatenate([jnp.where(lane < HEAD_DIM, y, swapped),
                            jnp.where(lane >= HEAD_DIM, y, swapped)], axis=1)


QKV_CHUNK = _C_G // 3


def _inproj_kernel(x_ref, g_ref, b_ref, bias_ref, cos_ref, sa_ref, sb_ref, win_hbm,
                   qa_ref, ka_ref, va_ref, qb_ref, kb_ref, vb_ref,
                   w_ref, stage_ref, sem):
    half = ROPE_DIM // 2

    @pl.when((pl.program_id(0) == 0) & (pl.program_id(1) == 0))
    def _():
        cols = lambda c: pl.ds(c * QKV_CHUNK, QKV_CHUNK)
        _stage_weights_bf16([(win_hbm.at[0, :, cols(c)], w_ref.at[:, cols(c)])
                             for c in range(_C_G // QKV_CHUNK)], stage_ref, sem)

    for r0 in range(0, x_ref.shape[0], SUB_TILE_ROWS):
        rows = slice(r0, r0 + SUB_TILE_ROWS)
        h = _layer_norm(x_ref[rows, :], g_ref[...], b_ref[...])
        hb = h.astype(jnp.bfloat16)

        def proj(c0, c1):
            return (jnp.dot(hb, w_ref[:, c0:c1], preferred_element_type=jnp.float32)
                    + bias_ref[:, c0:c1])

        def rotary(y):
            cos, sa, sb = cos_ref[rows, :], sa_ref[rows, :], sb_ref[rows, :]
            outs = []
            for c in range(y.shape[1] // LANES):
                yc = y[:, c * LANES:(c + 1) * LANES]
                outs.append(yc * cos
                            + pltpu.roll(yc, half, 1) * sa
                            + pltpu.roll(yc, LANES - half, 1) * sb)
            return outs[0] if len(outs) == 1 else jnp.concatenate(outs, axis=1)

        kv = proj(_C_KB, _C_G)
        kb_ref[rows, :] = _dup_heads(rotary(kv[:, :SWA_KV_WIDTH])).astype(kb_ref.dtype)
        vb_ref[rows, :] = _dup_heads(kv[:, SWA_KV_WIDTH:]).astype(vb_ref.dtype)
        qb_ref[rows, :] = (rotary(proj(_C_QB, _C_KB)) * ATTN_SCALE).astype(qb_ref.dtype)
        qa_ref[rows, :] = (proj(_C_QA, _C_KA) * ATTN_SCALE).astype(qa_ref.dtype)
        ka_ref[rows, :] = proj(_C_KA, _C_VA).astype(ka_ref.dtype)
        va_ref[rows, :] = proj(_C_VA, _C_QB).astype(va_ref.dtype)


def _inproj(x2, g, b, bias, cos_t, sa_t, sb_t, w_in, batch, seq):
    t = batch * seq
    tm = INPROJ_TILE_ROWS
    nblk = seq // tm
    tok = lambda bi, i: (bi * nblk + i, 0)
    pos = lambda bi, i: (i, 0)
    bf = jnp.bfloat16
    out_widths = (NA_WIDTH, NA_WIDTH, NA_WIDTH, SWA_Q_WIDTH, _KV_DUP, _KV_DUP)
    resident = (2 * tm * D_MODEL * _F32_BYTES + 2 * tm * sum(out_widths) * _BF16_BYTES
                + 2 * 3 * tm * LANES * _F32_BYTES + D_MODEL * _C_G * _BF16_BYTES
                + 2 * D_MODEL * QKV_CHUNK * _F32_BYTES)
    return pl.pallas_call(
        _inproj_kernel,
        grid=(batch, nblk),
        in_specs=[
            pl.BlockSpec((tm, D_MODEL), tok),
            _const_spec((1, D_MODEL)), _const_spec((1, D_MODEL)),
            _const_spec((1, _C_END)),
            pl.BlockSpec((tm, LANES), pos), pl.BlockSpec((tm, LANES), pos),
            pl.BlockSpec((tm, LANES), pos),
            pl.BlockSpec(memory_space=pl.ANY),
        ],
        out_specs=tuple(pl.BlockSpec((tm, w), tok) for w in out_widths),
        out_shape=tuple(jax.ShapeDtypeStruct((t, w), bf) for w in out_widths),
        scratch_shapes=[
            pltpu.VMEM((D_MODEL, _C_G), bf),
            pltpu.VMEM((2, D_MODEL, QKV_CHUNK), jnp.float32),
            pltpu.SemaphoreType.DMA((2,)),
        ],
        compiler_params=pltpu.CompilerParams(
            dimension_semantics=("arbitrary", "arbitrary"),
            vmem_limit_bytes=_vmem_limit(resident)),
        name="ln_inproj_rotary",
    )(x2, g, b, bias, cos_t, sa_t, sb_t, w_in)


def _rotary_tables(seq):
    half = ROPE_DIM // 2
    inv_freq = np.power(ROPE_THETA, -np.arange(0, ROPE_DIM, 2, dtype=np.float64) / ROPE_DIM)
    ang = np.arange(seq, dtype=np.float64)[:, None] * inv_freq[None, :]
    cos, sin = np.cos(ang), np.sin(ang)
    ones = np.ones((seq, HEAD_DIM - ROPE_DIM))
    zeros = np.zeros((seq, HEAD_DIM - ROPE_DIM))
    zh = np.zeros((seq, half))
    cos_h = np.concatenate([cos, cos, ones], axis=1)
    sa_h = np.concatenate([zh, sin, zeros], axis=1)
    sb_h = np.concatenate([-sin, zh, zeros], axis=1)
    tile = lambda a: jnp.asarray(np.tile(a, (1, HEADS_PER_VREG)), dtype=jnp.float32)
    return tile(cos_h), tile(sa_h), tile(sb_h)


def _stack_heads(q):
    lane = lax.broadcasted_iota(jnp.int32, q.shape, 1)
    zero = jnp.zeros_like(q)
    return jnp.concatenate([jnp.where(lane < HEAD_DIM, q, zero),
                            jnp.where(lane >= HEAD_DIM, q, zero)], axis=0)


def _unstack_heads(o2):
    m = o2.shape[0] // 2
    lane = lax.broadcasted_iota(jnp.int32, (m, LANES), 1)
    return jnp.where(lane < HEAD_DIM, o2[:m], o2[m:])


_NT = (((1,), (1,)), ((), ()))


def _na_bias_kernel(rpb_ref, o_ref):
    w = lax.broadcasted_iota(jnp.int32, (GRID_W, LANES), 0)
    lane = lax.broadcasted_iota(jnp.int32, (GRID_W, LANES), 1)
    c = lane % GRID_W
    col_start = jnp.clip(w - NA_KW // 2, 0, GRID_W - NA_KW)
    inside = (c >= col_start) & (c < col_start + NA_KW)
    lower = lane < GRID_W
    n_dr = 2 * NA_KH - 1
    for e in range(HEADS_PER_VREG):
        lo, hi = [], []
        for dr in range(n_dr):
            x = jnp.broadcast_to(rpb_ref[0, e, dr:dr + 1, :], (GRID_W, LANES))
            lo.append(pltpu.roll(x, LANES - (NA_KW - 1), 1, stride=1, stride_axis=0))
            hi.append(pltpu.roll(x, GRID_W - (NA_KW - 1), 1, stride=1, stride_axis=0))
        for v in range(NA_KH):
            for kk in range(NA_KH // HEADS_PER_VREG):
                dr_a = 2 * kk - v + NA_KH - 1
                bias = jnp.where(lower, lo[dr_a], hi[dr_a + 1]) * LOG2E
                tile = jnp.where(inside, bias, MASK_VALUE)
                o_ref[v, 0, e * GRID_W:(e + 1) * GRID_W, kk * LANES:(kk + 1) * LANES] = tile


def _na_bias_table(rpb):
    n_pairs = NA_HEADS // HEADS_PER_VREG
    n_dr, n_dc = rpb.shape[1], rpb.shape[2]
    dr_pad = pl.cdiv(n_dr, SUBLANES) * SUBLANES
    rp = jnp.pad(rpb, ((0, 0), (0, dr_pad - n_dr), (0, LANES - n_dc)))
    rp = rp.reshape(n_pairs, HEADS_PER_VREG, dr_pad, LANES)
    m2, win = HEADS_PER_VREG * GRID_W, NA_KH * GRID_W
    return pl.pallas_call(
        _na_bias_kernel,
        grid=(n_pairs,),
        in_specs=[pl.BlockSpec((1, HEADS_PER_VREG, dr_pad, LANES), lambda j: (j, 0, 0, 0))],
        out_specs=pl.BlockSpec((NA_KH, 1, m2, win), lambda j: (0, j, 0, 0)),
        out_shape=jax.ShapeDtypeStruct((NA_KH, n_pairs, m2, win), jnp.float32),
        compiler_params=pltpu.CompilerParams(dimension_semantics=("parallel",)),
        name="na_bias_table",
    )(rp)


def _na_kernel(q_ref, k_ref, v_ref, tbl_ref, o_ref, s_scr, *, n_rows):
    win = NA_KH * GRID_W
    ones = jnp.ones((win, LANES), jnp.bfloat16)

    def body(i, carry):
        def window(t):
            r = i * NA_ROWS_PER_ITER + t
            rs = jnp.clip(r - NA_KH // 2, 0, n_rows - NA_KH)
            return r - rs, pl.multiple_of(rs * GRID_W, GRID_W)

        def q_rows(t):
            return pl.ds(pl.multiple_of((i * NA_ROWS_PER_ITER + t) * GRID_W, GRID_W), GRID_W)

        def score(t):
            variant, start = window(t)
            lhs = _stack_heads(q_ref[0, q_rows(t), :])
            kw = k_ref[0, pl.ds(start, win), :]
            s = lax.dot_general(lhs, kw, _NT, preferred_element_type=jnp.float32)
            s_scr[t % NA_SCORE_SLOTS] = s + tbl_ref[variant, 0]

        for t in range(min(NA_LOOKAHEAD, NA_ROWS_PER_ITER)):
            score(t)
        for t in range(NA_ROWS_PER_ITER):
            if t + NA_LOOKAHEAD < NA_ROWS_PER_ITER:
                score(t + NA_LOOKAHEAD)
            s = s_scr[t % NA_SCORE_SLOTS]
            m = jnp.max(s, axis=-1, keepdims=True)
            p = jnp.exp2(s - m)
            _, start = window(t)
            vw = jnp.concatenate([v_ref[0, pl.ds(start, win), :], ones], axis=1)
            o2 = jnp.dot(p.astype(jnp.bfloat16), vw, preferred_element_type=jnp.float32)
            o2 = o2[:, :LANES] / o2[:, LANES:]
            o_ref[0, q_rows(t), :] = _unstack_heads(o2).astype(o_ref.dtype)
        return carry

    lax.fori_loop(0, n_rows // NA_ROWS_PER_ITER, body, 0)


def _swa_mask_table():
    qi = np.arange(SWA_BLOCK)[:, None]
    kc = np.arange(3 * SWA_BLOCK)[None, :]
    tbl = [np.where(np.abs(kc - d * SWA_BLOCK - qi) <= SWA_WINDOW, 0.0, MASK_VALUE)
           for d in range(3)]
    return jnp.asarray(np.stack(tbl), dtype=jnp.float32)


def _swa_kernel(sink_ref, q_ref, k_ref, v_ref, mask_ref, o_ref, s_scr, *, seq):
    j = pl.program_id(1)
    blk = SWA_BLOCK
    win = 3 * blk
    row = lax.broadcasted_iota(jnp.int32, (2 * blk, 1), 0)
    sink_col = jnp.where(row < blk, sink_ref[2 * j], sink_ref[2 * j + 1]) * LOG2E
    ones = jnp.ones((win, LANES), jnp.bfloat16)

    def body(i, carry):
        def window(u):
            n = i * SWA_BLOCKS_PER_ITER + u
            start = jnp.clip((n - 1) * blk, 0, seq - win)
            return n - start // blk, pl.multiple_of(start, blk)

        def q_rows(u):
            return pl.ds(pl.multiple_of((i * SWA_BLOCKS_PER_ITER + u) * blk, blk), blk)

        def score(u):
            variant, start = window(u)
            lhs = _stack_heads(q_ref[0, q_rows(u), :])
            kw = k_ref[0, pl.ds(start, win), :]
            s = lax.dot_general(lhs, kw, _NT, preferred_element_type=jnp.float32)
            mask = mask_ref[variant]
            s_scr[u % SWA_SCORE_SLOTS] = s + jnp.concatenate([mask, mask], axis=0)

        for u in range(min(SWA_LOOKAHEAD, SWA_BLOCKS_PER_ITER)):
            score(u)
        for u in range(SWA_BLOCKS_PER_ITER):
            if u + SWA_LOOKAHEAD < SWA_BLOCKS_PER_ITER:
                score(u + SWA_LOOKAHEAD)
            s = s_scr[u % SWA_SCORE_SLOTS]
            m = jnp.maximum(jnp.max(s, axis=-1, keepdims=True), sink_col)
            e = jnp.exp2(s - m)
            _, start = window(u)
            vw = jnp.concatenate([v_ref[0, pl.ds(start, win), :], ones], axis=1)
            o2 = jnp.dot(e.astype(jnp.bfloat16), vw, preferred_element_type=jnp.float32)
            o2 = o2[:, :LANES] / (o2[:, LANES:] + jnp.exp2(sink_col - m))
            o_ref[0, q_rows(u), :] = _unstack_heads(o2).astype(o_ref.dtype)
        return carry

    lax.fori_loop(0, seq // (SWA_BLOCKS_PER_ITER * blk), body, 0)


def _attention_kernel(sink_ref, qa_ref, ka_ref, va_ref, tbl_ref, qb_ref, kb_ref, vb_ref,
                      mask_ref, ona_ref, oswa_ref, na_scr, swa_scr, *, n_rows, seq):
    _na_kernel(qa_ref, ka_ref, va_ref, tbl_ref, ona_ref, na_scr, n_rows=n_rows)
    _swa_kernel(sink_ref, qb_ref, kb_ref, vb_ref, mask_ref, oswa_ref, swa_scr, seq=seq)


def _attention(qa, ka, va, tbl, qb, kb, vb, sink):
    batch, seq, _ = qa.shape
    n_rows = seq // GRID_W
    n_pairs = NA_HEADS // HEADS_PER_VREG
    assert n_pairs == SWA_Q_HEADS // HEADS_PER_VREG
    pairs_per_group = n_pairs // SWA_KV_HEADS
    na_m, na_win = HEADS_PER_VREG * GRID_W, NA_KH * GRID_W
    swa_m, swa_win = HEADS_PER_VREG * SWA_BLOCK, 3 * SWA_BLOCK
    pair_block = pl.BlockSpec((1, seq, LANES), lambda b, j: (b, 0, j))
    kv_block = pl.BlockSpec((1, seq, LANES), lambda b, j: (b, 0, j // pairs_per_group))
    resident = (2 * 8 * seq * LANES * _BF16_BYTES + 2 * NA_KH * na_m * na_win * _F32_BYTES
                + 3 * SWA_BLOCK * swa_win * _F32_BYTES
                + NA_SCORE_SLOTS * na_m * na_win * _F32_BYTES
                + SWA_SCORE_SLOTS * swa_m * swa_win * _F32_BYTES)
    out = jax.ShapeDtypeStruct(qa.shape, jnp.bfloat16)
    return pl.pallas_call(
        functools.partial(_attention_kernel, n_rows=n_rows, seq=seq),
        grid=(batch, n_pairs),
        in_specs=[pl.BlockSpec(memory_space=pltpu.SMEM),
                  pair_block, pair_block, pair_block,
                  pl.BlockSpec((NA_KH, 1, na_m, na_win), lambda b, j: (0, j, 0, 0)),
                  pair_block, kv_block, kv_block,
                  _const_spec((3, SWA_BLOCK, swa_win))],
        out_specs=(pair_block, pair_block),
        out_shape=(out, out),
        scratch_shapes=[pltpu.VMEM((NA_SCORE_SLOTS, na_m, na_win), jnp.float32),
                        pltpu.VMEM((SWA_SCORE_SLOTS, swa_m, swa_win), jnp.float32)],
        compiler_params=pltpu.CompilerParams(
            dimension_semantics=("parallel", "parallel"),
            vmem_limit_bytes=_vmem_limit(resident)),
        name="attention",
    )(sink, qa, ka, va, tbl, qb, kb, vb, _swa_mask_table())


W_CHUNK = D_MODEL


def _tail_kernel(x_ref, g0_ref, b0_ref, na_ref, swa_ref, bin_ref, bout_ref, g1_ref, b1_ref,
                 bf1_ref, bf2_ref, g2_ref, b2_ref,
                 win_hbm, wna_hbm, wswa_hbm, wout_hbm, w1_hbm, w2_hbm,
                 o_ref,
                 wg_ref, wna_ref, wswa_ref, wout_ref, w1_ref, w2_ref, stage_ref, sem):
    @pl.when(pl.program_id(0) == 0)
    def _():
        cols = lambda c: pl.ds(c * W_CHUNK, W_CHUNK)
        pieces = [(win_hbm.at[0, :, pl.ds(_C_G + c * W_CHUNK, W_CHUNK)], wg_ref.at[:, cols(c)])
                  for c in range(2 * D_MODEL // W_CHUNK)]
        pieces += [(wna_hbm.at[0], wna_ref), (wswa_hbm.at[0], wswa_ref),
                   (wout_hbm.at[0], wout_ref)]
        pieces += [(w1_hbm.at[0, :, cols(c)], w1_ref.at[:, cols(c)])
                   for c in range(D_FF // W_CHUNK)]
        pieces += [(w2_hbm.at[0, cols(c), :], w2_ref.at[cols(c), :])
                   for c in range(D_FF // W_CHUNK)]
        _stage_weights_bf16(pieces, stage_ref, sem)

    h1 = []
    for r0 in range(0, x_ref.shape[0], SUB_TILE_ROWS):
        rows = slice(r0, r0 + SUB_TILE_ROWS)
        y_na = jnp.dot(na_ref[rows, :], wna_ref[...], preferred_element_type=jnp.float32)
        y_swa = jnp.dot(swa_ref[rows, :], wswa_ref[...], preferred_element_type=jnp.float32)
        h = _layer_norm(x_ref[rows, :], g0_ref[...], b0_ref[...])
        gates = (jnp.dot(h.astype(jnp.bfloat16), wg_ref[...],
                         preferred_element_type=jnp.float32) + bin_ref[:, _C_G:_C_END])
        mixed = (jax.nn.sigmoid(gates[:, :D_MODEL]) * y_na
                 + jax.nn.sigmoid(gates[:, D_MODEL:]) * y_swa)
        attn = jnp.dot(mixed.astype(jnp.bfloat16), wout_ref[...],
                       preferred_element_type=jnp.float32) + bout_ref[...]
        h1.append(_layer_norm(DEEPNORM_ALPHA * h + attn, g1_ref[...], b1_ref[...]))

    h = jnp.concatenate(h1, axis=0)
    hb = h.astype(jnp.bfloat16)
    acc = jnp.zeros(h.shape, jnp.float32)
    n_chunks = D_FF // FF_CHUNK
    for c in range(n_chunks):
        sl = slice(c * FF_CHUNK, (c + 1) * FF_CHUNK)
        if c == 0:
            u = jnp.concatenate(
                [jnp.dot(hs.astype(jnp.bfloat16), w1_ref[:, sl],
                         preferred_element_type=jnp.float32) for hs in h1], axis=0)
        else:
            u = jnp.dot(hb, w1_ref[:, sl], preferred_element_type=jnp.float32)
        u = jnp.square(jnp.maximum(u + bf1_ref[:, sl], 0.0)).astype(jnp.bfloat16)
        if c + 1 < n_chunks:
            acc = acc + jnp.dot(u, w2_ref[sl, :], preferred_element_type=jnp.float32)
        else:
            for r0 in range(0, h.shape[0], SUB_TILE_ROWS):
                rows = slice(r0, r0 + SUB_TILE_ROWS)
                z = acc[rows] + jnp.dot(u[rows], w2_ref[sl, :],
                                        preferred_element_type=jnp.float32)
                o_ref[rows, :] = _layer_norm(DEEPNORM_ALPHA * h[rows] + z + bf2_ref[...],
                                             g2_ref[...], b2_ref[...])


def _tail(x2, g0, b0, na, swa, b_in, bout, g1, b1, bf1, bf2, g2, b2,
          w_in, w_na, w_swa, w_out, w_ff1, w_ff2):
    t = x2.shape[0]
    tm = TAIL_TILE_ROWS
    tok = lambda i: (i, 0)
    vec = _const_spec((1, D_MODEL))
    hbm = pl.BlockSpec(memory_space=pl.ANY)
    bf = jnp.bfloat16
    weight_elems = (D_MODEL * 2 * D_MODEL + (NA_WIDTH + SWA_Q_WIDTH) * D_MODEL
                    + D_MODEL * D_MODEL + 2 * D_MODEL * D_FF)
    resident = (weight_elems * _BF16_BYTES + 2 * W_CHUNK * D_MODEL * _F32_BYTES
                + 2 * 2 * tm * D_MODEL * _F32_BYTES
                + 2 * tm * (NA_WIDTH + SWA_Q_WIDTH) * _BF16_BYTES)
    return pl.pallas_call(
        _tail_kernel,
        grid=(t // tm,),
        in_specs=[
            pl.BlockSpec((tm, D_MODEL), tok), vec, vec,
            pl.BlockSpec((tm, NA_WIDTH), tok), pl.BlockSpec((tm, SWA_Q_WIDTH), tok),
            _const_spec((1, _C_END)), vec, vec, vec,
            _const_spec((1, D_FF)), vec, vec, vec,
            hbm, hbm, hbm, hbm, hbm, hbm,
        ],
        out_specs=pl.BlockSpec((tm, D_MODEL), tok),
        out_shape=jax.ShapeDtypeStruct((t, D_MODEL), jnp.float32),
        scratch_shapes=[
            pltpu.VMEM((D_MODEL, 2 * D_MODEL), bf),
            pltpu.VMEM((NA_WIDTH, D_MODEL), bf), pltpu.VMEM((SWA_Q_WIDTH, D_MODEL), bf),
            pltpu.VMEM((D_MODEL, D_MODEL), bf),
            pltpu.VMEM((D_MODEL, D_FF), bf), pltpu.VMEM((D_FF, D_MODEL), bf),
            pltpu.VMEM((2, W_CHUNK, D_MODEL), jnp.float32),
            pltpu.SemaphoreType.DMA((2,)),
        ],
        compiler_params=pltpu.CompilerParams(
            dimension_semantics=("arbitrary",), vmem_limit_bytes=_vmem_limit(resident)),
        name="mix_ffn",
    )(x2, g0, b0, na, swa, b_in, bout, g1, b1, bf1, bf2, g2, b2,
      w_in, w_na, w_swa, w_out, w_ff1, w_ff2)


def kernel(x, ln0_g, ln0_b, w_in, b_in, na_rpb, swa_sink, w_branch_na, w_branch_swa,
           w_out, b_out, ln1_g, ln1_b, w_ff1, b_ff1, w_ff2, b_ff2, ln2_g, ln2_b):
    batch, seq, d = x.shape
    assert d == D_MODEL and w_in.shape == (DEPTH, D_MODEL, _C_END) and DEPTH == 1
    assert seq % INPROJ_TILE_ROWS == 0 and (batch * seq) % TAIL_TILE_ROWS == 0
    assert seq % (NA_ROWS_PER_ITER * GRID_W) == 0 and seq // GRID_W >= NA_KH
    assert seq % (SWA_BLOCKS_PER_ITER * SWA_BLOCK) == 0 and seq >= 3 * SWA_BLOCK
    row = lambda v: v.reshape(1, -1)
    x2 = x.reshape(batch * seq, d)
    b_in_row = row(b_in[0])

    cos_t, sa_t, sb_t = _rotary_tables(seq)
    qa, ka, va, qb, kb, vb = _inproj(
        x2, row(ln0_g), row(ln0_b), b_in_row, cos_t, sa_t, sb_t, w_in, batch, seq)

    three = lambda a: a.reshape(batch, seq, a.shape[-1])
    o_na, o_swa = _attention(three(qa), three(ka), three(va), _na_bias_table(na_rpb[0]),
                             three(qb), three(kb), three(vb), swa_sink[0])

    return _tail(x2, row(ln0_g), row(ln0_b), o_na.reshape(batch * seq, -1),
                 o_swa.reshape(batch * seq, -1), b_in_row, row(b_out[0]),
                 row(ln1_g[0]), row(ln1_b[0]), row(b_ff1[0]), row(b_ff2[0]),
                 row(ln2_g[0]), row(ln2_b[0]),
                 w_in, w_branch_na, w_branch_swa, w_out, w_ff1, w_ff2).reshape(batch, seq, d)
```
